```python
import jax, jax.numpy as jnp
from jax import lax
import numpy as np

D_MODEL = 1024
BATCH = 8
SEQ = 2048
DEPTH = 1
DEC_BATCH = 128
DEC_SEQ = 8
PAST_LEN = 16384
PAGE_SIZE = 128

BRANCH_W = 512
CHUNK = 128
GM_GROUPS = 8
GM_HD = BRANCH_W // GM_GROUPS
RW_HEADS = 8
RW_HD = BRANCH_W // RW_HEADS
RW_W_LORA = 64
RW_A_LORA = 64
RW_G_LORA = 128
RW_COLS = 3 * BRANCH_W + RW_W_LORA + RW_A_LORA + RW_G_LORA
XA_HEADS = 4
XA_HD = BRANCH_W // XA_HEADS
N_MEM = 256
N_BRANCH = 3
IN_COLS = 2 * BRANCH_W + RW_COLS + BRANCH_W + N_BRANCH * D_MODEL
IN_SPLITS = (BRANCH_W, 2 * BRANCH_W, 2 * BRANCH_W + RW_COLS, 3 * BRANCH_W + RW_COLS)
RW_SPLITS = (BRANCH_W, 2 * BRANCH_W, 3 * BRANCH_W, 3 * BRANCH_W + RW_W_LORA,
             3 * BRANCH_W + RW_W_LORA + RW_A_LORA)
N_GROUPS = 4
EXP_PER_GROUP = 8
EXP_TOPK = 2
EXP_FF = 512
RMS_EPS = 1e-6
LN_EPS = 1e-5
GN_EPS = 64e-5

kernel_name = "hybrid_gmlp_rwkv7_memxattn_hiermoe_step"


def rmsnorm(x, g):
    xf = x.astype(jnp.float32)
    y = xf * lax.rsqrt(jnp.mean(xf * xf, axis=-1, keepdims=True) + RMS_EPS)
    return (y * g.astype(jnp.float32)).astype(x.dtype)


def gmlp_branch(pu, pv, ln_g, ln_b, ws, bs):
    b_, t_, _ = pu.shape
    u = jax.nn.gelu(pu)
    vf = jax.nn.gelu(pv).astype(jnp.float32)
    mu = jnp.mean(vf, axis=-1, keepdims=True)
    var = jnp.mean(jnp.square(vf - mu), axis=-1, keepdims=True)
    vn = ((vf - mu) * lax.rsqrt(var + LN_EPS) * ln_g + ln_b).astype(pu.dtype)
    n_chunks = -(-t_ // CHUNK)
    tp = n_chunks * CHUNK
    vp = jnp.pad(vn, ((0, 0), (0, tp - t_), (0, 0))).reshape(b_, n_chunks, CHUNK, GM_GROUPS, GM_HD)
    w_causal = jnp.tril(ws)
    s = jnp.einsum('gts,bcsgd->bctgd', w_causal, vp) + bs.T[None, None, :, :, None]
    s = s.reshape(b_, tp, BRANCH_W)[:, :t_]
    return u * s, vn


def rwkv_branch(p, prev, s0, mu, w0, w2, a0, a2, g2, k_k, k_a, r_k, lnx_g, lnx_b):
    f32 = jnp.float32
    b_, t_, _ = p.shape
    shifted = jnp.concatenate([prev[:, None, :].astype(p.dtype), p[:, :-1]], axis=1)
    xs = p + (shifted - p) * mu
    r, k, v, wd, ad, gd = jnp.split(xs, RW_SPLITS, axis=-1)
    w_log = -jax.nn.softplus(-(w0 + jnp.tanh(wd) @ w2).astype(f32)) - 0.5
    decay = jnp.exp(-jnp.exp(w_log))
    a = jax.nn.sigmoid((a0 + ad @ a2).astype(f32))
    g = jax.nn.sigmoid(gd) @ g2

    def heads(z):
        return z.reshape(b_, t_, RW_HEADS, RW_HD)

    kf = k.astype(f32)
    kk = heads(kf * k_k)
    kk = kk * lax.rsqrt(jnp.maximum(jnp.sum(kk * kk, axis=-1, keepdims=True), 1e-24))
    kf = kf * (1.0 + (a - 1.0) * k_a)
    rh, kh, vh, wh, ah = heads(r.astype(f32)), heads(kf), heads(v.astype(f32)), heads(decay), heads(a)
    a_vec = -kk
    b_vec = kk * ah

    def step(S, inp):
        r_t, k_t, v_t, w_t, a_t, b_t = inp
        sa = jnp.einsum('bhij,bhj->bhi', S, a_t)
        S = S * w_t[:, :, None, :] + sa[..., None] * b_t[:, :, None, :] + v_t[..., None] * k_t[:, :, None, :]
        y = jnp.einsum('bhij,bhj->bhi', S, r_t)
        return S, y

    def tm(z):
        return jnp.moveaxis(z, 1, 0)

    s_fin, y = lax.scan(step, s0.astype(f32), (tm(rh), tm(kh), tm(vh), tm(wh), tm(a_vec), tm(b_vec)))
    y = jnp.moveaxis(y, 0, 1)
    ym = jnp.mean(y, axis=-1, keepdims=True)
    yv = jnp.mean(jnp.square(y - ym), axis=-1, keepdims=True)
    y = ((y - ym) * lax.rsqrt(yv + GN_EPS)).reshape(b_, t_, BRANCH_W) * lnx_g + lnx_b
    bonus = jnp.sum(rh * kh * r_k, axis=-1, keepdims=True) * vh
    y = (y + bonus.reshape(b_, t_, BRANCH_W)) * g.astype(f32)
    return y.astype(p.dtype), s_fin.astype(p.dtype), p[:, -1]


def mem_attend(q, mem_k, mem_v):
    b_, t_, _ = q.shape
    qh = q.reshape(b_, t_, XA_HEADS, XA_HD)
    s = jnp.einsum('bthd,bmhd->bhtm', qh, mem_k).astype(jnp.float32) * (XA_HD ** -0.5)
    pr = jax.nn.softmax(s, axis=-1).astype(q.dtype)
    o = jnp.einsum('bhtm,bmhd->bthd', pr, mem_v)
    return o.reshape(b_, t_, BRANCH_W)


def hier_moe(x, rg_w, rg_b, re_w, re_b, e_wg, e_wu, e_wd):
    f32 = jnp.float32
    b_, t_, d = x.shape
    xt = x.reshape(-1, d)
    pg = jax.nn.softmax((xt @ rg_w + rg_b).astype(f32), axis=-1)
    g_sel = jnp.argmax(pg, axis=-1)
    pg_top = jnp.max(pg, axis=-1)
    oh_g = jax.nn.one_hot(g_sel, N_GROUPS, dtype=f32)
    le = (jnp.einsum('nd,gde->nge', xt, re_w) + re_b).astype(f32)
    le_sel = jnp.einsum('nge,ng->ne', le, oh_g)
    pe = jax.nn.softmax(le_sel, axis=-1)
    top_v, top_i = lax.top_k(pe, EXP_TOPK)
    top_v = top_v / jnp.sum(top_v, axis=-1, keepdims=True)
    w_exp = jnp.sum(jax.nn.one_hot(top_i, EXP_PER_GROUP, dtype=f32) * top_v[..., None], axis=1)
    comb = (pg_top[:, None, None] * oh_g[:, :, None] * w_exp[:, None, :]).astype(x.dtype)
    out = jnp.zeros_like(xt)
    for gi in range(N_GROUPS):
        h = jax.nn.silu(jnp.einsum('nd,edf->nef', xt, e_wg[gi])) * jnp.einsum('nd,edf->nef', xt, e_wu[gi])
        out = out + jnp.einsum('nef,efd->nd', h * comb[:, gi, :, None], e_wd[gi])
    return out.reshape(b_, t_, d)


def decoder_layer(x, mem_k, mem_v, s0, prev, lw):
    (norm1_g, w_in, gate_b, gm_ln_g, gm_ln_b, gm_ws, gm_bs,
     rw_mu, rw_w0, rw_w2, rw_a0, rw_a2, rw_g2, rw_k_k, rw_k_a, rw_r_k, rw_lnx_g, rw_lnx_b,
     w_branch, w_out, norm2_g, rg_w, rg_b, re_w, re_b, e_wg, e_wu, e_wd) = lw
    b_, t_, _ = x.shape
    xn = rmsnorm(x, norm1_g)
    proj = xn @ w_in
    pu, pv, prw, pq, pgate = jnp.split(proj, IN_SPLITS, axis=-1)
    o_gm, v_rows = gmlp_branch(pu, pv, gm_ln_g, gm_ln_b, gm_ws, gm_bs)
    o_rw, s_new, prev_new = rwkv_branch(prw, prev, s0, rw_mu, rw_w0, rw_w2, rw_a0, rw_a2, rw_g2,
                                        rw_k_k, rw_k_a, rw_r_k, rw_lnx_g, rw_lnx_b)
    o_xa = mem_attend(pq, mem_k, mem_v)
    branches = jnp.stack([o_gm, o_rw, o_xa], axis=2)
    up = jnp.einsum('btnw,nwd->btnd', branches, w_branch)
    gates = jax.nn.sigmoid((pgate + gate_b).astype(jnp.float32)).reshape(b_, t_, N_BRANCH, D_MODEL)
    merged = jnp.sum(gates.astype(x.dtype) * up, axis=2)
    h = x + merged @ w_out
    h = h + hier_moe(rmsnorm(h, norm2_g), rg_w, rg_b, re_w, re_b, e_wg, e_wu, e_wd)
    return h, v_rows, s_new, prev_new


def setup_inputs(seed: int = 0) -> dict:
    key = jax.random.key(seed)
    ks = iter(jax.random.split(key, 48))
    L = DEPTH

    def nrm(shape, scale):
        return jax.random.normal(next(ks), shape, jnp.float32) * scale

    def gain(shape):
        return 1.0 + nrm(shape, 0.02)

    return {
        "x_prompt": nrm((BATCH, SEQ, D_MODEL), 1.0),
        "x_sample": nrm((DEC_BATCH, DEC_SEQ, D_MODEL), 1.0),
        "state_rwkv_S": nrm((L, DEC_BATCH, RW_HEADS, RW_HD, RW_HD), 1.0),
        "state_rwkv_shift": nrm((L, DEC_BATCH, RW_COLS), 1.0),
        "cache_mem_k": nrm((L, DEC_BATCH, N_MEM, XA_HEADS, XA_HD), 1.0),
        "cache_mem_v": nrm((L, DEC_BATCH, N_MEM, XA_HEADS, XA_HD), 1.0),
        "mem_prompt": nrm((BATCH, N_MEM, D_MODEL), 1.0),
        "norm1_g": gain((L, D_MODEL)),
        "w_in": nrm((L, D_MODEL, IN_COLS), D_MODEL ** -0.5),
        "gate_b": nrm((L, N_BRANCH * D_MODEL), 0.1),
        "gm_ln_g": gain((L, BRANCH_W)),
        "gm_ln_b": nrm((L, BRANCH_W), 0.02),
        "gm_ws": nrm((L, GM_GROUPS, CHUNK, CHUNK), CHUNK ** -0.5),
        "gm_bs": 1.0 + nrm((L, GM_GROUPS, CHUNK), 0.1),
        "rw_mu": jax.random.uniform(next(ks), (L, RW_COLS), jnp.float32, 0.0, 1.0),
        "rw_w0": jax.random.uniform(next(ks), (L, BRANCH_W), jnp.float32, -6.0, -1.0),
        "rw_w2": nrm((L, RW_W_LORA, BRANCH_W), 0.1),
        "rw_a0": nrm((L, BRANCH_W), 0.5),
        "rw_a2": nrm((L, RW_A_LORA, BRANCH_W), RW_A_LORA ** -0.5),
        "rw_g2": nrm((L, RW_G_LORA, BRANCH_W), RW_G_LORA ** -0.5),
        "rw_k_k": 0.85 + nrm((L, BRANCH_W), 0.1),
        "rw_k_a": 1.0 + nrm((L, BRANCH_W), 0.1),
        "rw_r_k": nrm((L, RW_HEADS, RW_HD), 0.1),
        "rw_lnx_g": gain((L, BRANCH_W)),
        "rw_lnx_b": nrm((L, BRANCH_W), 0.02),
        "xa_wk": nrm((L, D_MODEL, BRANCH_W), D_MODEL ** -0.5),
        "xa_wv": nrm((L, D_MODEL, BRANCH_W), D_MODEL ** -0.5),
        "w_branch": nrm((L, N_BRANCH, BRANCH_W, D_MODEL), BRANCH_W ** -0.5),
        "w_out": nrm((L, D_MODEL, D_MODEL), D_MODEL ** -0.5),
        "norm2_g": gain((L, D_MODEL)),
        "rg_w": nrm((L, D_MODEL, N_GROUPS), D_MODEL ** -0.5),
        "rg_b": nrm((L, N_GROUPS), 0.01),
        "re_w": nrm((L, N_GROUPS, D_MODEL, EXP_PER_GROUP), D_MODEL ** -0.5),
        "re_b": nrm((L, N_GROUPS, EXP_PER_GROUP), 0.01),
        "e_wg": nrm((L, N_GROUPS, EXP_PER_GROUP, D_MODEL, EXP_FF), D_MODEL ** -0.5),
        "e_wu": nrm((L, N_GROUPS, EXP_PER_GROUP, D_MODEL, EXP_FF), D_MODEL ** -0.5),
        "e_wd": nrm((L, N_GROUPS, EXP_PER_GROUP, EXP_FF, D_MODEL), EXP_FF ** -0.5),
        "final_g": gain((D_MODEL,)),
    }


def reference(x_prompt, x_sample, state_rwkv_S, state_rwkv_shift, cache_mem_k, cache_mem_v, mem_prompt,
              norm1_g, w_in, gate_b, gm_ln_g, gm_ln_b, gm_ws, gm_bs,
              rw_mu, rw_w0, rw_w2, rw_a0, rw_a2, rw_g2, rw_k_k, rw_k_a, rw_r_k, rw_lnx_g, rw_lnx_b,
              xa_wk, xa_wv, w_branch, w_out, norm2_g, rg_w, rg_b, re_w, re_b, e_wg, e_wu, e_wd, final_g):
    bp = x_prompt.shape[0]
    hp, hs = x_prompt, x_sample
    sp_list, shp_list, mkp_list, mvp_list = [], [], [], []
    ss_list, shs_list, vs_list = [], [], []
    for l in range(DEPTH):
        lw = (norm1_g[l], w_in[l], gate_b[l], gm_ln_g[l], gm_ln_b[l], gm_ws[l], gm_bs[l],
              rw_mu[l], rw_w0[l], rw_w2[l], rw_a0[l], rw_a2[l], rw_g2[l], rw_k_k[l], rw_k_a[l], rw_r_k[l],
              rw_lnx_g[l], rw_lnx_b[l], w_branch[l], w_out[l], norm2_g[l], rg_w[l], rg_b[l], re_w[l],
              re_b[l], e_wg[l], e_wu[l], e_wd[l])
        mk_p = (mem_prompt @ xa_wk[l]).reshape(bp, N_MEM, XA_HEADS, XA_HD)
        mv_p = (mem_prompt @ xa_wv[l]).reshape(bp, N_MEM, XA_HEADS, XA_HD)
        s0_p = jnp.zeros((bp, RW_HEADS, RW_HD, RW_HD), x_prompt.dtype)
        prev_p = jnp.zeros((bp, RW_COLS), x_prompt.dtype)
        hp, _, sp, shp = decoder_layer(hp, mk_p, mv_p, s0_p, prev_p, lw)
        hs, vs, ss, shs = decoder_layer(hs, cache_mem_k[l], cache_mem_v[l], state_rwkv_S[l],
                                        state_rwkv_shift[l], lw)
        sp_list.append(sp); shp_list.append(shp); mkp_list.append(mk_p); mvp_list.append(mv_p)
        ss_list.append(ss); shs_list.append(shs); vs_list.append(vs)
    y_prompt = rmsnorm(hp, final_g)
    y_sample = rmsnorm(hs, final_g)
    new_rwkv_S_prompt = jnp.stack(sp_list, axis=0)
    new_rwkv_shift_prompt = jnp.stack(shp_list, axis=0)
    new_mem_k_prompt = jnp.stack(mkp_list, axis=0)
    new_mem_v_prompt = jnp.stack(mvp_list, axis=0)
    new_rwkv_S_sample = jnp.stack(ss_list, axis=0)
    new_rwkv_shift_sample = jnp.stack(shs_list, axis=0)
    new_gmlp_v_sample = jnp.stack(vs_list, axis=0)
    return (y_prompt, y_sample, new_rwkv_S_prompt, new_rwkv_shift_prompt, new_mem_k_prompt, new_mem_v_prompt,
            new_rwkv_S_sample, new_rwkv_shift_sample, new_gmlp_v_sample)
```

```python
import functools

import numpy as np
import jax
import jax.numpy as jnp
from jax import lax
from jax.experimental import pallas as pl
from jax.experimental.pallas import tpu as pltpu

F32 = jnp.float32
BF16 = jnp.bfloat16

D_MODEL = 1024
BRANCH_W = 512
CHUNK = 128
GM_GROUPS = 8
GM_HD = BRANCH_W // GM_GROUPS
RW_HEADS = 8
RW_HD = BRANCH_W // RW_HEADS
RW_W_LORA = 64
RW_A_LORA = 64
RW_G_LORA = 128
RW_COLS = 3 * BRANCH_W + RW_W_LORA + RW_A_LORA + RW_G_LORA
XA_HEADS = 4
XA_HD = BRANCH_W // XA_HEADS
N_MEM = 256
N_BRANCH = 3
GATE_COLS = N_BRANCH * D_MODEL
N_GROUPS = 4
EXP_PER_GROUP = 8
N_EXPERTS = N_GROUPS * EXP_PER_GROUP
EXP_FF = 512
RMS_EPS = 1e-6
LN_EPS = 1e-5
GN_EPS = 64e-5

LANES = 128
RW_ROWS = 64
MIB = 1024 * 1024


def _dot(a, b):
    return jnp.dot(a.astype(BF16), b.astype(BF16), preferred_element_type=F32)


def _dot_nt(a, b):
    return lax.dot_general(a.astype(BF16), b.astype(BF16), (((1,), (1,)), ((), ())),
                           preferred_element_type=F32)


def _dot_tn(a, b):
    return lax.dot_general(a.astype(BF16), b.astype(BF16), (((0,), (0,)), ((), ())),
                           preferred_element_type=F32)


def _sigmoid(x):
    return 1.0 / (1.0 + jnp.exp(-x))


def _gelu(x):
    c = np.float32(np.sqrt(2.0 / np.pi))
    return x * (0.5 * (1.0 + jnp.tanh(c * (x + 0.044715 * (x * x * x)))))


def _softplus(x):
    return jnp.maximum(x, 0.0) + jnp.log(1.0 + jnp.exp(-jnp.abs(x)))


def _rmsnorm(x, g):
    return x * lax.rsqrt(jnp.mean(x * x, axis=-1, keepdims=True) + RMS_EPS) * g


def _split3(x):
    hi = x.astype(BF16)
    r1 = x - hi.astype(F32)
    mid = r1.astype(BF16)
    lo = (r1 - mid.astype(F32)).astype(BF16)
    return hi, mid, lo


def _params(sem, vmem_mib):
    return pltpu.CompilerParams(dimension_semantics=sem, vmem_limit_bytes=vmem_mib * MIB)


def _full(shape):
    nd = len(shape)
    return pl.BlockSpec(shape, lambda *_: (0,) * nd)


def _in_proj_kernel(x_ref, g_ref, wgm_ref, wrw_ref, wq_ref, wgt_ref, gm_ref, rw_ref, q_ref, gt_ref):
    xb = _rmsnorm(x_ref[...], g_ref[...]).astype(BF16)
    gm_ref[...] = jnp.dot(xb, wgm_ref[...], preferred_element_type=F32)
    rw_ref[...] = jnp.dot(xb, wrw_ref[...], preferred_element_type=F32)
    q_ref[...] = jnp.dot(xb, wq_ref[...], preferred_element_type=F32)
    gt_ref[...] = jnp.dot(xb, wgt_ref[...], preferred_element_type=F32)


def _in_proj(x, g, w_segs, tm):
    n = x.shape[0]
    widths = [w.shape[1] for w in w_segs]
    return pl.pallas_call(
        _in_proj_kernel,
        grid=(n // tm,),
        in_specs=[pl.BlockSpec((tm, D_MODEL), lambda i: (i, 0)), _full((1, D_MODEL))]
        + [pl.BlockSpec((D_MODEL, wd), lambda i: (0, 0), pipeline_mode=pl.Buffered(1)) for wd in widths],
        out_specs=[pl.BlockSpec((tm, wd), lambda i: (i, 0)) for wd in widths],
        out_shape=[jax.ShapeDtypeStruct((n, wd), F32) for wd in widths],
        compiler_params=_params(("arbitrary",), 48),
        name="in_proj",
    )(x, g, *w_segs)


def _mem_kv_kernel(m_ref, wk_ref, wv_ref, k_ref, v_ref):
    mb = m_ref[...].astype(BF16)
    k_ref[...] = jnp.dot(mb, wk_ref[...], preferred_element_type=F32)
    v_ref[...] = jnp.dot(mb, wv_ref[...], preferred_element_type=F32)


def _mem_kv(mem, wk, wv, tm=512):
    n = mem.shape[0]
    return pl.pallas_call(
        _mem_kv_kernel,
        grid=(n // tm,),
        in_specs=[pl.BlockSpec((tm, D_MODEL), lambda i: (i, 0)),
                  _full((D_MODEL, BRANCH_W)), _full((D_MODEL, BRANCH_W))],
        out_specs=[pl.BlockSpec((tm, BRANCH_W), lambda i: (i, 0))] * 2,
        out_shape=[jax.ShapeDtypeStruct((n, BRANCH_W), F32)] * 2,
        compiler_params=_params(("arbitrary",), 32),
        name="mem_kv",
    )(mem, wk, wv)


def _gmlp_kernel(pu_ref, pv_ref, lng_ref, lnb_ref, wmix_ref, bmix_ref, o_ref, *vn_refs):
    u = _gelu(pu_ref[...])
    vf = _gelu(pv_ref[...])
    mu = jnp.mean(vf, axis=-1, keepdims=True)
    vc = vf - mu
    var = jnp.mean(vc * vc, axis=-1, keepdims=True)
    vn = vc * lax.rsqrt(var + LN_EPS) * lng_ref[...] + lnb_ref[...]
    if vn_refs:
        vn_refs[0][...] = vn
    lane = lax.broadcasted_iota(jnp.int32, (CHUNK, LANES), 1)
    lo_half = lane < GM_HD
    for p in range(GM_GROUPS // 2):
        vp = vn[:, p * LANES:(p + 1) * LANES]
        s = (_dot(wmix_ref[2 * p], jnp.where(lo_half, vp, 0.0))
             + _dot(wmix_ref[2 * p + 1], jnp.where(lo_half, 0.0, vp)))
        s = s + bmix_ref[:, p * LANES:(p + 1) * LANES]
        o_ref[:, p * LANES:(p + 1) * LANES] = (u[:, p * LANES:(p + 1) * LANES] * s).astype(o_ref.dtype)


def _gmlp(p_gm, ln_g, ln_b, wmix, bmix, emit_vn):
    n = p_gm.shape[0]
    out_shape = [jax.ShapeDtypeStruct((n, BRANCH_W), BF16)]
    out_specs = [pl.BlockSpec((CHUNK, BRANCH_W), lambda i: (i, 0))]
    if emit_vn:
        out_shape.append(jax.ShapeDtypeStruct((n, BRANCH_W), F32))
        out_specs.append(pl.BlockSpec((CHUNK, BRANCH_W), lambda i: (i, 0)))
    return pl.pallas_call(
        _gmlp_kernel,
        grid=(n // CHUNK,),
        in_specs=[pl.BlockSpec((CHUNK, BRANCH_W), lambda i: (i, 0)),
                  pl.BlockSpec((CHUNK, BRANCH_W), lambda i: (i, 1)),
                  _full((1, BRANCH_W)), _full((1, BRANCH_W)),
                  _full((GM_GROUPS, CHUNK, CHUNK)), _full((CHUNK, BRANCH_W))],
        out_specs=out_specs,
        out_shape=out_shape,
        compiler_params=_params(("arbitrary",), 32),
        name="gmlp",
    )(p_gm, p_gm, ln_g, ln_b, wmix, bmix)


def _rwkv_rows(p, first, prev, s_ref, o_ref, prm, nseq):
    (mu, w0, w2, a0, a2, g2, k_k, k_a, r_k, lnx_g, lnx_b) = prm
    rows = RW_ROWS
    ls = rows // nseq
    shifted = jnp.where(first, prev, pltpu.roll(p, 1, 0))
    xs = p + (shifted - p) * mu
    r = xs[:, 0:BRANCH_W]
    k = xs[:, BRANCH_W:2 * BRANCH_W]
    v = xs[:, 2 * BRANCH_W:3 * BRANCH_W]
    o = 3 * BRANCH_W
    wd = xs[:, o:o + RW_W_LORA]
    ad = xs[:, o + RW_W_LORA:o + RW_W_LORA + RW_A_LORA]
    gd = xs[:, o + RW_W_LORA + RW_A_LORA:RW_COLS]

    w_log = -_softplus(-(w0 + _dot(jnp.tanh(wd), w2))) - 0.5
    logw = -jnp.exp(w_log)
    a = _sigmoid(a0 + _dot(ad, a2))
    g = _dot(_sigmoid(gd), g2)
    kkr = k * k_k
    kf = k * (1.0 + (a - 1.0) * k_a)
    rkr = r * kf * r_k

    ri = lax.broadcasted_iota(jnp.int32, (rows, rows), 0)
    ci = lax.broadcasted_iota(jnp.int32, (rows, rows), 1)
    if nseq == 1:
        same = ci >= 0
    else:
        sh = int(np.log2(ls))
        same = lax.shift_right_logical(ri, sh) == lax.shift_right_logical(ci, sh)
    low_incl = same & (ci <= ri)
    low_strict = same & (ci < ri)
    m_incl = jnp.where(low_incl, 1.0, 0.0).astype(BF16)
    m_same = jnp.where(same, 1.0, 0.0).astype(BF16)
    hi, mid, lo = _split3(logw)
    cum = (jnp.dot(m_incl, hi, preferred_element_type=F32) + jnp.dot(m_incl, mid, preferred_element_type=F32)
           + jnp.dot(m_incl, lo, preferred_element_type=F32))
    tot = (jnp.dot(m_same, hi, preferred_element_type=F32) + jnp.dot(m_same, mid, preferred_element_type=F32)
           + jnp.dot(m_same, lo, preferred_element_type=F32))
    g_t = jnp.exp(cum)
    g_prev = jnp.exp(cum - logw)
    g_inv = jnp.exp(-cum)
    g_end = jnp.exp(tot - cum)
    g_tot = jnp.exp(tot)
    eye = jnp.where(ri == ci, 1.0, 0.0)

    outs = []
    for h in range(RW_HEADS):
        sl = slice(h * RW_HD, (h + 1) * RW_HD)
        kk = kkr[:, sl]
        kk = kk * lax.rsqrt(jnp.maximum(jnp.sum(kk * kk, axis=-1, keepdims=True), 1e-24))
        a_h = a[:, sl]
        av = -kk
        bv = kk * a_h
        k_h = kf[:, sl]
        v_h = v[:, sl]
        at = av * g_prev[:, sl]
        rt = r[:, sl] * g_t[:, sl]
        bt = bv * g_inv[:, sl]
        kt = k_h * g_inv[:, sl]
        bh = bv * g_end[:, sl]
        kh = k_h * g_end[:, sl]

        ar = jnp.concatenate([at, rt], axis=0)
        pb = _dot_nt(ar, bt)
        pk = _dot_nt(ar, kt)
        l_ab = jnp.where(low_strict, pb[:rows], 0.0)
        l_ak = jnp.where(low_strict, pk[:rows], 0.0)
        a_rb = jnp.where(low_incl, pb[rows:], 0.0)
        a_rk = jnp.where(low_incl, pk[rows:], 0.0)

        tm = eye + l_ab
        pw = _dot(l_ab, l_ab)
        n_dbl = int(np.log2(ls)) - 1
        for it in range(n_dbl):
            if it < n_dbl - 1:
                z = _dot(jnp.concatenate([tm, pw], axis=0), pw)
                tm = tm + z[:rows]
                pw = z[rows:]
            else:
                tm = tm + _dot(tm, pw)

        if nseq == 1:
            s0 = s_ref[0, h]
            ars = _dot_nt(ar, s0)
            as0 = ars[:rows]
            rs0 = ars[rows:]
        else:
            as_parts, rs_parts = [], []
            for b in range(nseq):
                rb = slice(b * ls, (b + 1) * ls)
                arb = jnp.concatenate([at[rb], rt[rb]], axis=0)
                z = _dot_nt(arb, s_ref[b, h])
                as_parts.append(z[:ls])
                rs_parts.append(z[ls:])
            as0 = jnp.concatenate(as_parts, axis=0)
            rs0 = jnp.concatenate(rs_parts, axis=0)

        u = _dot(tm, as0 + _dot(l_ak, v_h))
        y = rs0 + _dot(a_rb, u) + _dot(a_rk, v_h)

        for b in range(nseq):
            rb = slice(b * ls, (b + 1) * ls)
            uv = jnp.concatenate([u[rb], v_h[rb]], axis=0)
            bk = jnp.concatenate([bh[rb], kh[rb]], axis=0)
            s_ref[b, h] = s_ref[b, h] * g_tot[b * ls:b * ls + 1, sl] + _dot_tn(uv, bk)

        ym = jnp.mean(y, axis=-1, keepdims=True)
        yc = y - ym
        yv = jnp.mean(yc * yc, axis=-1, keepdims=True)
        yn = yc * lax.rsqrt(yv + GN_EPS) * lnx_g[:, sl] + lnx_b[:, sl]
        bonus = jnp.sum(rkr[:, sl], axis=-1, keepdims=True) * v_h
        outs.append((yn + bonus) * g[:, sl])
    o_ref[...] = jnp.concatenate(outs, axis=-1).astype(o_ref.dtype)


def _rwkv_prompt_kernel(p_ref, *refs):
    prm_refs, (o_ref, s_ref, sh_ref, carry_ref) = refs[:11], refs[11:]
    c = pl.program_id(1)

    @pl.when(c == 0)
    def _():
        s_ref[...] = jnp.zeros_like(s_ref)
        carry_ref[...] = jnp.zeros_like(carry_ref)

    p = p_ref[...]
    first = lax.broadcasted_iota(jnp.int32, (RW_ROWS, 1), 0) == 0
    prm = tuple(x[...] for x in prm_refs)
    _rwkv_rows(p, first, carry_ref[0:1, :], s_ref, o_ref, prm, nseq=1)
    last = p[RW_ROWS - 1:RW_ROWS, :]
    carry_ref[0:1, :] = last
    sh_ref[...] = last


def _rwkv_sample_kernel(p_ref, prev_ref, s0_ref, *refs):
    prm_refs, (o_ref, s_ref, sh_ref) = refs[:11], refs[11:]
    nseq = s0_ref.shape[0]
    ls = RW_ROWS // nseq
    s_ref[...] = s0_ref[...]
    p = p_ref[...]
    first = (lax.broadcasted_iota(jnp.int32, (RW_ROWS, 1), 0) & (ls - 1)) == 0
    prm = tuple(x[...] for x in prm_refs)
    _rwkv_rows(p, first, prev_ref[...], s_ref, o_ref, prm, nseq=nseq)
    for b in range(nseq):
        sh_ref[b:b + 1, :] = p_ref[(b + 1) * ls - 1:(b + 1) * ls, :]


def _rw_param_specs():
    shapes = [(1, RW_COLS), (1, BRANCH_W), (RW_W_LORA, BRANCH_W), (1, BRANCH_W), (RW_A_LORA, BRANCH_W),
              (RW_G_LORA, BRANCH_W), (1, BRANCH_W), (1, BRANCH_W), (1, BRANCH_W), (1, BRANCH_W), (1, BRANCH_W)]
    return [_full(s) for s in shapes]


def _rwkv_prompt(p_rw, prm, batch, seq):
    nc = seq // RW_ROWS
    p3 = p_rw.reshape(batch, seq, RW_COLS)
    o, s_new, sh = pl.pallas_call(
        _rwkv_prompt_kernel,
        grid=(batch, nc),
        in_specs=[pl.BlockSpec((None, RW_ROWS, RW_COLS), lambda b, c: (b, c, 0))] + _rw_param_specs(),
        out_specs=[pl.BlockSpec((None, RW_ROWS, BRANCH_W), lambda b, c: (b, c, 0)),
                   pl.BlockSpec((1, RW_HEADS, RW_HD, RW_HD), lambda b, c: (b, 0, 0, 0)),
                   pl.BlockSpec((None, 1, RW_COLS), lambda b, c: (b, 0, 0))],
        out_shape=[jax.ShapeDtypeStruct((batch, seq, BRANCH_W), BF16),
                   jax.ShapeDtypeStruct((batch, RW_HEADS, RW_HD, RW_HD), F32),
                   jax.ShapeDtypeStruct((batch, 1, RW_COLS), F32)],
        scratch_shapes=[pltpu.VMEM((8, RW_COLS), F32)],
        compiler_params=_params(("arbitrary", "arbitrary"), 32),
        name="rwkv_prompt",
    )(p3, *prm)
    return o.reshape(batch * seq, BRANCH_W), s_new, sh.reshape(batch, RW_COLS)


def _rwkv_sample(p_rw, prev_rows, s0, prm, seq):
    n = p_rw.shape[0]
    nseq = RW_ROWS // seq
    batch = n // seq
    return pl.pallas_call(
        _rwkv_sample_kernel,
        grid=(n // RW_ROWS,),
        in_specs=[pl.BlockSpec((RW_ROWS, RW_COLS), lambda i: (i, 0)),
                  pl.BlockSpec((RW_ROWS, RW_COLS), lambda i: (i, 0)),
                  pl.BlockSpec((nseq, RW_HEADS, RW_HD, RW_HD), lambda i: (i, 0, 0, 0))] + _rw_param_specs(),
        out_specs=[pl.BlockSpec((RW_ROWS, BRANCH_W), lambda i: (i, 0)),
                   pl.BlockSpec((nseq, RW_HEADS, RW_HD, RW_HD), lambda i: (i, 0, 0, 0)),
                   pl.BlockSpec((nseq, RW_COLS), lambda i: (i, 0))],
        out_shape=[jax.ShapeDtypeStruct((n, BRANCH_W), BF16),
                   jax.ShapeDtypeStruct((batch, RW_HEADS, RW_HD, RW_HD), F32),
                   jax.ShapeDtypeStruct((batch, RW_COLS), F32)],
        compiler_params=_params(("arbitrary",), 32),
        name="rwkv_sample",
    )(p_rw, prev_rows, s0, *prm)


def _attend(q, k, v):
    s = _dot_nt(q, k) * (XA_HD ** -0.5)
    e = jnp.exp(s - jnp.max(s, axis=-1, keepdims=True))
    pr = e / jnp.sum(e, axis=-1, keepdims=True)
    return _dot(pr, v)


def _xattn_prompt_kernel(q_ref, k_ref, v_ref, o_ref):
    for h in range(XA_HEADS):
        sl = slice(h * XA_HD, (h + 1) * XA_HD)
        o_ref[:, sl] = _attend(q_ref[:, sl], k_ref[:, sl], v_ref[:, sl]).astype(o_ref.dtype)


def _xattn_prompt(q, mk, mv, batch, seq, tq=512):
    nt = seq // tq
    return pl.pallas_call(
        _xattn_prompt_kernel,
        grid=(batch, nt),
        in_specs=[pl.BlockSpec((tq, BRANCH_W), lambda b, i: (b * nt + i, 0)),
                  pl.BlockSpec((N_MEM, BRANCH_W), lambda b, i: (b, 0)),
                  pl.BlockSpec((N_MEM, BRANCH_W), lambda b, i: (b, 0))],
        out_specs=pl.BlockSpec((tq, BRANCH_W), lambda b, i: (b * nt + i, 0)),
        out_shape=jax.ShapeDtypeStruct((batch * seq, BRANCH_W), BF16),
        compiler_params=_params(("arbitrary", "arbitrary"), 32),
        name="xattn_prompt",
    )(q, mk, mv)


def _xattn_sample_kernel(q_ref, k_ref, v_ref, o_ref):
    nb = k_ref.shape[0]
    ls = q_ref.shape[0] // nb
    for b in range(nb):
        for h in range(XA_HEADS):
            sl = slice(h * XA_HD, (h + 1) * XA_HD)
            o_ref[b * ls:(b + 1) * ls, sl] = _attend(
                q_ref[b * ls:(b + 1) * ls, sl], k_ref[b, :, sl], v_ref[b, :, sl]).astype(o_ref.dtype)


def _xattn_sample(q, mk, mv, seq, nb=8):
    batch = mk.shape[0]
    return pl.pallas_call(
        _xattn_sample_kernel,
        grid=(batch // nb,),
        in_specs=[pl.BlockSpec((nb * seq, BRANCH_W), lambda i: (i, 0)),
                  pl.BlockSpec((nb, N_MEM, BRANCH_W), lambda i: (i, 0, 0)),
                  pl.BlockSpec((nb, N_MEM, BRANCH_W), lambda i: (i, 0, 0))],
        out_specs=pl.BlockSpec((nb * seq, BRANCH_W), lambda i: (i, 0)),
        out_shape=jax.ShapeDtypeStruct((batch * seq, BRANCH_W), BF16),
        compiler_params=_params(("arbitrary",), 40),
        name="xattn_sample",
    )(q, mk, mv)


def _merge_kernel(x_ref, gm_ref, rw_ref, xa_ref, gt_ref, gb_ref, wb_ref, wo_ref, n2_ref,
                  wr_hi_ref, wr_lo_ref, br_ref, h_ref, hn_ref, comb_ref):
    merged = None
    for n, br in enumerate((gm_ref, rw_ref, xa_ref)):
        cs = slice(n * D_MODEL, (n + 1) * D_MODEL)
        up = jnp.dot(br[...], wb_ref[n], preferred_element_type=F32)
        term = _sigmoid(gt_ref[:, cs] + gb_ref[:, cs]) * up
        merged = term if merged is None else merged + term
    h = x_ref[...] + _dot(merged, wo_ref[...])
    h_ref[...] = h
    hn = _rmsnorm(h, n2_ref[...])
    hn_ref[...] = hn.astype(hn_ref.dtype)

    hn_hi = hn.astype(BF16)
    hn_lo = (hn - hn_hi.astype(F32)).astype(BF16)
    logits = (jnp.dot(hn_hi, wr_hi_ref[...], preferred_element_type=F32)
              + jnp.dot(hn_lo, wr_hi_ref[...], preferred_element_type=F32)
              + jnp.dot(hn_hi, wr_lo_ref[...], preferred_element_type=F32)) + br_ref[...]
    lane = lax.broadcasted_iota(jnp.int32, logits.shape, 1)
    neg = -jnp.inf
    big = jnp.int32(1 << 20)
    gmask = (lane >= N_EXPERTS) & (lane < N_EXPERTS + N_GROUPS)
    gl = jnp.where(gmask, logits, neg)
    gmax = jnp.max(gl, axis=-1, keepdims=True)
    gsel = jnp.min(jnp.where(gl == gmax, lane, big), axis=-1, keepdims=True) - N_EXPERTS
    gsum = jnp.sum(jnp.where(gmask, jnp.exp(logits - gmax), 0.0), axis=-1, keepdims=True)
    pg_top = 1.0 / gsum
    emask = (lane >= gsel * EXP_PER_GROUP) & (lane < (gsel + 1) * EXP_PER_GROUP)
    el = jnp.where(emask, logits, neg)
    m1 = jnp.max(el, axis=-1, keepdims=True)
    i1 = jnp.min(jnp.where(el == m1, lane, big), axis=-1, keepdims=True)
    el2 = jnp.where(lane == i1, neg, el)
    m2 = jnp.max(el2, axis=-1, keepdims=True)
    i2 = jnp.min(jnp.where(el2 == m2, lane, big), axis=-1, keepdims=True)
    t2 = jnp.exp(m2 - m1)
    w1 = pg_top / (1.0 + t2)
    w2 = pg_top * t2 / (1.0 + t2)
    comb_ref[...] = jnp.where(lane == i1, w1, 0.0) + jnp.where(lane == i2, w2, 0.0)


def _merge(x, o_gm, o_rw, o_xa, p_gate, gate_b, w_branch, w_out, n2_g, wr_hi, wr_lo, b_r, tm=256):
    n = x.shape[0]
    row = lambda wd: pl.BlockSpec((tm, wd), lambda i: (i, 0))
    return pl.pallas_call(
        _merge_kernel,
        grid=(n // tm,),
        in_specs=[row(D_MODEL), row(BRANCH_W), row(BRANCH_W), row(BRANCH_W), row(GATE_COLS),
                  _full((1, GATE_COLS)), _full((N_BRANCH, BRANCH_W, D_MODEL)), _full((D_MODEL, D_MODEL)),
                  _full((1, D_MODEL)), _full((D_MODEL, LANES)), _full((D_MODEL, LANES)), _full((1, LANES))],
        out_specs=[row(D_MODEL), row(D_MODEL), row(LANES)],
        out_shape=[jax.ShapeDtypeStruct((n, D_MODEL), F32), jax.ShapeDtypeStruct((n, D_MODEL), BF16),
                   jax.ShapeDtypeStruct((n, LANES), F32)],
        compiler_params=_params(("arbitrary",), 40),
        name="merge",
    )(x, o_gm, o_rw, o_xa, p_gate, gate_b, w_branch, w_out, n2_g, wr_hi, wr_lo, b_r)


def _moe_kernel(h_ref, hn_ref, comb_ref, wg_ref, wu_ref, wd_ref, fg_ref, y_ref, acc_ref):
    e = pl.program_id(1)

    @pl.when(e == 0)
    def _():
        acc_ref[...] = jnp.zeros_like(acc_ref)

    xb = hn_ref[...]
    gate = jnp.dot(xb, wg_ref[...], preferred_element_type=F32)
    up = jnp.dot(xb, wu_ref[...], preferred_element_type=F32)
    lane = lax.broadcasted_iota(jnp.int32, comb_ref.shape, 1)
    c_e = jnp.sum(jnp.where(lane == e, comb_ref[...], 0.0), axis=-1, keepdims=True)
    act = gate * _sigmoid(gate) * up * c_e
    acc_ref[...] += _dot(act, wd_ref[...])

    @pl.when(e == pl.num_programs(1) - 1)
    def _():
        y_ref[...] = _rmsnorm(h_ref[...] + acc_ref[...], fg_ref[...])


def _moe(h, hn, comb, wg, wu, wd, final_g, tm=512):
    n = h.shape[0]
    return pl.pallas_call(
        _moe_kernel,
        grid=(n // tm, N_EXPERTS),
        in_specs=[pl.BlockSpec((tm, D_MODEL), lambda i, e: (i, 0)),
                  pl.BlockSpec((tm, D_MODEL), lambda i, e: (i, 0)),
                  pl.BlockSpec((tm, LANES), lambda i, e: (i, 0)),
                  pl.BlockSpec((None, D_MODEL, EXP_FF), lambda i, e: (e, 0, 0)),
                  pl.BlockSpec((None, D_MODEL, EXP_FF), lambda i, e: (e, 0, 0)),
                  pl.BlockSpec((None, EXP_FF, D_MODEL), lambda i, e: (e, 0, 0)),
                  _full((1, D_MODEL))],
        out_specs=pl.BlockSpec((tm, D_MODEL), lambda i, e: (i, 0)),
        out_shape=jax.ShapeDtypeStruct((n, D_MODEL), F32),
        scratch_shapes=[pltpu.VMEM((tm, D_MODEL), F32)],
        compiler_params=_params(("arbitrary", "arbitrary"), 40),
        name="moe",
    )(h, hn, comb, wg, wu, wd, final_g)


def _layer(x2d, mem_k, mem_v, rw_state, w, *, prompt, batch, seq):
    n = x2d.shape[0]
    tm_in = 256
    p_gm, p_rw, p_q, p_gate = _in_proj(x2d, w["norm1_g"], w["w_in_segs"], tm_in)
    if prompt:
        o_gm = _gmlp(p_gm, w["gm_ln_g"], w["gm_ln_b"], w["gm_mix_prompt"], w["gm_bias_prompt"], False)[0]
        vn = None
        o_rw, s_new, sh_new = _rwkv_prompt(p_rw, w["rw_prm"], batch, seq)
        o_xa = _xattn_prompt(p_q, mem_k, mem_v, batch, seq)
    else:
        o_gm, vn = _gmlp(p_gm, w["gm_ln_g"], w["gm_ln_b"], w["gm_mix_sample"], w["gm_bias_sample"], True)
        s0, shift = rw_state
        prev_rows = jnp.pad(shift[:, None, :], ((0, 0), (0, seq - 1), (0, 0))).reshape(n, RW_COLS)
        o_rw, s_new, sh_new = _rwkv_sample(p_rw, prev_rows, s0, w["rw_prm"], seq)
        o_xa = _xattn_sample(p_q, mem_k, mem_v, seq)
    h, hn, comb = _merge(x2d, o_gm, o_rw, o_xa, p_gate, w["gate_b"], w["w_branch"], w["w_out"],
                         w["norm2_g"], w["wr_hi"], w["wr_lo"], w["b_r"])
    y = _moe(h, hn, comb, w["e_wg"], w["e_wu"], w["e_wd"], w["final_g"])
    return y, s_new, sh_new, vn


def kernel(x_prompt, x_sample, state_rwkv_S, state_rwkv_shift, cache_mem_k, cache_mem_v, mem_prompt, norm1_g, w_in, gate_b, gm_ln_g, gm_ln_b, gm_ws, gm_bs, rw_mu, rw_w0, rw_w2, rw_a0, rw_a2, rw_g2, rw_k_k, rw_k_a, rw_r_k, rw_lnx_g, rw_lnx_b, xa_wk, xa_wv, w_branch, w_out, norm2_g, rg_w, rg_b, re_w, re_b, e_wg, e_wu, e_wd, final_g):
    bp, seq_p, _ = x_prompt.shape
    bs, seq_s, _ = x_sample.shape
    depth = w_in.shape[0]
    assert depth == 1 and seq_p % CHUNK == 0 and RW_ROWS % seq_s == 0 and CHUNK % seq_s == 0

    l = 0
    row = lambda a: a.reshape(1, -1)
    seg = (0, 2 * BRANCH_W, 2 * BRANCH_W + RW_COLS, 3 * BRANCH_W + RW_COLS, 3 * BRANCH_W + RW_COLS + GATE_COLS)
    w_causal = jnp.tril(gm_ws[l])
    nrep = CHUNK // seq_s
    blk = w_causal[:, :seq_s, :seq_s]
    eye_rep = jnp.eye(nrep, dtype=F32)
    mix_sample = jnp.einsum("ab,gts->gatbs", eye_rep, blk).reshape(GM_GROUPS, CHUNK, CHUNK)
    bias_prompt = jnp.repeat(gm_bs[l].T, GM_HD, axis=1)
    bias_sample = jnp.tile(bias_prompt[:seq_s], (nrep, 1))

    wr = jnp.zeros((D_MODEL, LANES), F32)
    wr = wr.at[:, :N_EXPERTS].set(jnp.transpose(re_w[l], (1, 0, 2)).reshape(D_MODEL, N_EXPERTS))
    wr = wr.at[:, N_EXPERTS:N_EXPERTS + N_GROUPS].set(rg_w[l])
    wr_hi = wr.astype(BF16)
    wr_lo = (wr - wr_hi.astype(F32)).astype(BF16)
    b_r = jnp.zeros((1, LANES), F32)
    b_r = b_r.at[0, :N_EXPERTS].set(re_b[l].reshape(-1)).at[0, N_EXPERTS:N_EXPERTS + N_GROUPS].set(rg_b[l])

    w = dict(
        norm1_g=row(norm1_g[l]),
        w_in_segs=[w_in[l][:, a:b].astype(BF16) for a, b in zip(seg[:-1], seg[1:])],
        gm_ln_g=row(gm_ln_g[l]), gm_ln_b=row(gm_ln_b[l]),
        gm_mix_prompt=w_causal.astype(BF16), gm_bias_prompt=bias_prompt,
        gm_mix_sample=mix_sample.astype(BF16), gm_bias_sample=bias_sample,
        rw_prm=(row(rw_mu[l]), row(rw_w0[l]), rw_w2[l].astype(BF16), row(rw_a0[l]), rw_a2[l].astype(BF16),
                rw_g2[l].astype(BF16), row(rw_k_k[l]), row(rw_k_a[l]), row(rw_r_k[l]),
                row(rw_lnx_g[l]), row(rw_lnx_b[l])),
        gate_b=row(gate_b[l]), w_branch=w_branch[l].astype(BF16), w_out=w_out[l].astype(BF16),
        norm2_g=row(norm2_g[l]), wr_hi=wr_hi, wr_lo=wr_lo, b_r=b_r,
        e_wg=e_wg[l].reshape(N_EXPERTS, D_MODEL, EXP_FF).astype(BF16),
        e_wu=e_wu[l].reshape(N_EXPERTS, D_MODEL, EXP_FF).astype(BF16),
        e_wd=e_wd[l].reshape(N_EXPERTS, EXP_FF, D_MODEL).astype(BF16),
        final_g=row(final_g),
    )

    mk_p, mv_p = _mem_kv(mem_prompt.reshape(bp * N_MEM, D_MODEL), xa_wk[l].astype(BF16), xa_wv[l].astype(BF16))
    yp, sp, shp, _ = _layer(x_prompt.reshape(bp * seq_p, D_MODEL), mk_p, mv_p, None, w,
                            prompt=True, batch=bp, seq=seq_p)
    ys, ss, shs, vs = _layer(x_sample.reshape(bs * seq_s, D_MODEL),
                             cache_mem_k[l].reshape(bs, N_MEM, BRANCH_W), cache_mem_v[l].reshape(bs, N_MEM, BRANCH_W),
                             (state_rwkv_S[l], state_rwkv_shift[l]), w, prompt=False, batch=bs, seq=seq_s)

    return (yp.reshape(bp, seq_p, D_MODEL), ys.reshape(bs, seq_s, D_MODEL),
            sp[None], shp[None],
            mk_p.reshape(1, bp, N_MEM, XA_HEADS, XA_HD), mv_p.reshape(1, bp, N_MEM, XA_HEADS, XA_HD),
            ss[None], shs[None], vs.reshape(1, bs, seq_s, BRANCH_W))
```

```python
import functools

import numpy as np
import jax
import jax.numpy as jnp
from jax import lax
from jax.experimental import pallas as pl
from jax.experimental.pallas import tpu as pltpu

F32 = jnp.float32
BF16 = jnp.bfloat16

D_MODEL = 1024
BRANCH_W = 512
CHUNK = 128
GM_GROUPS = 8
GM_HD = BRANCH_W // GM_GROUPS
RW_HEADS = 8
RW_HD = BRANCH_W // RW_HEADS
RW_W_LORA = 64
RW_A_LORA = 64
RW_G_LORA = 128
RW_COLS = 3 * BRANCH_W + RW_W_LORA + RW_A_LORA + RW_G_LORA
XA_HEADS = 4
XA_HD = BRANCH_W // XA_HEADS
N_MEM = 256
N_BRANCH = 3
GATE_COLS = N_BRANCH * D_MODEL
N_GROUPS = 4
EXP_PER_GROUP = 8
N_EXPERTS = N_GROUPS * EXP_PER_GROUP
EXP_FF = 512
RMS_EPS = 1e-6
LN_EPS = 1e-5
GN_EPS = 64e-5

LANES = 128
RW_ROWS = 64
MIB = 1024 * 1024


def _dot(a, b):
    return jnp.dot(a.astype(BF16), b.astype(BF16), preferred_element_type=F32)


def _dot_nt(a, b):
    return lax.dot_general(a.astype(BF16), b.astype(BF16), (((1,), (1,)), ((), ())),
                           preferred_element_type=F32)


def _dot_tn(a, b):
    return lax.dot_general(a.astype(BF16), b.astype(BF16), (((0,), (0,)), ((), ())),
                           preferred_element_type=F32)


def _sigmoid(x):
    return 1.0 / (1.0 + jnp.exp(-x))


def _gelu(x):
    c = np.float32(np.sqrt(2.0 / np.pi))
    return x * (0.5 * (1.0 + jnp.tanh(c * (x + 0.044715 * (x * x * x)))))


def _softplus(x):
    return jnp.maximum(x, 0.0) + jnp.log(1.0 + jnp.exp(-jnp.abs(x)))


def _rmsnorm(x, g):
    return x * lax.rsqrt(jnp.mean(x * x, axis=-1, keepdims=True) + RMS_EPS) * g


def _split3(x):
    hi = x.astype(BF16)
    r1 = x - hi.astype(F32)
    mid = r1.astype(BF16)
    lo = (r1 - mid.astype(F32)).astype(BF16)
    return hi, mid, lo


def _params(sem, vmem_mib):
    return pltpu.CompilerParams(dimension_semantics=sem, vmem_limit_bytes=vmem_mib * MIB)


def _full(shape):
    nd = len(shape)
    return pl.BlockSpec(shape, lambda *_: (0,) * nd)


def _in_proj_kernel(x_ref, g_ref, wgm_ref, wrw_ref, wq_ref, wgt_ref, gm_ref, rw_ref, q_ref, gt_ref):
    xb = _rmsnorm(x_ref[...], g_ref[...]).astype(BF16)
    gm_ref[...] = jnp.dot(xb, wgm_ref[...], preferred_element_type=F32)
    rw_ref[...] = jnp.dot(xb, wrw_ref[...], preferred_element_type=F32)
    q_ref[...] = jnp.dot(xb, wq_ref[...], preferred_element_type=F32)
    gt_ref[...] = jnp.dot(xb, wgt_ref[...], preferred_element_type=F32)


def _in_proj(x, g, w_segs, tm):
    n = x.shape[0]
    widths = [w.shape[1] for w in w_segs]
    return pl.pallas_call(
        _in_proj_kernel,
        grid=(n // tm,),
        in_specs=[pl.BlockSpec((tm, D_MODEL), lambda i: (i, 0)), _full((1, D_MODEL))]
        + [pl.BlockSpec((D_MODEL, wd), lambda i: (0, 0), pipeline_mode=pl.Buffered(1)) for wd in widths],
        out_specs=[pl.BlockSpec((tm, wd), lambda i: (i, 0)) for wd in widths],
        out_shape=[jax.ShapeDtypeStruct((n, wd), F32) for wd in widths],
        compiler_params=_params(("arbitrary",), 48),
        name="in_proj",
    )(x, g, *w_segs)


def _mem_kv_kernel(m_ref, wk_ref, wv_ref, k_ref, v_ref):
    mb = m_ref[...].astype(BF16)
    k_ref[...] = jnp.dot(mb, wk_ref[...], preferred_element_type=F32)
    v_ref[...] = jnp.dot(mb, wv_ref[...], preferred_element_type=F32)


def _mem_kv(mem, wk, wv, tm=512):
    n = mem.shape[0]
    return pl.pallas_call(
        _mem_kv_kernel,
        grid=(n // tm,),
        in_specs=[pl.BlockSpec((tm, D_MODEL), lambda i: (i, 0)),
                  _full((D_MODEL, BRANCH_W)), _full((D_MODEL, BRANCH_W))],
        out_specs=[pl.BlockSpec((tm, BRANCH_W), lambda i: (i, 0))] * 2,
        out_shape=[jax.ShapeDtypeStruct((n, BRANCH_W), F32)] * 2,
        compiler_params=_params(("arbitrary",), 32),
        name="mem_kv",
    )(mem, wk, wv)


def _gmlp_kernel(pu_ref, pv_ref, lng_ref, lnb_ref, wmix_ref, bmix_ref, o_ref, *vn_refs):
    u = _gelu(pu_ref[...])
    vf = _gelu(pv_ref[...])
    mu = jnp.mean(vf, axis=-1, keepdims=True)
    vc = vf - mu
    var = jnp.mean(vc * vc, axis=-1, keepdims=True)
    vn = vc * lax.rsqrt(var + LN_EPS) * lng_ref[...] + lnb_ref[...]
    if vn_refs:
        vn_refs[0][...] = vn
    lane = lax.broadcasted_iota(jnp.int32, (CHUNK, LANES), 1)
    lo_half = lane < GM_HD
    for p in range(GM_GROUPS // 2):
        vp = vn[:, p * LANES:(p + 1) * LANES]
        s = (_dot(wmix_ref[2 * p], jnp.where(lo_half, vp, 0.0))
             + _dot(wmix_ref[2 * p + 1], jnp.where(lo_half, 0.0, vp)))
        s = s + bmix_ref[:, p * LANES:(p + 1) * LANES]
        o_ref[:, p * LANES:(p + 1) * LANES] = (u[:, p * LANES:(p + 1) * LANES] * s).astype(o_ref.dtype)


def _gmlp(p_gm, ln_g, ln_b, wmix, bmix, emit_vn):
    n = p_gm.shape[0]
    out_shape = [jax.ShapeDtypeStruct((n, BRANCH_W), BF16)]
    out_specs = [pl.BlockSpec((CHUNK, BRANCH_W), lambda i: (i, 0))]
    if emit_vn:
        out_shape.append(jax.ShapeDtypeStruct((n, BRANCH_W), F32))
        out_specs.append(pl.BlockSpec((CHUNK, BRANCH_W), lambda i: (i, 0)))
    return pl.pallas_call(
        _gmlp_kernel,
        grid=(n // CHUNK,),
        in_specs=[pl.BlockSpec((CHUNK, BRANCH_W), lambda i: (i, 0)),
                  pl.BlockSpec((CHUNK, BRANCH_W), lambda i: (i, 1)),
                  _full((1, BRANCH_W)), _full((1, BRANCH_W)),
                  _full((GM_GROUPS, CHUNK, CHUNK)), _full((CHUNK, BRANCH_W))],
        out_specs=out_specs,
        out_shape=out_shape,
        compiler_params=_params(("arbitrary",), 32),
        name="gmlp",
    )(p_gm, p_gm, ln_g, ln_b, wmix, bmix)


def _rwkv_rows(p, first, prev, s_ref, o_ref, prm, nseq):
    (mu, w0, w2, a0, a2, g2, k_k, k_a, r_k, lnx_g, lnx_b) = prm
    rows = RW_ROWS
    ls = rows // nseq
    shifted = jnp.where(first, prev, pltpu.roll(p, 1, 0))
    xs = p + (shifted - p) * mu
    r = xs[:, 0:BRANCH_W]
    k = xs[:, BRANCH_W:2 * BRANCH_W]
    v = xs[:, 2 * BRANCH_W:3 * BRANCH_W]
    o = 3 * BRANCH_W
    wd = xs[:, o:o + RW_W_LORA]
    ad = xs[:, o + RW_W_LORA:o + RW_W_LORA + RW_A_LORA]
    gd = xs[:, o + RW_W_LORA + RW_A_LORA:RW_COLS]

    w_log = -_softplus(-(w0 + _dot(jnp.tanh(wd), w2))) - 0.5
    logw = -jnp.exp(w_log)
    a = _sigmoid(a0 + _dot(ad, a2))
    g = _dot(_sigmoid(gd), g2)
    kkr = k * k_k
    kf = k * (1.0 + (a - 1.0) * k_a)
    rkr = r * kf * r_k

    ri = lax.broadcasted_iota(jnp.int32, (rows, rows), 0)
    ci = lax.broadcasted_iota(jnp.int32, (rows, rows), 1)
    if nseq == 1:
        same = ci >= 0
    else:
        sh = int(np.log2(ls))
        same = lax.shift_right_logical(ri, sh) == lax.shift_right_logical(ci, sh)
    low_incl = same & (ci <= ri)
    low_strict = same & (ci < ri)
    m_incl = jnp.where(low_incl, 1.0, 0.0).astype(BF16)
    m_same = jnp.where(same, 1.0, 0.0).astype(BF16)
    hi, mid, lo = _split3(logw)
    cum = (jnp.dot(m_incl, hi, preferred_element_type=F32) + jnp.dot(m_incl, mid, preferred_element_type=F32)
           + jnp.dot(m_incl, lo, preferred_element_type=F32))
    tot = (jnp.dot(m_same, hi, preferred_element_type=F32) + jnp.dot(m_same, mid, preferred_element_type=F32)
           + jnp.dot(m_same, lo, preferred_element_type=F32))
    g_t = jnp.exp(cum)
    g_prev = jnp.exp(cum - logw)
    g_inv = jnp.exp(-cum)
    g_end = jnp.exp(tot - cum)
    g_tot = jnp.exp(tot)
    eye = jnp.where(ri == ci, 1.0, 0.0)

    hs = range(RW_HEADS)
    sls = [slice(h * RW_HD, (h + 1) * RW_HD) for h in hs]
    kk = [kkr[:, sl] for sl in sls]
    kk = [x * lax.rsqrt(jnp.maximum(jnp.sum(x * x, axis=-1, keepdims=True), 1e-24)) for x in kk]
    bv = [kk[h] * a[:, sls[h]] for h in hs]
    k_h = [kf[:, sl] for sl in sls]
    v_h = [v[:, sl] for sl in sls]
    at = [-kk[h] * g_prev[:, sls[h]] for h in hs]
    rt = [r[:, sl] * g_t[:, sl] for sl in sls]
    bt = [bv[h] * g_inv[:, sls[h]] for h in hs]
    kt = [k_h[h] * g_inv[:, sls[h]] for h in hs]
    bh = [bv[h] * g_end[:, sls[h]] for h in hs]
    kh = [k_h[h] * g_end[:, sls[h]] for h in hs]

    ar = [jnp.concatenate([at[h], rt[h]], axis=0) for h in hs]
    pb = [_dot_nt(ar[h], bt[h]) for h in hs]
    pk = [_dot_nt(ar[h], kt[h]) for h in hs]
    l_ab = [jnp.where(low_strict, x[:rows], 0.0) for x in pb]
    l_ak = [jnp.where(low_strict, x[:rows], 0.0) for x in pk]
    a_rb = [jnp.where(low_incl, x[rows:], 0.0) for x in pb]
    a_rk = [jnp.where(low_incl, x[rows:], 0.0) for x in pk]

    tm = [eye + x for x in l_ab]
    pw = [_dot(x, x) for x in l_ab]
    n_dbl = int(np.log2(ls)) - 1
    for it in range(n_dbl):
        if it < n_dbl - 1:
            z = [_dot(jnp.concatenate([tm[h], pw[h]], axis=0), pw[h]) for h in hs]
            tm = [tm[h] + z[h][:rows] for h in hs]
            pw = [z[h][rows:] for h in hs]
        else:
            tm = [tm[h] + _dot(tm[h], pw[h]) for h in hs]

    if nseq == 1:
        ars = [_dot_nt(ar[h], s_ref[0, h]) for h in hs]
        as0 = [x[:rows] for x in ars]
        rs0 = [x[rows:] for x in ars]
    else:
        as0, rs0 = [], []
        for h in hs:
            zs = [_dot_nt(jnp.concatenate([at[h][b * ls:(b + 1) * ls], rt[h][b * ls:(b + 1) * ls]], axis=0),
                          s_ref[b, h]) for b in range(nseq)]
            as0.append(jnp.concatenate([x[:ls] for x in zs], axis=0))
            rs0.append(jnp.concatenate([x[ls:] for x in zs], axis=0))

    lv = [_dot(l_ak[h], v_h[h]) for h in hs]
    u = [_dot(tm[h], as0[h] + lv[h]) for h in hs]
    y = [rs0[h] + _dot(a_rb[h], u[h]) + _dot(a_rk[h], v_h[h]) for h in hs]

    for h in hs:
        for b in range(nseq):
            rb = slice(b * ls, (b + 1) * ls)
            uv = jnp.concatenate([u[h][rb], v_h[h][rb]], axis=0)
            bk = jnp.concatenate([bh[h][rb], kh[h][rb]], axis=0)
            s_ref[b, h] = s_ref[b, h] * g_tot[b * ls:b * ls + 1, sls[h]] + _dot_tn(uv, bk)

    outs = []
    for h in hs:
        ym = jnp.mean(y[h], axis=-1, keepdims=True)
        yc = y[h] - ym
        yv = jnp.mean(yc * yc, axis=-1, keepdims=True)
        yn = yc * lax.rsqrt(yv + GN_EPS) * lnx_g[:, sls[h]] + lnx_b[:, sls[h]]
        bonus = jnp.sum(rkr[:, sls[h]], axis=-1, keepdims=True) * v_h[h]
        outs.append((yn + bonus) * g[:, sls[h]])
    o_ref[...] = jnp.concatenate(outs, axis=-1).astype(o_ref.dtype)


def _rwkv_prompt_kernel(p_ref, *refs):
    prm_refs, (o_ref, s_ref, sh_ref, carry_ref) = refs[:11], refs[11:]
    c = pl.program_id(1)

    @pl.when(c == 0)
    def _():
        s_ref[...] = jnp.zeros_like(s_ref)
        carry_ref[...] = jnp.zeros_like(carry_ref)

    p = p_ref[...]
    first = lax.broadcasted_iota(jnp.int32, (RW_ROWS, 1), 0) == 0
    prm = tuple(x[...] for x in prm_refs)
    _rwkv_rows(p, first, carry_ref[0:1, :], s_ref, o_ref, prm, nseq=1)
    last = p[RW_ROWS - 1:RW_ROWS, :]
    carry_ref[0:1, :] = last
    sh_ref[...] = last


def _rwkv_sample_kernel(p_ref, prev_ref, s0_ref, *refs):
    prm_refs, (o_ref, s_ref, sh_ref) = refs[:11], refs[11:]
    nseq = s0_ref.shape[0]
    ls = RW_ROWS // nseq
    s_ref[...] = s0_ref[...]
    p = p_ref[...]
    first = (lax.broadcasted_iota(jnp.int32, (RW_ROWS, 1), 0) & (ls - 1)) == 0
    prm = tuple(x[...] for x in prm_refs)
    _rwkv_rows(p, first, prev_ref[...], s_ref, o_ref, prm, nseq=nseq)
    for b in range(nseq):
        sh_ref[b:b + 1, :] = p_ref[(b + 1) * ls - 1:(b + 1) * ls, :]


def _rw_param_specs():
    shapes = [(1, RW_COLS), (1, BRANCH_W), (RW_W_LORA, BRANCH_W), (1, BRANCH_W), (RW_A_LORA, BRANCH_W),
              (RW_G_LORA, BRANCH_W), (1, BRANCH_W), (1, BRANCH_W), (1, BRANCH_W), (1, BRANCH_W), (1, BRANCH_W)]
    return [_full(s) for s in shapes]


def _rwkv_prompt(p_rw, prm, batch, seq):
    nc = seq // RW_ROWS
    p3 = p_rw.reshape(batch, seq, RW_COLS)
    o, s_new, sh = pl.pallas_call(
        _rwkv_prompt_kernel,
        grid=(batch, nc),
        in_specs=[pl.BlockSpec((None, RW_ROWS, RW_COLS), lambda b, c: (b, c, 0))] + _rw_param_specs(),
        out_specs=[pl.BlockSpec((None, RW_ROWS, BRANCH_W), lambda b, c: (b, c, 0)),
                   pl.BlockSpec((1, RW_HEADS, RW_HD, RW_HD), lambda b, c: (b, 0, 0, 0)),
                   pl.BlockSpec((None, 1, RW_COLS), lambda b, c: (b, 0, 0))],
        out_shape=[jax.ShapeDtypeStruct((batch, seq, BRANCH_W), BF16),
                   jax.ShapeDtypeStruct((batch, RW_HEADS, RW_HD, RW_HD), F32),
                   jax.ShapeDtypeStruct((batch, 1, RW_COLS), F32)],
        scratch_shapes=[pltpu.VMEM((8, RW_COLS), F32)],
        compiler_params=_params(("arbitrary", "arbitrary"), 32),
        name="rwkv_prompt",
    )(p3, *prm)
    return o.reshape(batch * seq, BRANCH_W), s_new, sh.reshape(batch, RW_COLS)


def _rwkv_sample(p_rw, prev_rows, s0, prm, seq):
    n = p_rw.shape[0]
    nseq = RW_ROWS // seq
    batch = n // seq
    return pl.pallas_call(
        _rwkv_sample_kernel,
        grid=(n // RW_ROWS,),
        in_specs=[pl.BlockSpec((RW_ROWS, RW_COLS), lambda i: (i, 0)),
                  pl.BlockSpec((RW_ROWS, RW_COLS), lambda i: (i, 0)),
                  pl.BlockSpec((nseq, RW_HEADS, RW_HD, RW_HD), lambda i: (i, 0, 0, 0))] + _rw_param_specs(),
        out_specs=[pl.BlockSpec((RW_ROWS, BRANCH_W), lambda i: (i, 0)),
                   pl.BlockSpec((nseq, RW_HEADS, RW_HD, RW_HD), lambda i: (i, 0, 0, 0)),
                   pl.BlockSpec((nseq, RW_COLS), lambda i: (i, 0))],
        out_shape=[jax.ShapeDtypeStruct((n, BRANCH_W), BF16),
                   jax.ShapeDtypeStruct((batch, RW_HEADS, RW_HD, RW_HD), F32),
                   jax.ShapeDtypeStruct((batch, RW_COLS), F32)],
        compiler_params=_params(("arbitrary",), 32),
        name="rwkv_sample",
    )(p_rw, prev_rows, s0, *prm)


def _attend(q, k, v):
    s = _dot_nt(q, k) * (XA_HD ** -0.5)
    e = jnp.exp(s - jnp.max(s, axis=-1, keepdims=True))
    pr = e / jnp.sum(e, axis=-1, keepdims=True)
    return _dot(pr, v)


def _xattn_prompt_kernel(q_ref, k_ref, v_ref, o_ref):
    for h in range(XA_HEADS):
        sl = slice(h * XA_HD, (h + 1) * XA_HD)
        o_ref[:, sl] = _attend(q_ref[:, sl], k_ref[:, sl], v_ref[:, sl]).astype(o_ref.dtype)


def _xattn_prompt(q, mk, mv, batch, seq, tq=512):
    nt = seq // tq
    return pl.pallas_call(
        _xattn_prompt_kernel,
        grid=(batch, nt),
        in_specs=[pl.BlockSpec((tq, BRANCH_W), lambda b, i: (b * nt + i, 0)),
                  pl.BlockSpec((N_MEM, BRANCH_W), lambda b, i: (b, 0)),
                  pl.BlockSpec((N_MEM, BRANCH_W), lambda b, i: (b, 0))],
        out_specs=pl.BlockSpec((tq, BRANCH_W), lambda b, i: (b * nt + i, 0)),
        out_shape=jax.ShapeDtypeStruct((batch * seq, BRANCH_W), BF16),
        compiler_params=_params(("arbitrary", "arbitrary"), 32),
        name="xattn_prompt",
    )(q, mk, mv)


def _xattn_sample_kernel(q_ref, k_ref, v_ref, o_ref):
    nb = k_ref.shape[0]
    ls = q_ref.shape[0] // nb
    for b in range(nb):
        for h in range(XA_HEADS):
            sl = slice(h * XA_HD, (h + 1) * XA_HD)
            o_ref[b * ls:(b + 1) * ls, sl] = _attend(
                q_ref[b * ls:(b + 1) * ls, sl], k_ref[b, :, sl], v_ref[b, :, sl]).astype(o_ref.dtype)


def _xattn_sample(q, mk, mv, seq, nb=8):
    batch = mk.shape[0]
    mem_spec = pl.BlockSpec((nb, N_MEM, BRANCH_W), lambda i: (i, 0, 0))
    return pl.pallas_call(
        _xattn_sample_kernel,
        grid=(batch // nb,),
        in_specs=[pl.BlockSpec((nb * seq, BRANCH_W), lambda i: (i, 0)), mem_spec, mem_spec],
        out_specs=pl.BlockSpec((nb * seq, BRANCH_W), lambda i: (i, 0)),
        out_shape=jax.ShapeDtypeStruct((batch * seq, BRANCH_W), BF16),
        compiler_params=_params(("arbitrary",), 40),
        name="xattn_sample",
    )(q, mk, mv)


def _merge_kernel(x_ref, gm_ref, rw_ref, xa_ref, gt_ref, gb_ref, wb_ref, wo_ref, n2_ref,
                  wr_hi_ref, wr_lo_ref, br_ref, h_ref, hn_ref, comb_ref):
    merged = None
    for n, br in enumerate((gm_ref, rw_ref, xa_ref)):
        cs = slice(n * D_MODEL, (n + 1) * D_MODEL)
        up = jnp.dot(br[...], wb_ref[n], preferred_element_type=F32)
        term = _sigmoid(gt_ref[:, cs] + gb_ref[:, cs]) * up
        merged = term if merged is None else merged + term
    h = x_ref[...] + _dot(merged, wo_ref[...])
    h_ref[...] = h
    hn = _rmsnorm(h, n2_ref[...])
    hn_ref[...] = hn.astype(hn_ref.dtype)

    hn_hi = hn.astype(BF16)
    hn_lo = (hn - hn_hi.astype(F32)).astype(BF16)
    logits = (jnp.dot(hn_hi, wr_hi_ref[...], preferred_element_type=F32)
              + jnp.dot(hn_lo, wr_hi_ref[...], preferred_element_type=F32)
              + jnp.dot(hn_hi, wr_lo_ref[...], preferred_element_type=F32)) + br_ref[...]
    lane = lax.broadcasted_iota(jnp.int32, logits.shape, 1)
    neg = -jnp.inf
    big = jnp.int32(1 << 20)
    gmask = (lane >= N_EXPERTS) & (lane < N_EXPERTS + N_GROUPS)
    gl = jnp.where(gmask, logits, neg)
    gmax = jnp.max(gl, axis=-1, keepdims=True)
    gsel = jnp.min(jnp.where(gl == gmax, lane, big), axis=-1, keepdims=True) - N_EXPERTS
    gsum = jnp.sum(jnp.where(gmask, jnp.exp(logits - gmax), 0.0), axis=-1, keepdims=True)
    pg_top = 1.0 / gsum
    emask = (lane >= gsel * EXP_PER_GROUP) & (lane < (gsel + 1) * EXP_PER_GROUP)
    el = jnp.where(emask, logits, neg)
    m1 = jnp.max(el, axis=-1, keepdims=True)
    i1 = jnp.min(jnp.where(el == m1, lane, big), axis=-1, keepdims=True)
    el2 = jnp.where(lane == i1, neg, el)
    m2 = jnp.max(el2, axis=-1, keepdims=True)
    i2 = jnp.min(jnp.where(el2 == m2, lane, big), axis=-1, keepdims=True)
    t2 = jnp.exp(m2 - m1)
    w1 = pg_top / (1.0 + t2)
    w2 = pg_top * t2 / (1.0 + t2)
    comb_ref[...] = jnp.where(lane == i1, w1, 0.0) + jnp.where(lane == i2, w2, 0.0)


def _merge(x, o_gm, o_rw, o_xa, p_gate, gate_b, w_branch, w_out, n2_g, wr_hi, wr_lo, b_r, tm=256):
    n = x.shape[0]
    row = lambda wd: pl.BlockSpec((tm, wd), lambda i: (i, 0))
    return pl.pallas_call(
        _merge_kernel,
        grid=(n // tm,),
        in_specs=[row(D_MODEL), row(BRANCH_W), row(BRANCH_W), row(BRANCH_W), row(GATE_COLS),
                  _full((1, GATE_COLS)), _full((N_BRANCH, BRANCH_W, D_MODEL)), _full((D_MODEL, D_MODEL)),
                  _full((1, D_MODEL)), _full((D_MODEL, LANES)), _full((D_MODEL, LANES)), _full((1, LANES))],
        out_specs=[row(D_MODEL), row(D_MODEL), row(LANES)],
        out_shape=[jax.ShapeDtypeStruct((n, D_MODEL), F32), jax.ShapeDtypeStruct((n, D_MODEL), BF16),
                   jax.ShapeDtypeStruct((n, LANES), F32)],
        compiler_params=_params(("arbitrary",), 40),
        name="merge",
    )(x, o_gm, o_rw, o_xa, p_gate, gate_b, w_branch, w_out, n2_g, wr_hi, wr_lo, b_r)


def _moe_kernel(h_ref, hn_ref, comb_ref, wg_ref, wu_ref, wd_ref, fg_ref, y_ref, acc_ref):
    e = pl.program_id(1)

    @pl.when(e == 0)
    def _():
        acc_ref[...] = jnp.zeros_like(acc_ref)

    xb = hn_ref[...]
    gate = jnp.dot(xb, wg_ref[...], preferred_element_type=F32)
    up = jnp.dot(xb, wu_ref[...], preferred_element_type=F32)
    lane = lax.broadcasted_iota(jnp.int32, comb_ref.shape, 1)
    c_e = jnp.sum(jnp.where(lane == e, comb_ref[...], 0.0), axis=-1, keepdims=True)
    act = gate * _sigmoid(gate) * up * c_e
    acc_ref[...] += _dot(act, wd_ref[...])

    @pl.when(e == pl.num_programs(1) - 1)
    def _():
        y_ref[...] = _rmsnorm(h_ref[...] + acc_ref[...], fg_ref[...])


def _moe(h, hn, comb, wg, wu, wd, final_g, tm=512):
    n = h.shape[0]
    return pl.pallas_call(
        _moe_kernel,
        grid=(n // tm, N_EXPERTS),
        in_specs=[pl.BlockSpec((tm, D_MODEL), lambda i, e: (i, 0)),
                  pl.BlockSpec((tm, D_MODEL), lambda i, e: (i, 0)),
                  pl.BlockSpec((tm, LANES), lambda i, e: (i, 0)),
                  pl.BlockSpec((None, D_MODEL, EXP_FF), lambda i, e: (e, 0, 0)),
                  pl.BlockSpec((None, D_MODEL, EXP_FF), lambda i, e: (e, 0, 0)),
                  pl.BlockSpec((None, EXP_FF, D_MODEL), lambda i, e: (e, 0, 0)),
                  _full((1, D_MODEL))],
        out_specs=pl.BlockSpec((tm, D_MODEL), lambda i, e: (i, 0)),
        out_shape=jax.ShapeDtypeStruct((n, D_MODEL), F32),
        scratch_shapes=[pltpu.VMEM((tm, D_MODEL), F32)],
        compiler_params=_params(("arbitrary", "arbitrary"), 40),
        name="moe",
    )(h, hn, comb, wg, wu, wd, final_g)


def _layer(x2d, mem_k, mem_v, rw_state, w, *, prompt, batch, seq):
    n = x2d.shape[0]
    tm_in = 256
    p_gm, p_rw, p_q, p_gate = _in_proj(x2d, w["norm1_g"], w["w_in_segs"], tm_in)
    if prompt:
        o_gm = _gmlp(p_gm, w["gm_ln_g"], w["gm_ln_b"], w["gm_mix_prompt"], w["gm_bias_prompt"], False)[0]
        vn = None
        o_rw, s_new, sh_new = _rwkv_prompt(p_rw, w["rw_prm"], batch, seq)
        o_xa = _xattn_prompt(p_q, mem_k, mem_v, batch, seq)
    else:
        o_gm, vn = _gmlp(p_gm, w["gm_ln_g"], w["gm_ln_b"], w["gm_mix_sample"], w["gm_bias_sample"], True)
        s0, shift = rw_state
        prev_rows = jnp.pad(shift[:, None, :], ((0, 0), (0, seq - 1), (0, 0))).reshape(n, RW_COLS)
        o_rw, s_new, sh_new = _rwkv_sample(p_rw, prev_rows, s0, w["rw_prm"], seq)
        o_xa = _xattn_sample(p_q, mem_k, mem_v, seq)
    h, hn, comb = _merge(x2d, o_gm, o_rw, o_xa, p_gate, w["gate_b"], w["w_branch"], w["w_out"],
                         w["norm2_g"], w["wr_hi"], w["wr_lo"], w["b_r"])
    y = _moe(h, hn, comb, w["e_wg"], w["e_wu"], w["e_wd"], w["final_g"])
    return y, s_new, sh_new, vn


def kernel(x_prompt, x_sample, state_rwkv_S, state_rwkv_shift, cache_mem_k, cache_mem_v, mem_prompt, norm1_g, w_in, gate_b, gm_ln_g, gm_ln_b, gm_ws, gm_bs, rw_mu, rw_w0, rw_w2, rw_a0, rw_a2, rw_g2, rw_k_k, rw_k_a, rw_r_k, rw_lnx_g, rw_lnx_b, xa_wk, xa_wv, w_branch, w_out, norm2_g, rg_w, rg_b, re_w, re_b, e_wg, e_wu, e_wd, final_g):
    bp, seq_p, _ = x_prompt.shape
    bs, seq_s, _ = x_sample.shape
    depth = w_in.shape[0]
    assert depth == 1 and seq_p % CHUNK == 0 and RW_ROWS % seq_s == 0 and CHUNK % seq_s == 0

    l = 0
    row = lambda a: a.reshape(1, -1)
    seg = (0, 2 * BRANCH_W, 2 * BRANCH_W + RW_COLS, 3 * BRANCH_W + RW_COLS, 3 * BRANCH_W + RW_COLS + GATE_COLS)
    w_causal = jnp.tril(gm_ws[l])
    nrep = CHUNK // seq_s
    blk = w_causal[:, :seq_s, :seq_s]
    eye_rep = jnp.eye(nrep, dtype=F32)
    mix_sample = jnp.einsum("ab,gts->gatbs", eye_rep, blk).reshape(GM_GROUPS, CHUNK, CHUNK)
    bias_prompt = jnp.repeat(gm_bs[l].T, GM_HD, axis=1)
    bias_sample = jnp.tile(bias_prompt[:seq_s], (nrep, 1))

    wr = jnp.zeros((D_MODEL, LANES), F32)
    wr = wr.at[:, :N_EXPERTS].set(jnp.transpose(re_w[l], (1, 0, 2)).reshape(D_MODEL, N_EXPERTS))
    wr = wr.at[:, N_EXPERTS:N_EXPERTS + N_GROUPS].set(rg_w[l])
    wr_hi = wr.astype(BF16)
    wr_lo = (wr - wr_hi.astype(F32)).astype(BF16)
    b_r = jnp.zeros((1, LANES), F32)
    b_r = b_r.at[0, :N_EXPERTS].set(re_b[l].reshape(-1)).at[0, N_EXPERTS:N_EXPERTS + N_GROUPS].set(rg_b[l])

    w = dict(
        norm1_g=row(norm1_g[l]),
        w_in_segs=[w_in[l][:, a:b].astype(BF16) for a, b in zip(seg[:-1], seg[1:])],
        gm_ln_g=row(gm_ln_g[l]), gm_ln_b=row(gm_ln_b[l]),
        gm_mix_prompt=w_causal.astype(BF16), gm_bias_prompt=bias_prompt,
        gm_mix_sample=mix_sample.astype(BF16), gm_bias_sample=bias_sample,
        rw_prm=(row(rw_mu[l]), row(rw_w0[l]), rw_w2[l].astype(BF16), row(rw_a0[l]), rw_a2[l].astype(BF16),
                rw_g2[l].astype(BF16), row(rw_k_k[l]), row(rw_k_a[l]), row(rw_r_k[l]),
                row(rw_lnx_g[l]), row(rw_lnx_b[l])),
        gate_b=row(gate_b[l]), w_branch=w_branch[l].astype(BF16), w_out=w_out[l].astype(BF16),
        norm2_g=row(norm2_g[l]), wr_hi=wr_hi, wr_lo=wr_lo, b_r=b_r,
        e_wg=e_wg[l].reshape(N_EXPERTS, D_MODEL, EXP_FF).astype(BF16),
        e_wu=e_wu[l].reshape(N_EXPERTS, D_MODEL, EXP_FF).astype(BF16),
        e_wd=e_wd[l].reshape(N_EXPERTS, EXP_FF, D_MODEL).astype(BF16),
        final_g=row(final_g),
    )

    mk_p, mv_p = _mem_kv(mem_prompt.reshape(bp * N_MEM, D_MODEL), xa_wk[l].astype(BF16), xa_wv[l].astype(BF16))
    yp, sp, shp, _ = _layer(x_prompt.reshape(bp * seq_p, D_MODEL), mk_p, mv_p, None, w,
                            prompt=True, batch=bp, seq=seq_p)
    ys, ss, shs, vs = _layer(x_sample.reshape(bs * seq_s, D_MODEL),
                             cache_mem_k[l].reshape(bs, N_MEM, BRANCH_W).astype(BF16),
                             cache_mem_v[l].reshape(bs, N_MEM, BRANCH_W).astype(BF16),
                             (state_rwkv_S[l], state_rwkv_shift[l]), w, prompt=False, batch=bs, seq=seq_s)

    return (yp.reshape(bp, seq_p, D_MODEL), ys.reshape(bs, seq_s, D_MODEL),
            sp[None], shp[None],
            mk_p.reshape(1, bp, N_MEM, XA_HEADS, XA_HD), mv_p.reshape(1, bp, N_MEM, XA_HEADS, XA_HD),
            ss[None], shs[None], vs.reshape(1, bs, seq_s, BRANCH_W))
```

```python
import functools

import numpy as np
import jax
import jax.numpy as jnp
from jax import lax
from jax.experimental import pallas as pl
from jax.experimental.pallas import tpu as pltpu

F32 = jnp.float32
BF16 = jnp.bfloat16

D_MODEL = 1024
BRANCH_W = 512
CHUNK = 128
GM_GROUPS = 8
GM_HD = BRANCH_W // GM_GROUPS
RW_HEADS = 8
RW_HD = BRANCH_W // RW_HEADS
RW_W_LORA = 64
RW_A_LORA = 64
RW_G_LORA = 128
RW_COLS = 3 * BRANCH_W + RW_W_LORA + RW_A_LORA + RW_G_LORA
XA_HEADS = 4
XA_HD = BRANCH_W // XA_HEADS
N_MEM = 256
N_BRANCH = 3
GATE_COLS = N_BRANCH * D_MODEL
N_GROUPS = 4
EXP_PER_GROUP = 8
N_EXPERTS = N_GROUPS * EXP_PER_GROUP
EXP_FF = 512
RMS_EPS = 1e-6
LN_EPS = 1e-5
GN_EPS = 64e-5

LANES = 128
RW_ROWS = 64
MIB = 1024 * 1024
ROUTE_E1, ROUTE_E2, ROUTE_W1, ROUTE_W2, ROUTE_R1, ROUTE_R2 = range(6)


def _dot(a, b):
    return jnp.dot(a.astype(BF16), b.astype(BF16), preferred_element_type=F32)


def _dot_nt(a, b):
    return lax.dot_general(a.astype(BF16), b.astype(BF16), (((1,), (1,)), ((), ())),
                           preferred_element_type=F32)


def _dot_tn(a, b):
    return lax.dot_general(a.astype(BF16), b.astype(BF16), (((0,), (0,)), ((), ())),
                           preferred_element_type=F32)


def _sigmoid(x):
    return 1.0 / (1.0 + jnp.exp(-x))


def _gelu(x):
    c = np.float32(np.sqrt(2.0 / np.pi))
    return x * (0.5 * (1.0 + jnp.tanh(c * (x + 0.044715 * (x * x * x)))))


def _softplus(x):
    return jnp.maximum(x, 0.0) + jnp.log(1.0 + jnp.exp(-jnp.abs(x)))


def _rmsnorm(x, g):
    return x * lax.rsqrt(jnp.mean(x * x, axis=-1, keepdims=True) + RMS_EPS) * g


def _split3(x):
    hi = x.astype(BF16)
    r1 = x - hi.astype(F32)
    mid = r1.astype(BF16)
    lo = (r1 - mid.astype(F32)).astype(BF16)
    return hi, mid, lo


def _params(sem, vmem_mib):
    return pltpu.CompilerParams(dimension_semantics=sem, vmem_limit_bytes=vmem_mib * MIB)


def _full(shape):
    nd = len(shape)
    return pl.BlockSpec(shape, lambda *_: (0,) * nd)


def _in_proj_kernel(x_ref, g_ref, wgm_ref, wrw_ref, wq_ref, wgt_ref, gm_ref, rw_ref, q_ref, gt_ref):
    xb = _rmsnorm(x_ref[...], g_ref[...]).astype(BF16)
    gm_ref[...] = jnp.dot(xb, wgm_ref[...], preferred_element_type=F32)
    rw_ref[...] = jnp.dot(xb, wrw_ref[...], preferred_element_type=F32)
    q_ref[...] = jnp.dot(xb, wq_ref[...], preferred_element_type=F32)
    gt_ref[...] = jnp.dot(xb, wgt_ref[...], preferred_element_type=F32)


def _in_proj(x, g, w_segs, tm):
    n = x.shape[0]
    widths = [w.shape[1] for w in w_segs]
    return pl.pallas_call(
        _in_proj_kernel,
        grid=(n // tm,),
        in_specs=[pl.BlockSpec((tm, D_MODEL), lambda i: (i, 0)), _full((1, D_MODEL))]
        + [pl.BlockSpec((D_MODEL, wd), lambda i: (0, 0), pipeline_mode=pl.Buffered(1)) for wd in widths],
        out_specs=[pl.BlockSpec((tm, wd), lambda i: (i, 0)) for wd in widths],
        out_shape=[jax.ShapeDtypeStruct((n, wd), F32) for wd in widths],
        compiler_params=_params(("arbitrary",), 48),
        name="in_proj",
    )(x, g, *w_segs)


def _mem_kv_kernel(m_ref, wk_ref, wv_ref, k_ref, v_ref):
    mb = m_ref[...].astype(BF16)
    k_ref[...] = jnp.dot(mb, wk_ref[...], preferred_element_type=F32)
    v_ref[...] = jnp.dot(mb, wv_ref[...], preferred_element_type=F32)


def _mem_kv(mem, wk, wv, tm=512):
    n = mem.shape[0]
    return pl.pallas_call(
        _mem_kv_kernel,
        grid=(n // tm,),
        in_specs=[pl.BlockSpec((tm, D_MODEL), lambda i: (i, 0)),
                  _full((D_MODEL, BRANCH_W)), _full((D_MODEL, BRANCH_W))],
        out_specs=[pl.BlockSpec((tm, BRANCH_W), lambda i: (i, 0))] * 2,
        out_shape=[jax.ShapeDtypeStruct((n, BRANCH_W), F32)] * 2,
        compiler_params=_params(("arbitrary",), 32),
        name="mem_kv",
    )(mem, wk, wv)


def _gmlp_kernel(pu_ref, pv_ref, lng_ref, lnb_ref, wmix_ref, bmix_ref, o_ref, *vn_refs):
    u = _gelu(pu_ref[...])
    vf = _gelu(pv_ref[...])
    mu = jnp.mean(vf, axis=-1, keepdims=True)
    vc = vf - mu
    var = jnp.mean(vc * vc, axis=-1, keepdims=True)
    vn = vc * lax.rsqrt(var + LN_EPS) * lng_ref[...] + lnb_ref[...]
    if vn_refs:
        vn_refs[0][...] = vn
    lane = lax.broadcasted_iota(jnp.int32, (CHUNK, LANES), 1)
    lo_half = lane < GM_HD
    for p in range(GM_GROUPS // 2):
        vp = vn[:, p * LANES:(p + 1) * LANES]
        s = (_dot(wmix_ref[2 * p], jnp.where(lo_half, vp, 0.0))
             + _dot(wmix_ref[2 * p + 1], jnp.where(lo_half, 0.0, vp)))
        s = s + bmix_ref[:, p * LANES:(p + 1) * LANES]
        o_ref[:, p * LANES:(p + 1) * LANES] = (u[:, p * LANES:(p + 1) * LANES] * s).astype(o_ref.dtype)


def _gmlp(p_gm, ln_g, ln_b, wmix, bmix, emit_vn):
    n = p_gm.shape[0]
    out_shape = [jax.ShapeDtypeStruct((n, BRANCH_W), BF16)]
    out_specs = [pl.BlockSpec((CHUNK, BRANCH_W), lambda i: (i, 0))]
    if emit_vn:
        out_shape.append(jax.ShapeDtypeStruct((n, BRANCH_W), F32))
        out_specs.append(pl.BlockSpec((CHUNK, BRANCH_W), lambda i: (i, 0)))
    return pl.pallas_call(
        _gmlp_kernel,
        grid=(n // CHUNK,),
        in_specs=[pl.BlockSpec((CHUNK, BRANCH_W), lambda i: (i, 0)),
                  pl.BlockSpec((CHUNK, BRANCH_W), lambda i: (i, 1)),
                  _full((1, BRANCH_W)), _full((1, BRANCH_W)),
                  _full((GM_GROUPS, CHUNK, CHUNK)), _full((CHUNK, BRANCH_W))],
        out_specs=out_specs,
        out_shape=out_shape,
        compiler_params=_params(("arbitrary",), 32),
        name="gmlp",
    )(p_gm, p_gm, ln_g, ln_b, wmix, bmix)


def _rwkv_rows(p, first, prev, s_ref, o_ref, prm, nseq):
    (mu, w0, w2, a0, a2, g2, k_k, k_a, r_k, lnx_g, lnx_b) = prm
    rows = RW_ROWS
    ls = rows // nseq
    shifted = jnp.where(first, prev, pltpu.roll(p, 1, 0))
    xs = p + (shifted - p) * mu
    r = xs[:, 0:BRANCH_W]
    k = xs[:, BRANCH_W:2 * BRANCH_W]
    v = xs[:, 2 * BRANCH_W:3 * BRANCH_W]
    o = 3 * BRANCH_W
    wd = xs[:, o:o + RW_W_LORA]
    ad = xs[:, o + RW_W_LORA:o + RW_W_LORA + RW_A_LORA]
    gd = xs[:, o + RW_W_LORA + RW_A_LORA:RW_COLS]

    w_log = -_softplus(-(w0 + _dot(jnp.tanh(wd), w2))) - 0.5
    logw = -jnp.exp(w_log)
    a = _sigmoid(a0 + _dot(ad, a2))
    g = _dot(_sigmoid(gd), g2)
    kkr = k * k_k
    kf = k * (1.0 + (a - 1.0) * k_a)
    rkr = r * kf * r_k

    ri = lax.broadcasted_iota(jnp.int32, (rows, rows), 0)
    ci = lax.broadcasted_iota(jnp.int32, (rows, rows), 1)
    if nseq == 1:
        same = ci >= 0
    else:
        sh = int(np.log2(ls))
        same = lax.shift_right_logical(ri, sh) == lax.shift_right_logical(ci, sh)
    low_incl = same & (ci <= ri)
    low_strict = same & (ci < ri)
    m_incl = jnp.where(low_incl, 1.0, 0.0).astype(BF16)
    m_same = jnp.where(same, 1.0, 0.0).astype(BF16)
    hi, mid, lo = _split3(logw)
    cum = (jnp.dot(m_incl, hi, preferred_element_type=F32) + jnp.dot(m_incl, mid, preferred_element_type=F32)
           + jnp.dot(m_incl, lo, preferred_element_type=F32))
    tot = (jnp.dot(m_same, hi, preferred_element_type=F32) + jnp.dot(m_same, mid, preferred_element_type=F32)
           + jnp.dot(m_same, lo, preferred_element_type=F32))
    g_t = jnp.exp(cum)
    g_prev = jnp.exp(cum - logw)
    g_inv = jnp.exp(-cum)
    g_end = jnp.exp(tot - cum)
    g_tot = jnp.exp(tot)
    eye = jnp.where(ri == ci, 1.0, 0.0)

    hs = range(RW_HEADS)
    sls = [slice(h * RW_HD, (h + 1) * RW_HD) for h in hs]
    kk = [kkr[:, sl] for sl in sls]
    kk = [x * lax.rsqrt(jnp.maximum(jnp.sum(x * x, axis=-1, keepdims=True), 1e-24)) for x in kk]
    bv = [kk[h] * a[:, sls[h]] for h in hs]
    k_h = [kf[:, sl] for sl in sls]
    v_h = [v[:, sl] for sl in sls]
    at = [-kk[h] * g_prev[:, sls[h]] for h in hs]
    rt = [r[:, sl] * g_t[:, sl] for sl in sls]
    bt = [bv[h] * g_inv[:, sls[h]] for h in hs]
    kt = [k_h[h] * g_inv[:, sls[h]] for h in hs]
    bh = [bv[h] * g_end[:, sls[h]] for h in hs]
    kh = [k_h[h] * g_end[:, sls[h]] for h in hs]

    ar = [jnp.concatenate([at[h], rt[h]], axis=0) for h in hs]
    pb = [_dot_nt(ar[h], bt[h]) for h in hs]
    pk = [_dot_nt(ar[h], kt[h]) for h in hs]
    l_ab = [jnp.where(low_strict, x[:rows], 0.0) for x in pb]
    l_ak = [jnp.where(low_strict, x[:rows], 0.0) for x in pk]
    a_rb = [jnp.where(low_incl, x[rows:], 0.0) for x in pb]
    a_rk = [jnp.where(low_incl, x[rows:], 0.0) for x in pk]

    tm = [eye + x for x in l_ab]
    pw = [_dot(x, x) for x in l_ab]
    n_dbl = int(np.log2(ls)) - 1
    for it in range(n_dbl):
        if it < n_dbl - 1:
            z = [_dot(jnp.concatenate([tm[h], pw[h]], axis=0), pw[h]) for h in hs]
            tm = [tm[h] + z[h][:rows] for h in hs]
            pw = [z[h][rows:] for h in hs]
        else:
            tm = [tm[h] + _dot(tm[h], pw[h]) for h in hs]

    if nseq == 1:
        ars = [_dot_nt(ar[h], s_ref[0, h]) for h in hs]
        as0 = [x[:rows] for x in ars]
        rs0 = [x[rows:] for x in ars]
    else:
        as0, rs0 = [], []
        for h in hs:
            zs = [_dot_nt(jnp.concatenate([at[h][b * ls:(b + 1) * ls], rt[h][b * ls:(b + 1) * ls]], axis=0),
                          s_ref[b, h]) for b in range(nseq)]
            as0.append(jnp.concatenate([x[:ls] for x in zs], axis=0))
            rs0.append(jnp.concatenate([x[ls:] for x in zs], axis=0))

    lv = [_dot(l_ak[h], v_h[h]) for h in hs]
    u = [_dot(tm[h], as0[h] + lv[h]) for h in hs]
    y = [rs0[h] + _dot(a_rb[h], u[h]) + _dot(a_rk[h], v_h[h]) for h in hs]

    for h in hs:
        for b in range(nseq):
            rb = slice(b * ls, (b + 1) * ls)
            uv = jnp.concatenate([u[h][rb], v_h[h][rb]], axis=0)
            bk = jnp.concatenate([bh[h][rb], kh[h][rb]], axis=0)
            s_ref[b, h] = s_ref[b, h] * g_tot[b * ls:b * ls + 1, sls[h]] + _dot_tn(uv, bk)

    outs = []
    for h in hs:
        ym = jnp.mean(y[h], axis=-1, keepdims=True)
        yc = y[h] - ym
        yv = jnp.mean(yc * yc, axis=-1, keepdims=True)
        yn = yc * lax.rsqrt(yv + GN_EPS) * lnx_g[:, sls[h]] + lnx_b[:, sls[h]]
        bonus = jnp.sum(rkr[:, sls[h]], axis=-1, keepdims=True) * v_h[h]
        outs.append((yn + bonus) * g[:, sls[h]])
    o_ref[...] = jnp.concatenate(outs, axis=-1).astype(o_ref.dtype)


def _rwkv_prompt_kernel(p_ref, *refs):
    prm_refs, (o_ref, s_ref, sh_ref, carry_ref) = refs[:11], refs[11:]
    c = pl.program_id(1)

    @pl.when(c == 0)
    def _():
        s_ref[...] = jnp.zeros_like(s_ref)
        carry_ref[...] = jnp.zeros_like(carry_ref)

    p = p_ref[...]
    first = lax.broadcasted_iota(jnp.int32, (RW_ROWS, 1), 0) == 0
    prm = tuple(x[...] for x in prm_refs)
    _rwkv_rows(p, first, carry_ref[0:1, :], s_ref, o_ref, prm, nseq=1)
    last = p[RW_ROWS - 1:RW_ROWS, :]
    carry_ref[0:1, :] = last
    sh_ref[...] = last


def _rwkv_sample_kernel(p_ref, prev_ref, s0_ref, *refs):
    prm_refs, (o_ref, s_ref, sh_ref) = refs[:11], refs[11:]
    nseq = s0_ref.shape[0]
    ls = RW_ROWS // nseq
    s_ref[...] = s0_ref[...]
    p = p_ref[...]
    first = (lax.broadcasted_iota(jnp.int32, (RW_ROWS, 1), 0) & (ls - 1)) == 0
    prm = tuple(x[...] for x in prm_refs)
    _rwkv_rows(p, first, prev_ref[...], s_ref, o_ref, prm, nseq=nseq)
    for b in range(nseq):
        sh_ref[b:b + 1, :] = p_ref[(b + 1) * ls - 1:(b + 1) * ls, :]


def _rw_param_specs():
    shapes = [(1, RW_COLS), (1, BRANCH_W), (RW_W_LORA, BRANCH_W), (1, BRANCH_W), (RW_A_LORA, BRANCH_W),
              (RW_G_LORA, BRANCH_W), (1, BRANCH_W), (1, BRANCH_W), (1, BRANCH_W), (1, BRANCH_W), (1, BRANCH_W)]
    return [_full(s) for s in shapes]


def _rwkv_prompt(p_rw, prm, batch, seq):
    nc = seq // RW_ROWS
    p3 = p_rw.reshape(batch, seq, RW_COLS)
    o, s_new, sh = pl.pallas_call(
        _rwkv_prompt_kernel,
        grid=(batch, nc),
        in_specs=[pl.BlockSpec((None, RW_ROWS, RW_COLS), lambda b, c: (b, c, 0))] + _rw_param_specs(),
        out_specs=[pl.BlockSpec((None, RW_ROWS, BRANCH_W), lambda b, c: (b, c, 0)),
                   pl.BlockSpec((1, RW_HEADS, RW_HD, RW_HD), lambda b, c: (b, 0, 0, 0)),
                   pl.BlockSpec((None, 1, RW_COLS), lambda b, c: (b, 0, 0))],
        out_shape=[jax.ShapeDtypeStruct((batch, seq, BRANCH_W), BF16),
                   jax.ShapeDtypeStruct((batch, RW_HEADS, RW_HD, RW_HD), F32),
                   jax.ShapeDtypeStruct((batch, 1, RW_COLS), F32)],
        scratch_shapes=[pltpu.VMEM((8, RW_COLS), F32)],
        compiler_params=_params(("arbitrary", "arbitrary"), 32),
        name="rwkv_prompt",
    )(p3, *prm)
    return o.reshape(batch * seq, BRANCH_W), s_new, sh.reshape(batch, RW_COLS)


def _rwkv_sample(p_rw, prev_rows, s0, prm, seq):
    n = p_rw.shape[0]
    nseq = RW_ROWS // seq
    batch = n // seq
    return pl.pallas_call(
        _rwkv_sample_kernel,
        grid=(n // RW_ROWS,),
        in_specs=[pl.BlockSpec((RW_ROWS, RW_COLS), lambda i: (i, 0)),
                  pl.BlockSpec((RW_ROWS, RW_COLS), lambda i: (i, 0)),
                  pl.BlockSpec((nseq, RW_HEADS, RW_HD, RW_HD), lambda i: (i, 0, 0, 0))] + _rw_param_specs(),
        out_specs=[pl.BlockSpec((RW_ROWS, BRANCH_W), lambda i: (i, 0)),
                   pl.BlockSpec((nseq, RW_HEADS, RW_HD, RW_HD), lambda i: (i, 0, 0, 0)),
                   pl.BlockSpec((nseq, RW_COLS), lambda i: (i, 0))],
        out_shape=[jax.ShapeDtypeStruct((n, BRANCH_W), BF16),
                   jax.ShapeDtypeStruct((batch, RW_HEADS, RW_HD, RW_HD), F32),
                   jax.ShapeDtypeStruct((batch, RW_COLS), F32)],
        compiler_params=_params(("arbitrary",), 32),
        name="rwkv_sample",
    )(p_rw, prev_rows, s0, *prm)


def _attend(q, k, v):
    s = _dot_nt(q, k) * (XA_HD ** -0.5)
    e = jnp.exp(s - jnp.max(s, axis=-1, keepdims=True))
    pr = e / jnp.sum(e, axis=-1, keepdims=True)
    return _dot(pr, v)


def _xattn_prompt_kernel(q_ref, k_ref, v_ref, o_ref):
    for h in range(XA_HEADS):
        sl = slice(h * XA_HD, (h + 1) * XA_HD)
        o_ref[:, sl] = _attend(q_ref[:, sl], k_ref[:, sl], v_ref[:, sl]).astype(o_ref.dtype)


def _xattn_prompt(q, mk, mv, batch, seq, tq=512):
    nt = seq // tq
    return pl.pallas_call(
        _xattn_prompt_kernel,
        grid=(batch, nt),
        in_specs=[pl.BlockSpec((tq, BRANCH_W), lambda b, i: (b * nt + i, 0)),
                  pl.BlockSpec((N_MEM, BRANCH_W), lambda b, i: (b, 0)),
                  pl.BlockSpec((N_MEM, BRANCH_W), lambda b, i: (b, 0))],
        out_specs=pl.BlockSpec((tq, BRANCH_W), lambda b, i: (b * nt + i, 0)),
        out_shape=jax.ShapeDtypeStruct((batch * seq, BRANCH_W), BF16),
        compiler_params=_params(("arbitrary", "arbitrary"), 32),
        name="xattn_prompt",
    )(q, mk, mv)


def _xattn_sample_kernel(q_ref, k_ref, v_ref, o_ref):
    nb = k_ref.shape[0]
    ls = q_ref.shape[0] // nb
    for b in range(nb):
        for h in range(XA_HEADS):
            sl = slice(h * XA_HD, (h + 1) * XA_HD)
            o_ref[b * ls:(b + 1) * ls, sl] = _attend(
                q_ref[b * ls:(b + 1) * ls, sl], k_ref[b, :, sl], v_ref[b, :, sl]).astype(o_ref.dtype)


def _xattn_sample(q, mk, mv, seq, nb=8):
    batch = mk.shape[0]
    mem_spec = pl.BlockSpec((nb, N_MEM, BRANCH_W), lambda i: (i, 0, 0))
    return pl.pallas_call(
        _xattn_sample_kernel,
        grid=(batch // nb,),
        in_specs=[pl.BlockSpec((nb * seq, BRANCH_W), lambda i: (i, 0)), mem_spec, mem_spec],
        out_specs=pl.BlockSpec((nb * seq, BRANCH_W), lambda i: (i, 0)),
        out_shape=jax.ShapeDtypeStruct((batch * seq, BRANCH_W), BF16),
        compiler_params=_params(("arbitrary",), 40),
        name="xattn_sample",
    )(q, mk, mv)


def _merge_kernel(x_ref, gm_ref, rw_ref, xa_ref, gt_ref, gb_ref, wb_ref, wo_ref, n2_ref,
                  wr_hi_ref, wr_lo_ref, br_ref, h_ref, hn_ref, route_ref, cnt_ref):
    merged = None
    for n, br in enumerate((gm_ref, rw_ref, xa_ref)):
        cs = slice(n * D_MODEL, (n + 1) * D_MODEL)
        up = jnp.dot(br[...], wb_ref[n], preferred_element_type=F32)
        term = _sigmoid(gt_ref[:, cs] + gb_ref[:, cs]) * up
        merged = term if merged is None else merged + term
    h = x_ref[...] + _dot(merged, wo_ref[...])
    h_ref[...] = h
    hn = _rmsnorm(h, n2_ref[...])
    hn_ref[...] = hn.astype(hn_ref.dtype)

    hn_hi = hn.astype(BF16)
    hn_lo = (hn - hn_hi.astype(F32)).astype(BF16)
    logits = (jnp.dot(hn_hi, wr_hi_ref[...], preferred_element_type=F32)
              + jnp.dot(hn_lo, wr_hi_ref[...], preferred_element_type=F32)
              + jnp.dot(hn_hi, wr_lo_ref[...], preferred_element_type=F32)) + br_ref[...]
    lane = lax.broadcasted_iota(jnp.int32, logits.shape, 1)
    neg = -jnp.inf
    big = jnp.int32(1 << 20)
    gmask = (lane >= N_EXPERTS) & (lane < N_EXPERTS + N_GROUPS)
    gl = jnp.where(gmask, logits, neg)
    gmax = jnp.max(gl, axis=-1, keepdims=True)
    gsel = jnp.min(jnp.where(gl == gmax, lane, big), axis=-1, keepdims=True) - N_EXPERTS
    gsum = jnp.sum(jnp.where(gmask, jnp.exp(logits - gmax), 0.0), axis=-1, keepdims=True)
    pg_top = 1.0 / gsum
    emask = (lane >= gsel * EXP_PER_GROUP) & (lane < (gsel + 1) * EXP_PER_GROUP)
    el = jnp.where(emask, logits, neg)
    m1 = jnp.max(el, axis=-1, keepdims=True)
    i1 = jnp.min(jnp.where(el == m1, lane, big), axis=-1, keepdims=True)
    el2 = jnp.where(lane == i1, neg, el)
    m2 = jnp.max(el2, axis=-1, keepdims=True)
    i2 = jnp.min(jnp.where(el2 == m2, lane, big), axis=-1, keepdims=True)
    t2 = jnp.exp(m2 - m1)
    w1 = pg_top / (1.0 + t2)
    w2 = pg_top * t2 / (1.0 + t2)

    @pl.when(pl.program_id(0) == 0)
    def _():
        cnt_ref[...] = jnp.zeros_like(cnt_ref)

    tm = logits.shape[0]
    onehot = jnp.where(lane == i1, 1.0, 0.0) + jnp.where(lane == i2, 1.0, 0.0)
    ri = lax.broadcasted_iota(jnp.int32, (tm, tm), 0)
    ci = lax.broadcasted_iota(jnp.int32, (tm, tm), 1)
    earlier = jnp.where(ci < ri, 1.0, 0.0).astype(BF16)
    base = cnt_ref[0:1, :]
    seen = base + jnp.dot(earlier, onehot.astype(BF16), preferred_element_type=F32)
    r1 = jnp.sum(jnp.where(lane == i1, seen, 0.0), axis=-1, keepdims=True)
    r2 = jnp.sum(jnp.where(lane == i2, seen, 0.0), axis=-1, keepdims=True)
    cnt_ref[0:1, :] = base + jnp.sum(onehot, axis=0, keepdims=True)

    route = jnp.zeros(logits.shape, F32)
    for col, val in ((ROUTE_E1, i1.astype(F32)), (ROUTE_E2, i2.astype(F32)), (ROUTE_W1, w1), (ROUTE_W2, w2),
                     (ROUTE_R1, r1), (ROUTE_R2, r2)):
        route = jnp.where(lane == col, val, route)
    route_ref[...] = route


def _merge(x, o_gm, o_rw, o_xa, p_gate, gate_b, w_branch, w_out, n2_g, wr_hi, wr_lo, b_r, tm=256):
    n = x.shape[0]
    row = lambda wd: pl.BlockSpec((tm, wd), lambda i: (i, 0))
    return pl.pallas_call(
        _merge_kernel,
        grid=(n // tm,),
        in_specs=[row(D_MODEL), row(BRANCH_W), row(BRANCH_W), row(BRANCH_W), row(GATE_COLS),
                  _full((1, GATE_COLS)), _full((N_BRANCH, BRANCH_W, D_MODEL)), _full((D_MODEL, D_MODEL)),
                  _full((1, D_MODEL)), _full((D_MODEL, LANES)), _full((D_MODEL, LANES)), _full((1, LANES))],
        out_specs=[row(D_MODEL), row(D_MODEL), row(LANES), _full((8, LANES))],
        out_shape=[jax.ShapeDtypeStruct((n, D_MODEL), F32), jax.ShapeDtypeStruct((n, D_MODEL), F32),
                   jax.ShapeDtypeStruct((n, LANES), F32), jax.ShapeDtypeStruct((8, LANES), F32)],
        compiler_params=_params(("arbitrary",), 40),
        name="merge",
    )(x, o_gm, o_rw, o_xa, p_gate, gate_b, w_branch, w_out, n2_g, wr_hi, wr_lo, b_r)


def _route_plan(route, counts, tm):
    n = route.shape[0]
    n_tiles = (2 * n) // tm + N_EXPERTS
    e1 = route[:, ROUTE_E1].astype(jnp.int32)
    e2 = route[:, ROUTE_E2].astype(jnp.int32)
    r1 = route[:, ROUTE_R1].astype(jnp.int32)
    r2 = route[:, ROUTE_R2].astype(jnp.int32)
    cnt = counts[0, :N_EXPERTS].astype(jnp.int32)
    tiles = (cnt + tm - 1) // tm
    tile_end = jnp.cumsum(tiles)
    row_start = (tile_end - tiles) * tm
    pos1 = row_start[e1] + r1
    pos2 = row_start[e2] + r2
    pos = jnp.concatenate([pos1.reshape(n // tm, 1, tm), pos2.reshape(n // tm, 1, tm)], axis=2)
    n_used = tile_end[N_EXPERTS - 1:]
    t = jnp.minimum(jnp.arange(n_tiles, dtype=jnp.int32), n_used - 1)
    tile_expert = jnp.sum((t[:, None] >= tile_end[None, :]).astype(jnp.int32), axis=1)
    return pos, tile_expert, n_used, n_tiles


def _row_copy(src_ref, src_row, dst_ref, dst_row, sem):
    return pltpu.make_async_copy(src_ref.at[pl.ds(src_row, 1)], dst_ref.at[pl.ds(dst_row, 1)], sem)


def _dispatch_kernel(pos_ref, hn_ref, xs_in_ref, xs_ref, sem):
    del xs_in_ref
    tm = hn_ref.shape[0]

    def issue(i, c):
        _row_copy(hn_ref, i, xs_ref, pos_ref[0, 0, i], sem).start()
        _row_copy(hn_ref, i, xs_ref, pos_ref[0, 0, tm + i], sem).start()
        return c

    def drain(i, c):
        _row_copy(hn_ref, 0, xs_ref, 0, sem).wait()
        return c

    lax.fori_loop(0, tm, issue, 0, unroll=8)
    lax.fori_loop(0, 2 * tm, drain, 0, unroll=8)


def _dispatch(hn, pos, n_rows, tm):
    n = hn.shape[0]
    return pl.pallas_call(
        _dispatch_kernel,
        grid=(n // tm,),
        in_specs=[pl.BlockSpec((1, 1, 2 * tm), lambda i: (i, 0, 0), memory_space=pltpu.SMEM),
                  pl.BlockSpec((tm, D_MODEL), lambda i: (i, 0)),
                  pl.BlockSpec(memory_space=pl.ANY)],
        out_specs=pl.BlockSpec(memory_space=pl.ANY),
        out_shape=jax.ShapeDtypeStruct((n_rows, D_MODEL), F32),
        scratch_shapes=[pltpu.SemaphoreType.DMA],
        input_output_aliases={2: 0},
        compiler_params=_params(("arbitrary",), 32),
        name="moe_dispatch",
    )(pos, hn, jnp.zeros((n_rows, D_MODEL), F32))


def _experts_kernel(te_ref, nu_ref, x_ref, wg_ref, wu_ref, wd_ref, y_ref):
    del te_ref
    used = pl.program_id(0) < nu_ref[0]

    @pl.when(used)
    def _():
        xb = x_ref[...].astype(BF16)
        gate = jnp.dot(xb, wg_ref[...], preferred_element_type=F32)
        up = jnp.dot(xb, wu_ref[...], preferred_element_type=F32)
        y_ref[...] = _dot(gate * _sigmoid(gate) * up, wd_ref[...])

    @pl.when(jnp.logical_not(used))
    def _():
        y_ref[...] = jnp.zeros_like(y_ref)


def _experts(xs, tile_expert, n_used, wg, wu, wd, tm):
    n_rows = xs.shape[0]
    rows = lambda t, te, nu: (jnp.minimum(t, nu[0] - 1), 0)
    return pl.pallas_call(
        _experts_kernel,
        grid_spec=pltpu.PrefetchScalarGridSpec(
            num_scalar_prefetch=2,
            grid=(n_rows // tm,),
            in_specs=[pl.BlockSpec((tm, D_MODEL), rows),
                      pl.BlockSpec((None, D_MODEL, EXP_FF), lambda t, te, nu: (te[t], 0, 0)),
                      pl.BlockSpec((None, D_MODEL, EXP_FF), lambda t, te, nu: (te[t], 0, 0)),
                      pl.BlockSpec((None, EXP_FF, D_MODEL), lambda t, te, nu: (te[t], 0, 0))],
            out_specs=pl.BlockSpec((tm, D_MODEL), lambda t, te, nu: (t, 0))),
        out_shape=jax.ShapeDtypeStruct((n_rows, D_MODEL), F32),
        compiler_params=_params(("arbitrary",), 40),
        name="moe_experts",
    )(tile_expert, n_used, xs, wg, wu, wd)


def _combine_kernel(pos_ref, h_ref, route_ref, fg_ref, ys_ref, y_ref, o1_ref, o2_ref, sem):
    tm = h_ref.shape[0]

    def issue(i, c):
        _row_copy(ys_ref, pos_ref[0, 0, i], o1_ref, i, sem).start()
        _row_copy(ys_ref, pos_ref[0, 0, tm + i], o2_ref, i, sem).start()
        return c

    def drain(i, c):
        _row_copy(ys_ref, 0, o1_ref, 0, sem).wait()
        return c

    lax.fori_loop(0, tm, issue, 0, unroll=8)
    lax.fori_loop(0, 2 * tm, drain, 0, unroll=8)
    rt = route_ref[...]
    lane = lax.broadcasted_iota(jnp.int32, rt.shape, 1)
    w1 = jnp.sum(jnp.where(lane == ROUTE_W1, rt, 0.0), axis=-1, keepdims=True)
    w2 = jnp.sum(jnp.where(lane == ROUTE_W2, rt, 0.0), axis=-1, keepdims=True)
    y_ref[...] = _rmsnorm(h_ref[...] + w1 * o1_ref[...] + w2 * o2_ref[...], fg_ref[...])


def _combine(h, route, pos, ys, final_g, tm):
    n = h.shape[0]
    return pl.pallas_call(
        _combine_kernel,
        grid=(n // tm,),
        in_specs=[pl.BlockSpec((1, 1, 2 * tm), lambda i: (i, 0, 0), memory_space=pltpu.SMEM),
                  pl.BlockSpec((tm, D_MODEL), lambda i: (i, 0)),
                  pl.BlockSpec((tm, LANES), lambda i: (i, 0)),
                  _full((1, D_MODEL)),
                  pl.BlockSpec(memory_space=pl.ANY)],
        out_specs=pl.BlockSpec((tm, D_MODEL), lambda i: (i, 0)),
        out_shape=jax.ShapeDtypeStruct((n, D_MODEL), F32),
        scratch_shapes=[pltpu.VMEM((tm, D_MODEL), F32), pltpu.VMEM((tm, D_MODEL), F32),
                        pltpu.SemaphoreType.DMA],
        compiler_params=_params(("arbitrary",), 32),
        name="moe_combine",
    )(pos, h, route, final_g, ys)


def _moe(h, hn, route, counts, wg, wu, wd, final_g, tm=256):
    pos, tile_expert, n_used, n_tiles = _route_plan(route, counts, tm)
    xs = _dispatch(hn, pos, n_tiles * tm, tm)
    ys = _experts(xs, tile_expert, n_used, wg, wu, wd, tm)
    return _combine(h, route, pos, ys, final_g, tm)


def _layer(x2d, mem_k, mem_v, rw_state, w, *, prompt, batch, seq):
    n = x2d.shape[0]
    tm_in = 256
    p_gm, p_rw, p_q, p_gate = _in_proj(x2d, w["norm1_g"], w["w_in_segs"], tm_in)
    if prompt:
        o_gm = _gmlp(p_gm, w["gm_ln_g"], w["gm_ln_b"], w["gm_mix_prompt"], w["gm_bias_prompt"], False)[0]
        vn = None
        o_rw, s_new, sh_new = _rwkv_prompt(p_rw, w["rw_prm"], batch, seq)
        o_xa = _xattn_prompt(p_q, mem_k, mem_v, batch, seq)
    else:
        o_gm, vn = _gmlp(p_gm, w["gm_ln_g"], w["gm_ln_b"], w["gm_mix_sample"], w["gm_bias_sample"], True)
        s0, shift = rw_state
        prev_rows = jnp.pad(shift[:, None, :], ((0, 0), (0, seq - 1), (0, 0))).reshape(n, RW_COLS)
        o_rw, s_new, sh_new = _rwkv_sample(p_rw, prev_rows, s0, w["rw_prm"], seq)
        o_xa = _xattn_sample(p_q, mem_k, mem_v, seq)
    h, hn, route, counts = _merge(x2d, o_gm, o_rw, o_xa, p_gate, w["gate_b"], w["w_branch"], w["w_out"],
                                  w["norm2_g"], w["wr_hi"], w["wr_lo"], w["b_r"])
    y = _moe(h, hn, route, counts, w["e_wg"], w["e_wu"], w["e_wd"], w["final_g"])
    return y, s_new, sh_new, vn


def kernel(x_prompt, x_sample, state_rwkv_S, state_rwkv_shift, cache_mem_k, cache_mem_v, mem_prompt, norm1_g, w_in, gate_b, gm_ln_g, gm_ln_b, gm_ws, gm_bs, rw_mu, rw_w0, rw_w2, rw_a0, rw_a2, rw_g2, rw_k_k, rw_k_a, rw_r_k, rw_lnx_g, rw_lnx_b, xa_wk, xa_wv, w_branch, w_out, norm2_g, rg_w, rg_b, re_w, re_b, e_wg, e_wu, e_wd, final_g):
    bp, seq_p, _ = x_prompt.shape
    bs, seq_s, _ = x_sample.shape
    depth = w_in.shape[0]
    assert depth == 1 and seq_p % CHUNK == 0 and RW_ROWS % seq_s == 0 and CHUNK % seq_s == 0

    l = 0
    row = lambda a: a.reshape(1, -1)
    seg = (0, 2 * BRANCH_W, 2 * BRANCH_W + RW_COLS, 3 * BRANCH_W + RW_COLS, 3 * BRANCH_W + RW_COLS + GATE_COLS)
    w_causal = jnp.tril(gm_ws[l])
    nrep = CHUNK // seq_s
    blk = w_causal[:, :seq_s, :seq_s]
    eye_rep = jnp.eye(nrep, dtype=F32)
    mix_sample = jnp.einsum("ab,gts->gatbs", eye_rep, blk).reshape(GM_GROUPS, CHUNK, CHUNK)
    bias_prompt = jnp.repeat(gm_bs[l].T, GM_HD, axis=1)
    bias_sample = jnp.tile(bias_prompt[:seq_s], (nrep, 1))

    wr = jnp.zeros((D_MODEL, LANES), F32)
    wr = wr.at[:, :N_EXPERTS].set(jnp.transpose(re_w[l], (1, 0, 2)).reshape(D_MODEL, N_EXPERTS))
    wr = wr.at[:, N_EXPERTS:N_EXPERTS + N_GROUPS].set(rg_w[l])
    wr_hi = wr.astype(BF16)
    wr_lo = (wr - wr_hi.astype(F32)).astype(BF16)
    b_r = jnp.zeros((1, LANES), F32)
    b_r = b_r.at[0, :N_EXPERTS].set(re_b[l].reshape(-1)).at[0, N_EXPERTS:N_EXPERTS + N_GROUPS].set(rg_b[l])

    w = dict(
        norm1_g=row(norm1_g[l]),
        w_in_segs=[w_in[l][:, a:b].astype(BF16) for a, b in zip(seg[:-1], seg[1:])],
        gm_ln_g=row(gm_ln_g[l]), gm_ln_b=row(gm_ln_b[l]),
        gm_mix_prompt=w_causal.astype(BF16), gm_bias_prompt=bias_prompt,
        gm_mix_sample=mix_sample.astype(BF16), gm_bias_sample=bias_sample,
        rw_prm=(row(rw_mu[l]), row(rw_w0[l]), rw_w2[l].astype(BF16), row(rw_a0[l]), rw_a2[l].astype(BF16),
                rw_g2[l].astype(BF16), row(rw_k_k[l]), row(rw_k_a[l]), row(rw_r_k[l]),
                row(rw_lnx_g[l]), row(rw_lnx_b[l])),
        gate_b=row(gate_b[l]), w_branch=w_branch[l].astype(BF16), w_out=w_out[l].astype(BF16),
        norm2_g=row(norm2_g[l]), wr_hi=wr_hi, wr_lo=wr_lo, b_r=b_r,
        e_wg=e_wg[l].reshape(N_EXPERTS, D_MODEL, EXP_FF).astype(BF16),
        e_wu=e_wu[l].reshape(N_EXPERTS, D_MODEL, EXP_FF).astype(BF16),
        e_wd=e_wd[l].reshape(N_EXPERTS, EXP_FF, D_MODEL).astype(BF16),
        final_g=row(final_g),
    )

    mk_p, mv_p = _mem_kv(mem_prompt.reshape(bp * N_MEM, D_MODEL), xa_wk[l].astype(BF16), xa_wv[l].astype(BF16))
    yp, sp, shp, _ = _layer(x_prompt.reshape(bp * seq_p, D_MODEL), mk_p, mv_p, None, w,
                            prompt=True, batch=bp, seq=seq_p)
    ys, ss, shs, vs = _layer(x_sample.reshape(bs * seq_s, D_MODEL),
                             cache_mem_k[l].reshape(bs, N_MEM, BRANCH_W).astype(BF16),
                             cache_mem_v[l].reshape(bs, N_MEM, BRANCH_W).astype(BF16),
                             (state_rwkv_S[l], state_rwkv_shift[l]), w, prompt=False, batch=bs, seq=seq_s)

    return (yp.reshape(bp, seq_p, D_MODEL), ys.reshape(bs, seq_s, D_MODEL),
            sp[None], shp[None],
            mk_p.reshape(1, bp, N_MEM, XA_HEADS, XA_HD), mv_p.reshape(1, bp, N_MEM, XA_HEADS, XA_HD),
            ss[None], shs[None], vs.reshape(1, bs, seq_s, BRANCH_W))
```

```python
import itertools

import numpy as np
import jax
import jax.numpy as jnp
from jax import lax
from jax.experimental import pallas as pl
from jax.experimental.pallas import tpu as pltpu

F32 = jnp.float32
BF16 = jnp.bfloat16

D_MODEL = 1024
BRANCH_W = 512
CHUNK = 128
GM_GROUPS = 8
GM_HD = BRANCH_W // GM_GROUPS
RW_HEADS = 8
RW_HD = BRANCH_W // RW_HEADS
RW_W_LORA = 64
RW_A_LORA = 64
RW_G_LORA = 128
RW_COLS = 3 * BRANCH_W + RW_W_LORA + RW_A_LORA + RW_G_LORA
XA_HEADS = 4
XA_HD = BRANCH_W // XA_HEADS
N_MEM = 256
N_BRANCH = 3
GATE_COLS = N_BRANCH * D_MODEL
N_GROUPS = 4
EXP_PER_GROUP = 8
N_EXPERTS = N_GROUPS * EXP_PER_GROUP
EXP_FF = 512
RMS_EPS = 1e-6
LN_EPS = 1e-5
GN_EPS = 64e-5

LANES = 128
RW_ROWS = 64
RW_GROUPS = 2
MIB = 1024 * 1024
ROUTE_E1, ROUTE_E2, ROUTE_W1, ROUTE_W2, ROUTE_R1, ROUTE_R2 = range(6)


def _dot(a, b):
    return jnp.dot(a.astype(BF16), b.astype(BF16), preferred_element_type=F32)


def _dot_nt(a, b):
    return lax.dot_general(a.astype(BF16), b.astype(BF16), (((1,), (1,)), ((), ())),
                           preferred_element_type=F32)


def _dot_tn(a, b):
    return lax.dot_general(a.astype(BF16), b.astype(BF16), (((0,), (0,)), ((), ())),
                           preferred_element_type=F32)


def _sigmoid(x):
    return 1.0 / (1.0 + jnp.exp(-x))


def _gelu(x):
    c = np.float32(np.sqrt(2.0 / np.pi))
    return x * (0.5 * (1.0 + jnp.tanh(c * (x + 0.044715 * (x * x * x)))))


def _softplus(x):
    return jnp.maximum(x, 0.0) + jnp.log(1.0 + jnp.exp(-jnp.abs(x)))


def _rmsnorm(x, g):
    return x * lax.rsqrt(jnp.mean(x * x, axis=-1, keepdims=True) + RMS_EPS) * g


def _split3(x):
    hi = x.astype(BF16)
    r1 = x - hi.astype(F32)
    mid = r1.astype(BF16)
    lo = (r1 - mid.astype(F32)).astype(BF16)
    return hi, mid, lo


def _params(sem, vmem_mib):
    return pltpu.CompilerParams(dimension_semantics=sem, vmem_limit_bytes=vmem_mib * MIB)


def _full(shape):
    nd = len(shape)
    return pl.BlockSpec(shape, lambda *_: (0,) * nd)


def _in_proj_kernel(x_ref, g_ref, wgm_ref, wrw_ref, wq_ref, wgt_ref, gm_ref, rw_ref, q_ref, gt_ref):
    xb = _rmsnorm(x_ref[...], g_ref[...]).astype(BF16)
    gm_ref[...] = jnp.dot(xb, wgm_ref[...], preferred_element_type=F32)
    rw_ref[...] = jnp.dot(xb, wrw_ref[...], preferred_element_type=F32)
    q_ref[...] = jnp.dot(xb, wq_ref[...], preferred_element_type=F32)
    gt_ref[...] = jnp.dot(xb, wgt_ref[...], preferred_element_type=F32)


def _in_proj(x, g, w_segs, tm):
    n = x.shape[0]
    widths = [w.shape[1] for w in w_segs]
    return pl.pallas_call(
        _in_proj_kernel,
        grid=(n // tm,),
        in_specs=[pl.BlockSpec((tm, D_MODEL), lambda i: (i, 0)), _full((1, D_MODEL))]
        + [pl.BlockSpec((D_MODEL, wd), lambda i: (0, 0), pipeline_mode=pl.Buffered(1)) for wd in widths],
        out_specs=[pl.BlockSpec((tm, wd), lambda i: (i, 0)) for wd in widths],
        out_shape=[jax.ShapeDtypeStruct((n, wd), F32) for wd in widths],
        compiler_params=_params(("arbitrary",), 48),
        name="in_proj",
    )(x, g, *w_segs)


def _mem_kv_kernel(m_ref, wk_ref, wv_ref, k_ref, v_ref):
    mb = m_ref[...].astype(BF16)
    k_ref[...] = jnp.dot(mb, wk_ref[...], preferred_element_type=F32)
    v_ref[...] = jnp.dot(mb, wv_ref[...], preferred_element_type=F32)


def _mem_kv(mem, wk, wv, tm=512):
    n = mem.shape[0]
    return pl.pallas_call(
        _mem_kv_kernel,
        grid=(n // tm,),
        in_specs=[pl.BlockSpec((tm, D_MODEL), lambda i: (i, 0)),
                  _full((D_MODEL, BRANCH_W)), _full((D_MODEL, BRANCH_W))],
        out_specs=[pl.BlockSpec((tm, BRANCH_W), lambda i: (i, 0))] * 2,
        out_shape=[jax.ShapeDtypeStruct((n, BRANCH_W), F32)] * 2,
        compiler_params=_params(("arbitrary",), 32),
        name="mem_kv",
    )(mem, wk, wv)


def _gmlp_kernel(pu_ref, pv_ref, lng_ref, lnb_ref, wmix_ref, bmix_ref, o_ref, *vn_refs):
    u = _gelu(pu_ref[...])
    vf = _gelu(pv_ref[...])
    mu = jnp.mean(vf, axis=-1, keepdims=True)
    vc = vf - mu
    var = jnp.mean(vc * vc, axis=-1, keepdims=True)
    vn = vc * lax.rsqrt(var + LN_EPS) * lng_ref[...] + lnb_ref[...]
    if vn_refs:
        vn_refs[0][...] = vn
    lane = lax.broadcasted_iota(jnp.int32, (CHUNK, LANES), 1)
    lo_half = lane < GM_HD
    for p in range(GM_GROUPS // 2):
        vp = vn[:, p * LANES:(p + 1) * LANES]
        s = (_dot(wmix_ref[2 * p], jnp.where(lo_half, vp, 0.0))
             + _dot(wmix_ref[2 * p + 1], jnp.where(lo_half, 0.0, vp)))
        s = s + bmix_ref[:, p * LANES:(p + 1) * LANES]
        o_ref[:, p * LANES:(p + 1) * LANES] = (u[:, p * LANES:(p + 1) * LANES] * s).astype(o_ref.dtype)


def _gmlp(p_gm, ln_g, ln_b, wmix, bmix, emit_vn):
    n = p_gm.shape[0]
    out_shape = [jax.ShapeDtypeStruct((n, BRANCH_W), BF16)]
    out_specs = [pl.BlockSpec((CHUNK, BRANCH_W), lambda i: (i, 0))]
    if emit_vn:
        out_shape.append(jax.ShapeDtypeStruct((n, BRANCH_W), F32))
        out_specs.append(pl.BlockSpec((CHUNK, BRANCH_W), lambda i: (i, 0)))
    return pl.pallas_call(
        _gmlp_kernel,
        grid=(n // CHUNK,),
        in_specs=[pl.BlockSpec((CHUNK, BRANCH_W), lambda i: (i, 0)),
                  pl.BlockSpec((CHUNK, BRANCH_W), lambda i: (i, 1)),
                  _full((1, BRANCH_W)), _full((1, BRANCH_W)),
                  _full((GM_GROUPS, CHUNK, CHUNK)), _full((CHUNK, BRANCH_W))],
        out_specs=out_specs,
        out_shape=out_shape,
        compiler_params=_params(("arbitrary",), 32),
        name="gmlp",
    )(p_gm, p_gm, ln_g, ln_b, wmix, bmix)


def _rwkv_rows(p, first, prev, s_ref, o_ref, prm, nseq):
    (mu, w0, w2, a0, a2, g2, k_k, k_a, r_k, lnx_g, lnx_b) = prm
    rows = RW_ROWS
    ls = rows // nseq
    shifted = jnp.where(first, prev, pltpu.roll(p, 1, 0))
    xs = p + (shifted - p) * mu
    r = xs[:, 0:BRANCH_W]
    k = xs[:, BRANCH_W:2 * BRANCH_W]
    v = xs[:, 2 * BRANCH_W:3 * BRANCH_W]
    o = 3 * BRANCH_W
    wd = xs[:, o:o + RW_W_LORA]
    ad = xs[:, o + RW_W_LORA:o + RW_W_LORA + RW_A_LORA]
    gd = xs[:, o + RW_W_LORA + RW_A_LORA:RW_COLS]

    w_log = -_softplus(-(w0 + _dot(jnp.tanh(wd), w2))) - 0.5
    logw = -jnp.exp(w_log)
    a = _sigmoid(a0 + _dot(ad, a2))
    g = _dot(_sigmoid(gd), g2)
    kkr = k * k_k
    kf = k * (1.0 + (a - 1.0) * k_a)
    rkr = r * kf * r_k

    ri = lax.broadcasted_iota(jnp.int32, (rows, rows), 0)
    ci = lax.broadcasted_iota(jnp.int32, (rows, rows), 1)
    if nseq == 1:
        same = ci >= 0
    else:
        sh = int(np.log2(ls))
        same = lax.shift_right_logical(ri, sh) == lax.shift_right_logical(ci, sh)
    low_incl = same & (ci <= ri)
    low_strict = same & (ci < ri)
    m_incl = jnp.where(low_incl, 1.0, 0.0).astype(BF16)
    m_same = jnp.where(same, 1.0, 0.0).astype(BF16)
    hi, mid, lo = _split3(logw)
    cum = (jnp.dot(m_incl, hi, preferred_element_type=F32) + jnp.dot(m_incl, mid, preferred_element_type=F32)
           + jnp.dot(m_incl, lo, preferred_element_type=F32))
    tot = (jnp.dot(m_same, hi, preferred_element_type=F32) + jnp.dot(m_same, mid, preferred_element_type=F32)
           + jnp.dot(m_same, lo, preferred_element_type=F32))
    g_t = jnp.exp(cum)
    g_prev = jnp.exp(cum - logw)
    g_inv = jnp.exp(-cum)
    g_end = jnp.exp(tot - cum)
    g_tot = jnp.exp(tot)
    eye = jnp.where(ri == ci, 1.0, 0.0)
    yield

    hs = range(RW_HEADS)
    sls = [slice(h * RW_HD, (h + 1) * RW_HD) for h in hs]
    kk = [kkr[:, sl] for sl in sls]
    kk = [x * lax.rsqrt(jnp.maximum(jnp.sum(x * x, axis=-1, keepdims=True), 1e-24)) for x in kk]
    bv = [kk[h] * a[:, sls[h]] for h in hs]
    k_h = [kf[:, sl] for sl in sls]
    v_h = [v[:, sl] for sl in sls]
    at = [-kk[h] * g_prev[:, sls[h]] for h in hs]
    rt = [r[:, sl] * g_t[:, sl] for sl in sls]
    bt = [bv[h] * g_inv[:, sls[h]] for h in hs]
    kt = [k_h[h] * g_inv[:, sls[h]] for h in hs]
    bh = [bv[h] * g_end[:, sls[h]] for h in hs]
    kh = [k_h[h] * g_end[:, sls[h]] for h in hs]

    ar = [jnp.concatenate([at[h], rt[h]], axis=0) for h in hs]
    pb = [_dot_nt(ar[h], bt[h]) for h in hs]
    pk = [_dot_nt(ar[h], kt[h]) for h in hs]
    l_ab = [jnp.where(low_strict, x[:rows], 0.0) for x in pb]
    l_ak = [jnp.where(low_strict, x[:rows], 0.0) for x in pk]
    a_rb = [jnp.where(low_incl, x[rows:], 0.0) for x in pb]
    a_rk = [jnp.where(low_incl, x[rows:], 0.0) for x in pk]
    yield

    tm = [eye + x for x in l_ab]
    pw = [_dot(x, x) for x in l_ab]
    n_dbl = int(np.log2(ls)) - 1
    for it in range(n_dbl):
        yield
        if it < n_dbl - 1:
            z = [_dot(jnp.concatenate([tm[h], pw[h]], axis=0), pw[h]) for h in hs]
            tm = [tm[h] + z[h][:rows] for h in hs]
            pw = [z[h][rows:] for h in hs]
        else:
            tm = [tm[h] + _dot(tm[h], pw[h]) for h in hs]

    if nseq == 1:
        ars = [_dot_nt(ar[h], s_ref[0, h]) for h in hs]
        as0 = [x[:rows] for x in ars]
        rs0 = [x[rows:] for x in ars]
    else:
        as0, rs0 = [], []
        for h in hs:
            zs = [_dot_nt(jnp.concatenate([at[h][b * ls:(b + 1) * ls], rt[h][b * ls:(b + 1) * ls]], axis=0),
                          s_ref[b, h]) for b in range(nseq)]
            as0.append(jnp.concatenate([x[:ls] for x in zs], axis=0))
            rs0.append(jnp.concatenate([x[ls:] for x in zs], axis=0))

    lv = [_dot(l_ak[h], v_h[h]) for h in hs]
    yield
    u = [_dot(tm[h], as0[h] + lv[h]) for h in hs]
    yield
    y = [rs0[h] + _dot(a_rb[h], u[h]) + _dot(a_rk[h], v_h[h]) for h in hs]
    yield

    for h in hs:
        for b in range(nseq):
            rb = slice(b * ls, (b + 1) * ls)
            uv = jnp.concatenate([u[h][rb], v_h[h][rb]], axis=0)
            bk = jnp.concatenate([bh[h][rb], kh[h][rb]], axis=0)
            s_ref[b, h] = s_ref[b, h] * g_tot[b * ls:b * ls + 1, sls[h]] + _dot_tn(uv, bk)

    outs = []
    for h in hs:
        ym = jnp.mean(y[h], axis=-1, keepdims=True)
        yc = y[h] - ym
        yv = jnp.mean(yc * yc, axis=-1, keepdims=True)
        yn = yc * lax.rsqrt(yv + GN_EPS) * lnx_g[:, sls[h]] + lnx_b[:, sls[h]]
        bonus = jnp.sum(rkr[:, sls[h]], axis=-1, keepdims=True) * v_h[h]
        outs.append((yn + bonus) * g[:, sls[h]])
    o_ref[...] = jnp.concatenate(outs, axis=-1).astype(o_ref.dtype)


def _interleave(gens):
    for _ in itertools.zip_longest(*gens):
        pass


def _rwkv_prompt_kernel(p_ref, *refs):
    prm_refs, (o_ref, s_ref, sh_ref, carry_ref) = refs[:11], refs[11:]
    c = pl.program_id(1)

    @pl.when(c == 0)
    def _():
        s_ref[...] = jnp.zeros_like(s_ref)
        carry_ref[...] = jnp.zeros_like(carry_ref)

    first = lax.broadcasted_iota(jnp.int32, (RW_ROWS, 1), 0) == 0
    prm = tuple(x[...] for x in prm_refs)
    groups = range(p_ref.shape[0])
    _interleave([_rwkv_rows(p_ref[g], first, carry_ref[8 * g:8 * g + 1, :], s_ref.at[pl.ds(g, 1)],
                            o_ref.at[g], prm, nseq=1) for g in groups])
    for g in groups:
        last = p_ref[g, RW_ROWS - 1:RW_ROWS, :]
        carry_ref[8 * g:8 * g + 1, :] = last
        sh_ref[g] = last


def _rwkv_sample_kernel(p_ref, prev_ref, s0_ref, *refs):
    prm_refs, (o_ref, s_ref, sh_ref) = refs[:11], refs[11:]
    nseq = s0_ref.shape[0] // p_ref.shape[0]
    ls = RW_ROWS // nseq
    s_ref[...] = s0_ref[...]
    first = (lax.broadcasted_iota(jnp.int32, (RW_ROWS, 1), 0) & (ls - 1)) == 0
    prm = tuple(x[...] for x in prm_refs)
    groups = range(p_ref.shape[0])
    _interleave([_rwkv_rows(p_ref[g], first, prev_ref[g], s_ref.at[pl.ds(g * nseq, nseq)], o_ref.at[g],
                            prm, nseq=nseq) for g in groups])
    for g in groups:
        for b in range(nseq):
            sh_ref[g * nseq + b:g * nseq + b + 1, :] = p_ref[g, (b + 1) * ls - 1:(b + 1) * ls, :]


def _rw_param_specs():
    shapes = [(1, RW_COLS), (1, BRANCH_W), (RW_W_LORA, BRANCH_W), (1, BRANCH_W), (RW_A_LORA, BRANCH_W),
              (RW_G_LORA, BRANCH_W), (1, BRANCH_W), (1, BRANCH_W), (1, BRANCH_W), (1, BRANCH_W), (1, BRANCH_W)]
    return [_full(s) for s in shapes]


def _rwkv_prompt(p_rw, prm, batch, seq):
    nc = seq // RW_ROWS
    g = RW_GROUPS
    p3 = p_rw.reshape(batch, seq, RW_COLS)
    o, s_new, sh = pl.pallas_call(
        _rwkv_prompt_kernel,
        grid=(batch // g, nc),
        in_specs=[pl.BlockSpec((g, RW_ROWS, RW_COLS), lambda b, c: (b, c, 0))] + _rw_param_specs(),
        out_specs=[pl.BlockSpec((g, RW_ROWS, BRANCH_W), lambda b, c: (b, c, 0)),
                   pl.BlockSpec((g, RW_HEADS, RW_HD, RW_HD), lambda b, c: (b, 0, 0, 0)),
                   pl.BlockSpec((g, 1, RW_COLS), lambda b, c: (b, 0, 0))],
        out_shape=[jax.ShapeDtypeStruct((batch, seq, BRANCH_W), BF16),
                   jax.ShapeDtypeStruct((batch, RW_HEADS, RW_HD, RW_HD), F32),
                   jax.ShapeDtypeStruct((batch, 1, RW_COLS), F32)],
        scratch_shapes=[pltpu.VMEM((8 * g, RW_COLS), F32)],
        compiler_params=_params(("arbitrary", "arbitrary"), 56),
        name="rwkv_prompt",
    )(p3, *prm)
    return o.reshape(batch * seq, BRANCH_W), s_new, sh.reshape(batch, RW_COLS)


def _rwkv_sample(p_rw, prev_rows, s0, prm, seq):
    n = p_rw.shape[0]
    nseq = RW_ROWS // seq
    batch = n // seq
    g = RW_GROUPS
    ng = n // RW_ROWS
    o, s_new, sh = pl.pallas_call(
        _rwkv_sample_kernel,
        grid=(ng // g,),
        in_specs=[pl.BlockSpec((g, RW_ROWS, RW_COLS), lambda i: (i, 0, 0)),
                  pl.BlockSpec((g, RW_ROWS, RW_COLS), lambda i: (i, 0, 0)),
                  pl.BlockSpec((g * nseq, RW_HEADS, RW_HD, RW_HD), lambda i: (i, 0, 0, 0))] + _rw_param_specs(),
        out_specs=[pl.BlockSpec((g, RW_ROWS, BRANCH_W), lambda i: (i, 0, 0)),
                   pl.BlockSpec((g * nseq, RW_HEADS, RW_HD, RW_HD), lambda i: (i, 0, 0, 0)),
                   pl.BlockSpec((g * nseq, RW_COLS), lambda i: (i, 0))],
        out_shape=[jax.ShapeDtypeStruct((ng, RW_ROWS, BRANCH_W), BF16),
                   jax.ShapeDtypeStruct((batch, RW_HEADS, RW_HD, RW_HD), F32),
                   jax.ShapeDtypeStruct((batch, RW_COLS), F32)],
        compiler_params=_params(("arbitrary",), 56),
        name="rwkv_sample",
    )(p_rw.reshape(ng, RW_ROWS, RW_COLS), prev_rows.reshape(ng, RW_ROWS, RW_COLS), s0, *prm)
    return o.reshape(n, BRANCH_W), s_new, sh


def _attend(q, k, v):
    s = _dot_nt(q, k) * (XA_HD ** -0.5)
    e = jnp.exp(s - jnp.max(s, axis=-1, keepdims=True))
    pr = e / jnp.sum(e, axis=-1, keepdims=True)
    return _dot(pr, v)


def _xattn_prompt_kernel(q_ref, k_ref, v_ref, o_ref):
    for h in range(XA_HEADS):
        sl = slice(h * XA_HD, (h + 1) * XA_HD)
        o_ref[:, sl] = _attend(q_ref[:, sl], k_ref[:, sl], v_ref[:, sl]).astype(o_ref.dtype)


def _xattn_prompt(q, mk, mv, batch, seq, tq=512):
    nt = seq // tq
    return pl.pallas_call(
        _xattn_prompt_kernel,
        grid=(batch, nt),
        in_specs=[pl.BlockSpec((tq, BRANCH_W), lambda b, i: (b * nt + i, 0)),
                  pl.BlockSpec((N_MEM, BRANCH_W), lambda b, i: (b, 0)),
                  pl.BlockSpec((N_MEM, BRANCH_W), lambda b, i: (b, 0))],
        out_specs=pl.BlockSpec((tq, BRANCH_W), lambda b, i: (b * nt + i, 0)),
        out_shape=jax.ShapeDtypeStruct((batch * seq, BRANCH_W), BF16),
        compiler_params=_params(("arbitrary", "arbitrary"), 32),
        name="xattn_prompt",
    )(q, mk, mv)


def _xattn_sample_kernel(q_ref, k_ref, v_ref, o_ref):
    nb = k_ref.shape[0]
    ls = q_ref.shape[0] // nb
    for b in range(nb):
        for h in range(XA_HEADS):
            sl = slice(h * XA_HD, (h + 1) * XA_HD)
            o_ref[b * ls:(b + 1) * ls, sl] = _attend(
                q_ref[b * ls:(b + 1) * ls, sl], k_ref[b, :, sl], v_ref[b, :, sl]).astype(o_ref.dtype)


def _xattn_sample(q, mk, mv, seq, nb=8):
    batch = mk.shape[0]
    mem_spec = pl.BlockSpec((nb, N_MEM, BRANCH_W), lambda i: (i, 0, 0))
    return pl.pallas_call(
        _xattn_sample_kernel,
        grid=(batch // nb,),
        in_specs=[pl.BlockSpec((nb * seq, BRANCH_W), lambda i: (i, 0)), mem_spec, mem_spec],
        out_specs=pl.BlockSpec((nb * seq, BRANCH_W), lambda i: (i, 0)),
        out_shape=jax.ShapeDtypeStruct((batch * seq, BRANCH_W), BF16),
        compiler_params=_params(("arbitrary",), 40),
        name="xattn_sample",
    )(q, mk, mv)


def _merge_kernel(x_ref, gm_ref, rw_ref, xa_ref, gt_ref, gb_ref, wb_ref, wo_ref, n2_ref,
                  wr_hi_ref, wr_lo_ref, br_ref, h_ref, hn_ref, route_ref, ridx_ref, cnt_ref):
    merged = None
    for n, br in enumerate((gm_ref, rw_ref, xa_ref)):
        cs = slice(n * D_MODEL, (n + 1) * D_MODEL)
        up = jnp.dot(br[...], wb_ref[n], preferred_element_type=F32)
        term = _sigmoid(gt_ref[:, cs] + gb_ref[:, cs]) * up
        merged = term if merged is None else merged + term
    h = x_ref[...] + _dot(merged, wo_ref[...])
    h_ref[...] = h
    hn = _rmsnorm(h, n2_ref[...])
    hn_ref[...] = hn.astype(hn_ref.dtype)

    hn_hi = hn.astype(BF16)
    hn_lo = (hn - hn_hi.astype(F32)).astype(BF16)
    logits = (jnp.dot(hn_hi, wr_hi_ref[...], preferred_element_type=F32)
              + jnp.dot(hn_lo, wr_hi_ref[...], preferred_element_type=F32)
              + jnp.dot(hn_hi, wr_lo_ref[...], preferred_element_type=F32)) + br_ref[...]
    lane = lax.broadcasted_iota(jnp.int32, logits.shape, 1)
    neg = -jnp.inf
    big = jnp.int32(1 << 20)
    gmask = (lane >= N_EXPERTS) & (lane < N_EXPERTS + N_GROUPS)
    gl = jnp.where(gmask, logits, neg)
    gmax = jnp.max(gl, axis=-1, keepdims=True)
    gsel = jnp.min(jnp.where(gl == gmax, lane, big), axis=-1, keepdims=True) - N_EXPERTS
    gsum = jnp.sum(jnp.where(gmask, jnp.exp(logits - gmax), 0.0), axis=-1, keepdims=True)
    pg_top = 1.0 / gsum
    emask = (lane >= gsel * EXP_PER_GROUP) & (lane < (gsel + 1) * EXP_PER_GROUP)
    el = jnp.where(emask, logits, neg)
    m1 = jnp.max(el, axis=-1, keepdims=True)
    i1 = jnp.min(jnp.where(el == m1, lane, big), axis=-1, keepdims=True)
    el2 = jnp.where(lane == i1, neg, el)
    m2 = jnp.max(el2, axis=-1, keepdims=True)
    i2 = jnp.min(jnp.where(el2 == m2, lane, big), axis=-1, keepdims=True)
    t2 = jnp.exp(m2 - m1)
    w1 = pg_top / (1.0 + t2)
    w2 = pg_top * t2 / (1.0 + t2)

    @pl.when(pl.program_id(0) == 0)
    def _():
        cnt_ref[...] = jnp.zeros_like(cnt_ref)

    tm = logits.shape[0]
    onehot = jnp.where(lane == i1, 1.0, 0.0) + jnp.where(lane == i2, 1.0, 0.0)
    ri = lax.broadcasted_iota(jnp.int32, (tm, tm), 0)
    ci = lax.broadcasted_iota(jnp.int32, (tm, tm), 1)
    earlier = jnp.where(ci < ri, 1.0, 0.0).astype(BF16)
    base = cnt_ref[0:1, :]
    seen = base + jnp.dot(earlier, onehot.astype(BF16), preferred_element_type=F32)
    r1 = jnp.sum(jnp.where(lane == i1, seen, 0.0), axis=-1, keepdims=True)
    r2 = jnp.sum(jnp.where(lane == i2, seen, 0.0), axis=-1, keepdims=True)
    cnt_ref[0:1, :] = base + jnp.sum(onehot, axis=0, keepdims=True)

    route = jnp.zeros(logits.shape, F32)
    for col, val in ((ROUTE_E1, i1.astype(F32)), (ROUTE_E2, i2.astype(F32)), (ROUTE_W1, w1), (ROUTE_W2, w2),
                     (ROUTE_R1, r1), (ROUTE_R2, r2)):
        route = jnp.where(lane == col, val, route)
    route_ref[...] = route
    ridx_ref[...] = jnp.transpose(route)[0:8, :].astype(jnp.int32)


def _merge(x, o_gm, o_rw, o_xa, p_gate, gate_b, w_branch, w_out, n2_g, wr_hi, wr_lo, b_r, tm=256):
    n = x.shape[0]
    row = lambda wd: pl.BlockSpec((tm, wd), lambda i: (i, 0))
    return pl.pallas_call(
        _merge_kernel,
        grid=(n // tm,),
        in_specs=[row(D_MODEL), row(BRANCH_W), row(BRANCH_W), row(BRANCH_W), row(GATE_COLS),
                  _full((1, GATE_COLS)), _full((N_BRANCH, BRANCH_W, D_MODEL)), _full((D_MODEL, D_MODEL)),
                  _full((1, D_MODEL)), _full((D_MODEL, LANES)), _full((D_MODEL, LANES)), _full((1, LANES))],
        out_specs=[row(D_MODEL), row(D_MODEL), row(LANES), pl.BlockSpec((8, tm), lambda i: (0, i)),
                   _full((8, LANES))],
        out_shape=[jax.ShapeDtypeStruct((n, D_MODEL), F32), jax.ShapeDtypeStruct((n, D_MODEL), F32),
                   jax.ShapeDtypeStruct((n, LANES), F32), jax.ShapeDtypeStruct((8, n), jnp.int32),
                   jax.ShapeDtypeStruct((8, LANES), F32)],
        compiler_params=_params(("arbitrary",), 40),
        name="merge",
    )(x, o_gm, o_rw, o_xa, p_gate, gate_b, w_branch, w_out, n2_g, wr_hi, wr_lo, b_r)


def _route_plan(counts, n, tm):
    n_tiles = (2 * n) // tm + N_EXPERTS
    cnt = counts[0, :N_EXPERTS].astype(jnp.int32)
    tiles = (cnt + tm - 1) // tm
    tile_end = jnp.cumsum(tiles)
    row_start = (tile_end - tiles) * tm
    n_used = tile_end[N_EXPERTS - 1:]
    pad_tile_row = jnp.where(tiles > 0, (tile_end - 1) * tm, -1)
    t = jnp.minimum(jnp.arange(n_tiles, dtype=jnp.int32), n_used - 1)
    tile_expert = jnp.sum((t[:, None] >= tile_end[None, :]).astype(jnp.int32), axis=1)
    return row_start, pad_tile_row, tile_expert, n_used, n_tiles


def _row_copy(src_ref, src_row, dst_ref, dst_row, sem):
    return pltpu.make_async_copy(src_ref.at[pl.ds(src_row, 1)], dst_ref.at[pl.ds(dst_row, 1)], sem)


def _dispatch_kernel(rs_ref, pad_ref, nu_ref, ridx_ref, hn_ref, xs_ref, zero_ref, sem):
    tm = hn_ref.shape[0]
    n_tiles = xs_ref.shape[0] // tm

    @pl.when(pl.program_id(0) == 0)
    def _():
        zero_ref[...] = jnp.zeros_like(zero_ref)

        def fill(row):
            return pltpu.make_async_copy(zero_ref, xs_ref.at[pl.ds(pl.multiple_of(row, tm), tm)], sem)

        def each_fill(act):
            for e in range(N_EXPERTS):
                @pl.when(pad_ref[e] >= 0)
                def _():
                    act(fill(pad_ref[e]))

            def tail(t, c):
                act(fill(t * tm))
                return c

            lax.fori_loop(nu_ref[0], n_tiles, tail, 0)

        each_fill(lambda f: f.start())
        each_fill(lambda f: f.wait())

    def issue(i, c):
        _row_copy(hn_ref, i, xs_ref, rs_ref[ridx_ref[ROUTE_E1, i]] + ridx_ref[ROUTE_R1, i], sem).start()
        _row_copy(hn_ref, i, xs_ref, rs_ref[ridx_ref[ROUTE_E2, i]] + ridx_ref[ROUTE_R2, i], sem).start()
        return c

    def drain(i, c):
        _row_copy(hn_ref, 0, xs_ref, 0, sem).wait()
        return c

    lax.fori_loop(0, tm, issue, 0, unroll=8)
    lax.fori_loop(0, 2 * tm, drain, 0, unroll=8)


def _dispatch(hn, ridx, row_start, pad_tile_row, n_used, n_rows, tm):
    n = hn.shape[0]
    return pl.pallas_call(
        _dispatch_kernel,
        grid_spec=pltpu.PrefetchScalarGridSpec(
            num_scalar_prefetch=3,
            grid=(n // tm,),
            in_specs=[pl.BlockSpec((8, tm), lambda i, *_: (0, i), memory_space=pltpu.SMEM),
                      pl.BlockSpec((tm, D_MODEL), lambda i, *_: (i, 0))],
            out_specs=pl.BlockSpec(memory_space=pl.ANY),
            scratch_shapes=[pltpu.VMEM((tm, D_MODEL), F32), pltpu.SemaphoreType.DMA]),
        out_shape=jax.ShapeDtypeStruct((n_rows, D_MODEL), F32),
        compiler_params=_params(("arbitrary",), 32),
        name="moe_dispatch",
    )(row_start, pad_tile_row, n_used, ridx, hn)


def _experts_kernel(te_ref, nu_ref, x_ref, wg_ref, wu_ref, wd_ref, y_ref):
    del te_ref
    used = pl.program_id(0) < nu_ref[0]

    @pl.when(used)
    def _():
        xb = x_ref[...].astype(BF16)
        gate = jnp.dot(xb, wg_ref[...], preferred_element_type=F32)
        up = jnp.dot(xb, wu_ref[...], preferred_element_type=F32)
        y_ref[...] = _dot(gate * _sigmoid(gate) * up, wd_ref[...])

    @pl.when(jnp.logical_not(used))
    def _():
        y_ref[...] = jnp.zeros_like(y_ref)


def _experts(xs, tile_expert, n_used, wg, wu, wd, tm):
    n_rows = xs.shape[0]
    rows = lambda t, te, nu: (jnp.minimum(t, nu[0] - 1), 0)
    return pl.pallas_call(
        _experts_kernel,
        grid_spec=pltpu.PrefetchScalarGridSpec(
            num_scalar_prefetch=2,
            grid=(n_rows // tm,),
            in_specs=[pl.BlockSpec((tm, D_MODEL), rows),
                      pl.BlockSpec((None, D_MODEL, EXP_FF), lambda t, te, nu: (te[t], 0, 0)),
                      pl.BlockSpec((None, D_MODEL, EXP_FF), lambda t, te, nu: (te[t], 0, 0)),
                      pl.BlockSpec((None, EXP_FF, D_MODEL), lambda t, te, nu: (te[t], 0, 0))],
            out_specs=pl.BlockSpec((tm, D_MODEL), lambda t, te, nu: (t, 0))),
        out_shape=jax.ShapeDtypeStruct((n_rows, D_MODEL), F32),
        compiler_params=_params(("arbitrary",), 40),
        name="moe_experts",
    )(tile_expert, n_used, xs, wg, wu, wd)


def _combine_kernel(rs_ref, ridx_ref, h_ref, route_ref, fg_ref, ys_ref, y_ref, o1_ref, o2_ref, sem):
    tm = h_ref.shape[0]

    def issue(i, c):
        _row_copy(ys_ref, rs_ref[ridx_ref[ROUTE_E1, i]] + ridx_ref[ROUTE_R1, i], o1_ref, i, sem).start()
        _row_copy(ys_ref, rs_ref[ridx_ref[ROUTE_E2, i]] + ridx_ref[ROUTE_R2, i], o2_ref, i, sem).start()
        return c

    def drain(i, c):
        _row_copy(ys_ref, 0, o1_ref, 0, sem).wait()
        return c

    lax.fori_loop(0, tm, issue, 0, unroll=8)
    lax.fori_loop(0, 2 * tm, drain, 0, unroll=8)
    rt = route_ref[...]
    lane = lax.broadcasted_iota(jnp.int32, rt.shape, 1)
    w1 = jnp.sum(jnp.where(lane == ROUTE_W1, rt, 0.0), axis=-1, keepdims=True)
    w2 = jnp.sum(jnp.where(lane == ROUTE_W2, rt, 0.0), axis=-1, keepdims=True)
    y_ref[...] = _rmsnorm(h_ref[...] + w1 * o1_ref[...] + w2 * o2_ref[...], fg_ref[...])


def _combine(h, route, ridx, row_start, ys, final_g, tm):
    n = h.shape[0]
    return pl.pallas_call(
        _combine_kernel,
        grid_spec=pltpu.PrefetchScalarGridSpec(
            num_scalar_prefetch=1,
            grid=(n // tm,),
            in_specs=[pl.BlockSpec((8, tm), lambda i, rs: (0, i), memory_space=pltpu.SMEM),
                      pl.BlockSpec((tm, D_MODEL), lambda i, rs: (i, 0)),
                      pl.BlockSpec((tm, LANES), lambda i, rs: (i, 0)),
                      pl.BlockSpec((1, D_MODEL), lambda i, rs: (0, 0)),
                      pl.BlockSpec(memory_space=pl.ANY)],
            out_specs=pl.BlockSpec((tm, D_MODEL), lambda i, rs: (i, 0)),
            scratch_shapes=[pltpu.VMEM((tm, D_MODEL), F32), pltpu.VMEM((tm, D_MODEL), F32),
                            pltpu.SemaphoreType.DMA]),
        out_shape=jax.ShapeDtypeStruct((n, D_MODEL), F32),
        compiler_params=_params(("arbitrary",), 32),
        name="moe_combine",
    )(row_start, ridx, h, route, final_g, ys)


def _moe(h, hn, route, ridx, counts, wg, wu, wd, final_g, tm=256):
    row_start, pad_tile_row, tile_expert, n_used, n_tiles = _route_plan(counts, h.shape[0], tm)
    xs = _dispatch(hn, ridx, row_start, pad_tile_row, n_used, n_tiles * tm, tm)
    ys = _experts(xs, tile_expert, n_used, wg, wu, wd, tm)
    return _combine(h, route, ridx, row_start, ys, final_g, tm)


def _layer(x2d, mem_k, mem_v, rw_state, w, *, prompt, batch, seq):
    n = x2d.shape[0]
    tm_in = 256
    p_gm, p_rw, p_q, p_gate = _in_proj(x2d, w["norm1_g"], w["w_in_segs"], tm_in)
    if prompt:
        o_gm = _gmlp(p_gm, w["gm_ln_g"], w["gm_ln_b"], w["gm_mix_prompt"], w["gm_bias_prompt"], False)[0]
        vn = None
        o_rw, s_new, sh_new = _rwkv_prompt(p_rw, w["rw_prm"], batch, seq)
        o_xa = _xattn_prompt(p_q, mem_k, mem_v, batch, seq)
    else:
        o_gm, vn = _gmlp(p_gm, w["gm_ln_g"], w["gm_ln_b"], w["gm_mix_sample"], w["gm_bias_sample"], True)
        s0, shift = rw_state
        prev_rows = jnp.pad(shift[:, None, :], ((0, 0), (0, seq - 1), (0, 0))).reshape(n, RW_COLS)
        o_rw, s_new, sh_new = _rwkv_sample(p_rw, prev_rows, s0, w["rw_prm"], seq)
        o_xa = _xattn_sample(p_q, mem_k, mem_v, seq)
    h, hn, route, ridx, counts = _merge(x2d, o_gm, o_rw, o_xa, p_gate, w["gate_b"], w["w_branch"], w["w_out"],
                                        w["norm2_g"], w["wr_hi"], w["wr_lo"], w["b_r"])
    y = _moe(h, hn, route, ridx, counts, w["e_wg"], w["e_wu"], w["e_wd"], w["final_g"])
    return y, s_new, sh_new, vn


def kernel(x_prompt, x_sample, state_rwkv_S, state_rwkv_shift, cache_mem_k, cache_mem_v, mem_prompt, norm1_g, w_in, gate_b, gm_ln_g, gm_ln_b, gm_ws, gm_bs, rw_mu, rw_w0, rw_w2, rw_a0, rw_a2, rw_g2, rw_k_k, rw_k_a, rw_r_k, rw_lnx_g, rw_lnx_b, xa_wk, xa_wv, w_branch, w_out, norm2_g, rg_w, rg_b, re_w, re_b, e_wg, e_wu, e_wd, final_g):
    bp, seq_p, _ = x_prompt.shape
    bs, seq_s, _ = x_sample.shape
    depth = w_in.shape[0]
    assert depth == 1 and seq_p % CHUNK == 0 and RW_ROWS % seq_s == 0 and CHUNK % seq_s == 0

    l = 0
    row = lambda a: a.reshape(1, -1)
    seg = (0, 2 * BRANCH_W, 2 * BRANCH_W + RW_COLS, 3 * BRANCH_W + RW_COLS, 3 * BRANCH_W + RW_COLS + GATE_COLS)
    w_causal = jnp.tril(gm_ws[l])
    nrep = CHUNK // seq_s
    blk = w_causal[:, :seq_s, :seq_s]
    eye_rep = jnp.eye(nrep, dtype=F32)
    mix_sample = jnp.einsum("ab,gts->gatbs", eye_rep, blk).reshape(GM_GROUPS, CHUNK, CHUNK)
    bias_prompt = jnp.repeat(gm_bs[l].T, GM_HD, axis=1)
    bias_sample = jnp.tile(bias_prompt[:seq_s], (nrep, 1))

    wr = jnp.zeros((D_MODEL, LANES), F32)
    wr = wr.at[:, :N_EXPERTS].set(jnp.transpose(re_w[l], (1, 0, 2)).reshape(D_MODEL, N_EXPERTS))
    wr = wr.at[:, N_EXPERTS:N_EXPERTS + N_GROUPS].set(rg_w[l])
    wr_hi = wr.astype(BF16)
    wr_lo = (wr - wr_hi.astype(F32)).astype(BF16)
    b_r = jnp.zeros((1, LANES), F32)
    b_r = b_r.at[0, :N_EXPERTS].set(re_b[l].reshape(-1)).at[0, N_EXPERTS:N_EXPERTS + N_GROUPS].set(rg_b[l])

    w = dict(
        norm1_g=row(norm1_g[l]),
        w_in_segs=[w_in[l][:, a:b].astype(BF16) for a, b in zip(seg[:-1], seg[1:])],
        gm_ln_g=row(gm_ln_g[l]), gm_ln_b=row(gm_ln_b[l]),
        gm_mix_prompt=w_causal.astype(BF16), gm_bias_prompt=bias_prompt,
        gm_mix_sample=mix_sample.astype(BF16), gm_bias_sample=bias_sample,
        rw_prm=(row(rw_mu[l]), row(rw_w0[l]), rw_w2[l].astype(BF16), row(rw_a0[l]), rw_a2[l].astype(BF16),
                rw_g2[l].astype(BF16), row(rw_k_k[l]), row(rw_k_a[l]), row(rw_r_k[l]),
                row(rw_lnx_g[l]), row(rw_lnx_b[l])),
        gate_b=row(gate_b[l]), w_branch=w_branch[l].astype(BF16), w_out=w_out[l].astype(BF16),
        norm2_g=row(norm2_g[l]), wr_hi=wr_hi, wr_lo=wr_lo, b_r=b_r,
        e_wg=e_wg[l].reshape(N_EXPERTS, D_MODEL, EXP_FF).astype(BF16),
        e_wu=e_wu[l].reshape(N_EXPERTS, D_MODEL, EXP_FF).astype(BF16),
        e_wd=e_wd[l].reshape(N_EXPERTS, EXP_FF, D_MODEL).astype(BF16),
        final_g=row(final_g),
    )

    mk_p, mv_p = _mem_kv(mem_prompt.reshape(bp * N_MEM, D_MODEL), xa_wk[l].astype(BF16), xa_wv[l].astype(BF16))
    yp, sp, shp, _ = _layer(x_prompt.reshape(bp * seq_p, D_MODEL), mk_p, mv_p, None, w,
                            prompt=True, batch=bp, seq=seq_p)
    ys, ss, shs, vs = _layer(x_sample.reshape(bs * seq_s, D_MODEL),
                             cache_mem_k[l].reshape(bs, N_MEM, BRANCH_W).astype(BF16),
                             cache_mem_v[l].reshape(bs, N_MEM, BRANCH_W).astype(BF16),
                             (state_rwkv_S[l], state_rwkv_shift[l]), w, prompt=False, batch=bs, seq=seq_s)

    return (yp.reshape(bp, seq_p, D_MODEL), ys.reshape(bs, seq_s, D_MODEL),
            sp[None], shp[None],
            mk_p.reshape(1, bp, N_MEM, XA_HEADS, XA_HD), mv_p.reshape(1, bp, N_MEM, XA_HEADS, XA_HD),
            ss[None], shs[None], vs.reshape(1, bs, seq_s, BRANCH_W))
```

```python
import itertools

import numpy as np
import jax
import jax.numpy as jnp
from jax import lax
from jax.experimental import pallas as pl
from jax.experimental.pallas import tpu as pltpu

F32 = jnp.float32
BF16 = jnp.bfloat16

D_MODEL = 1024
BRANCH_W = 512
CHUNK = 128
GM_GROUPS = 8
GM_HD = BRANCH_W // GM_GROUPS
RW_HEADS = 8
RW_HD = BRANCH_W // RW_HEADS
RW_W_LORA = 64
RW_A_LORA = 64
RW_G_LORA = 128
RW_COLS = 3 * BRANCH_W + RW_W_LORA + RW_A_LORA + RW_G_LORA
XA_HEADS = 4
XA_HD = BRANCH_W // XA_HEADS
N_MEM = 256
N_BRANCH = 3
GATE_COLS = N_BRANCH * D_MODEL
N_GROUPS = 4
EXP_PER_GROUP = 8
N_EXPERTS = N_GROUPS * EXP_PER_GROUP
EXP_FF = 512
RMS_EPS = 1e-6
LN_EPS = 1e-5
GN_EPS = 64e-5

LANES = 128
RW_ROWS = 64
RW_GROUPS = 2
MIB = 1024 * 1024
ROUTE_E1, ROUTE_E2, ROUTE_W1, ROUTE_W2, ROUTE_R1, ROUTE_R2 = range(6)
ROW_TILE = (D_MODEL // LANES, LANES)


def _dot(a, b):
    return jnp.dot(a.astype(BF16), b.astype(BF16), preferred_element_type=F32)


def _dot_nt(a, b):
    return lax.dot_general(a.astype(BF16), b.astype(BF16), (((1,), (1,)), ((), ())),
                           preferred_element_type=F32)


def _dot_tn(a, b):
    return lax.dot_general(a.astype(BF16), b.astype(BF16), (((0,), (0,)), ((), ())),
                           preferred_element_type=F32)


def _sigmoid(x):
    return 1.0 / (1.0 + jnp.exp(-x))


def _gelu(x):
    c = np.float32(np.sqrt(2.0 / np.pi))
    return x * (0.5 * (1.0 + jnp.tanh(c * (x + 0.044715 * (x * x * x)))))


def _softplus(x):
    return jnp.maximum(x, 0.0) + jnp.log(1.0 + jnp.exp(-jnp.abs(x)))


def _rmsnorm(x, g):
    return x * lax.rsqrt(jnp.mean(x * x, axis=-1, keepdims=True) + RMS_EPS) * g


def _split3(x):
    hi = x.astype(BF16)
    r1 = x - hi.astype(F32)
    mid = r1.astype(BF16)
    lo = (r1 - mid.astype(F32)).astype(BF16)
    return hi, mid, lo


def _rows_to_tiles(ref, x):
    for g in range(ROW_TILE[0]):
        ref[:, g, :] = x[:, g * LANES:(g + 1) * LANES]


def _tiles_to_rows(ref):
    return jnp.concatenate([ref[:, g, :] for g in range(ROW_TILE[0])], axis=-1)


def _params(sem, vmem_mib):
    return pltpu.CompilerParams(dimension_semantics=sem, vmem_limit_bytes=vmem_mib * MIB)


def _full(shape):
    nd = len(shape)
    return pl.BlockSpec(shape, lambda *_: (0,) * nd)


def _in_proj_kernel(x_ref, g_ref, wgm_ref, wrw_ref, wq_ref, wgt_ref, gm_ref, rw_ref, q_ref, gt_ref):
    xb = _rmsnorm(x_ref[...], g_ref[...]).astype(BF16)
    gm_ref[...] = jnp.dot(xb, wgm_ref[...], preferred_element_type=F32)
    rw_ref[...] = jnp.dot(xb, wrw_ref[...], preferred_element_type=F32)
    q_ref[...] = jnp.dot(xb, wq_ref[...], preferred_element_type=F32)
    gt_ref[...] = jnp.dot(xb, wgt_ref[...], preferred_element_type=F32)


def _in_proj(x, g, w_segs, tm):
    n = x.shape[0]
    widths = [w.shape[1] for w in w_segs]
    return pl.pallas_call(
        _in_proj_kernel,
        grid=(n // tm,),
        in_specs=[pl.BlockSpec((tm, D_MODEL), lambda i: (i, 0)), _full((1, D_MODEL))]
        + [pl.BlockSpec((D_MODEL, wd), lambda i: (0, 0), pipeline_mode=pl.Buffered(1)) for wd in widths],
        out_specs=[pl.BlockSpec((tm, wd), lambda i: (i, 0)) for wd in widths],
        out_shape=[jax.ShapeDtypeStruct((n, wd), F32) for wd in widths],
        compiler_params=_params(("arbitrary",), 48),
        name="in_proj",
    )(x, g, *w_segs)


def _mem_kv_kernel(m_ref, wk_ref, wv_ref, k_ref, v_ref):
    mb = m_ref[...].astype(BF16)
    k_ref[...] = jnp.dot(mb, wk_ref[...], preferred_element_type=F32)
    v_ref[...] = jnp.dot(mb, wv_ref[...], preferred_element_type=F32)


def _mem_kv(mem, wk, wv, tm=512):
    n = mem.shape[0]
    return pl.pallas_call(
        _mem_kv_kernel,
        grid=(n // tm,),
        in_specs=[pl.BlockSpec((tm, D_MODEL), lambda i: (i, 0)),
                  _full((D_MODEL, BRANCH_W)), _full((D_MODEL, BRANCH_W))],
        out_specs=[pl.BlockSpec((tm, BRANCH_W), lambda i: (i, 0))] * 2,
        out_shape=[jax.ShapeDtypeStruct((n, BRANCH_W), F32)] * 2,
        compiler_params=_params(("arbitrary",), 32),
        name="mem_kv",
    )(mem, wk, wv)


def _gmlp_kernel(pu_ref, pv_ref, lng_ref, lnb_ref, wmix_ref, bmix_ref, o_ref, *vn_refs):
    u = _gelu(pu_ref[...])
    vf = _gelu(pv_ref[...])
    mu = jnp.mean(vf, axis=-1, keepdims=True)
    vc = vf - mu
    var = jnp.mean(vc * vc, axis=-1, keepdims=True)
    vn = vc * lax.rsqrt(var + LN_EPS) * lng_ref[...] + lnb_ref[...]
    if vn_refs:
        vn_refs[0][...] = vn
    lane = lax.broadcasted_iota(jnp.int32, (CHUNK, LANES), 1)
    lo_half = lane < GM_HD
    for p in range(GM_GROUPS // 2):
        vp = vn[:, p * LANES:(p + 1) * LANES]
        s = (_dot(wmix_ref[2 * p], jnp.where(lo_half, vp, 0.0))
             + _dot(wmix_ref[2 * p + 1], jnp.where(lo_half, 0.0, vp)))
        s = s + bmix_ref[:, p * LANES:(p + 1) * LANES]
        o_ref[:, p * LANES:(p + 1) * LANES] = (u[:, p * LANES:(p + 1) * LANES] * s).astype(o_ref.dtype)


def _gmlp(p_gm, ln_g, ln_b, wmix, bmix, emit_vn):
    n = p_gm.shape[0]
    out_shape = [jax.ShapeDtypeStruct((n, BRANCH_W), BF16)]
    out_specs = [pl.BlockSpec((CHUNK, BRANCH_W), lambda i: (i, 0))]
    if emit_vn:
        out_shape.append(jax.ShapeDtypeStruct((n, BRANCH_W), F32))
        out_specs.append(pl.BlockSpec((CHUNK, BRANCH_W), lambda i: (i, 0)))
    return pl.pallas_call(
        _gmlp_kernel,
        grid=(n // CHUNK,),
        in_specs=[pl.BlockSpec((CHUNK, BRANCH_W), lambda i: (i, 0)),
                  pl.BlockSpec((CHUNK, BRANCH_W), lambda i: (i, 1)),
                  _full((1, BRANCH_W)), _full((1, BRANCH_W)),
                  _full((GM_GROUPS, CHUNK, CHUNK)), _full((CHUNK, BRANCH_W))],
        out_specs=out_specs,
        out_shape=out_shape,
        compiler_params=_params(("arbitrary",), 32),
        name="gmlp",
    )(p_gm, p_gm, ln_g, ln_b, wmix, bmix)


def _rwkv_rows(p, first, prev, s_ref, o_ref, prm, nseq):
    (mu, w0, w2, a0, a2, g2, k_k, k_a, r_k, lnx_g, lnx_b) = prm
    rows = RW_ROWS
    ls = rows // nseq
    shifted = jnp.where(first, prev, pltpu.roll(p, 1, 0))
    xs = p + (shifted - p) * mu
    r = xs[:, 0:BRANCH_W]
    k = xs[:, BRANCH_W:2 * BRANCH_W]
    v = xs[:, 2 * BRANCH_W:3 * BRANCH_W]
    o = 3 * BRANCH_W
    wd = xs[:, o:o + RW_W_LORA]
    ad = xs[:, o + RW_W_LORA:o + RW_W_LORA + RW_A_LORA]
    gd = xs[:, o + RW_W_LORA + RW_A_LORA:RW_COLS]

    w_log = -_softplus(-(w0 + _dot(jnp.tanh(wd), w2))) - 0.5
    logw = -jnp.exp(w_log)
    a = _sigmoid(a0 + _dot(ad, a2))
    g = _dot(_sigmoid(gd), g2)
    kkr = k * k_k
    kf = k * (1.0 + (a - 1.0) * k_a)
    rkr = r * kf * r_k

    ri = lax.broadcasted_iota(jnp.int32, (rows, rows), 0)
    ci = lax.broadcasted_iota(jnp.int32, (rows, rows), 1)
    if nseq == 1:
        same = ci >= 0
    else:
        sh = int(np.log2(ls))
        same = lax.shift_right_logical(ri, sh) == lax.shift_right_logical(ci, sh)
    low_incl = same & (ci <= ri)
    low_strict = same & (ci < ri)
    m_incl = jnp.where(low_incl, 1.0, 0.0).astype(BF16)
    m_same = jnp.where(same, 1.0, 0.0).astype(BF16)
    hi, mid, lo = _split3(logw)
    cum = (jnp.dot(m_incl, hi, preferred_element_type=F32) + jnp.dot(m_incl, mid, preferred_element_type=F32)
           + jnp.dot(m_incl, lo, preferred_element_type=F32))
    tot = (jnp.dot(m_same, hi, preferred_element_type=F32) + jnp.dot(m_same, mid, preferred_element_type=F32)
           + jnp.dot(m_same, lo, preferred_element_type=F32))
    g_t = jnp.exp(cum)
    g_prev = jnp.exp(cum - logw)
    g_inv = jnp.exp(-cum)
    g_end = jnp.exp(tot - cum)
    g_tot = jnp.exp(tot)
    eye = jnp.where(ri == ci, 1.0, 0.0)
    yield

    hs = range(RW_HEADS)
    sls = [slice(h * RW_HD, (h + 1) * RW_HD) for h in hs]
    kk = [kkr[:, sl] for sl in sls]
    kk = [x * lax.rsqrt(jnp.maximum(jnp.sum(x * x, axis=-1, keepdims=True), 1e-24)) for x in kk]
    bv = [kk[h] * a[:, sls[h]] for h in hs]
    k_h = [kf[:, sl] for sl in sls]
    v_h = [v[:, sl] for sl in sls]
    at = [-kk[h] * g_prev[:, sls[h]] for h in hs]
    rt = [r[:, sl] * g_t[:, sl] for sl in sls]
    bt = [bv[h] * g_inv[:, sls[h]] for h in hs]
    kt = [k_h[h] * g_inv[:, sls[h]] for h in hs]
    bh = [bv[h] * g_end[:, sls[h]] for h in hs]
    kh = [k_h[h] * g_end[:, sls[h]] for h in hs]

    ar = [jnp.concatenate([at[h], rt[h]], axis=0) for h in hs]
    pb = [_dot_nt(ar[h], bt[h]) for h in hs]
    pk = [_dot_nt(ar[h], kt[h]) for h in hs]
    l_ab = [jnp.where(low_strict, x[:rows], 0.0) for x in pb]
    l_ak = [jnp.where(low_strict, x[:rows], 0.0) for x in pk]
    a_rb = [jnp.where(low_incl, x[rows:], 0.0) for x in pb]
    a_rk = [jnp.where(low_incl, x[rows:], 0.0) for x in pk]
    yield

    tm = [eye + x for x in l_ab]
    pw = [_dot(x, x) for x in l_ab]
    n_dbl = int(np.log2(ls)) - 1
    for it in range(n_dbl):
        yield
        if it < n_dbl - 1:
            z = [_dot(jnp.concatenate([tm[h], pw[h]], axis=0), pw[h]) for h in hs]
            tm = [tm[h] + z[h][:rows] for h in hs]
            pw = [z[h][rows:] for h in hs]
        else:
            tm = [tm[h] + _dot(tm[h], pw[h]) for h in hs]

    if nseq == 1:
        ars = [_dot_nt(ar[h], s_ref[0, h]) for h in hs]
        as0 = [x[:rows] for x in ars]
        rs0 = [x[rows:] for x in ars]
    else:
        as0, rs0 = [], []
        for h in hs:
            zs = [_dot_nt(jnp.concatenate([at[h][b * ls:(b + 1) * ls], rt[h][b * ls:(b + 1) * ls]], axis=0),
                          s_ref[b, h]) for b in range(nseq)]
            as0.append(jnp.concatenate([x[:ls] for x in zs], axis=0))
            rs0.append(jnp.concatenate([x[ls:] for x in zs], axis=0))

    lv = [_dot(l_ak[h], v_h[h]) for h in hs]
    yield
    u = [_dot(tm[h], as0[h] + lv[h]) for h in hs]
    yield
    y = [rs0[h] + _dot(a_rb[h], u[h]) + _dot(a_rk[h], v_h[h]) for h in hs]
    yield

    for h in hs:
        for b in range(nseq):
            rb = slice(b * ls, (b + 1) * ls)
            uv = jnp.concatenate([u[h][rb], v_h[h][rb]], axis=0)
            bk = jnp.concatenate([bh[h][rb], kh[h][rb]], axis=0)
            s_ref[b, h] = s_ref[b, h] * g_tot[b * ls:b * ls + 1, sls[h]] + _dot_tn(uv, bk)

    outs = []
    for h in hs:
        ym = jnp.mean(y[h], axis=-1, keepdims=True)
        yc = y[h] - ym
        yv = jnp.mean(yc * yc, axis=-1, keepdims=True)
        yn = yc * lax.rsqrt(yv + GN_EPS) * lnx_g[:, sls[h]] + lnx_b[:, sls[h]]
        bonus = jnp.sum(rkr[:, sls[h]], axis=-1, keepdims=True) * v_h[h]
        outs.append((yn + bonus) * g[:, sls[h]])
    o_ref[...] = jnp.concatenate(outs, axis=-1).astype(o_ref.dtype)


def _interleave(gens):
    for _ in itertools.zip_longest(*gens):
        pass


def _rwkv_prompt_kernel(p_ref, *refs):
    prm_refs, (o_ref, s_ref, sh_ref, carry_ref) = refs[:11], refs[11:]
    c = pl.program_id(1)

    @pl.when(c == 0)
    def _():
        s_ref[...] = jnp.zeros_like(s_ref)
        carry_ref[...] = jnp.zeros_like(carry_ref)

    first = lax.broadcasted_iota(jnp.int32, (RW_ROWS, 1), 0) == 0
    prm = tuple(x[...] for x in prm_refs)
    groups = range(p_ref.shape[0])
    _interleave([_rwkv_rows(p_ref[g], first, carry_ref[8 * g:8 * g + 1, :], s_ref.at[pl.ds(g, 1)],
                            o_ref.at[g], prm, nseq=1) for g in groups])
    for g in groups:
        last = p_ref[g, RW_ROWS - 1:RW_ROWS, :]
        carry_ref[8 * g:8 * g + 1, :] = last
        sh_ref[g] = last


def _rwkv_sample_kernel(p_ref, prev_ref, s0_ref, *refs):
    prm_refs, (o_ref, s_ref, sh_ref) = refs[:11], refs[11:]
    nseq = s0_ref.shape[0] // p_ref.shape[0]
    ls = RW_ROWS // nseq
    s_ref[...] = s0_ref[...]
    first = (lax.broadcasted_iota(jnp.int32, (RW_ROWS, 1), 0) & (ls - 1)) == 0
    prm = tuple(x[...] for x in prm_refs)
    groups = range(p_ref.shape[0])
    _interleave([_rwkv_rows(p_ref[g], first, prev_ref[g], s_ref.at[pl.ds(g * nseq, nseq)], o_ref.at[g],
                            prm, nseq=nseq) for g in groups])
    for g in groups:
        for b in range(nseq):
            sh_ref[g * nseq + b:g * nseq + b + 1, :] = p_ref[g, (b + 1) * ls - 1:(b + 1) * ls, :]


def _rw_param_specs():
    shapes = [(1, RW_COLS), (1, BRANCH_W), (RW_W_LORA, BRANCH_W), (1, BRANCH_W), (RW_A_LORA, BRANCH_W),
              (RW_G_LORA, BRANCH_W), (1, BRANCH_W), (1, BRANCH_W), (1, BRANCH_W), (1, BRANCH_W), (1, BRANCH_W)]
    return [_full(s) for s in shapes]


def _rwkv_prompt(p_rw, prm, batch, seq):
    nc = seq // RW_ROWS
    g = RW_GROUPS
    p3 = p_rw.reshape(batch, seq, RW_COLS)
    o, s_new, sh = pl.pallas_call(
        _rwkv_prompt_kernel,
        grid=(batch // g, nc),
        in_specs=[pl.BlockSpec((g, RW_ROWS, RW_COLS), lambda b, c: (b, c, 0))] + _rw_param_specs(),
        out_specs=[pl.BlockSpec((g, RW_ROWS, BRANCH_W), lambda b, c: (b, c, 0)),
                   pl.BlockSpec((g, RW_HEADS, RW_HD, RW_HD), lambda b, c: (b, 0, 0, 0)),
                   pl.BlockSpec((g, 1, RW_COLS), lambda b, c: (b, 0, 0))],
        out_shape=[jax.ShapeDtypeStruct((batch, seq, BRANCH_W), BF16),
                   jax.ShapeDtypeStruct((batch, RW_HEADS, RW_HD, RW_HD), F32),
                   jax.ShapeDtypeStruct((batch, 1, RW_COLS), F32)],
        scratch_shapes=[pltpu.VMEM((8 * g, RW_COLS), F32)],
        compiler_params=_params(("arbitrary", "arbitrary"), 56),
        name="rwkv_prompt",
    )(p3, *prm)
    return o.reshape(batch * seq, BRANCH_W), s_new, sh.reshape(batch, RW_COLS)


def _rwkv_sample(p_rw, prev_rows, s0, prm, seq):
    n = p_rw.shape[0]
    nseq = RW_ROWS // seq
    batch = n // seq
    g = RW_GROUPS
    ng = n // RW_ROWS
    o, s_new, sh = pl.pallas_call(
        _rwkv_sample_kernel,
        grid=(ng // g,),
        in_specs=[pl.BlockSpec((g, RW_ROWS, RW_COLS), lambda i: (i, 0, 0)),
                  pl.BlockSpec((g, RW_ROWS, RW_COLS), lambda i: (i, 0, 0)),
                  pl.BlockSpec((g * nseq, RW_HEADS, RW_HD, RW_HD), lambda i: (i, 0, 0, 0))] + _rw_param_specs(),
        out_specs=[pl.BlockSpec((g, RW_ROWS, BRANCH_W), lambda i: (i, 0, 0)),
                   pl.BlockSpec((g * nseq, RW_HEADS, RW_HD, RW_HD), lambda i: (i, 0, 0, 0)),
                   pl.BlockSpec((g * nseq, RW_COLS), lambda i: (i, 0))],
        out_shape=[jax.ShapeDtypeStruct((ng, RW_ROWS, BRANCH_W), BF16),
                   jax.ShapeDtypeStruct((batch, RW_HEADS, RW_HD, RW_HD), F32),
                   jax.ShapeDtypeStruct((batch, RW_COLS), F32)],
        compiler_params=_params(("arbitrary",), 56),
        name="rwkv_sample",
    )(p_rw.reshape(ng, RW_ROWS, RW_COLS), prev_rows.reshape(ng, RW_ROWS, RW_COLS), s0, *prm)
    return o.reshape(n, BRANCH_W), s_new, sh


def _attend(q, k, v):
    s = _dot_nt(q, k) * (XA_HD ** -0.5)
    e = jnp.exp(s - jnp.max(s, axis=-1, keepdims=True))
    pr = e / jnp.sum(e, axis=-1, keepdims=True)
    return _dot(pr, v)


def _xattn_prompt_kernel(q_ref, k_ref, v_ref, o_ref):
    for h in range(XA_HEADS):
        sl = slice(h * XA_HD, (h + 1) * XA_HD)
        o_ref[:, sl] = _attend(q_ref[:, sl], k_ref[:, sl], v_ref[:, sl]).astype(o_ref.dtype)


def _xattn_prompt(q, mk, mv, batch, seq, tq=512):
    nt = seq // tq
    return pl.pallas_call(
        _xattn_prompt_kernel,
        grid=(batch, nt),
        in_specs=[pl.BlockSpec((tq, BRANCH_W), lambda b, i: (b * nt + i, 0)),
                  pl.BlockSpec((N_MEM, BRANCH_W), lambda b, i: (b, 0)),
                  pl.BlockSpec((N_MEM, BRANCH_W), lambda b, i: (b, 0))],
        out_specs=pl.BlockSpec((tq, BRANCH_W), lambda b, i: (b * nt + i, 0)),
        out_shape=jax.ShapeDtypeStruct((batch * seq, BRANCH_W), BF16),
        compiler_params=_params(("arbitrary", "arbitrary"), 32),
        name="xattn_prompt",
    )(q, mk, mv)


def _xattn_sample_kernel(q_ref, k_ref, v_ref, o_ref):
    nb = k_ref.shape[0]
    ls = q_ref.shape[0] // nb
    for b in range(nb):
        for h in range(XA_HEADS):
            sl = slice(h * XA_HD, (h + 1) * XA_HD)
            o_ref[b * ls:(b + 1) * ls, sl] = _attend(
                q_ref[b * ls:(b + 1) * ls, sl], k_ref[b, :, sl], v_ref[b, :, sl]).astype(o_ref.dtype)


def _xattn_sample(q, mk, mv, seq, nb=8):
    batch = mk.shape[0]
    mem_spec = pl.BlockSpec((nb, N_MEM, BRANCH_W), lambda i: (i, 0, 0))
    return pl.pallas_call(
        _xattn_sample_kernel,
        grid=(batch // nb,),
        in_specs=[pl.BlockSpec((nb * seq, BRANCH_W), lambda i: (i, 0)), mem_spec, mem_spec],
        out_specs=pl.BlockSpec((nb * seq, BRANCH_W), lambda i: (i, 0)),
        out_shape=jax.ShapeDtypeStruct((batch * seq, BRANCH_W), BF16),
        compiler_params=_params(("arbitrary",), 40),
        name="xattn_sample",
    )(q, mk, mv)


def _merge_kernel(x_ref, gm_ref, rw_ref, xa_ref, gt_ref, gb_ref, wb_ref, wo_ref, n2_ref,
                  wr_hi_ref, wr_lo_ref, br_ref, h_ref, hn_ref, route_ref, ridx_ref, cnt_ref):
    merged = None
    for n, br in enumerate((gm_ref, rw_ref, xa_ref)):
        cs = slice(n * D_MODEL, (n + 1) * D_MODEL)
        up = jnp.dot(br[...], wb_ref[n], preferred_element_type=F32)
        term = _sigmoid(gt_ref[:, cs] + gb_ref[:, cs]) * up
        merged = term if merged is None else merged + term
    h = x_ref[...] + _dot(merged, wo_ref[...])
    h_ref[...] = h
    hn = _rmsnorm(h, n2_ref[...])
    _rows_to_tiles(hn_ref, hn)

    hn_hi = hn.astype(BF16)
    hn_lo = (hn - hn_hi.astype(F32)).astype(BF16)
    logits = (jnp.dot(hn_hi, wr_hi_ref[...], preferred_element_type=F32)
              + jnp.dot(hn_lo, wr_hi_ref[...], preferred_element_type=F32)
              + jnp.dot(hn_hi, wr_lo_ref[...], preferred_element_type=F32)) + br_ref[...]
    lane = lax.broadcasted_iota(jnp.int32, logits.shape, 1)
    neg = -jnp.inf
    big = jnp.int32(1 << 20)
    gmask = (lane >= N_EXPERTS) & (lane < N_EXPERTS + N_GROUPS)
    gl = jnp.where(gmask, logits, neg)
    gmax = jnp.max(gl, axis=-1, keepdims=True)
    gsel = jnp.min(jnp.where(gl == gmax, lane, big), axis=-1, keepdims=True) - N_EXPERTS
    gsum = jnp.sum(jnp.where(gmask, jnp.exp(logits - gmax), 0.0), axis=-1, keepdims=True)
    pg_top = 1.0 / gsum
    emask = (lane >= gsel * EXP_PER_GROUP) & (lane < (gsel + 1) * EXP_PER_GROUP)
    el = jnp.where(emask, logits, neg)
    m1 = jnp.max(el, axis=-1, keepdims=True)
    i1 = jnp.min(jnp.where(el == m1, lane, big), axis=-1, keepdims=True)
    el2 = jnp.where(lane == i1, neg, el)
    m2 = jnp.max(el2, axis=-1, keepdims=True)
    i2 = jnp.min(jnp.where(el2 == m2, lane, big), axis=-1, keepdims=True)
    t2 = jnp.exp(m2 - m1)
    w1 = pg_top / (1.0 + t2)
    w2 = pg_top * t2 / (1.0 + t2)

    @pl.when(pl.program_id(0) == 0)
    def _():
        cnt_ref[...] = jnp.zeros_like(cnt_ref)

    tm = logits.shape[0]
    onehot = jnp.where(lane == i1, 1.0, 0.0) + jnp.where(lane == i2, 1.0, 0.0)
    ri = lax.broadcasted_iota(jnp.int32, (tm, tm), 0)
    ci = lax.broadcasted_iota(jnp.int32, (tm, tm), 1)
    earlier = jnp.where(ci < ri, 1.0, 0.0).astype(BF16)
    base = cnt_ref[0:1, :]
    seen = base + jnp.dot(earlier, onehot.astype(BF16), preferred_element_type=F32)
    r1 = jnp.sum(jnp.where(lane == i1, seen, 0.0), axis=-1, keepdims=True)
    r2 = jnp.sum(jnp.where(lane == i2, seen, 0.0), axis=-1, keepdims=True)
    cnt_ref[0:1, :] = base + jnp.sum(onehot, axis=0, keepdims=True)

    route = jnp.zeros(logits.shape, F32)
    for col, val in ((ROUTE_E1, i1.astype(F32)), (ROUTE_E2, i2.astype(F32)), (ROUTE_W1, w1), (ROUTE_W2, w2),
                     (ROUTE_R1, r1), (ROUTE_R2, r2)):
        route = jnp.where(lane == col, val, route)
    route_ref[...] = route
    ridx_ref[...] = jnp.transpose(route)[0:8, :].astype(jnp.int32)


def _merge(x, o_gm, o_rw, o_xa, p_gate, gate_b, w_branch, w_out, n2_g, wr_hi, wr_lo, b_r, tm=256):
    n = x.shape[0]
    row = lambda wd: pl.BlockSpec((tm, wd), lambda i: (i, 0))
    return pl.pallas_call(
        _merge_kernel,
        grid=(n // tm,),
        in_specs=[row(D_MODEL), row(BRANCH_W), row(BRANCH_W), row(BRANCH_W), row(GATE_COLS),
                  _full((1, GATE_COLS)), _full((N_BRANCH, BRANCH_W, D_MODEL)), _full((D_MODEL, D_MODEL)),
                  _full((1, D_MODEL)), _full((D_MODEL, LANES)), _full((D_MODEL, LANES)), _full((1, LANES))],
        out_specs=[row(D_MODEL), pl.BlockSpec((tm,) + ROW_TILE, lambda i: (i, 0, 0)), row(LANES),
                   pl.BlockSpec((8, tm), lambda i: (0, i)), _full((8, LANES))],
        out_shape=[jax.ShapeDtypeStruct((n, D_MODEL), F32), jax.ShapeDtypeStruct((n,) + ROW_TILE, F32),
                   jax.ShapeDtypeStruct((n, LANES), F32), jax.ShapeDtypeStruct((8, n), jnp.int32),
                   jax.ShapeDtypeStruct((8, LANES), F32)],
        compiler_params=_params(("arbitrary",), 40),
        name="merge",
    )(x, o_gm, o_rw, o_xa, p_gate, gate_b, w_branch, w_out, n2_g, wr_hi, wr_lo, b_r)


def _route_plan(counts, n, tm):
    n_tiles = (2 * n) // tm + N_EXPERTS
    cnt = counts[0, :N_EXPERTS].astype(jnp.int32)
    tiles = (cnt + tm - 1) // tm
    tile_end = jnp.cumsum(tiles)
    row_start = (tile_end - tiles) * tm
    n_used = tile_end[N_EXPERTS - 1:]
    pad_tile_row = jnp.where(tiles > 0, (tile_end - 1) * tm, -1)
    t = jnp.minimum(jnp.arange(n_tiles, dtype=jnp.int32), n_used - 1)
    tile_expert = jnp.sum((t[:, None] >= tile_end[None, :]).astype(jnp.int32), axis=1)
    return row_start, pad_tile_row, tile_expert, n_used, n_tiles


def _row_copy(src_ref, src_row, dst_ref, dst_row, sem):
    return pltpu.make_async_copy(src_ref.at[pl.ds(src_row, 1)], dst_ref.at[pl.ds(dst_row, 1)], sem)


def _tile_rows_to_smem(rs_ref, ridx_ref, pos_v, pos_s, sem):
    ridx = ridx_ref[...]
    for c, (ce, cr) in enumerate(((ROUTE_E1, ROUTE_R1), (ROUTE_E2, ROUTE_R2))):
        e = ridx[ce:ce + 1, :]
        pos = ridx[cr:cr + 1, :]
        for x in range(N_EXPERTS):
            pos = pos + jnp.where(e == x, rs_ref[x], 0)
        pos_v[c:c + 1, :] = pos
    cp = pltpu.make_async_copy(pos_v, pos_s, sem)
    cp.start()
    cp.wait()


def _dispatch_kernel(rs_ref, pad_ref, nu_ref, ridx_ref, hn_ref, xs_ref, zero_ref, pos_v, pos_s, sem, psem):
    tm = hn_ref.shape[0]
    n_tiles = xs_ref.shape[0] // tm

    @pl.when(pl.program_id(0) == 0)
    def _():
        zero_ref[...] = jnp.zeros_like(zero_ref)

        def fill(row):
            return pltpu.make_async_copy(zero_ref, xs_ref.at[pl.ds(pl.multiple_of(row, tm), tm)], sem)

        def each_fill(act):
            for e in range(N_EXPERTS):
                @pl.when(pad_ref[e] >= 0)
                def _():
                    act(fill(pad_ref[e]))

            def tail(t, c):
                act(fill(t * tm))
                return c

            lax.fori_loop(nu_ref[0], n_tiles, tail, 0)

        each_fill(lambda f: f.start())
        each_fill(lambda f: f.wait())

    _tile_rows_to_smem(rs_ref, ridx_ref, pos_v, pos_s, psem)

    def issue(i, c):
        _row_copy(hn_ref, i, xs_ref, pos_s[0, i], sem).start()
        _row_copy(hn_ref, i, xs_ref, pos_s[1, i], sem).start()
        return c

    def drain(i, c):
        _row_copy(hn_ref, 0, xs_ref, 0, sem).wait()
        return c

    lax.fori_loop(0, tm, issue, 0, unroll=8)
    lax.fori_loop(0, 2 * tm, drain, 0, unroll=8)


def _dispatch(hn, ridx, row_start, pad_tile_row, n_used, n_rows, tm):
    n = hn.shape[0]
    return pl.pallas_call(
        _dispatch_kernel,
        grid_spec=pltpu.PrefetchScalarGridSpec(
            num_scalar_prefetch=3,
            grid=(n // tm,),
            in_specs=[pl.BlockSpec((8, tm), lambda i, *_: (0, i)),
                      pl.BlockSpec((tm,) + ROW_TILE, lambda i, *_: (i, 0, 0))],
            out_specs=pl.BlockSpec(memory_space=pl.ANY),
            scratch_shapes=[pltpu.VMEM((tm,) + ROW_TILE, F32), pltpu.VMEM((2, tm), jnp.int32),
                            pltpu.SMEM((2, tm), jnp.int32), pltpu.SemaphoreType.DMA, pltpu.SemaphoreType.DMA]),
        out_shape=jax.ShapeDtypeStruct((n_rows,) + ROW_TILE, F32),
        compiler_params=_params(("arbitrary",), 32),
        name="moe_dispatch",
    )(row_start, pad_tile_row, n_used, ridx, hn)


def _experts_kernel(te_ref, nu_ref, x_ref, wg_ref, wu_ref, wd_ref, y_ref):
    del te_ref
    used = pl.program_id(0) < nu_ref[0]

    @pl.when(used)
    def _():
        xb = _tiles_to_rows(x_ref).astype(BF16)
        gate = jnp.dot(xb, wg_ref[...], preferred_element_type=F32)
        up = jnp.dot(xb, wu_ref[...], preferred_element_type=F32)
        _rows_to_tiles(y_ref, _dot(gate * _sigmoid(gate) * up, wd_ref[...]))

    @pl.when(jnp.logical_not(used))
    def _():
        y_ref[...] = jnp.zeros_like(y_ref)


def _experts(xs, tile_expert, n_used, wg, wu, wd, tm):
    n_rows = xs.shape[0]
    rows = lambda t, te, nu: (jnp.minimum(t, nu[0] - 1), 0, 0)
    return pl.pallas_call(
        _experts_kernel,
        grid_spec=pltpu.PrefetchScalarGridSpec(
            num_scalar_prefetch=2,
            grid=(n_rows // tm,),
            in_specs=[pl.BlockSpec((tm,) + ROW_TILE, rows),
                      pl.BlockSpec((None, D_MODEL, EXP_FF), lambda t, te, nu: (te[t], 0, 0)),
                      pl.BlockSpec((None, D_MODEL, EXP_FF), lambda t, te, nu: (te[t], 0, 0)),
                      pl.BlockSpec((None, EXP_FF, D_MODEL), lambda t, te, nu: (te[t], 0, 0))],
            out_specs=pl.BlockSpec((tm,) + ROW_TILE, lambda t, te, nu: (t, 0, 0))),
        out_shape=jax.ShapeDtypeStruct((n_rows,) + ROW_TILE, F32),
        compiler_params=_params(("arbitrary",), 40),
        name="moe_experts",
    )(tile_expert, n_used, xs, wg, wu, wd)


def _combine_kernel(rs_ref, ridx_ref, h_ref, route_ref, fg_ref, ys_ref, y_ref, o1_ref, o2_ref, pos_v, pos_s,
                    sem, psem):
    tm = h_ref.shape[0]
    _tile_rows_to_smem(rs_ref, ridx_ref, pos_v, pos_s, psem)

    def issue(i, c):
        _row_copy(ys_ref, pos_s[0, i], o1_ref, i, sem).start()
        _row_copy(ys_ref, pos_s[1, i], o2_ref, i, sem).start()
        return c

    def drain(i, c):
        _row_copy(ys_ref, 0, o1_ref, 0, sem).wait()
        return c

    lax.fori_loop(0, tm, issue, 0, unroll=8)
    lax.fori_loop(0, 2 * tm, drain, 0, unroll=8)
    rt = route_ref[...]
    lane = lax.broadcasted_iota(jnp.int32, rt.shape, 1)
    w1 = jnp.sum(jnp.where(lane == ROUTE_W1, rt, 0.0), axis=-1, keepdims=True)
    w2 = jnp.sum(jnp.where(lane == ROUTE_W2, rt, 0.0), axis=-1, keepdims=True)
    y_ref[...] = _rmsnorm(h_ref[...] + w1 * _tiles_to_rows(o1_ref) + w2 * _tiles_to_rows(o2_ref), fg_ref[...])


def _combine(h, route, ridx, row_start, ys, final_g, tm):
    n = h.shape[0]
    return pl.pallas_call(
        _combine_kernel,
        grid_spec=pltpu.PrefetchScalarGridSpec(
            num_scalar_prefetch=1,
            grid=(n // tm,),
            in_specs=[pl.BlockSpec((8, tm), lambda i, rs: (0, i)),
                      pl.BlockSpec((tm, D_MODEL), lambda i, rs: (i, 0)),
                      pl.BlockSpec((tm, LANES), lambda i, rs: (i, 0)),
                      pl.BlockSpec((1, D_MODEL), lambda i, rs: (0, 0)),
                      pl.BlockSpec(memory_space=pl.ANY)],
            out_specs=pl.BlockSpec((tm, D_MODEL), lambda i, rs: (i, 0)),
            scratch_shapes=[pltpu.VMEM((tm,) + ROW_TILE, F32), pltpu.VMEM((tm,) + ROW_TILE, F32),
                            pltpu.VMEM((2, tm), jnp.int32), pltpu.SMEM((2, tm), jnp.int32),
                            pltpu.SemaphoreType.DMA, pltpu.SemaphoreType.DMA]),
        out_shape=jax.ShapeDtypeStruct((n, D_MODEL), F32),
        compiler_params=_params(("arbitrary",), 32),
        name="moe_combine",
    )(row_start, ridx, h, route, final_g, ys)


def _moe(h, hn, route, ridx, counts, wg, wu, wd, final_g, tm=256):
    row_start, pad_tile_row, tile_expert, n_used, n_tiles = _route_plan(counts, h.shape[0], tm)
    xs = _dispatch(hn, ridx, row_start, pad_tile_row, n_used, n_tiles * tm, tm)
    ys = _experts(xs, tile_expert, n_used, wg, wu, wd, tm)
    return _combine(h, route, ridx, row_start, ys, final_g, tm)


def _layer(x2d, mem_k, mem_v, rw_state, w, *, prompt, batch, seq):
    n = x2d.shape[0]
    tm_in = 256
    p_gm, p_rw, p_q, p_gate = _in_proj(x2d, w["norm1_g"], w["w_in_segs"], tm_in)
    if prompt:
        o_gm = _gmlp(p_gm, w["gm_ln_g"], w["gm_ln_b"], w["gm_mix_prompt"], w["gm_bias_prompt"], False)[0]
        vn = None
        o_rw, s_new, sh_new = _rwkv_prompt(p_rw, w["rw_prm"], batch, seq)
        o_xa = _xattn_prompt(p_q, mem_k, mem_v, batch, seq)
    else:
        o_gm, vn = _gmlp(p_gm, w["gm_ln_g"], w["gm_ln_b"], w["gm_mix_sample"], w["gm_bias_sample"], True)
        s0, shift = rw_state
        prev_rows = jnp.pad(shift[:, None, :], ((0, 0), (0, seq - 1), (0, 0))).reshape(n, RW_COLS)
        o_rw, s_new, sh_new = _rwkv_sample(p_rw, prev_rows, s0, w["rw_prm"], seq)
        o_xa = _xattn_sample(p_q, mem_k, mem_v, seq)
    h, hn, route, ridx, counts = _merge(x2d, o_gm, o_rw, o_xa, p_gate, w["gate_b"], w["w_branch"], w["w_out"],
                                        w["norm2_g"], w["wr_hi"], w["wr_lo"], w["b_r"])
    y = _moe(h, hn, route, ridx, counts, w["e_wg"], w["e_wu"], w["e_wd"], w["final_g"])
    return y, s_new, sh_new, vn


def kernel(x_prompt, x_sample, state_rwkv_S, state_rwkv_shift, cache_mem_k, cache_mem_v, mem_prompt, norm1_g, w_in, gate_b, gm_ln_g, gm_ln_b, gm_ws, gm_bs, rw_mu, rw_w0, rw_w2, rw_a0, rw_a2, rw_g2, rw_k_k, rw_k_a, rw_r_k, rw_lnx_g, rw_lnx_b, xa_wk, xa_wv, w_branch, w_out, norm2_g, rg_w, rg_b, re_w, re_b, e_wg, e_wu, e_wd, final_g):
    bp, seq_p, _ = x_prompt.shape
    bs, seq_s, _ = x_sample.shape
    depth = w_in.shape[0]
    assert depth == 1 and seq_p % CHUNK == 0 and RW_ROWS % seq_s == 0 and CHUNK % seq_s == 0

    l = 0
    row = lambda a: a.reshape(1, -1)
    seg = (0, 2 * BRANCH_W, 2 * BRANCH_W + RW_COLS, 3 * BRANCH_W + RW_COLS, 3 * BRANCH_W + RW_COLS + GATE_COLS)
    w_causal = jnp.tril(gm_ws[l])
    nrep = CHUNK // seq_s
    blk = w_causal[:, :seq_s, :seq_s]
    eye_rep = jnp.eye(nrep, dtype=F32)
    mix_sample = jnp.einsum("ab,gts->gatbs", eye_rep, blk).reshape(GM_GROUPS, CHUNK, CHUNK)
    bias_prompt = jnp.repeat(gm_bs[l].T, GM_HD, axis=1)
    bias_sample = jnp.tile(bias_prompt[:seq_s], (nrep, 1))

    wr = jnp.zeros((D_MODEL, LANES), F32)
    wr = wr.at[:, :N_EXPERTS].set(jnp.transpose(re_w[l], (1, 0, 2)).reshape(D_MODEL, N_EXPERTS))
    wr = wr.at[:, N_EXPERTS:N_EXPERTS + N_GROUPS].set(rg_w[l])
    wr_hi = wr.astype(BF16)
    wr_lo = (wr - wr_hi.astype(F32)).astype(BF16)
    b_r = jnp.zeros((1, LANES), F32)
    b_r = b_r.at[0, :N_EXPERTS].set(re_b[l].reshape(-1)).at[0, N_EXPERTS:N_EXPERTS + N_GROUPS].set(rg_b[l])

    w = dict(
        norm1_g=row(norm1_g[l]),
        w_in_segs=[w_in[l][:, a:b].astype(BF16) for a, b in zip(seg[:-1], seg[1:])],
        gm_ln_g=row(gm_ln_g[l]), gm_ln_b=row(gm_ln_b[l]),
        gm_mix_prompt=w_causal.astype(BF16), gm_bias_prompt=bias_prompt,
        gm_mix_sample=mix_sample.astype(BF16), gm_bias_sample=bias_sample,
        rw_prm=(row(rw_mu[l]), row(rw_w0[l]), rw_w2[l].astype(BF16), row(rw_a0[l]), rw_a2[l].astype(BF16),
                rw_g2[l].astype(BF16), row(rw_k_k[l]), row(rw_k_a[l]), row(rw_r_k[l]),
                row(rw_lnx_g[l]), row(rw_lnx_b[l])),
        gate_b=row(gate_b[l]), w_branch=w_branch[l].astype(BF16), w_out=w_out[l].astype(BF16),
        norm2_g=row(norm2_g[l]), wr_hi=wr_hi, wr_lo=wr_lo, b_r=b_r,
        e_wg=e_wg[l].reshape(N_EXPERTS, D_MODEL, EXP_FF).astype(BF16),
        e_wu=e_wu[l].reshape(N_EXPERTS, D_MODEL, EXP_FF).astype(BF16),
        e_wd=e_wd[l].reshape(N_EXPERTS, EXP_FF, D_MODEL).astype(BF16),
        final_g=row(final_g),
    )

    mk_p, mv_p = _mem_kv(mem_prompt.reshape(bp * N_MEM, D_MODEL), xa_wk[l].astype(BF16), xa_wv[l].astype(BF16))
    yp, sp, shp, _ = _layer(x_prompt.reshape(bp * seq_p, D_MODEL), mk_p, mv_p, None, w,
                            prompt=True, batch=bp, seq=seq_p)
    ys, ss, shs, vs = _layer(x_sample.reshape(bs * seq_s, D_MODEL),
                             cache_mem_k[l].reshape(bs, N_MEM, BRANCH_W).astype(BF16),
                             cache_mem_v[l].reshape(bs, N_MEM, BRANCH_W).astype(BF16),
                             (state_rwkv_S[l], state_rwkv_shift[l]), w, prompt=False, batch=bs, seq=seq_s)

    return (yp.reshape(bp, seq_p, D_MODEL), ys.reshape(bs, seq_s, D_MODEL),
            sp[None], shp[None],
            mk_p.reshape(1, bp, N_MEM, XA_HEADS, XA_HD), mv_p.reshape(1, bp, N_MEM, XA_HEADS, XA_HD),
            ss[None], shs[None], vs.reshape(1, bs, seq_s, BRANCH_W))
```

```python
import itertools

import numpy as np
import jax
import jax.numpy as jnp
from jax import lax
from jax.experimental import pallas as pl
from jax.experimental.pallas import tpu as pltpu

F32 = jnp.float32
BF16 = jnp.bfloat16

D_MODEL = 1024
BRANCH_W = 512
CHUNK = 128
GM_GROUPS = 8
GM_HD = BRANCH_W // GM_GROUPS
RW_HEADS = 8
RW_HD = BRANCH_W // RW_HEADS
RW_W_LORA = 64
RW_A_LORA = 64
RW_G_LORA = 128
RW_COLS = 3 * BRANCH_W + RW_W_LORA + RW_A_LORA + RW_G_LORA
XA_HEADS = 4
XA_HD = BRANCH_W // XA_HEADS
N_MEM = 256
N_BRANCH = 3
GATE_COLS = N_BRANCH * D_MODEL
N_GROUPS = 4
EXP_PER_GROUP = 8
N_EXPERTS = N_GROUPS * EXP_PER_GROUP
EXP_FF = 512
RMS_EPS = 1e-6
LN_EPS = 1e-5
GN_EPS = 64e-5

LANES = 128
RW_ROWS = 64
RW_GROUPS = 2
MIB = 1024 * 1024
ROUTE_E1, ROUTE_E2, ROUTE_W1, ROUTE_W2, ROUTE_R1, ROUTE_R2 = range(6)


def _dot(a, b):
    return jnp.dot(a.astype(BF16), b.astype(BF16), preferred_element_type=F32)


def _dot_nt(a, b):
    return lax.dot_general(a.astype(BF16), b.astype(BF16), (((1,), (1,)), ((), ())),
                           preferred_element_type=F32)


def _dot_tn(a, b):
    return lax.dot_general(a.astype(BF16), b.astype(BF16), (((0,), (0,)), ((), ())),
                           preferred_element_type=F32)


def _sigmoid(x):
    return 1.0 / (1.0 + jnp.exp(-x))


def _gelu(x):
    c = np.float32(np.sqrt(2.0 / np.pi))
    return x * (0.5 * (1.0 + jnp.tanh(c * (x + 0.044715 * (x * x * x)))))


def _softplus(x):
    return jnp.maximum(x, 0.0) + jnp.log(1.0 + jnp.exp(-jnp.abs(x)))


def _rmsnorm(x, g):
    return x * lax.rsqrt(jnp.mean(x * x, axis=-1, keepdims=True) + RMS_EPS) * g


def _split3(x):
    hi = x.astype(BF16)
    r1 = x - hi.astype(F32)
    mid = r1.astype(BF16)
    lo = (r1 - mid.astype(F32)).astype(BF16)
    return hi, mid, lo


def _params(sem, vmem_mib):
    return pltpu.CompilerParams(dimension_semantics=sem, vmem_limit_bytes=vmem_mib * MIB)


def _full(shape):
    nd = len(shape)
    return pl.BlockSpec(shape, lambda *_: (0,) * nd)


def _in_proj_kernel(x_ref, g_ref, wgm_ref, wrw_ref, wq_ref, wgt_ref, gm_ref, rw_ref, q_ref, gt_ref):
    xb = _rmsnorm(x_ref[...], g_ref[...]).astype(BF16)
    gm_ref[...] = jnp.dot(xb, wgm_ref[...], preferred_element_type=F32)
    rw_ref[...] = jnp.dot(xb, wrw_ref[...], preferred_element_type=F32)
    q_ref[...] = jnp.dot(xb, wq_ref[...], preferred_element_type=F32)
    gt_ref[...] = jnp.dot(xb, wgt_ref[...], preferred_element_type=F32)


def _in_proj(x, g, w_segs, tm):
    n = x.shape[0]
    widths = [w.shape[1] for w in w_segs]
    return pl.pallas_call(
        _in_proj_kernel,
        grid=(n // tm,),
        in_specs=[pl.BlockSpec((tm, D_MODEL), lambda i: (i, 0)), _full((1, D_MODEL))]
        + [pl.BlockSpec((D_MODEL, wd), lambda i: (0, 0), pipeline_mode=pl.Buffered(1)) for wd in widths],
        out_specs=[pl.BlockSpec((tm, wd), lambda i: (i, 0)) for wd in widths],
        out_shape=[jax.ShapeDtypeStruct((n, wd), F32) for wd in widths],
        compiler_params=_params(("arbitrary",), 48),
        name="in_proj",
    )(x, g, *w_segs)


def _mem_kv_kernel(m_ref, wk_ref, wv_ref, k_ref, v_ref):
    mb = m_ref[...].astype(BF16)
    k_ref[...] = jnp.dot(mb, wk_ref[...], preferred_element_type=F32)
    v_ref[...] = jnp.dot(mb, wv_ref[...], preferred_element_type=F32)


def _mem_kv(mem, wk, wv, tm=512):
    n = mem.shape[0]
    return pl.pallas_call(
        _mem_kv_kernel,
        grid=(n // tm,),
        in_specs=[pl.BlockSpec((tm, D_MODEL), lambda i: (i, 0)),
                  _full((D_MODEL, BRANCH_W)), _full((D_MODEL, BRANCH_W))],
        out_specs=[pl.BlockSpec((tm, BRANCH_W), lambda i: (i, 0))] * 2,
        out_shape=[jax.ShapeDtypeStruct((n, BRANCH_W), F32)] * 2,
        compiler_params=_params(("arbitrary",), 32),
        name="mem_kv",
    )(mem, wk, wv)


def _gmlp_kernel(pu_ref, pv_ref, lng_ref, lnb_ref, wmix_ref, bmix_ref, o_ref, *vn_refs):
    u = _gelu(pu_ref[...])
    vf = _gelu(pv_ref[...])
    mu = jnp.mean(vf, axis=-1, keepdims=True)
    vc = vf - mu
    var = jnp.mean(vc * vc, axis=-1, keepdims=True)
    vn = vc * lax.rsqrt(var + LN_EPS) * lng_ref[...] + lnb_ref[...]
    if vn_refs:
        vn_refs[0][...] = vn
    lane = lax.broadcasted_iota(jnp.int32, (CHUNK, LANES), 1)
    lo_half = lane < GM_HD
    for p in range(GM_GROUPS // 2):
        vp = vn[:, p * LANES:(p + 1) * LANES]
        s = (_dot(wmix_ref[2 * p], jnp.where(lo_half, vp, 0.0))
             + _dot(wmix_ref[2 * p + 1], jnp.where(lo_half, 0.0, vp)))
        s = s + bmix_ref[:, p * LANES:(p + 1) * LANES]
        o_ref[:, p * LANES:(p + 1) * LANES] = (u[:, p * LANES:(p + 1) * LANES] * s).astype(o_ref.dtype)


def _gmlp(p_gm, ln_g, ln_b, wmix, bmix, emit_vn):
    n = p_gm.shape[0]
    out_shape = [jax.ShapeDtypeStruct((n, BRANCH_W), BF16)]
    out_specs = [pl.BlockSpec((CHUNK, BRANCH_W), lambda i: (i, 0))]
    if emit_vn:
        out_shape.append(jax.ShapeDtypeStruct((n, BRANCH_W), F32))
        out_specs.append(pl.BlockSpec((CHUNK, BRANCH_W), lambda i: (i, 0)))
    return pl.pallas_call(
        _gmlp_kernel,
        grid=(n // CHUNK,),
        in_specs=[pl.BlockSpec((CHUNK, BRANCH_W), lambda i: (i, 0)),
                  pl.BlockSpec((CHUNK, BRANCH_W), lambda i: (i, 1)),
                  _full((1, BRANCH_W)), _full((1, BRANCH_W)),
                  _full((GM_GROUPS, CHUNK, CHUNK)), _full((CHUNK, BRANCH_W))],
        out_specs=out_specs,
        out_shape=out_shape,
        compiler_params=_params(("arbitrary",), 32),
        name="gmlp",
    )(p_gm, p_gm, ln_g, ln_b, wmix, bmix)


def _rwkv_rows(p, first, prev, s_ref, o_ref, prm, nseq):
    (mu, w0, w2, a0, a2, g2, k_k, k_a, r_k, lnx_g, lnx_b) = prm
    rows = RW_ROWS
    ls = rows // nseq
    shifted = jnp.where(first, prev, pltpu.roll(p, 1, 0))
    xs = p + (shifted - p) * mu
    r = xs[:, 0:BRANCH_W]
    k = xs[:, BRANCH_W:2 * BRANCH_W]
    v = xs[:, 2 * BRANCH_W:3 * BRANCH_W]
    o = 3 * BRANCH_W
    wd = xs[:, o:o + RW_W_LORA]
    ad = xs[:, o + RW_W_LORA:o + RW_W_LORA + RW_A_LORA]
    gd = xs[:, o + RW_W_LORA + RW_A_LORA:RW_COLS]

    w_log = -_softplus(-(w0 + _dot(jnp.tanh(wd), w2))) - 0.5
    logw = -jnp.exp(w_log)
    a = _sigmoid(a0 + _dot(ad, a2))
    g = _dot(_sigmoid(gd), g2)
    kkr = k * k_k
    kf = k * (1.0 + (a - 1.0) * k_a)
    rkr = r * kf * r_k

    ri = lax.broadcasted_iota(jnp.int32, (rows, rows), 0)
    ci = lax.broadcasted_iota(jnp.int32, (rows, rows), 1)
    if nseq == 1:
        same = ci >= 0
    else:
        sh = int(np.log2(ls))
        same = lax.shift_right_logical(ri, sh) == lax.shift_right_logical(ci, sh)
    low_incl = same & (ci <= ri)
    low_strict = same & (ci < ri)
    m_incl = jnp.where(low_incl, 1.0, 0.0).astype(BF16)
    m_same = jnp.where(same, 1.0, 0.0).astype(BF16)
    hi, mid, lo = _split3(logw)
    cum = (jnp.dot(m_incl, hi, preferred_element_type=F32) + jnp.dot(m_incl, mid, preferred_element_type=F32)
           + jnp.dot(m_incl, lo, preferred_element_type=F32))
    tot = (jnp.dot(m_same, hi, preferred_element_type=F32) + jnp.dot(m_same, mid, preferred_element_type=F32)
           + jnp.dot(m_same, lo, preferred_element_type=F32))
    g_t = jnp.exp(cum)
    g_prev = jnp.exp(cum - logw)
    g_inv = jnp.exp(-cum)
    g_end = jnp.exp(tot - cum)
    g_tot = jnp.exp(tot)
    eye = jnp.where(ri == ci, 1.0, 0.0)
    yield

    hs = range(RW_HEADS)
    sls = [slice(h * RW_HD, (h + 1) * RW_HD) for h in hs]
    kk = [kkr[:, sl] for sl in sls]
    kk = [x * lax.rsqrt(jnp.maximum(jnp.sum(x * x, axis=-1, keepdims=True), 1e-24)) for x in kk]
    bv = [kk[h] * a[:, sls[h]] for h in hs]
    k_h = [kf[:, sl] for sl in sls]
    v_h = [v[:, sl] for sl in sls]
    at = [-kk[h] * g_prev[:, sls[h]] for h in hs]
    rt = [r[:, sl] * g_t[:, sl] for sl in sls]
    bt = [bv[h] * g_inv[:, sls[h]] for h in hs]
    kt = [k_h[h] * g_inv[:, sls[h]] for h in hs]
    bh = [bv[h] * g_end[:, sls[h]] for h in hs]
    kh = [k_h[h] * g_end[:, sls[h]] for h in hs]

    ar = [jnp.concatenate([at[h], rt[h]], axis=0) for h in hs]
    pb = [_dot_nt(ar[h], bt[h]) for h in hs]
    pk = [_dot_nt(ar[h], kt[h]) for h in hs]
    l_ab = [jnp.where(low_strict, x[:rows], 0.0) for x in pb]
    l_ak = [jnp.where(low_strict, x[:rows], 0.0) for x in pk]
    a_rb = [jnp.where(low_incl, x[rows:], 0.0) for x in pb]
    a_rk = [jnp.where(low_incl, x[rows:], 0.0) for x in pk]
    yield

    tm = [eye + x for x in l_ab]
    pw = [_dot(x, x) for x in l_ab]
    n_dbl = int(np.log2(ls)) - 1
    for it in range(n_dbl):
        yield
        if it < n_dbl - 1:
            z = [_dot(jnp.concatenate([tm[h], pw[h]], axis=0), pw[h]) for h in hs]
            tm = [tm[h] + z[h][:rows] for h in hs]
            pw = [z[h][rows:] for h in hs]
        else:
            tm = [tm[h] + _dot(tm[h], pw[h]) for h in hs]

    if nseq == 1:
        ars = [_dot_nt(ar[h], s_ref[0, h]) for h in hs]
        as0 = [x[:rows] for x in ars]
        rs0 = [x[rows:] for x in ars]
    else:
        as0, rs0 = [], []
        for h in hs:
            zs = [_dot_nt(jnp.concatenate([at[h][b * ls:(b + 1) * ls], rt[h][b * ls:(b + 1) * ls]], axis=0),
                          s_ref[b, h]) for b in range(nseq)]
            as0.append(jnp.concatenate([x[:ls] for x in zs], axis=0))
            rs0.append(jnp.concatenate([x[ls:] for x in zs], axis=0))

    lv = [_dot(l_ak[h], v_h[h]) for h in hs]
    yield
    u = [_dot(tm[h], as0[h] + lv[h]) for h in hs]
    yield
    y = [rs0[h] + _dot(a_rb[h], u[h]) + _dot(a_rk[h], v_h[h]) for h in hs]
    yield

    for h in hs:
        for b in range(nseq):
            rb = slice(b * ls, (b + 1) * ls)
            uv = jnp.concatenate([u[h][rb], v_h[h][rb]], axis=0)
            bk = jnp.concatenate([bh[h][rb], kh[h][rb]], axis=0)
            s_ref[b, h] = s_ref[b, h] * g_tot[b * ls:b * ls + 1, sls[h]] + _dot_tn(uv, bk)

    outs = []
    for h in hs:
        ym = jnp.mean(y[h], axis=-1, keepdims=True)
        yc = y[h] - ym
        yv = jnp.mean(yc * yc, axis=-1, keepdims=True)
        yn = yc * lax.rsqrt(yv + GN_EPS) * lnx_g[:, sls[h]] + lnx_b[:, sls[h]]
        bonus = jnp.sum(rkr[:, sls[h]], axis=-1, keepdims=True) * v_h[h]
        outs.append((yn + bonus) * g[:, sls[h]])
    o_ref[...] = jnp.concatenate(outs, axis=-1).astype(o_ref.dtype)


def _interleave(gens):
    for _ in itertools.zip_longest(*gens):
        pass


def _rwkv_prompt_kernel(p_ref, *refs):
    prm_refs, (o_ref, s_ref, sh_ref, carry_ref) = refs[:11], refs[11:]
    c = pl.program_id(1)

    @pl.when(c == 0)
    def _():
        s_ref[...] = jnp.zeros_like(s_ref)
        carry_ref[...] = jnp.zeros_like(carry_ref)

    first = lax.broadcasted_iota(jnp.int32, (RW_ROWS, 1), 0) == 0
    prm = tuple(x[...] for x in prm_refs)
    groups = range(p_ref.shape[0])
    _interleave([_rwkv_rows(p_ref[g], first, carry_ref[8 * g:8 * g + 1, :], s_ref.at[pl.ds(g, 1)],
                            o_ref.at[g], prm, nseq=1) for g in groups])
    for g in groups:
        last = p_ref[g, RW_ROWS - 1:RW_ROWS, :]
        carry_ref[8 * g:8 * g + 1, :] = last
        sh_ref[g] = last


def _rwkv_sample_kernel(p_ref, prev_ref, s0_ref, *refs):
    prm_refs, (o_ref, s_ref, sh_ref) = refs[:11], refs[11:]
    nseq = s0_ref.shape[0] // p_ref.shape[0]
    ls = RW_ROWS // nseq
    s_ref[...] = s0_ref[...]
    first = (lax.broadcasted_iota(jnp.int32, (RW_ROWS, 1), 0) & (ls - 1)) == 0
    prm = tuple(x[...] for x in prm_refs)
    groups = range(p_ref.shape[0])
    _interleave([_rwkv_rows(p_ref[g], first, prev_ref[g], s_ref.at[pl.ds(g * nseq, nseq)], o_ref.at[g],
                            prm, nseq=nseq) for g in groups])
    for g in groups:
        for b in range(nseq):
            sh_ref[g * nseq + b:g * nseq + b + 1, :] = p_ref[g, (b + 1) * ls - 1:(b + 1) * ls, :]


def _rw_param_specs():
    shapes = [(1, RW_COLS), (1, BRANCH_W), (RW_W_LORA, BRANCH_W), (1, BRANCH_W), (RW_A_LORA, BRANCH_W),
              (RW_G_LORA, BRANCH_W), (1, BRANCH_W), (1, BRANCH_W), (1, BRANCH_W), (1, BRANCH_W), (1, BRANCH_W)]
    return [_full(s) for s in shapes]


def _rwkv_prompt(p_rw, prm, batch, seq):
    nc = seq // RW_ROWS
    g = RW_GROUPS
    p3 = p_rw.reshape(batch, seq, RW_COLS)
    o, s_new, sh = pl.pallas_call(
        _rwkv_prompt_kernel,
        grid=(batch // g, nc),
        in_specs=[pl.BlockSpec((g, RW_ROWS, RW_COLS), lambda b, c: (b, c, 0))] + _rw_param_specs(),
        out_specs=[pl.BlockSpec((g, RW_ROWS, BRANCH_W), lambda b, c: (b, c, 0)),
                   pl.BlockSpec((g, RW_HEADS, RW_HD, RW_HD), lambda b, c: (b, 0, 0, 0)),
                   pl.BlockSpec((g, 1, RW_COLS), lambda b, c: (b, 0, 0))],
        out_shape=[jax.ShapeDtypeStruct((batch, seq, BRANCH_W), BF16),
                   jax.ShapeDtypeStruct((batch, RW_HEADS, RW_HD, RW_HD), F32),
                   jax.ShapeDtypeStruct((batch, 1, RW_COLS), F32)],
        scratch_shapes=[pltpu.VMEM((8 * g, RW_COLS), F32)],
        compiler_params=_params(("arbitrary", "arbitrary"), 56),
        name="rwkv_prompt",
    )(p3, *prm)
    return o.reshape(batch * seq, BRANCH_W), s_new, sh.reshape(batch, RW_COLS)


def _rwkv_sample(p_rw, prev_rows, s0, prm, seq):
    n = p_rw.shape[0]
    nseq = RW_ROWS // seq
    batch = n // seq
    g = RW_GROUPS
    ng = n // RW_ROWS
    o, s_new, sh = pl.pallas_call(
        _rwkv_sample_kernel,
        grid=(ng // g,),
        in_specs=[pl.BlockSpec((g, RW_ROWS, RW_COLS), lambda i: (i, 0, 0)),
                  pl.BlockSpec((g, RW_ROWS, RW_COLS), lambda i: (i, 0, 0)),
                  pl.BlockSpec((g * nseq, RW_HEADS, RW_HD, RW_HD), lambda i: (i, 0, 0, 0))] + _rw_param_specs(),
        out_specs=[pl.BlockSpec((g, RW_ROWS, BRANCH_W), lambda i: (i, 0, 0)),
                   pl.BlockSpec((g * nseq, RW_HEADS, RW_HD, RW_HD), lambda i: (i, 0, 0, 0)),
                   pl.BlockSpec((g * nseq, RW_COLS), lambda i: (i, 0))],
        out_shape=[jax.ShapeDtypeStruct((ng, RW_ROWS, BRANCH_W), BF16),
                   jax.ShapeDtypeStruct((batch, RW_HEADS, RW_HD, RW_HD), F32),
                   jax.ShapeDtypeStruct((batch, RW_COLS), F32)],
        compiler_params=_params(("arbitrary",), 56),
        name="rwkv_sample",
    )(p_rw.reshape(ng, RW_ROWS, RW_COLS), prev_rows.reshape(ng, RW_ROWS, RW_COLS), s0, *prm)
    return o.reshape(n, BRANCH_W), s_new, sh


def _attend(q, k, v):
    s = _dot_nt(q, k) * (XA_HD ** -0.5)
    e = jnp.exp(s - jnp.max(s, axis=-1, keepdims=True))
    pr = e / jnp.sum(e, axis=-1, keepdims=True)
    return _dot(pr, v)


def _xattn_prompt_kernel(q_ref, k_ref, v_ref, o_ref):
    for h in range(XA_HEADS):
        sl = slice(h * XA_HD, (h + 1) * XA_HD)
        o_ref[:, sl] = _attend(q_ref[:, sl], k_ref[:, sl], v_ref[:, sl]).astype(o_ref.dtype)


def _xattn_prompt(q, mk, mv, batch, seq, tq=512):
    nt = seq // tq
    return pl.pallas_call(
        _xattn_prompt_kernel,
        grid=(batch, nt),
        in_specs=[pl.BlockSpec((tq, BRANCH_W), lambda b, i: (b * nt + i, 0)),
                  pl.BlockSpec((N_MEM, BRANCH_W), lambda b, i: (b, 0)),
                  pl.BlockSpec((N_MEM, BRANCH_W), lambda b, i: (b, 0))],
        out_specs=pl.BlockSpec((tq, BRANCH_W), lambda b, i: (b * nt + i, 0)),
        out_shape=jax.ShapeDtypeStruct((batch * seq, BRANCH_W), BF16),
        compiler_params=_params(("arbitrary", "arbitrary"), 32),
        name="xattn_prompt",
    )(q, mk, mv)


def _xattn_sample_kernel(q_ref, k_ref, v_ref, o_ref):
    nb = k_ref.shape[0]
    ls = q_ref.shape[0] // nb
    for b in range(nb):
        for h in range(XA_HEADS):
            sl = slice(h * XA_HD, (h + 1) * XA_HD)
            o_ref[b * ls:(b + 1) * ls, sl] = _attend(
                q_ref[b * ls:(b + 1) * ls, sl], k_ref[b, :, sl], v_ref[b, :, sl]).astype(o_ref.dtype)


def _xattn_sample(q, mk, mv, seq, nb=8):
    batch = mk.shape[0]
    mem_spec = pl.BlockSpec((nb, N_MEM, BRANCH_W), lambda i: (i, 0, 0))
    return pl.pallas_call(
        _xattn_sample_kernel,
        grid=(batch // nb,),
        in_specs=[pl.BlockSpec((nb * seq, BRANCH_W), lambda i: (i, 0)), mem_spec, mem_spec],
        out_specs=pl.BlockSpec((nb * seq, BRANCH_W), lambda i: (i, 0)),
        out_shape=jax.ShapeDtypeStruct((batch * seq, BRANCH_W), BF16),
        compiler_params=_params(("arbitrary",), 40),
        name="xattn_sample",
    )(q, mk, mv)


def _merge_kernel(x_ref, gm_ref, rw_ref, xa_ref, gt_ref, gb_ref, wb_ref, wo_ref, n2_ref,
                  wr_hi_ref, wr_lo_ref, br_ref, h_ref, hn_ref, route_ref, ridx_ref, cnt_ref):
    merged = None
    for n, br in enumerate((gm_ref, rw_ref, xa_ref)):
        cs = slice(n * D_MODEL, (n + 1) * D_MODEL)
        up = jnp.dot(br[...], wb_ref[n], preferred_element_type=F32)
        term = _sigmoid(gt_ref[:, cs] + gb_ref[:, cs]) * up
        merged = term if merged is None else merged + term
    h = x_ref[...] + _dot(merged, wo_ref[...])
    h_ref[...] = h
    hn = _rmsnorm(h, n2_ref[...])
    hn_ref[...] = hn

    hn_hi = hn.astype(BF16)
    hn_lo = (hn - hn_hi.astype(F32)).astype(BF16)
    logits = (jnp.dot(hn_hi, wr_hi_ref[...], preferred_element_type=F32)
              + jnp.dot(hn_lo, wr_hi_ref[...], preferred_element_type=F32)
              + jnp.dot(hn_hi, wr_lo_ref[...], preferred_element_type=F32)) + br_ref[...]
    lane = lax.broadcasted_iota(jnp.int32, logits.shape, 1)
    neg = -jnp.inf
    big = jnp.int32(1 << 20)
    gmask = (lane >= N_EXPERTS) & (lane < N_EXPERTS + N_GROUPS)
    gl = jnp.where(gmask, logits, neg)
    gmax = jnp.max(gl, axis=-1, keepdims=True)
    gsel = jnp.min(jnp.where(gl == gmax, lane, big), axis=-1, keepdims=True) - N_EXPERTS
    gsum = jnp.sum(jnp.where(gmask, jnp.exp(logits - gmax), 0.0), axis=-1, keepdims=True)
    pg_top = 1.0 / gsum
    emask = (lane >= gsel * EXP_PER_GROUP) & (lane < (gsel + 1) * EXP_PER_GROUP)
    el = jnp.where(emask, logits, neg)
    m1 = jnp.max(el, axis=-1, keepdims=True)
    i1 = jnp.min(jnp.where(el == m1, lane, big), axis=-1, keepdims=True)
    el2 = jnp.where(lane == i1, neg, el)
    m2 = jnp.max(el2, axis=-1, keepdims=True)
    i2 = jnp.min(jnp.where(el2 == m2, lane, big), axis=-1, keepdims=True)
    t2 = jnp.exp(m2 - m1)
    w1 = pg_top / (1.0 + t2)
    w2 = pg_top * t2 / (1.0 + t2)

    @pl.when(pl.program_id(0) == 0)
    def _():
        cnt_ref[...] = jnp.zeros_like(cnt_ref)

    tm = logits.shape[0]
    onehot = jnp.where(lane == i1, 1.0, 0.0) + jnp.where(lane == i2, 1.0, 0.0)
    ri = lax.broadcasted_iota(jnp.int32, (tm, tm), 0)
    ci = lax.broadcasted_iota(jnp.int32, (tm, tm), 1)
    earlier = jnp.where(ci < ri, 1.0, 0.0).astype(BF16)
    base = cnt_ref[0:1, :]
    seen = base + jnp.dot(earlier, onehot.astype(BF16), preferred_element_type=F32)
    r1 = jnp.sum(jnp.where(lane == i1, seen, 0.0), axis=-1, keepdims=True)
    r2 = jnp.sum(jnp.where(lane == i2, seen, 0.0), axis=-1, keepdims=True)
    cnt_ref[0:1, :] = base + jnp.sum(onehot, axis=0, keepdims=True)

    route = jnp.zeros(logits.shape, F32)
    for col, val in ((ROUTE_E1, i1.astype(F32)), (ROUTE_E2, i2.astype(F32)), (ROUTE_W1, w1), (ROUTE_W2, w2),
                     (ROUTE_R1, r1), (ROUTE_R2, r2)):
        route = jnp.where(lane == col, val, route)
    route_ref[...] = route
    ridx_ref[...] = jnp.transpose(route)[0:8, :].astype(jnp.int32)


def _merge(x, o_gm, o_rw, o_xa, p_gate, gate_b, w_branch, w_out, n2_g, wr_hi, wr_lo, b_r, tm=256):
    n = x.shape[0]
    row = lambda wd: pl.BlockSpec((tm, wd), lambda i: (i, 0))
    return pl.pallas_call(
        _merge_kernel,
        grid=(n // tm,),
        in_specs=[row(D_MODEL), row(BRANCH_W), row(BRANCH_W), row(BRANCH_W), row(GATE_COLS),
                  _full((1, GATE_COLS)), _full((N_BRANCH, BRANCH_W, D_MODEL)), _full((D_MODEL, D_MODEL)),
                  _full((1, D_MODEL)), _full((D_MODEL, LANES)), _full((D_MODEL, LANES)), _full((1, LANES))],
        out_specs=[row(D_MODEL), row(D_MODEL), row(LANES),
                   pl.BlockSpec((8, tm), lambda i: (0, i)), _full((8, LANES))],
        out_shape=[jax.ShapeDtypeStruct((n, D_MODEL), F32), jax.ShapeDtypeStruct((n, D_MODEL), F32),
                   jax.ShapeDtypeStruct((n, LANES), F32), jax.ShapeDtypeStruct((8, n), jnp.int32),
                   jax.ShapeDtypeStruct((8, LANES), F32)],
        compiler_params=_params(("arbitrary",), 40),
        name="merge",
    )(x, o_gm, o_rw, o_xa, p_gate, gate_b, w_branch, w_out, n2_g, wr_hi, wr_lo, b_r)


def _route_plan(counts, n, tm):
    n_tiles = (2 * n) // tm + N_EXPERTS
    cnt = counts[0, :N_EXPERTS].astype(jnp.int32)
    tiles = (cnt + tm - 1) // tm
    tile_end = jnp.cumsum(tiles)
    row_start = (tile_end - tiles) * tm
    n_used = tile_end[N_EXPERTS - 1:]
    pad_tile_row = jnp.where(tiles > 0, (tile_end - 1) * tm, -1)
    t = jnp.minimum(jnp.arange(n_tiles, dtype=jnp.int32), n_used - 1)
    tile_expert = jnp.sum((t[:, None] >= tile_end[None, :]).astype(jnp.int32), axis=1)
    return row_start, pad_tile_row, tile_expert, n_used, n_tiles


def _row_copy(src_ref, src_row, dst_ref, dst_row, sem):
    return pltpu.make_async_copy(src_ref.at[pl.ds(src_row, 1)], dst_ref.at[pl.ds(dst_row, 1)], sem)


def _tile_rows_to_smem(rs_ref, ridx_ref, pos_v, pos_s, sem):
    ridx = ridx_ref[...]
    for c, (ce, cr) in enumerate(((ROUTE_E1, ROUTE_R1), (ROUTE_E2, ROUTE_R2))):
        e = ridx[ce:ce + 1, :]
        pos = ridx[cr:cr + 1, :]
        for x in range(N_EXPERTS):
            pos = pos + jnp.where(e == x, rs_ref[x], 0)
        pos_v[c:c + 1, :] = pos
    cp = pltpu.make_async_copy(pos_v, pos_s, sem)
    cp.start()
    cp.wait()


def _dispatch_kernel(rs_ref, pad_ref, nu_ref, ridx_ref, hn_ref, xs_ref, zero_ref, pos_v, pos_s, sem, psem):
    tm = hn_ref.shape[0]
    n_tiles = xs_ref.shape[0] // tm

    @pl.when(pl.program_id(0) == 0)
    def _():
        zero_ref[...] = jnp.zeros_like(zero_ref)

        def fill(row):
            return pltpu.make_async_copy(zero_ref, xs_ref.at[pl.ds(pl.multiple_of(row, tm), tm)], sem)

        def each_fill(act):
            for e in range(N_EXPERTS):
                @pl.when(pad_ref[e] >= 0)
                def _():
                    act(fill(pad_ref[e]))

            def tail(t, c):
                act(fill(t * tm))
                return c

            lax.fori_loop(nu_ref[0], n_tiles, tail, 0)

        each_fill(lambda f: f.start())
        each_fill(lambda f: f.wait())

    _tile_rows_to_smem(rs_ref, ridx_ref, pos_v, pos_s, psem)

    def issue(i, c):
        _row_copy(hn_ref, i, xs_ref, pos_s[0, i], sem).start(priority=0)
        _row_copy(hn_ref, i, xs_ref, pos_s[1, i], sem).start(priority=1)
        return c

    def drain(i, c):
        _row_copy(hn_ref, 0, xs_ref, 0, sem).wait()
        return c

    lax.fori_loop(0, tm, issue, 0, unroll=8)
    lax.fori_loop(0, 2 * tm, drain, 0, unroll=8)


def _dispatch(hn, ridx, row_start, pad_tile_row, n_used, n_rows, tm):
    n = hn.shape[0]
    return pl.pallas_call(
        _dispatch_kernel,
        grid_spec=pltpu.PrefetchScalarGridSpec(
            num_scalar_prefetch=3,
            grid=(n // tm,),
            in_specs=[pl.BlockSpec((8, tm), lambda i, *_: (0, i)),
                      pl.BlockSpec((tm, D_MODEL), lambda i, *_: (i, 0))],
            out_specs=pl.BlockSpec(memory_space=pl.ANY),
            scratch_shapes=[pltpu.VMEM((tm, D_MODEL), F32), pltpu.VMEM((2, tm), jnp.int32),
                            pltpu.SMEM((2, tm), jnp.int32), pltpu.SemaphoreType.DMA, pltpu.SemaphoreType.DMA]),
        out_shape=jax.ShapeDtypeStruct((n_rows, D_MODEL), F32),
        compiler_params=_params(("arbitrary",), 32),
        name="moe_dispatch",
    )(row_start, pad_tile_row, n_used, ridx, hn)


def _experts_kernel(te_ref, nu_ref, x_ref, wg_ref, wu_ref, wd_ref, y_ref, wg_s, wu_s, wd_s):
    t = pl.program_id(0)
    used = t < nu_ref[0]

    @pl.when((t == 0) | (te_ref[t] != te_ref[jnp.maximum(t - 1, 0)]))
    def _():
        wg_s[...] = wg_ref[...].astype(BF16)
        wu_s[...] = wu_ref[...].astype(BF16)
        wd_s[...] = wd_ref[...].astype(BF16)

    @pl.when(used)
    def _():
        xb = x_ref[...].astype(BF16)
        gate = jnp.dot(xb, wg_s[...], preferred_element_type=F32)
        up = jnp.dot(xb, wu_s[...], preferred_element_type=F32)
        y_ref[...] = _dot(gate * _sigmoid(gate) * up, wd_s[...])

    @pl.when(jnp.logical_not(used))
    def _():
        y_ref[...] = jnp.zeros_like(y_ref)


def _experts(xs, tile_expert, n_used, wg, wu, wd, tm):
    n_rows = xs.shape[0]
    rows = lambda t, te, nu: (jnp.minimum(t, nu[0] - 1), 0)
    return pl.pallas_call(
        _experts_kernel,
        grid_spec=pltpu.PrefetchScalarGridSpec(
            num_scalar_prefetch=2,
            grid=(n_rows // tm,),
            in_specs=[pl.BlockSpec((tm, D_MODEL), rows),
                      pl.BlockSpec((None, D_MODEL, EXP_FF), lambda t, te, nu: (te[t], 0, 0)),
                      pl.BlockSpec((None, D_MODEL, EXP_FF), lambda t, te, nu: (te[t], 0, 0)),
                      pl.BlockSpec((None, EXP_FF, D_MODEL), lambda t, te, nu: (te[t], 0, 0))],
            out_specs=pl.BlockSpec((tm, D_MODEL), lambda t, te, nu: (t, 0)),
            scratch_shapes=[pltpu.VMEM((D_MODEL, EXP_FF), BF16), pltpu.VMEM((D_MODEL, EXP_FF), BF16),
                            pltpu.VMEM((EXP_FF, D_MODEL), BF16)]),
        out_shape=jax.ShapeDtypeStruct((n_rows, D_MODEL), F32),
        compiler_params=_params(("arbitrary",), 40),
        name="moe_experts",
    )(tile_expert, n_used, xs, wg, wu, wd)


def _combine_kernel(rs_ref, ridx_ref, h_ref, route_ref, fg_ref, ys_ref, y_ref, o1_ref, o2_ref, pos_v, pos_s,
                    sem, psem):
    tm = h_ref.shape[0]
    _tile_rows_to_smem(rs_ref, ridx_ref, pos_v, pos_s, psem)

    def issue(i, c):
        _row_copy(ys_ref, pos_s[0, i], o1_ref, i, sem).start(priority=0)
        _row_copy(ys_ref, pos_s[1, i], o2_ref, i, sem).start(priority=1)
        return c

    def drain(i, c):
        _row_copy(ys_ref, 0, o1_ref, 0, sem).wait()
        return c

    lax.fori_loop(0, tm, issue, 0, unroll=8)
    lax.fori_loop(0, 2 * tm, drain, 0, unroll=8)
    rt = route_ref[...]
    lane = lax.broadcasted_iota(jnp.int32, rt.shape, 1)
    w1 = jnp.sum(jnp.where(lane == ROUTE_W1, rt, 0.0), axis=-1, keepdims=True)
    w2 = jnp.sum(jnp.where(lane == ROUTE_W2, rt, 0.0), axis=-1, keepdims=True)
    y_ref[...] = _rmsnorm(h_ref[...] + w1 * o1_ref[...] + w2 * o2_ref[...], fg_ref[...])


def _combine(h, route, ridx, row_start, ys, final_g, tm):
    n = h.shape[0]
    return pl.pallas_call(
        _combine_kernel,
        grid_spec=pltpu.PrefetchScalarGridSpec(
            num_scalar_prefetch=1,
            grid=(n // tm,),
            in_specs=[pl.BlockSpec((8, tm), lambda i, rs: (0, i)),
                      pl.BlockSpec((tm, D_MODEL), lambda i, rs: (i, 0)),
                      pl.BlockSpec((tm, LANES), lambda i, rs: (i, 0)),
                      pl.BlockSpec((1, D_MODEL), lambda i, rs: (0, 0)),
                      pl.BlockSpec(memory_space=pl.ANY)],
            out_specs=pl.BlockSpec((tm, D_MODEL), lambda i, rs: (i, 0)),
            scratch_shapes=[pltpu.VMEM((tm, D_MODEL), F32), pltpu.VMEM((tm, D_MODEL), F32),
                            pltpu.VMEM((2, tm), jnp.int32), pltpu.SMEM((2, tm), jnp.int32),
                            pltpu.SemaphoreType.DMA, pltpu.SemaphoreType.DMA]),
        out_shape=jax.ShapeDtypeStruct((n, D_MODEL), F32),
        compiler_params=_params(("arbitrary",), 32),
        name="moe_combine",
    )(row_start, ridx, h, route, final_g, ys)


def _moe(h, hn, route, ridx, counts, wg, wu, wd, final_g, tm=256):
    row_start, pad_tile_row, tile_expert, n_used, n_tiles = _route_plan(counts, h.shape[0], tm)
    xs = _dispatch(hn, ridx, row_start, pad_tile_row, n_used, n_tiles * tm, tm)
    ys = _experts(xs, tile_expert, n_used, wg, wu, wd, tm)
    return _combine(h, route, ridx, row_start, ys, final_g, tm)


def _layer(x2d, mem_k, mem_v, rw_state, w, *, prompt, batch, seq):
    n = x2d.shape[0]
    tm_in = 256
    p_gm, p_rw, p_q, p_gate = _in_proj(x2d, w["norm1_g"], w["w_in_segs"], tm_in)
    if prompt:
        o_gm = _gmlp(p_gm, w["gm_ln_g"], w["gm_ln_b"], w["gm_mix_prompt"], w["gm_bias_prompt"], False)[0]
        vn = None
        o_rw, s_new, sh_new = _rwkv_prompt(p_rw, w["rw_prm"], batch, seq)
        o_xa = _xattn_prompt(p_q, mem_k, mem_v, batch, seq)
    else:
        o_gm, vn = _gmlp(p_gm, w["gm_ln_g"], w["gm_ln_b"], w["gm_mix_sample"], w["gm_bias_sample"], True)
        s0, shift = rw_state
        prev_rows = jnp.pad(shift[:, None, :], ((0, 0), (0, seq - 1), (0, 0))).reshape(n, RW_COLS)
        o_rw, s_new, sh_new = _rwkv_sample(p_rw, prev_rows, s0, w["rw_prm"], seq)
        o_xa = _xattn_sample(p_q, mem_k, mem_v, seq)
    h, hn, route, ridx, counts = _merge(x2d, o_gm, o_rw, o_xa, p_gate, w["gate_b"], w["w_branch"], w["w_out"],
                                        w["norm2_g"], w["wr_hi"], w["wr_lo"], w["b_r"])
    y = _moe(h, hn, route, ridx, counts, w["e_wg"], w["e_wu"], w["e_wd"], w["final_g"])
    return y, s_new, sh_new, vn


def kernel(x_prompt, x_sample, state_rwkv_S, state_rwkv_shift, cache_mem_k, cache_mem_v, mem_prompt, norm1_g, w_in, gate_b, gm_ln_g, gm_ln_b, gm_ws, gm_bs, rw_mu, rw_w0, rw_w2, rw_a0, rw_a2, rw_g2, rw_k_k, rw_k_a, rw_r_k, rw_lnx_g, rw_lnx_b, xa_wk, xa_wv, w_branch, w_out, norm2_g, rg_w, rg_b, re_w, re_b, e_wg, e_wu, e_wd, final_g):
    bp, seq_p, _ = x_prompt.shape
    bs, seq_s, _ = x_sample.shape
    depth = w_in.shape[0]
    assert depth == 1 and seq_p % CHUNK == 0 and RW_ROWS % seq_s == 0 and CHUNK % seq_s == 0

    l = 0
    row = lambda a: a.reshape(1, -1)
    seg = (0, 2 * BRANCH_W, 2 * BRANCH_W + RW_COLS, 3 * BRANCH_W + RW_COLS, 3 * BRANCH_W + RW_COLS + GATE_COLS)
    w_causal = jnp.tril(gm_ws[l])
    nrep = CHUNK // seq_s
    blk = w_causal[:, :seq_s, :seq_s]
    eye_rep = jnp.eye(nrep, dtype=F32)
    mix_sample = jnp.einsum("ab,gts->gatbs", eye_rep, blk).reshape(GM_GROUPS, CHUNK, CHUNK)
    bias_prompt = jnp.repeat(gm_bs[l].T, GM_HD, axis=1)
    bias_sample = jnp.tile(bias_prompt[:seq_s], (nrep, 1))

    wr = jnp.zeros((D_MODEL, LANES), F32)
    wr = wr.at[:, :N_EXPERTS].set(jnp.transpose(re_w[l], (1, 0, 2)).reshape(D_MODEL, N_EXPERTS))
    wr = wr.at[:, N_EXPERTS:N_EXPERTS + N_GROUPS].set(rg_w[l])
    wr_hi = wr.astype(BF16)
    wr_lo = (wr - wr_hi.astype(F32)).astype(BF16)
    b_r = jnp.zeros((1, LANES), F32)
    b_r = b_r.at[0, :N_EXPERTS].set(re_b[l].reshape(-1)).at[0, N_EXPERTS:N_EXPERTS + N_GROUPS].set(rg_b[l])

    w = dict(
        norm1_g=row(norm1_g[l]),
        w_in_segs=[w_in[l][:, a:b].astype(BF16) for a, b in zip(seg[:-1], seg[1:])],
        gm_ln_g=row(gm_ln_g[l]), gm_ln_b=row(gm_ln_b[l]),
        gm_mix_prompt=w_causal.astype(BF16), gm_bias_prompt=bias_prompt,
        gm_mix_sample=mix_sample.astype(BF16), gm_bias_sample=bias_sample,
        rw_prm=(row(rw_mu[l]), row(rw_w0[l]), rw_w2[l].astype(BF16), row(rw_a0[l]), rw_a2[l].astype(BF16),
                rw_g2[l].astype(BF16), row(rw_k_k[l]), row(rw_k_a[l]), row(rw_r_k[l]),
                row(rw_lnx_g[l]), row(rw_lnx_b[l])),
        gate_b=row(gate_b[l]), w_branch=w_branch[l].astype(BF16), w_out=w_out[l].astype(BF16),
        norm2_g=row(norm2_g[l]), wr_hi=wr_hi, wr_lo=wr_lo, b_r=b_r,
        e_wg=e_wg[l].reshape(N_EXPERTS, D_MODEL, EXP_FF),
        e_wu=e_wu[l].reshape(N_EXPERTS, D_MODEL, EXP_FF),
        e_wd=e_wd[l].reshape(N_EXPERTS, EXP_FF, D_MODEL),
        final_g=row(final_g),
    )

    mk_p, mv_p = _mem_kv(mem_prompt.reshape(bp * N_MEM, D_MODEL), xa_wk[l].astype(BF16), xa_wv[l].astype(BF16))
    yp, sp, shp, _ = _layer(x_prompt.reshape(bp * seq_p, D_MODEL), mk_p, mv_p, None, w,
                            prompt=True, batch=bp, seq=seq_p)
    ys, ss, shs, vs = _layer(x_sample.reshape(bs * seq_s, D_MODEL),
                             cache_mem_k[l].reshape(bs, N_MEM, BRANCH_W).astype(BF16),
                             cache_mem_v[l].reshape(bs, N_MEM, BRANCH_W).astype(BF16),
                             (state_rwkv_S[l], state_rwkv_shift[l]), w, prompt=False, batch=bs, seq=seq_s)

    return (yp.reshape(bp, seq_p, D_MODEL), ys.reshape(bs, seq_s, D_MODEL),
            sp[None], shp[None],
            mk_p.reshape(1, bp, N_MEM, XA_HEADS, XA_HD), mv_p.reshape(1, bp, N_MEM, XA_HEADS, XA_HD),
            ss[None], shs[None], vs.reshape(1, bs, seq_s, BRANCH_W))
```

```python
import itertools

import numpy as np
import jax
import jax.numpy as jnp
from jax import lax
from jax.experimental import pallas as pl
from jax.experimental.pallas import tpu as pltpu

F32 = jnp.float32
BF16 = jnp.bfloat16

D_MODEL = 1024
BRANCH_W = 512
CHUNK = 128
GM_GROUPS = 8
GM_HD = BRANCH_W // GM_GROUPS
RW_HEADS = 8
RW_HD = BRANCH_W // RW_HEADS
RW_W_LORA = 64
RW_A_LORA = 64
RW_G_LORA = 128
RW_COLS = 3 * BRANCH_W + RW_W_LORA + RW_A_LORA + RW_G_LORA
XA_HEADS = 4
XA_HD = BRANCH_W // XA_HEADS
N_MEM = 256
N_BRANCH = 3
GATE_COLS = N_BRANCH * D_MODEL
N_GROUPS = 4
EXP_PER_GROUP = 8
N_EXPERTS = N_GROUPS * EXP_PER_GROUP
EXP_FF = 512
RMS_EPS = 1e-6
LN_EPS = 1e-5
GN_EPS = 64e-5

LANES = 128
RW_ROWS = 64
RW_GROUPS = 2
MIB = 1024 * 1024
ROUTE_E1, ROUTE_E2, ROUTE_W1, ROUTE_W2, ROUTE_R1, ROUTE_R2 = range(6)


def _dot(a, b):
    return jnp.dot(a.astype(BF16), b.astype(BF16), preferred_element_type=F32)


def _dot_nt(a, b):
    return lax.dot_general(a.astype(BF16), b.astype(BF16), (((1,), (1,)), ((), ())),
                           preferred_element_type=F32)


def _dot_tn(a, b):
    return lax.dot_general(a.astype(BF16), b.astype(BF16), (((0,), (0,)), ((), ())),
                           preferred_element_type=F32)


def _sigmoid(x):
    return 1.0 / (1.0 + jnp.exp(-x))


def _gelu(x):
    c = np.float32(np.sqrt(2.0 / np.pi))
    return x * (0.5 * (1.0 + jnp.tanh(c * (x + 0.044715 * (x * x * x)))))


def _softplus(x):
    return jnp.maximum(x, 0.0) + jnp.log(1.0 + jnp.exp(-jnp.abs(x)))


def _rmsnorm(x, g):
    return x * lax.rsqrt(jnp.mean(x * x, axis=-1, keepdims=True) + RMS_EPS) * g


def _split3(x):
    hi = x.astype(BF16)
    r1 = x - hi.astype(F32)
    mid = r1.astype(BF16)
    lo = (r1 - mid.astype(F32)).astype(BF16)
    return hi, mid, lo


def _params(sem, vmem_mib):
    return pltpu.CompilerParams(dimension_semantics=sem, vmem_limit_bytes=vmem_mib * MIB)


def _full(shape):
    nd = len(shape)
    return pl.BlockSpec(shape, lambda *_: (0,) * nd)


def _gmlp_chunk(pu, pv, ln_g, ln_b, wmix_ref, bmix_ref):
    u = _gelu(pu)
    vf = _gelu(pv)
    mu = jnp.mean(vf, axis=-1, keepdims=True)
    vc = vf - mu
    var = jnp.mean(vc * vc, axis=-1, keepdims=True)
    vn = vc * lax.rsqrt(var + LN_EPS) * ln_g + ln_b
    lane = lax.broadcasted_iota(jnp.int32, (CHUNK, LANES), 1)
    lo_half = lane < GM_HD
    outs = []
    for p in range(GM_GROUPS // 2):
        vp = vn[:, p * LANES:(p + 1) * LANES]
        s = (_dot(wmix_ref[2 * p], jnp.where(lo_half, vp, 0.0))
             + _dot(wmix_ref[2 * p + 1], jnp.where(lo_half, 0.0, vp)))
        outs.append(u[:, p * LANES:(p + 1) * LANES] * (s + bmix_ref[:, p * LANES:(p + 1) * LANES]))
    return jnp.concatenate(outs, axis=-1), vn


def _in_proj_kernel(x_ref, g_ref, wgm_ref, wrw_ref, wq_ref, wgt_ref, lng_ref, lnb_ref, wmix_ref, bmix_ref,
                    ogm_ref, rw_ref, q_ref, gt_ref, *vn_refs):
    xb = _rmsnorm(x_ref[...], g_ref[...]).astype(BF16)
    gm = jnp.dot(xb, wgm_ref[...], preferred_element_type=F32)
    rw_ref[...] = jnp.dot(xb, wrw_ref[...], preferred_element_type=F32)
    q_ref[...] = jnp.dot(xb, wq_ref[...], preferred_element_type=F32)
    gt_ref[...] = jnp.dot(xb, wgt_ref[...], preferred_element_type=F32)
    for c in range(x_ref.shape[0] // CHUNK):
        rows = slice(c * CHUNK, (c + 1) * CHUNK)
        o, vn = _gmlp_chunk(gm[rows, 0:BRANCH_W], gm[rows, BRANCH_W:2 * BRANCH_W], lng_ref[...], lnb_ref[...],
                            wmix_ref, bmix_ref)
        ogm_ref[rows, :] = o.astype(ogm_ref.dtype)
        if vn_refs:
            vn_refs[0][rows, :] = vn


def _in_proj(x, g, w_segs, ln_g, ln_b, wmix, bmix, emit_vn, tm):
    n = x.shape[0]
    widths = [w.shape[1] for w in w_segs]
    out_w = [BRANCH_W] + widths[1:] + ([BRANCH_W] if emit_vn else [])
    out_dt = [BF16] + [F32] * (len(out_w) - 1)
    return pl.pallas_call(
        _in_proj_kernel,
        grid=(n // tm,),
        in_specs=[pl.BlockSpec((tm, D_MODEL), lambda i: (i, 0)), _full((1, D_MODEL))]
        + [pl.BlockSpec((D_MODEL, wd), lambda i: (0, 0), pipeline_mode=pl.Buffered(1)) for wd in widths]
        + [_full((1, BRANCH_W)), _full((1, BRANCH_W)), _full((GM_GROUPS, CHUNK, CHUNK)),
           _full((CHUNK, BRANCH_W))],
        out_specs=[pl.BlockSpec((tm, wd), lambda i: (i, 0)) for wd in out_w],
        out_shape=[jax.ShapeDtypeStruct((n, wd), dt) for wd, dt in zip(out_w, out_dt)],
        compiler_params=_params(("arbitrary",), 48),
        name="in_proj",
    )(x, g, *w_segs, ln_g, ln_b, wmix, bmix)


def _mem_kv_kernel(m_ref, wk_ref, wv_ref, k_ref, v_ref):
    mb = m_ref[...].astype(BF16)
    k_ref[...] = jnp.dot(mb, wk_ref[...], preferred_element_type=F32)
    v_ref[...] = jnp.dot(mb, wv_ref[...], preferred_element_type=F32)


def _mem_kv(mem, wk, wv, tm=512):
    n = mem.shape[0]
    return pl.pallas_call(
        _mem_kv_kernel,
        grid=(n // tm,),
        in_specs=[pl.BlockSpec((tm, D_MODEL), lambda i: (i, 0)),
                  _full((D_MODEL, BRANCH_W)), _full((D_MODEL, BRANCH_W))],
        out_specs=[pl.BlockSpec((tm, BRANCH_W), lambda i: (i, 0))] * 2,
        out_shape=[jax.ShapeDtypeStruct((n, BRANCH_W), F32)] * 2,
        compiler_params=_params(("arbitrary",), 32),
        name="mem_kv",
    )(mem, wk, wv)


def _rwkv_rows(p, first, prev, s_ref, o_ref, prm, nseq):
    (mu, w0, w2, a0, a2, g2, k_k, k_a, r_k, lnx_g, lnx_b) = prm
    rows = RW_ROWS
    ls = rows // nseq
    shifted = jnp.where(first, prev, pltpu.roll(p, 1, 0))
    xs = p + (shifted - p) * mu
    r = xs[:, 0:BRANCH_W]
    k = xs[:, BRANCH_W:2 * BRANCH_W]
    v = xs[:, 2 * BRANCH_W:3 * BRANCH_W]
    o = 3 * BRANCH_W
    wd = xs[:, o:o + RW_W_LORA]
    ad = xs[:, o + RW_W_LORA:o + RW_W_LORA + RW_A_LORA]
    gd = xs[:, o + RW_W_LORA + RW_A_LORA:RW_COLS]

    w_log = -_softplus(-(w0 + _dot(jnp.tanh(wd), w2))) - 0.5
    logw = -jnp.exp(w_log)
    a = _sigmoid(a0 + _dot(ad, a2))
    g = _dot(_sigmoid(gd), g2)
    kkr = k * k_k
    kf = k * (1.0 + (a - 1.0) * k_a)
    rkr = r * kf * r_k

    ri = lax.broadcasted_iota(jnp.int32, (rows, rows), 0)
    ci = lax.broadcasted_iota(jnp.int32, (rows, rows), 1)
    if nseq == 1:
        same = ci >= 0
    else:
        sh = int(np.log2(ls))
        same = lax.shift_right_logical(ri, sh) == lax.shift_right_logical(ci, sh)
    low_incl = same & (ci <= ri)
    low_strict = same & (ci < ri)
    m_incl = jnp.where(low_incl, 1.0, 0.0).astype(BF16)
    m_same = jnp.where(same, 1.0, 0.0).astype(BF16)
    hi, mid, lo = _split3(logw)
    cum = (jnp.dot(m_incl, hi, preferred_element_type=F32) + jnp.dot(m_incl, mid, preferred_element_type=F32)
           + jnp.dot(m_incl, lo, preferred_element_type=F32))
    tot = (jnp.dot(m_same, hi, preferred_element_type=F32) + jnp.dot(m_same, mid, preferred_element_type=F32)
           + jnp.dot(m_same, lo, preferred_element_type=F32))
    g_t = jnp.exp(cum)
    g_prev = jnp.exp(cum - logw)
    g_inv = jnp.exp(-cum)
    g_end = jnp.exp(tot - cum)
    g_tot = jnp.exp(tot)
    eye = jnp.where(ri == ci, 1.0, 0.0)
    yield

    hs = range(RW_HEADS)
    sls = [slice(h * RW_HD, (h + 1) * RW_HD) for h in hs]
    kk = [kkr[:, sl] for sl in sls]
    kk = [x * lax.rsqrt(jnp.maximum(jnp.sum(x * x, axis=-1, keepdims=True), 1e-24)) for x in kk]
    bv = [kk[h] * a[:, sls[h]] for h in hs]
    k_h = [kf[:, sl] for sl in sls]
    v_h = [v[:, sl] for sl in sls]
    at = [-kk[h] * g_prev[:, sls[h]] for h in hs]
    rt = [r[:, sl] * g_t[:, sl] for sl in sls]
    bt = [bv[h] * g_inv[:, sls[h]] for h in hs]
    kt = [k_h[h] * g_inv[:, sls[h]] for h in hs]
    bh = [bv[h] * g_end[:, sls[h]] for h in hs]
    kh = [k_h[h] * g_end[:, sls[h]] for h in hs]

    ar = [jnp.concatenate([at[h], rt[h]], axis=0) for h in hs]
    pb = [_dot_nt(ar[h], bt[h]) for h in hs]
    pk = [_dot_nt(ar[h], kt[h]) for h in hs]
    l_ab = [jnp.where(low_strict, x[:rows], 0.0) for x in pb]
    l_ak = [jnp.where(low_strict, x[:rows], 0.0) for x in pk]
    a_rb = [jnp.where(low_incl, x[rows:], 0.0) for x in pb]
    a_rk = [jnp.where(low_incl, x[rows:], 0.0) for x in pk]
    yield

    tm = [eye + x for x in l_ab]
    pw = [_dot(x, x) for x in l_ab]
    n_dbl = int(np.log2(ls)) - 1
    for it in range(n_dbl):
        yield
        if it < n_dbl - 1:
            z = [_dot(jnp.concatenate([tm[h], pw[h]], axis=0), pw[h]) for h in hs]
            tm = [tm[h] + z[h][:rows] for h in hs]
            pw = [z[h][rows:] for h in hs]
        else:
            tm = [tm[h] + _dot(tm[h], pw[h]) for h in hs]

    if nseq == 1:
        ars = [_dot_nt(ar[h], s_ref[0, h]) for h in hs]
        as0 = [x[:rows] for x in ars]
        rs0 = [x[rows:] for x in ars]
    else:
        as0, rs0 = [], []
        for h in hs:
            zs = [_dot_nt(jnp.concatenate([at[h][b * ls:(b + 1) * ls], rt[h][b * ls:(b + 1) * ls]], axis=0),
                          s_ref[b, h]) for b in range(nseq)]
            as0.append(jnp.concatenate([x[:ls] for x in zs], axis=0))
            rs0.append(jnp.concatenate([x[ls:] for x in zs], axis=0))

    lv = [_dot(l_ak[h], v_h[h]) for h in hs]
    yield
    u = [_dot(tm[h], as0[h] + lv[h]) for h in hs]
    yield
    y = [rs0[h] + _dot(a_rb[h], u[h]) + _dot(a_rk[h], v_h[h]) for h in hs]
    yield

    for h in hs:
        for b in range(nseq):
            rb = slice(b * ls, (b + 1) * ls)
            uv = jnp.concatenate([u[h][rb], v_h[h][rb]], axis=0)
            bk = jnp.concatenate([bh[h][rb], kh[h][rb]], axis=0)
            s_ref[b, h] = s_ref[b, h] * g_tot[b * ls:b * ls + 1, sls[h]] + _dot_tn(uv, bk)

    outs = []
    for h in hs:
        ym = jnp.mean(y[h], axis=-1, keepdims=True)
        yc = y[h] - ym
        yv = jnp.mean(yc * yc, axis=-1, keepdims=True)
        yn = yc * lax.rsqrt(yv + GN_EPS) * lnx_g[:, sls[h]] + lnx_b[:, sls[h]]
        bonus = jnp.sum(rkr[:, sls[h]], axis=-1, keepdims=True) * v_h[h]
        outs.append((yn + bonus) * g[:, sls[h]])
    o_ref[...] = jnp.concatenate(outs, axis=-1).astype(o_ref.dtype)


def _interleave(gens):
    for _ in itertools.zip_longest(*gens):
        pass


def _rwkv_prompt_kernel(p_ref, *refs):
    prm_refs, (o_ref, s_ref, sh_ref, carry_ref) = refs[:11], refs[11:]
    c = pl.program_id(1)

    @pl.when(c == 0)
    def _():
        s_ref[...] = jnp.zeros_like(s_ref)
        carry_ref[...] = jnp.zeros_like(carry_ref)

    first = lax.broadcasted_iota(jnp.int32, (RW_ROWS, 1), 0) == 0
    prm = tuple(x[...] for x in prm_refs)
    groups = range(p_ref.shape[0])
    _interleave([_rwkv_rows(p_ref[g], first, carry_ref[8 * g:8 * g + 1, :], s_ref.at[pl.ds(g, 1)],
                            o_ref.at[g], prm, nseq=1) for g in groups])
    for g in groups:
        last = p_ref[g, RW_ROWS - 1:RW_ROWS, :]
        carry_ref[8 * g:8 * g + 1, :] = last
        sh_ref[g] = last


def _rwkv_sample_kernel(p_ref, prev_ref, s0_ref, *refs):
    prm_refs, (o_ref, s_ref, sh_ref) = refs[:11], refs[11:]
    nseq = s0_ref.shape[0] // p_ref.shape[0]
    ls = RW_ROWS // nseq
    s_ref[...] = s0_ref[...]
    first = (lax.broadcasted_iota(jnp.int32, (RW_ROWS, 1), 0) & (ls - 1)) == 0
    prm = tuple(x[...] for x in prm_refs)
    groups = range(p_ref.shape[0])
    _interleave([_rwkv_rows(p_ref[g], first, prev_ref[g], s_ref.at[pl.ds(g * nseq, nseq)], o_ref.at[g],
                            prm, nseq=nseq) for g in groups])
    for g in groups:
        for b in range(nseq):
            sh_ref[g * nseq + b:g * nseq + b + 1, :] = p_ref[g, (b + 1) * ls - 1:(b + 1) * ls, :]


def _rw_param_specs():
    shapes = [(1, RW_COLS), (1, BRANCH_W), (RW_W_LORA, BRANCH_W), (1, BRANCH_W), (RW_A_LORA, BRANCH_W),
              (RW_G_LORA, BRANCH_W), (1, BRANCH_W), (1, BRANCH_W), (1, BRANCH_W), (1, BRANCH_W), (1, BRANCH_W)]
    return [_full(s) for s in shapes]


def _rwkv_prompt(p_rw, prm, batch, seq):
    nc = seq // RW_ROWS
    g = RW_GROUPS
    p3 = p_rw.reshape(batch, seq, RW_COLS)
    o, s_new, sh = pl.pallas_call(
        _rwkv_prompt_kernel,
        grid=(batch // g, nc),
        in_specs=[pl.BlockSpec((g, RW_ROWS, RW_COLS), lambda b, c: (b, c, 0))] + _rw_param_specs(),
        out_specs=[pl.BlockSpec((g, RW_ROWS, BRANCH_W), lambda b, c: (b, c, 0)),
                   pl.BlockSpec((g, RW_HEADS, RW_HD, RW_HD), lambda b, c: (b, 0, 0, 0)),
                   pl.BlockSpec((g, 1, RW_COLS), lambda b, c: (b, 0, 0))],
        out_shape=[jax.ShapeDtypeStruct((batch, seq, BRANCH_W), BF16),
                   jax.ShapeDtypeStruct((batch, RW_HEADS, RW_HD, RW_HD), F32),
                   jax.ShapeDtypeStruct((batch, 1, RW_COLS), F32)],
        scratch_shapes=[pltpu.VMEM((8 * g, RW_COLS), F32)],
        compiler_params=_params(("arbitrary", "arbitrary"), 56),
        name="rwkv_prompt",
    )(p3, *prm)
    return o.reshape(batch * seq, BRANCH_W), s_new, sh.reshape(batch, RW_COLS)


def _rwkv_sample(p_rw, prev_rows, s0, prm, seq):
    n = p_rw.shape[0]
    nseq = RW_ROWS // seq
    batch = n // seq
    g = RW_GROUPS
    ng = n // RW_ROWS
    o, s_new, sh = pl.pallas_call(
        _rwkv_sample_kernel,
        grid=(ng // g,),
        in_specs=[pl.BlockSpec((g, RW_ROWS, RW_COLS), lambda i: (i, 0, 0)),
                  pl.BlockSpec((g, RW_ROWS, RW_COLS), lambda i: (i, 0, 0)),
                  pl.BlockSpec((g * nseq, RW_HEADS, RW_HD, RW_HD), lambda i: (i, 0, 0, 0))] + _rw_param_specs(),
        out_specs=[pl.BlockSpec((g, RW_ROWS, BRANCH_W), lambda i: (i, 0, 0)),
                   pl.BlockSpec((g * nseq, RW_HEADS, RW_HD, RW_HD), lambda i: (i, 0, 0, 0)),
                   pl.BlockSpec((g * nseq, RW_COLS), lambda i: (i, 0))],
        out_shape=[jax.ShapeDtypeStruct((ng, RW_ROWS, BRANCH_W), BF16),
                   jax.ShapeDtypeStruct((batch, RW_HEADS, RW_HD, RW_HD), F32),
                   jax.ShapeDtypeStruct((batch, RW_COLS), F32)],
        compiler_params=_params(("arbitrary",), 56),
        name="rwkv_sample",
    )(p_rw.reshape(ng, RW_ROWS, RW_COLS), prev_rows.reshape(ng, RW_ROWS, RW_COLS), s0, *prm)
    return o.reshape(n, BRANCH_W), s_new, sh


def _attend_all(qkv):
    s = [_dot_nt(q, k) * (XA_HD ** -0.5) for q, k, _ in qkv]
    e = [jnp.exp(x - jnp.max(x, axis=-1, keepdims=True)) for x in s]
    pr = [x / jnp.sum(x, axis=-1, keepdims=True) for x in e]
    return [_dot(p, v) for p, (_, _, v) in zip(pr, qkv)]


def _xattn_prompt_kernel(q_ref, k_ref, v_ref, o_ref):
    sls = [slice(h * XA_HD, (h + 1) * XA_HD) for h in range(XA_HEADS)]
    outs = _attend_all([(q_ref[:, sl], k_ref[:, sl], v_ref[:, sl]) for sl in sls])
    for sl, o in zip(sls, outs):
        o_ref[:, sl] = o.astype(o_ref.dtype)


def _xattn_prompt(q, mk, mv, batch, seq, tq=512):
    nt = seq // tq
    return pl.pallas_call(
        _xattn_prompt_kernel,
        grid=(batch, nt),
        in_specs=[pl.BlockSpec((tq, BRANCH_W), lambda b, i: (b * nt + i, 0)),
                  pl.BlockSpec((N_MEM, BRANCH_W), lambda b, i: (b, 0)),
                  pl.BlockSpec((N_MEM, BRANCH_W), lambda b, i: (b, 0))],
        out_specs=pl.BlockSpec((tq, BRANCH_W), lambda b, i: (b * nt + i, 0)),
        out_shape=jax.ShapeDtypeStruct((batch * seq, BRANCH_W), BF16),
        compiler_params=_params(("arbitrary", "arbitrary"), 32),
        name="xattn_prompt",
    )(q, mk, mv)


def _xattn_sample_kernel(q_ref, k_ref, v_ref, o_ref):
    nb = k_ref.shape[0]
    ls = q_ref.shape[0] // nb
    where = [(b, slice(b * ls, (b + 1) * ls), slice(h * XA_HD, (h + 1) * XA_HD))
             for b in range(nb) for h in range(XA_HEADS)]
    outs = _attend_all([(q_ref[rows, sl], k_ref[b, :, sl], v_ref[b, :, sl]) for b, rows, sl in where])
    for (b, rows, sl), o in zip(where, outs):
        o_ref[rows, sl] = o.astype(o_ref.dtype)


def _xattn_sample(q, mk, mv, seq, nb=8):
    batch = mk.shape[0]
    mem_spec = pl.BlockSpec((nb, N_MEM, BRANCH_W), lambda i: (i, 0, 0))
    return pl.pallas_call(
        _xattn_sample_kernel,
        grid=(batch // nb,),
        in_specs=[pl.BlockSpec((nb * seq, BRANCH_W), lambda i: (i, 0)), mem_spec, mem_spec],
        out_specs=pl.BlockSpec((nb * seq, BRANCH_W), lambda i: (i, 0)),
        out_shape=jax.ShapeDtypeStruct((batch * seq, BRANCH_W), BF16),
        compiler_params=_params(("arbitrary",), 40),
        name="xattn_sample",
    )(q, mk, mv)


def _merge_kernel(x_ref, gm_ref, rw_ref, xa_ref, gt_ref, gb_ref, wb_ref, wo_ref, n2_ref,
                  wr_hi_ref, wr_lo_ref, br_ref, h_ref, hn_ref, route_ref, ridx_ref, cnt_ref):
    merged = None
    for n, br in enumerate((gm_ref, rw_ref, xa_ref)):
        cs = slice(n * D_MODEL, (n + 1) * D_MODEL)
        up = jnp.dot(br[...], wb_ref[n], preferred_element_type=F32)
        term = _sigmoid(gt_ref[:, cs] + gb_ref[:, cs]) * up
        merged = term if merged is None else merged + term
    h = x_ref[...] + _dot(merged, wo_ref[...])
    h_ref[...] = h
    hn = _rmsnorm(h, n2_ref[...])
    hn_ref[...] = hn

    hn_hi = hn.astype(BF16)
    hn_lo = (hn - hn_hi.astype(F32)).astype(BF16)
    logits = (jnp.dot(hn_hi, wr_hi_ref[...], preferred_element_type=F32)
              + jnp.dot(hn_lo, wr_hi_ref[...], preferred_element_type=F32)
              + jnp.dot(hn_hi, wr_lo_ref[...], preferred_element_type=F32)) + br_ref[...]
    lane = lax.broadcasted_iota(jnp.int32, logits.shape, 1)
    neg = -jnp.inf
    big = jnp.int32(1 << 20)
    gmask = (lane >= N_EXPERTS) & (lane < N_EXPERTS + N_GROUPS)
    gl = jnp.where(gmask, logits, neg)
    gmax = jnp.max(gl, axis=-1, keepdims=True)
    gsel = jnp.min(jnp.where(gl == gmax, lane, big), axis=-1, keepdims=True) - N_EXPERTS
    gsum = jnp.sum(jnp.where(gmask, jnp.exp(logits - gmax), 0.0), axis=-1, keepdims=True)
    pg_top = 1.0 / gsum
    emask = (lane >= gsel * EXP_PER_GROUP) & (lane < (gsel + 1) * EXP_PER_GROUP)
    el = jnp.where(emask, logits, neg)
    m1 = jnp.max(el, axis=-1, keepdims=True)
    i1 = jnp.min(jnp.where(el == m1, lane, big), axis=-1, keepdims=True)
    el2 = jnp.where(lane == i1, neg, el)
    m2 = jnp.max(el2, axis=-1, keepdims=True)
    i2 = jnp.min(jnp.where(el2 == m2, lane, big), axis=-1, keepdims=True)
    t2 = jnp.exp(m2 - m1)
    w1 = pg_top / (1.0 + t2)
    w2 = pg_top * t2 / (1.0 + t2)

    @pl.when(pl.program_id(0) == 0)
    def _():
        cnt_ref[...] = jnp.zeros_like(cnt_ref)

    tm = logits.shape[0]
    onehot = jnp.where(lane == i1, 1.0, 0.0) + jnp.where(lane == i2, 1.0, 0.0)
    ri = lax.broadcasted_iota(jnp.int32, (tm, tm), 0)
    ci = lax.broadcasted_iota(jnp.int32, (tm, tm), 1)
    earlier = jnp.where(ci < ri, 1.0, 0.0).astype(BF16)
    base = cnt_ref[0:1, :]
    seen = base + jnp.dot(earlier, onehot.astype(BF16), preferred_element_type=F32)
    r1 = jnp.sum(jnp.where(lane == i1, seen, 0.0), axis=-1, keepdims=True)
    r2 = jnp.sum(jnp.where(lane == i2, seen, 0.0), axis=-1, keepdims=True)
    cnt_ref[0:1, :] = base + jnp.sum(onehot, axis=0, keepdims=True)

    route = jnp.zeros(logits.shape, F32)
    for col, val in ((ROUTE_E1, i1.astype(F32)), (ROUTE_E2, i2.astype(F32)), (ROUTE_W1, w1), (ROUTE_W2, w2),
                     (ROUTE_R1, r1), (ROUTE_R2, r2)):
        route = jnp.where(lane == col, val, route)
    route_ref[...] = route
    ridx_ref[...] = jnp.transpose(route)[0:8, :].astype(jnp.int32)


def _merge(x, o_gm, o_rw, o_xa, p_gate, gate_b, w_branch, w_out, n2_g, wr_hi, wr_lo, b_r, tm=256):
    n = x.shape[0]
    row = lambda wd: pl.BlockSpec((tm, wd), lambda i: (i, 0))
    return pl.pallas_call(
        _merge_kernel,
        grid=(n // tm,),
        in_specs=[row(D_MODEL), row(BRANCH_W), row(BRANCH_W), row(BRANCH_W), row(GATE_COLS),
                  _full((1, GATE_COLS)), _full((N_BRANCH, BRANCH_W, D_MODEL)), _full((D_MODEL, D_MODEL)),
                  _full((1, D_MODEL)), _full((D_MODEL, LANES)), _full((D_MODEL, LANES)), _full((1, LANES))],
        out_specs=[row(D_MODEL), row(D_MODEL), row(LANES),
                   pl.BlockSpec((8, tm), lambda i: (0, i)), _full((8, LANES))],
        out_shape=[jax.ShapeDtypeStruct((n, D_MODEL), F32), jax.ShapeDtypeStruct((n, D_MODEL), F32),
                   jax.ShapeDtypeStruct((n, LANES), F32), jax.ShapeDtypeStruct((8, n), jnp.int32),
                   jax.ShapeDtypeStruct((8, LANES), F32)],
        compiler_params=_params(("arbitrary",), 40),
        name="merge",
    )(x, o_gm, o_rw, o_xa, p_gate, gate_b, w_branch, w_out, n2_g, wr_hi, wr_lo, b_r)


def _route_plan(counts, n, tm):
    n_tiles = (2 * n) // tm + N_EXPERTS
    cnt = counts[0, :N_EXPERTS].astype(jnp.int32)
    tiles = (cnt + tm - 1) // tm
    tile_end = jnp.cumsum(tiles)
    row_start = (tile_end - tiles) * tm
    n_used = tile_end[N_EXPERTS - 1:]
    pad_tile_row = jnp.where(tiles > 0, (tile_end - 1) * tm, -1)
    t = jnp.minimum(jnp.arange(n_tiles, dtype=jnp.int32), n_used - 1)
    tile_expert = jnp.sum((t[:, None] >= tile_end[None, :]).astype(jnp.int32), axis=1)
    return row_start, pad_tile_row, tile_expert, n_used, n_tiles


def _row_copy(src_ref, src_row, dst_ref, dst_row, sem):
    return pltpu.make_async_copy(src_ref.at[pl.ds(src_row, 1)], dst_ref.at[pl.ds(dst_row, 1)], sem)


def _tile_rows_to_smem(rs_ref, ridx_ref, pos_v, pos_s, sem):
    ridx = ridx_ref[...]
    for c, (ce, cr) in enumerate(((ROUTE_E1, ROUTE_R1), (ROUTE_E2, ROUTE_R2))):
        e = ridx[ce:ce + 1, :]
        pos = ridx[cr:cr + 1, :]
        for x in range(N_EXPERTS):
            pos = pos + jnp.where(e == x, rs_ref[x], 0)
        pos_v[c:c + 1, :] = pos
    cp = pltpu.make_async_copy(pos_v, pos_s, sem)
    cp.start()
    cp.wait()


def _dispatch_kernel(rs_ref, pad_ref, nu_ref, ridx_ref, hn_ref, xs_ref, zero_ref, pos_v, pos_s, sem, psem):
    tm = hn_ref.shape[0]
    n_tiles = xs_ref.shape[0] // tm

    @pl.when(pl.program_id(0) == 0)
    def _():
        zero_ref[...] = jnp.zeros_like(zero_ref)

        def fill(row):
            return pltpu.make_async_copy(zero_ref, xs_ref.at[pl.ds(pl.multiple_of(row, tm), tm)], sem)

        def each_fill(act):
            for e in range(N_EXPERTS):
                @pl.when(pad_ref[e] >= 0)
                def _():
                    act(fill(pad_ref[e]))

            def tail(t, c):
                act(fill(t * tm))
                return c

            lax.fori_loop(nu_ref[0], n_tiles, tail, 0)

        each_fill(lambda f: f.start())
        each_fill(lambda f: f.wait())

    _tile_rows_to_smem(rs_ref, ridx_ref, pos_v, pos_s, psem)

    def issue(i, c):
        _row_copy(hn_ref, i, xs_ref, pos_s[0, i], sem).start(priority=0)
        _row_copy(hn_ref, i, xs_ref, pos_s[1, i], sem).start(priority=1)
        return c

    def drain(i, c):
        _row_copy(hn_ref, 0, xs_ref, 0, sem).wait()
        return c

    lax.fori_loop(0, tm, issue, 0, unroll=8)
    lax.fori_loop(0, 2 * tm, drain, 0, unroll=8)


def _dispatch(hn, ridx, row_start, pad_tile_row, n_used, n_rows, tm):
    n = hn.shape[0]
    return pl.pallas_call(
        _dispatch_kernel,
        grid_spec=pltpu.PrefetchScalarGridSpec(
            num_scalar_prefetch=3,
            grid=(n // tm,),
            in_specs=[pl.BlockSpec((8, tm), lambda i, *_: (0, i)),
                      pl.BlockSpec((tm, D_MODEL), lambda i, *_: (i, 0))],
            out_specs=pl.BlockSpec(memory_space=pl.ANY),
            scratch_shapes=[pltpu.VMEM((tm, D_MODEL), F32), pltpu.VMEM((2, tm), jnp.int32),
                            pltpu.SMEM((2, tm), jnp.int32), pltpu.SemaphoreType.DMA, pltpu.SemaphoreType.DMA]),
        out_shape=jax.ShapeDtypeStruct((n_rows, D_MODEL), F32),
        compiler_params=_params(("arbitrary",), 32),
        name="moe_dispatch",
    )(row_start, pad_tile_row, n_used, ridx, hn)


def _experts_kernel(te_ref, nu_ref, x_ref, wg_ref, wu_ref, wd_ref, y_ref, wg_s, wu_s, wd_s):
    t = pl.program_id(0)
    used = t < nu_ref[0]

    @pl.when((t == 0) | (te_ref[t] != te_ref[jnp.maximum(t - 1, 0)]))
    def _():
        wg_s[...] = wg_ref[...].astype(BF16)
        wu_s[...] = wu_ref[...].astype(BF16)
        wd_s[...] = wd_ref[...].astype(BF16)

    @pl.when(used)
    def _():
        xb = x_ref[...].astype(BF16)
        gate = jnp.dot(xb, wg_s[...], preferred_element_type=F32)
        up = jnp.dot(xb, wu_s[...], preferred_element_type=F32)
        y_ref[...] = _dot(gate * _sigmoid(gate) * up, wd_s[...])

    @pl.when(jnp.logical_not(used))
    def _():
        y_ref[...] = jnp.zeros_like(y_ref)


def _experts(xs, tile_expert, n_used, wg, wu, wd, tm):
    n_rows = xs.shape[0]
    rows = lambda t, te, nu: (jnp.minimum(t, nu[0] - 1), 0)
    return pl.pallas_call(
        _experts_kernel,
        grid_spec=pltpu.PrefetchScalarGridSpec(
            num_scalar_prefetch=2,
            grid=(n_rows // tm,),
            in_specs=[pl.BlockSpec((tm, D_MODEL), rows),
                      pl.BlockSpec((None, D_MODEL, EXP_FF), lambda t, te, nu: (te[t], 0, 0)),
                      pl.BlockSpec((None, D_MODEL, EXP_FF), lambda t, te, nu: (te[t], 0, 0)),
                      pl.BlockSpec((None, EXP_FF, D_MODEL), lambda t, te, nu: (te[t], 0, 0))],
            out_specs=pl.BlockSpec((tm, D_MODEL), lambda t, te, nu: (t, 0)),
            scratch_shapes=[pltpu.VMEM((D_MODEL, EXP_FF), BF16), pltpu.VMEM((D_MODEL, EXP_FF), BF16),
                            pltpu.VMEM((EXP_FF, D_MODEL), BF16)]),
        out_shape=jax.ShapeDtypeStruct((n_rows, D_MODEL), F32),
        compiler_params=_params(("arbitrary",), 40),
        name="moe_experts",
    )(tile_expert, n_used, xs, wg, wu, wd)


def _combine_kernel(rs_ref, ridx_ref, h_ref, route_ref, fg_ref, ys_ref, y_ref, o1_ref, o2_ref, pos_v, pos_s,
                    sem, psem):
    tm = h_ref.shape[0]
    _tile_rows_to_smem(rs_ref, ridx_ref, pos_v, pos_s, psem)

    def issue(i, c):
        _row_copy(ys_ref, pos_s[0, i], o1_ref, i, sem).start(priority=0)
        _row_copy(ys_ref, pos_s[1, i], o2_ref, i, sem).start(priority=1)
        return c

    def drain(i, c):
        _row_copy(ys_ref, 0, o1_ref, 0, sem).wait()
        return c

    lax.fori_loop(0, tm, issue, 0, unroll=8)
    lax.fori_loop(0, 2 * tm, drain, 0, unroll=8)
    rt = route_ref[...]
    lane = lax.broadcasted_iota(jnp.int32, rt.shape, 1)
    w1 = jnp.sum(jnp.where(lane == ROUTE_W1, rt, 0.0), axis=-1, keepdims=True)
    w2 = jnp.sum(jnp.where(lane == ROUTE_W2, rt, 0.0), axis=-1, keepdims=True)
    y_ref[...] = _rmsnorm(h_ref[...] + w1 * o1_ref[...] + w2 * o2_ref[...], fg_ref[...])


def _combine(h, route, ridx, row_start, ys, final_g, tm):
    n = h.shape[0]
    return pl.pallas_call(
        _combine_kernel,
        grid_spec=pltpu.PrefetchScalarGridSpec(
            num_scalar_prefetch=1,
            grid=(n // tm,),
            in_specs=[pl.BlockSpec((8, tm), lambda i, rs: (0, i)),
                      pl.BlockSpec((tm, D_MODEL), lambda i, rs: (i, 0)),
                      pl.BlockSpec((tm, LANES), lambda i, rs: (i, 0)),
                      pl.BlockSpec((1, D_MODEL), lambda i, rs: (0, 0)),
                      pl.BlockSpec(memory_space=pl.ANY)],
            out_specs=pl.BlockSpec((tm, D_MODEL), lambda i, rs: (i, 0)),
            scratch_shapes=[pltpu.VMEM((tm, D_MODEL), F32), pltpu.VMEM((tm, D_MODEL), F32),
                            pltpu.VMEM((2, tm), jnp.int32), pltpu.SMEM((2, tm), jnp.int32),
                            pltpu.SemaphoreType.DMA, pltpu.SemaphoreType.DMA]),
        out_shape=jax.ShapeDtypeStruct((n, D_MODEL), F32),
        compiler_params=_params(("arbitrary",), 32),
        name="moe_combine",
    )(row_start, ridx, h, route, final_g, ys)


def _moe(h, hn, route, ridx, counts, wg, wu, wd, final_g, tm=256):
    row_start, pad_tile_row, tile_expert, n_used, n_tiles = _route_plan(counts, h.shape[0], tm)
    xs = _dispatch(hn, ridx, row_start, pad_tile_row, n_used, n_tiles * tm, tm)
    ys = _experts(xs, tile_expert, n_used, wg, wu, wd, tm)
    return _combine(h, route, ridx, row_start, ys, final_g, tm)


def _layer(x2d, mem_k, mem_v, rw_state, w, *, prompt, batch, seq):
    n = x2d.shape[0]
    tm_in = 256
    grp = "prompt" if prompt else "sample"
    o_gm, p_rw, p_q, p_gate, *vn = _in_proj(x2d, w["norm1_g"], w["w_in_segs"], w["gm_ln_g"], w["gm_ln_b"],
                                            w["gm_mix_" + grp], w["gm_bias_" + grp], not prompt, tm_in)
    vn = vn[0] if vn else None
    if prompt:
        o_rw, s_new, sh_new = _rwkv_prompt(p_rw, w["rw_prm"], batch, seq)
        o_xa = _xattn_prompt(p_q, mem_k, mem_v, batch, seq)
    else:
        s0, shift = rw_state
        prev_rows = jnp.pad(shift[:, None, :], ((0, 0), (0, seq - 1), (0, 0))).reshape(n, RW_COLS)
        o_rw, s_new, sh_new = _rwkv_sample(p_rw, prev_rows, s0, w["rw_prm"], seq)
        o_xa = _xattn_sample(p_q, mem_k, mem_v, seq)
    h, hn, route, ridx, counts = _merge(x2d, o_gm, o_rw, o_xa, p_gate, w["gate_b"], w["w_branch"], w["w_out"],
                                        w["norm2_g"], w["wr_hi"], w["wr_lo"], w["b_r"])
    y = _moe(h, hn, route, ridx, counts, w["e_wg"], w["e_wu"], w["e_wd"], w["final_g"])
    return y, s_new, sh_new, vn


def kernel(x_prompt, x_sample, state_rwkv_S, state_rwkv_shift, cache_mem_k, cache_mem_v, mem_prompt, norm1_g, w_in, gate_b, gm_ln_g, gm_ln_b, gm_ws, gm_bs, rw_mu, rw_w0, rw_w2, rw_a0, rw_a2, rw_g2, rw_k_k, rw_k_a, rw_r_k, rw_lnx_g, rw_lnx_b, xa_wk, xa_wv, w_branch, w_out, norm2_g, rg_w, rg_b, re_w, re_b, e_wg, e_wu, e_wd, final_g):
    bp, seq_p, _ = x_prompt.shape
    bs, seq_s, _ = x_sample.shape
    depth = w_in.shape[0]
    assert depth == 1 and seq_p % CHUNK == 0 and RW_ROWS % seq_s == 0 and CHUNK % seq_s == 0

    l = 0
    row = lambda a: a.reshape(1, -1)
    seg = (0, 2 * BRANCH_W, 2 * BRANCH_W + RW_COLS, 3 * BRANCH_W + RW_COLS, 3 * BRANCH_W + RW_COLS + GATE_COLS)
    w_causal = jnp.tril(gm_ws[l])
    nrep = CHUNK // seq_s
    blk = w_causal[:, :seq_s, :seq_s]
    eye_rep = jnp.eye(nrep, dtype=F32)
    mix_sample = jnp.einsum("ab,gts->gatbs", eye_rep, blk).reshape(GM_GROUPS, CHUNK, CHUNK)
    bias_prompt = jnp.repeat(gm_bs[l].T, GM_HD, axis=1)
    bias_sample = jnp.tile(bias_prompt[:seq_s], (nrep, 1))

    wr = jnp.zeros((D_MODEL, LANES), F32)
    wr = wr.at[:, :N_EXPERTS].set(jnp.transpose(re_w[l], (1, 0, 2)).reshape(D_MODEL, N_EXPERTS))
    wr = wr.at[:, N_EXPERTS:N_EXPERTS + N_GROUPS].set(rg_w[l])
    wr_hi = wr.astype(BF16)
    wr_lo = (wr - wr_hi.astype(F32)).astype(BF16)
    b_r = jnp.zeros((1, LANES), F32)
    b_r = b_r.at[0, :N_EXPERTS].set(re_b[l].reshape(-1)).at[0, N_EXPERTS:N_EXPERTS + N_GROUPS].set(rg_b[l])

    w = dict(
        norm1_g=row(norm1_g[l]),
        w_in_segs=[w_in[l][:, a:b].astype(BF16) for a, b in zip(seg[:-1], seg[1:])],
        gm_ln_g=row(gm_ln_g[l]), gm_ln_b=row(gm_ln_b[l]),
        gm_mix_prompt=w_causal.astype(BF16), gm_bias_prompt=bias_prompt,
        gm_mix_sample=mix_sample.astype(BF16), gm_bias_sample=bias_sample,
        rw_prm=(row(rw_mu[l]), row(rw_w0[l]), rw_w2[l].astype(BF16), row(rw_a0[l]), rw_a2[l].astype(BF16),
                rw_g2[l].astype(BF16), row(rw_k_k[l]), row(rw_k_a[l]), row(rw_r_k[l]),
                row(rw_lnx_g[l]), row(rw_lnx_b[l])),
        gate_b=row(gate_b[l]), w_branch=w_branch[l].astype(BF16), w_out=w_out[l].astype(BF16),
        norm2_g=row(norm2_g[l]), wr_hi=wr_hi, wr_lo=wr_lo, b_r=b_r,
        e_wg=e_wg[l].reshape(N_EXPERTS, D_MODEL, EXP_FF),
        e_wu=e_wu[l].reshape(N_EXPERTS, D_MODEL, EXP_FF),
        e_wd=e_wd[l].reshape(N_EXPERTS, EXP_FF, D_MODEL),
        final_g=row(final_g),
    )

    mk_p, mv_p = _mem_kv(mem_prompt.reshape(bp * N_MEM, D_MODEL), xa_wk[l].astype(BF16), xa_wv[l].astype(BF16))
    yp, sp, shp, _ = _layer(x_prompt.reshape(bp * seq_p, D_MODEL), mk_p, mv_p, None, w,
                            prompt=True, batch=bp, seq=seq_p)
    ys, ss, shs, vs = _layer(x_sample.reshape(bs * seq_s, D_MODEL),
                             cache_mem_k[l].reshape(bs, N_MEM, BRANCH_W).astype(BF16),
                             cache_mem_v[l].reshape(bs, N_MEM, BRANCH_W).astype(BF16),
                             (state_rwkv_S[l], state_rwkv_shift[l]), w, prompt=False, batch=bs, seq=seq_s)

    return (yp.reshape(bp, seq_p, D_MODEL), ys.reshape(bs, seq_s, D_MODEL),
            sp[None], shp[None],
            mk_p.reshape(1, bp, N_MEM, XA_HEADS, XA_HD), mv_p.reshape(1, bp, N_MEM, XA_HEADS, XA_HD),
            ss[None], shs[None], vs.reshape(1, bs, seq_s, BRANCH_W))
```

```python
import itertools

import numpy as np
import jax
import jax.numpy as jnp
from jax import lax
from jax.experimental import pallas as pl
from jax.experimental.pallas import tpu as pltpu

F32 = jnp.float32
BF16 = jnp.bfloat16

D_MODEL = 1024
BRANCH_W = 512
CHUNK = 128
GM_GROUPS = 8
GM_HD = BRANCH_W // GM_GROUPS
RW_HEADS = 8
RW_HD = BRANCH_W // RW_HEADS
RW_W_LORA = 64
RW_A_LORA = 64
RW_G_LORA = 128
RW_COLS = 3 * BRANCH_W + RW_W_LORA + RW_A_LORA + RW_G_LORA
XA_HEADS = 4
XA_HD = BRANCH_W // XA_HEADS
N_MEM = 256
N_BRANCH = 3
GATE_COLS = N_BRANCH * D_MODEL
N_GROUPS = 4
EXP_PER_GROUP = 8
N_EXPERTS = N_GROUPS * EXP_PER_GROUP
EXP_FF = 512
RMS_EPS = 1e-6
LN_EPS = 1e-5
GN_EPS = 64e-5

LANES = 128
RW_ROWS = 64
RW_GROUPS = 2
MIB = 1024 * 1024
ROUTE_E1, ROUTE_E2, ROUTE_W1, ROUTE_W2 = range(4)


def _dot(a, b):
    return jnp.dot(a.astype(BF16), b.astype(BF16), preferred_element_type=F32)


def _dot_nt(a, b):
    return lax.dot_general(a.astype(BF16), b.astype(BF16), (((1,), (1,)), ((), ())),
                           preferred_element_type=F32)


def _dot_tn(a, b):
    return lax.dot_general(a.astype(BF16), b.astype(BF16), (((0,), (0,)), ((), ())),
                           preferred_element_type=F32)


def _sigmoid(x):
    return 1.0 / (1.0 + jnp.exp(-x))


def _gelu(x):
    c = np.float32(np.sqrt(2.0 / np.pi))
    return x * (0.5 * (1.0 + jnp.tanh(c * (x + 0.044715 * (x * x * x)))))


def _softplus(x):
    return jnp.maximum(x, 0.0) + jnp.log(1.0 + jnp.exp(-jnp.abs(x)))


def _rmsnorm(x, g):
    return x * lax.rsqrt(jnp.mean(x * x, axis=-1, keepdims=True) + RMS_EPS) * g


def _split3(x):
    hi = x.astype(BF16)
    r1 = x - hi.astype(F32)
    mid = r1.astype(BF16)
    lo = (r1 - mid.astype(F32)).astype(BF16)
    return hi, mid, lo


def _params(sem, vmem_mib):
    return pltpu.CompilerParams(dimension_semantics=sem, vmem_limit_bytes=vmem_mib * MIB)


def _full(shape):
    nd = len(shape)
    return pl.BlockSpec(shape, lambda *_: (0,) * nd)


def _gmlp_chunk(pu, pv, ln_g, ln_b, wmix_ref, bmix_ref):
    u = _gelu(pu)
    vf = _gelu(pv)
    mu = jnp.mean(vf, axis=-1, keepdims=True)
    vc = vf - mu
    var = jnp.mean(vc * vc, axis=-1, keepdims=True)
    vn = vc * lax.rsqrt(var + LN_EPS) * ln_g + ln_b
    lane = lax.broadcasted_iota(jnp.int32, (CHUNK, LANES), 1)
    lo_half = lane < GM_HD
    outs = []
    for p in range(GM_GROUPS // 2):
        vp = vn[:, p * LANES:(p + 1) * LANES]
        s = (_dot(wmix_ref[2 * p], jnp.where(lo_half, vp, 0.0))
             + _dot(wmix_ref[2 * p + 1], jnp.where(lo_half, 0.0, vp)))
        outs.append(u[:, p * LANES:(p + 1) * LANES] * (s + bmix_ref[:, p * LANES:(p + 1) * LANES]))
    return jnp.concatenate(outs, axis=-1), vn


def _in_proj_kernel(x_ref, g_ref, wgm_ref, wrw_ref, wq_ref, wgt_ref, lng_ref, lnb_ref, wmix_ref, bmix_ref,
                    ogm_ref, rw_ref, q_ref, gt_ref, *vn_refs):
    xb = _rmsnorm(x_ref[...], g_ref[...]).astype(BF16)
    gm = jnp.dot(xb, wgm_ref[...], preferred_element_type=F32)
    rw_ref[...] = jnp.dot(xb, wrw_ref[...], preferred_element_type=F32)
    q_ref[...] = jnp.dot(xb, wq_ref[...], preferred_element_type=F32)
    gt_ref[...] = jnp.dot(xb, wgt_ref[...], preferred_element_type=F32)
    for c in range(x_ref.shape[0] // CHUNK):
        rows = slice(c * CHUNK, (c + 1) * CHUNK)
        o, vn = _gmlp_chunk(gm[rows, 0:BRANCH_W], gm[rows, BRANCH_W:2 * BRANCH_W], lng_ref[...], lnb_ref[...],
                            wmix_ref, bmix_ref)
        ogm_ref[rows, :] = o.astype(ogm_ref.dtype)
        if vn_refs:
            vn_refs[0][rows, :] = vn


def _in_proj(x, g, w_segs, ln_g, ln_b, wmix, bmix, emit_vn, tm):
    n = x.shape[0]
    widths = [w.shape[1] for w in w_segs]
    out_w = [BRANCH_W] + widths[1:] + ([BRANCH_W] if emit_vn else [])
    out_dt = [BF16] + [F32] * (len(out_w) - 1)
    return pl.pallas_call(
        _in_proj_kernel,
        grid=(n // tm,),
        in_specs=[pl.BlockSpec((tm, D_MODEL), lambda i: (i, 0)), _full((1, D_MODEL))]
        + [pl.BlockSpec((D_MODEL, wd), lambda i: (0, 0), pipeline_mode=pl.Buffered(1)) for wd in widths]
        + [_full((1, BRANCH_W)), _full((1, BRANCH_W)), _full((GM_GROUPS, CHUNK, CHUNK)),
           _full((CHUNK, BRANCH_W))],
        out_specs=[pl.BlockSpec((tm, wd), lambda i: (i, 0)) for wd in out_w],
        out_shape=[jax.ShapeDtypeStruct((n, wd), dt) for wd, dt in zip(out_w, out_dt)],
        compiler_params=_params(("arbitrary",), 48),
        name="in_proj",
    )(x, g, *w_segs, ln_g, ln_b, wmix, bmix)


def _mem_kv_kernel(m_ref, wk_ref, wv_ref, k_ref, v_ref):
    mb = m_ref[...].astype(BF16)
    k_ref[...] = jnp.dot(mb, wk_ref[...], preferred_element_type=F32)
    v_ref[...] = jnp.dot(mb, wv_ref[...], preferred_element_type=F32)


def _mem_kv(mem, wk, wv, tm=512):
    n = mem.shape[0]
    return pl.pallas_call(
        _mem_kv_kernel,
        grid=(n // tm,),
        in_specs=[pl.BlockSpec((tm, D_MODEL), lambda i: (i, 0)),
                  _full((D_MODEL, BRANCH_W)), _full((D_MODEL, BRANCH_W))],
        out_specs=[pl.BlockSpec((tm, BRANCH_W), lambda i: (i, 0))] * 2,
        out_shape=[jax.ShapeDtypeStruct((n, BRANCH_W), F32)] * 2,
        compiler_params=_params(("arbitrary",), 32),
        name="mem_kv",
    )(mem, wk, wv)


def _rwkv_rows(p, first, prev, s_ref, o_ref, prm, nseq):
    (mu, w0, w2, a0, a2, g2, k_k, k_a, r_k, lnx_g, lnx_b) = prm
    rows = RW_ROWS
    ls = rows // nseq
    shifted = jnp.where(first, prev, pltpu.roll(p, 1, 0))
    xs = p + (shifted - p) * mu
    r = xs[:, 0:BRANCH_W]
    k = xs[:, BRANCH_W:2 * BRANCH_W]
    v = xs[:, 2 * BRANCH_W:3 * BRANCH_W]
    o = 3 * BRANCH_W
    wd = xs[:, o:o + RW_W_LORA]
    ad = xs[:, o + RW_W_LORA:o + RW_W_LORA + RW_A_LORA]
    gd = xs[:, o + RW_W_LORA + RW_A_LORA:RW_COLS]

    w_log = -_softplus(-(w0 + _dot(jnp.tanh(wd), w2))) - 0.5
    logw = -jnp.exp(w_log)
    a = _sigmoid(a0 + _dot(ad, a2))
    g = _dot(_sigmoid(gd), g2)
    kkr = k * k_k
    kf = k * (1.0 + (a - 1.0) * k_a)
    rkr = r * kf * r_k

    ri = lax.broadcasted_iota(jnp.int32, (rows, rows), 0)
    ci = lax.broadcasted_iota(jnp.int32, (rows, rows), 1)
    if nseq == 1:
        same = ci >= 0
    else:
        sh = int(np.log2(ls))
        same = lax.shift_right_logical(ri, sh) == lax.shift_right_logical(ci, sh)
    low_incl = same & (ci <= ri)
    low_strict = same & (ci < ri)
    m_incl = jnp.where(low_incl, 1.0, 0.0).astype(BF16)
    m_same = jnp.where(same, 1.0, 0.0).astype(BF16)
    hi, mid, lo = _split3(logw)
    cum = (jnp.dot(m_incl, hi, preferred_element_type=F32) + jnp.dot(m_incl, mid, preferred_element_type=F32)
           + jnp.dot(m_incl, lo, preferred_element_type=F32))
    tot = (jnp.dot(m_same, hi, preferred_element_type=F32) + jnp.dot(m_same, mid, preferred_element_type=F32)
           + jnp.dot(m_same, lo, preferred_element_type=F32))
    g_t = jnp.exp(cum)
    g_prev = jnp.exp(cum - logw)
    g_inv = jnp.exp(-cum)
    g_end = jnp.exp(tot - cum)
    g_tot = jnp.exp(tot)
    eye = jnp.where(ri == ci, 1.0, 0.0)
    yield

    hs = range(RW_HEADS)
    sls = [slice(h * RW_HD, (h + 1) * RW_HD) for h in hs]
    kk = [kkr[:, sl] for sl in sls]
    kk = [x * lax.rsqrt(jnp.maximum(jnp.sum(x * x, axis=-1, keepdims=True), 1e-24)) for x in kk]
    bv = [kk[h] * a[:, sls[h]] for h in hs]
    k_h = [kf[:, sl] for sl in sls]
    v_h = [v[:, sl] for sl in sls]
    at = [-kk[h] * g_prev[:, sls[h]] for h in hs]
    rt = [r[:, sl] * g_t[:, sl] for sl in sls]
    bt = [bv[h] * g_inv[:, sls[h]] for h in hs]
    kt = [k_h[h] * g_inv[:, sls[h]] for h in hs]
    bh = [bv[h] * g_end[:, sls[h]] for h in hs]
    kh = [k_h[h] * g_end[:, sls[h]] for h in hs]

    ar = [jnp.concatenate([at[h], rt[h]], axis=0) for h in hs]
    pb = [_dot_nt(ar[h], bt[h]) for h in hs]
    pk = [_dot_nt(ar[h], kt[h]) for h in hs]
    l_ab = [jnp.where(low_strict, x[:rows], 0.0) for x in pb]
    l_ak = [jnp.where(low_strict, x[:rows], 0.0) for x in pk]
    a_rb = [jnp.where(low_incl, x[rows:], 0.0) for x in pb]
    a_rk = [jnp.where(low_incl, x[rows:], 0.0) for x in pk]
    yield

    tm = [eye + x for x in l_ab]
    pw = [_dot(x, x) for x in l_ab]
    n_dbl = int(np.log2(ls)) - 1
    for it in range(n_dbl):
        yield
        if it < n_dbl - 1:
            z = [_dot(jnp.concatenate([tm[h], pw[h]], axis=0), pw[h]) for h in hs]
            tm = [tm[h] + z[h][:rows] for h in hs]
            pw = [z[h][rows:] for h in hs]
        else:
            tm = [tm[h] + _dot(tm[h], pw[h]) for h in hs]

    if nseq == 1:
        ars = [_dot_nt(ar[h], s_ref[0, h]) for h in hs]
        as0 = [x[:rows] for x in ars]
        rs0 = [x[rows:] for x in ars]
    else:
        as0, rs0 = [], []
        for h in hs:
            zs = [_dot_nt(jnp.concatenate([at[h][b * ls:(b + 1) * ls], rt[h][b * ls:(b + 1) * ls]], axis=0),
                          s_ref[b, h]) for b in range(nseq)]
            as0.append(jnp.concatenate([x[:ls] for x in zs], axis=0))
            rs0.append(jnp.concatenate([x[ls:] for x in zs], axis=0))

    lv = [_dot(l_ak[h], v_h[h]) for h in hs]
    yield
    u = [_dot(tm[h], as0[h] + lv[h]) for h in hs]
    yield
    y = [rs0[h] + _dot(a_rb[h], u[h]) + _dot(a_rk[h], v_h[h]) for h in hs]
    yield

    for h in hs:
        for b in range(nseq):
            rb = slice(b * ls, (b + 1) * ls)
            uv = jnp.concatenate([u[h][rb], v_h[h][rb]], axis=0)
            bk = jnp.concatenate([bh[h][rb], kh[h][rb]], axis=0)
            s_ref[b, h] = s_ref[b, h] * g_tot[b * ls:b * ls + 1, sls[h]] + _dot_tn(uv, bk)

    outs = []
    for h in hs:
        ym = jnp.mean(y[h], axis=-1, keepdims=True)
        yc = y[h] - ym
        yv = jnp.mean(yc * yc, axis=-1, keepdims=True)
        yn = yc * lax.rsqrt(yv + GN_EPS) * lnx_g[:, sls[h]] + lnx_b[:, sls[h]]
        bonus = jnp.sum(rkr[:, sls[h]], axis=-1, keepdims=True) * v_h[h]
        outs.append((yn + bonus) * g[:, sls[h]])
    o_ref[...] = jnp.concatenate(outs, axis=-1).astype(o_ref.dtype)


def _interleave(gens):
    for _ in itertools.zip_longest(*gens):
        pass


def _rwkv_prompt_kernel(p_ref, *refs):
    prm_refs, (o_ref, s_ref, sh_ref, carry_ref) = refs[:11], refs[11:]
    c = pl.program_id(1)

    @pl.when(c == 0)
    def _():
        s_ref[...] = jnp.zeros_like(s_ref)
        carry_ref[...] = jnp.zeros_like(carry_ref)

    first = lax.broadcasted_iota(jnp.int32, (RW_ROWS, 1), 0) == 0
    prm = tuple(x[...] for x in prm_refs)
    groups = range(p_ref.shape[0])
    _interleave([_rwkv_rows(p_ref[g], first, carry_ref[8 * g:8 * g + 1, :], s_ref.at[pl.ds(g, 1)],
                            o_ref.at[g], prm, nseq=1) for g in groups])
    for g in groups:
        last = p_ref[g, RW_ROWS - 1:RW_ROWS, :]
        carry_ref[8 * g:8 * g + 1, :] = last
        sh_ref[g] = last


def _rwkv_sample_kernel(p_ref, prev_ref, s0_ref, *refs):
    prm_refs, (o_ref, s_ref, sh_ref) = refs[:11], refs[11:]
    nseq = s0_ref.shape[0] // p_ref.shape[0]
    ls = RW_ROWS // nseq
    s_ref[...] = s0_ref[...]
    first = (lax.broadcasted_iota(jnp.int32, (RW_ROWS, 1), 0) & (ls - 1)) == 0
    prm = tuple(x[...] for x in prm_refs)
    groups = range(p_ref.shape[0])
    _interleave([_rwkv_rows(p_ref[g], first, prev_ref[g], s_ref.at[pl.ds(g * nseq, nseq)], o_ref.at[g],
                            prm, nseq=nseq) for g in groups])
    for g in groups:
        for b in range(nseq):
            sh_ref[g * nseq + b:g * nseq + b + 1, :] = p_ref[g, (b + 1) * ls - 1:(b + 1) * ls, :]


def _rw_param_specs():
    shapes = [(1, RW_COLS), (1, BRANCH_W), (RW_W_LORA, BRANCH_W), (1, BRANCH_W), (RW_A_LORA, BRANCH_W),
              (RW_G_LORA, BRANCH_W), (1, BRANCH_W), (1, BRANCH_W), (1, BRANCH_W), (1, BRANCH_W), (1, BRANCH_W)]
    return [_full(s) for s in shapes]


def _rwkv_prompt(p_rw, prm, batch, seq):
    nc = seq // RW_ROWS
    g = RW_GROUPS
    p3 = p_rw.reshape(batch, seq, RW_COLS)
    o, s_new, sh = pl.pallas_call(
        _rwkv_prompt_kernel,
        grid=(batch // g, nc),
        in_specs=[pl.BlockSpec((g, RW_ROWS, RW_COLS), lambda b, c: (b, c, 0))] + _rw_param_specs(),
        out_specs=[pl.BlockSpec((g, RW_ROWS, BRANCH_W), lambda b, c: (b, c, 0)),
                   pl.BlockSpec((g, RW_HEADS, RW_HD, RW_HD), lambda b, c: (b, 0, 0, 0)),
                   pl.BlockSpec((g, 1, RW_COLS), lambda b, c: (b, 0, 0))],
        out_shape=[jax.ShapeDtypeStruct((batch, seq, BRANCH_W), BF16),
                   jax.ShapeDtypeStruct((batch, RW_HEADS, RW_HD, RW_HD), F32),
                   jax.ShapeDtypeStruct((batch, 1, RW_COLS), F32)],
        scratch_shapes=[pltpu.VMEM((8 * g, RW_COLS), F32)],
        compiler_params=_params(("arbitrary", "arbitrary"), 56),
        name="rwkv_prompt",
    )(p3, *prm)
    return o.reshape(batch * seq, BRANCH_W), s_new, sh.reshape(batch, RW_COLS)


def _rwkv_sample(p_rw, prev_rows, s0, prm, seq):
    n = p_rw.shape[0]
    nseq = RW_ROWS // seq
    batch = n // seq
    g = RW_GROUPS
    ng = n // RW_ROWS
    o, s_new, sh = pl.pallas_call(
        _rwkv_sample_kernel,
        grid=(ng // g,),
        in_specs=[pl.BlockSpec((g, RW_ROWS, RW_COLS), lambda i: (i, 0, 0)),
                  pl.BlockSpec((g, RW_ROWS, RW_COLS), lambda i: (i, 0, 0)),
                  pl.BlockSpec((g * nseq, RW_HEADS, RW_HD, RW_HD), lambda i: (i, 0, 0, 0))] + _rw_param_specs(),
        out_specs=[pl.BlockSpec((g, RW_ROWS, BRANCH_W), lambda i: (i, 0, 0)),
                   pl.BlockSpec((g * nseq, RW_HEADS, RW_HD, RW_HD), lambda i: (i, 0, 0, 0)),
                   pl.BlockSpec((g * nseq, RW_COLS), lambda i: (i, 0))],
        out_shape=[jax.ShapeDtypeStruct((ng, RW_ROWS, BRANCH_W), BF16),
                   jax.ShapeDtypeStruct((batch, RW_HEADS, RW_HD, RW_HD), F32),
                   jax.ShapeDtypeStruct((batch, RW_COLS), F32)],
        compiler_params=_params(("arbitrary",), 56),
        name="rwkv_sample",
    )(p_rw.reshape(ng, RW_ROWS, RW_COLS), prev_rows.reshape(ng, RW_ROWS, RW_COLS), s0, *prm)
    return o.reshape(n, BRANCH_W), s_new, sh


def _attend_all(qkv):
    s = [_dot_nt(q, k) * (XA_HD ** -0.5) for q, k, _ in qkv]
    e = [jnp.exp(x - jnp.max(x, axis=-1, keepdims=True)) for x in s]
    pr = [x / jnp.sum(x, axis=-1, keepdims=True) for x in e]
    return [_dot(p, v) for p, (_, _, v) in zip(pr, qkv)]


def _xattn_prompt_kernel(q_ref, k_ref, v_ref, o_ref):
    sls = [slice(h * XA_HD, (h + 1) * XA_HD) for h in range(XA_HEADS)]
    outs = _attend_all([(q_ref[:, sl], k_ref[:, sl], v_ref[:, sl]) for sl in sls])
    for sl, o in zip(sls, outs):
        o_ref[:, sl] = o.astype(o_ref.dtype)


def _xattn_prompt(q, mk, mv, batch, seq, tq=512):
    nt = seq // tq
    return pl.pallas_call(
        _xattn_prompt_kernel,
        grid=(batch, nt),
        in_specs=[pl.BlockSpec((tq, BRANCH_W), lambda b, i: (b * nt + i, 0)),
                  pl.BlockSpec((N_MEM, BRANCH_W), lambda b, i: (b, 0)),
                  pl.BlockSpec((N_MEM, BRANCH_W), lambda b, i: (b, 0))],
        out_specs=pl.BlockSpec((tq, BRANCH_W), lambda b, i: (b * nt + i, 0)),
        out_shape=jax.ShapeDtypeStruct((batch * seq, BRANCH_W), BF16),
        compiler_params=_params(("arbitrary", "arbitrary"), 32),
        name="xattn_prompt",
    )(q, mk, mv)


def _xattn_sample_kernel(q_ref, k_ref, v_ref, o_ref):
    nb = k_ref.shape[0]
    ls = q_ref.shape[0] // nb
    where = [(b, slice(b * ls, (b + 1) * ls), slice(h * XA_HD, (h + 1) * XA_HD))
             for b in range(nb) for h in range(XA_HEADS)]
    outs = _attend_all([(q_ref[rows, sl], k_ref[b, :, sl], v_ref[b, :, sl]) for b, rows, sl in where])
    for (b, rows, sl), o in zip(where, outs):
        o_ref[rows, sl] = o.astype(o_ref.dtype)


def _xattn_sample(q, mk, mv, seq, nb=8):
    batch = mk.shape[0]
    mem_spec = pl.BlockSpec((nb, N_MEM, BRANCH_W), lambda i: (i, 0, 0))
    return pl.pallas_call(
        _xattn_sample_kernel,
        grid=(batch // nb,),
        in_specs=[pl.BlockSpec((nb * seq, BRANCH_W), lambda i: (i, 0)), mem_spec, mem_spec],
        out_specs=pl.BlockSpec((nb * seq, BRANCH_W), lambda i: (i, 0)),
        out_shape=jax.ShapeDtypeStruct((batch * seq, BRANCH_W), BF16),
        compiler_params=_params(("arbitrary",), 40),
        name="xattn_sample",
    )(q, mk, mv)


def _merge_kernel(x_ref, gm_ref, rw_ref, xa_ref, gt_ref, gb_ref, wb_ref, wo_ref, n2_ref,
                  wr_hi_ref, wr_lo_ref, br_ref, h_ref, hn_ref, route_ref, ridx_ref, cnt_ref):
    merged = None
    for n, br in enumerate((gm_ref, rw_ref, xa_ref)):
        cs = slice(n * D_MODEL, (n + 1) * D_MODEL)
        up = jnp.dot(br[...], wb_ref[n], preferred_element_type=F32)
        term = _sigmoid(gt_ref[:, cs] + gb_ref[:, cs]) * up
        merged = term if merged is None else merged + term
    h = x_ref[...] + _dot(merged, wo_ref[...])
    h_ref[...] = h
    hn = _rmsnorm(h, n2_ref[...])
    hn_ref[...] = hn

    hn_hi = hn.astype(BF16)
    hn_lo = (hn - hn_hi.astype(F32)).astype(BF16)
    logits = (jnp.dot(hn_hi, wr_hi_ref[...], preferred_element_type=F32)
              + jnp.dot(hn_lo, wr_hi_ref[...], preferred_element_type=F32)
              + jnp.dot(hn_hi, wr_lo_ref[...], preferred_element_type=F32)) + br_ref[...]
    lane = lax.broadcasted_iota(jnp.int32, logits.shape, 1)
    neg = -jnp.inf
    big = jnp.int32(1 << 20)
    gmask = (lane >= N_EXPERTS) & (lane < N_EXPERTS + N_GROUPS)
    gl = jnp.where(gmask, logits, neg)
    gmax = jnp.max(gl, axis=-1, keepdims=True)
    gsel = jnp.min(jnp.where(gl == gmax, lane, big), axis=-1, keepdims=True) - N_EXPERTS
    gsum = jnp.sum(jnp.where(gmask, jnp.exp(logits - gmax), 0.0), axis=-1, keepdims=True)
    pg_top = 1.0 / gsum
    emask = (lane >= gsel * EXP_PER_GROUP) & (lane < (gsel + 1) * EXP_PER_GROUP)
    el = jnp.where(emask, logits, neg)
    m1 = jnp.max(el, axis=-1, keepdims=True)
    i1 = jnp.min(jnp.where(el == m1, lane, big), axis=-1, keepdims=True)
    el2 = jnp.where(lane == i1, neg, el)
    m2 = jnp.max(el2, axis=-1, keepdims=True)
    i2 = jnp.min(jnp.where(el2 == m2, lane, big), axis=-1, keepdims=True)
    t2 = jnp.exp(m2 - m1)
    w1 = pg_top / (1.0 + t2)
    w2 = pg_top * t2 / (1.0 + t2)

    @pl.when(pl.program_id(0) == 0)
    def _():
        cnt_ref[...] = jnp.zeros_like(cnt_ref)

    onehot = jnp.where(lane == i1, 1.0, 0.0) + jnp.where(lane == i2, 1.0, 0.0)
    cnt_ref[0:1, :] = cnt_ref[0:1, :] + jnp.sum(onehot, axis=0, keepdims=True)

    route = jnp.zeros(logits.shape, F32)
    for col, val in ((ROUTE_E1, i1.astype(F32)), (ROUTE_E2, i2.astype(F32)), (ROUTE_W1, w1), (ROUTE_W2, w2)):
        route = jnp.where(lane == col, val, route)
    route_ref[...] = route
    ridx_ref[...] = jnp.transpose(route)[0:8, :].astype(jnp.int32)


def _merge(x, o_gm, o_rw, o_xa, p_gate, gate_b, w_branch, w_out, n2_g, wr_hi, wr_lo, b_r, tm=256):
    n = x.shape[0]
    row = lambda wd: pl.BlockSpec((tm, wd), lambda i: (i, 0))
    return pl.pallas_call(
        _merge_kernel,
        grid=(n // tm,),
        in_specs=[row(D_MODEL), row(BRANCH_W), row(BRANCH_W), row(BRANCH_W), row(GATE_COLS),
                  _full((1, GATE_COLS)), _full((N_BRANCH, BRANCH_W, D_MODEL)), _full((D_MODEL, D_MODEL)),
                  _full((1, D_MODEL)), _full((D_MODEL, LANES)), _full((D_MODEL, LANES)), _full((1, LANES))],
        out_specs=[row(D_MODEL), row(D_MODEL), row(LANES),
                   pl.BlockSpec((8, tm), lambda i: (0, i)), _full((8, LANES))],
        out_shape=[jax.ShapeDtypeStruct((n, D_MODEL), F32), jax.ShapeDtypeStruct((n, D_MODEL), F32),
                   jax.ShapeDtypeStruct((n, LANES), F32), jax.ShapeDtypeStruct((8, n), jnp.int32),
                   jax.ShapeDtypeStruct((8, LANES), F32)],
        compiler_params=_params(("arbitrary",), 40),
        name="merge",
    )(x, o_gm, o_rw, o_xa, p_gate, gate_b, w_branch, w_out, n2_g, wr_hi, wr_lo, b_r)


def _route_plan(ridx, counts, tm):
    n = ridx.shape[1]
    n2 = 2 * n
    assert n2 & (n2 - 1) == 0
    n_tiles = n2 // tm + N_EXPERTS
    eid = ridx[ROUTE_E1:ROUTE_E2 + 1].reshape(n2)
    a_sorted = jnp.sort(eid * n2 + jnp.arange(n2, dtype=jnp.int32)) & (n2 - 1)
    cnt = counts[0, :N_EXPERTS].astype(jnp.int32)
    tiles = (cnt + tm - 1) // tm
    tile_end = jnp.cumsum(tiles)
    first_tile = tile_end - tiles
    first_sorted = jnp.cumsum(cnt) - cnt
    n_used = tile_end[N_EXPERTS - 1:]
    t = jnp.arange(n_tiles, dtype=jnp.int32)
    tc = jnp.minimum(t, n_used - 1)
    tile_expert = jnp.sum((tc[:, None] >= tile_end[None, :]).astype(jnp.int32), axis=1)
    k = tc - first_tile[tile_expert]
    start = first_sorted[tile_expert] + k * tm
    n_valid = jnp.where(t < n_used, jnp.clip(cnt[tile_expert] - k * tm, 0, tm), 0)
    n_pad = jnp.where(t < n_used, tm - n_valid, 0)
    pad_base = jnp.cumsum(n_pad) - n_pad
    window = jax.vmap(lambda s0: lax.dynamic_slice(jnp.pad(a_sorted, (0, tm)), (s0,), (tm,)))(start)
    r = jnp.arange(tm, dtype=jnp.int32)[None, :]
    valid = r < n_valid[:, None]
    src = jnp.where(valid, window & (n - 1), 0)
    dst = jnp.where(valid, window, n2 + pad_base[:, None] + r - n_valid[:, None])
    return src.reshape(n_tiles, 1, tm), dst.reshape(n_tiles, 1, tm), tile_expert, n_used


def _row_copy(src_ref, src_row, dst_ref, dst_row, sem):
    return pltpu.make_async_copy(src_ref.at[pl.ds(src_row, 1)], dst_ref.at[pl.ds(dst_row, 1)], sem)


def _experts_kernel(te_ref, nu_ref, src_ref, nxt_ref, dst_ref, hn_ref, wg_ref, wu_ref, wd_ref, o_ref,
                    xbuf, ybuf, wg_s, wu_s, wd_s, gsem, ssem):
    t = pl.program_id(0)
    tm = xbuf.shape[1]
    n2 = o_ref.shape[0] - N_EXPERTS * tm
    slot = t & 1
    other = 1 - slot

    def gather(idx_ref, s):
        for r in range(tm):
            _row_copy(hn_ref, idx_ref[0, 0, r], xbuf.at[s], r, gsem.at[s]).start()

    def wait_rows(buf, sem, s):
        for _ in range(tm):
            _row_copy(hn_ref, 0, buf.at[s], 0, sem.at[s]).wait()

    @pl.when(t == 0)
    def _():
        ybuf[1] = jnp.zeros(ybuf.shape[1:], ybuf.dtype)
        fills = [pltpu.make_async_copy(ybuf.at[1], o_ref.at[pl.ds(n2 + e * tm, tm)], ssem.at[1])
                 for e in range(N_EXPERTS)]
        for f in fills:
            f.start()
        for f in fills:
            f.wait()
        gather(src_ref, 0)

    @pl.when(t < nu_ref[0])
    def _():
        wait_rows(xbuf, gsem, slot)

        @pl.when(t >= 2)
        def _():
            wait_rows(ybuf, ssem, slot)

        @pl.when((t == 0) | (te_ref[t] != te_ref[jnp.maximum(t - 1, 0)]))
        def _():
            wg_s[...] = wg_ref[...].astype(BF16)
            wu_s[...] = wu_ref[...].astype(BF16)
            wd_s[...] = wd_ref[...].astype(BF16)

        gather(nxt_ref, other)
        xb = xbuf[slot].astype(BF16)
        gate = jnp.dot(xb, wg_s[...], preferred_element_type=F32)
        up = jnp.dot(xb, wu_s[...], preferred_element_type=F32)
        ybuf[slot] = _dot(gate * _sigmoid(gate) * up, wd_s[...])
        for r in range(tm):
            _row_copy(ybuf.at[slot], r, o_ref, dst_ref[0, 0, r], ssem.at[slot]).start()

        @pl.when(t == nu_ref[0] - 1)
        def _():
            wait_rows(xbuf, gsem, other)

            @pl.when(t >= 1)
            def _():
                wait_rows(ybuf, ssem, other)

            wait_rows(ybuf, ssem, slot)


def _experts(hn, src, dst, tile_expert, n_used, wg, wu, wd, tm):
    n = hn.shape[0]
    n_tiles = src.shape[0]
    idx = lambda f: pl.BlockSpec((1, 1, tm), lambda t, te, nu: (f(t), 0, 0), memory_space=pltpu.SMEM)
    weight = lambda shape: pl.BlockSpec((None,) + shape, lambda t, te, nu: (te[t], 0, 0))
    return pl.pallas_call(
        _experts_kernel,
        grid_spec=pltpu.PrefetchScalarGridSpec(
            num_scalar_prefetch=2,
            grid=(n_tiles,),
            in_specs=[idx(lambda t: t), idx(lambda t: jnp.minimum(t + 1, n_tiles - 1)), idx(lambda t: t),
                      pl.BlockSpec(memory_space=pl.ANY),
                      weight((D_MODEL, EXP_FF)), weight((D_MODEL, EXP_FF)), weight((EXP_FF, D_MODEL))],
            out_specs=pl.BlockSpec(memory_space=pl.ANY),
            scratch_shapes=[pltpu.VMEM((2, tm, D_MODEL), F32), pltpu.VMEM((2, tm, D_MODEL), F32),
                            pltpu.VMEM((D_MODEL, EXP_FF), BF16), pltpu.VMEM((D_MODEL, EXP_FF), BF16),
                            pltpu.VMEM((EXP_FF, D_MODEL), BF16),
                            pltpu.SemaphoreType.DMA((2,)), pltpu.SemaphoreType.DMA((2,))]),
        out_shape=jax.ShapeDtypeStruct((2 * n + N_EXPERTS * tm, D_MODEL), F32),
        compiler_params=_params(("arbitrary",), 40),
        name="moe_experts",
    )(tile_expert, n_used, src, src, dst, hn, wg, wu, wd)


def _combine_kernel(h_ref, route_ref, fg_ref, o1_ref, o2_ref, y_ref):
    rt = route_ref[...]
    lane = lax.broadcasted_iota(jnp.int32, rt.shape, 1)
    w1 = jnp.sum(jnp.where(lane == ROUTE_W1, rt, 0.0), axis=-1, keepdims=True)
    w2 = jnp.sum(jnp.where(lane == ROUTE_W2, rt, 0.0), axis=-1, keepdims=True)
    y_ref[...] = _rmsnorm(h_ref[...] + w1 * o1_ref[...] + w2 * o2_ref[...], fg_ref[...])


def _combine(h, route, o, final_g, tm=512):
    n = h.shape[0]
    nb = n // tm
    return pl.pallas_call(
        _combine_kernel,
        grid=(nb,),
        in_specs=[pl.BlockSpec((tm, D_MODEL), lambda i: (i, 0)),
                  pl.BlockSpec((tm, LANES), lambda i: (i, 0)),
                  _full((1, D_MODEL)),
                  pl.BlockSpec((tm, D_MODEL), lambda i: (i, 0)),
                  pl.BlockSpec((tm, D_MODEL), lambda i: (nb + i, 0))],
        out_specs=pl.BlockSpec((tm, D_MODEL), lambda i: (i, 0)),
        out_shape=jax.ShapeDtypeStruct((n, D_MODEL), F32),
        compiler_params=_params(("arbitrary",), 40),
        name="moe_combine",
    )(h, route, final_g, o, o)


def _moe(h, hn, route, ridx, counts, wg, wu, wd, final_g, tm=256):
    src, dst, tile_expert, n_used = _route_plan(ridx, counts, tm)
    o = _experts(hn, src, dst, tile_expert, n_used, wg, wu, wd, tm)
    return _combine(h, route, o, final_g)


def _layer(x2d, mem_k, mem_v, rw_state, w, *, prompt, batch, seq):
    n = x2d.shape[0]
    tm_in = 256
    grp = "prompt" if prompt else "sample"
    o_gm, p_rw, p_q, p_gate, *vn = _in_proj(x2d, w["norm1_g"], w["w_in_segs"], w["gm_ln_g"], w["gm_ln_b"],
                                            w["gm_mix_" + grp], w["gm_bias_" + grp], not prompt, tm_in)
    vn = vn[0] if vn else None
    if prompt:
        o_rw, s_new, sh_new = _rwkv_prompt(p_rw, w["rw_prm"], batch, seq)
        o_xa = _xattn_prompt(p_q, mem_k, mem_v, batch, seq)
    else:
        s0, shift = rw_state
        prev_rows = jnp.pad(shift[:, None, :], ((0, 0), (0, seq - 1), (0, 0))).reshape(n, RW_COLS)
        o_rw, s_new, sh_new = _rwkv_sample(p_rw, prev_rows, s0, w["rw_prm"], seq)
        o_xa = _xattn_sample(p_q, mem_k, mem_v, seq)
    h, hn, route, ridx, counts = _merge(x2d, o_gm, o_rw, o_xa, p_gate, w["gate_b"], w["w_branch"], w["w_out"],
                                        w["norm2_g"], w["wr_hi"], w["wr_lo"], w["b_r"])
    y = _moe(h, hn, route, ridx, counts, w["e_wg"], w["e_wu"], w["e_wd"], w["final_g"])
    return y, s_new, sh_new, vn


def kernel(x_prompt, x_sample, state_rwkv_S, state_rwkv_shift, cache_mem_k, cache_mem_v, mem_prompt, norm1_g, w_in, gate_b, gm_ln_g, gm_ln_b, gm_ws, gm_bs, rw_mu, rw_w0, rw_w2, rw_a0, rw_a2, rw_g2, rw_k_k, rw_k_a, rw_r_k, rw_lnx_g, rw_lnx_b, xa_wk, xa_wv, w_branch, w_out, norm2_g, rg_w, rg_b, re_w, re_b, e_wg, e_wu, e_wd, final_g):
    bp, seq_p, _ = x_prompt.shape
    bs, seq_s, _ = x_sample.shape
    depth = w_in.shape[0]
    assert depth == 1 and seq_p % CHUNK == 0 and RW_ROWS % seq_s == 0 and CHUNK % seq_s == 0

    l = 0
    row = lambda a: a.reshape(1, -1)
    seg = (0, 2 * BRANCH_W, 2 * BRANCH_W + RW_COLS, 3 * BRANCH_W + RW_COLS, 3 * BRANCH_W + RW_COLS + GATE_COLS)
    w_causal = jnp.tril(gm_ws[l])
    nrep = CHUNK // seq_s
    blk = w_causal[:, :seq_s, :seq_s]
    eye_rep = jnp.eye(nrep, dtype=F32)
    mix_sample = jnp.einsum("ab,gts->gatbs", eye_rep, blk).reshape(GM_GROUPS, CHUNK, CHUNK)
    bias_prompt = jnp.repeat(gm_bs[l].T, GM_HD, axis=1)
    bias_sample = jnp.tile(bias_prompt[:seq_s], (nrep, 1))

    wr = jnp.zeros((D_MODEL, LANES), F32)
    wr = wr.at[:, :N_EXPERTS].set(jnp.transpose(re_w[l], (1, 0, 2)).reshape(D_MODEL, N_EXPERTS))
    wr = wr.at[:, N_EXPERTS:N_EXPERTS + N_GROUPS].set(rg_w[l])
    wr_hi = wr.astype(BF16)
    wr_lo = (wr - wr_hi.astype(F32)).astype(BF16)
    b_r = jnp.zeros((1, LANES), F32)
    b_r = b_r.at[0, :N_EXPERTS].set(re_b[l].reshape(-1)).at[0, N_EXPERTS:N_EXPERTS + N_GROUPS].set(rg_b[l])

    w = dict(
        norm1_g=row(norm1_g[l]),
        w_in_segs=[w_in[l][:, a:b].astype(BF16) for a, b in zip(seg[:-1], seg[1:])],
        gm_ln_g=row(gm_ln_g[l]), gm_ln_b=row(gm_ln_b[l]),
        gm_mix_prompt=w_causal.astype(BF16), gm_bias_prompt=bias_prompt,
        gm_mix_sample=mix_sample.astype(BF16), gm_bias_sample=bias_sample,
        rw_prm=(row(rw_mu[l]), row(rw_w0[l]), rw_w2[l].astype(BF16), row(rw_a0[l]), rw_a2[l].astype(BF16),
                rw_g2[l].astype(BF16), row(rw_k_k[l]), row(rw_k_a[l]), row(rw_r_k[l]),
                row(rw_lnx_g[l]), row(rw_lnx_b[l])),
        gate_b=row(gate_b[l]), w_branch=w_branch[l].astype(BF16), w_out=w_out[l].astype(BF16),
        norm2_g=row(norm2_g[l]), wr_hi=wr_hi, wr_lo=wr_lo, b_r=b_r,
        e_wg=e_wg[l].reshape(N_EXPERTS, D_MODEL, EXP_FF),
        e_wu=e_wu[l].reshape(N_EXPERTS, D_MODEL, EXP_FF),
        e_wd=e_wd[l].reshape(N_EXPERTS, EXP_FF, D_MODEL),
        final_g=row(final_g),
    )

    mk_p, mv_p = _mem_kv(mem_prompt.reshape(bp * N_MEM, D_MODEL), xa_wk[l].astype(BF16), xa_wv[l].astype(BF16))
    yp, sp, shp, _ = _layer(x_prompt.reshape(bp * seq_p, D_MODEL), mk_p, mv_p, None, w,
                            prompt=True, batch=bp, seq=seq_p)
    ys, ss, shs, vs = _layer(x_sample.reshape(bs * seq_s, D_MODEL),
                             cache_mem_k[l].reshape(bs, N_MEM, BRANCH_W).astype(BF16),
                             cache_mem_v[l].reshape(bs, N_MEM, BRANCH_W).astype(BF16),
                             (state_rwkv_S[l], state_rwkv_shift[l]), w, prompt=False, batch=bs, seq=seq_s)

    return (yp.reshape(bp, seq_p, D_MODEL), ys.reshape(bs, seq_s, D_MODEL),
            sp[None], shp[None],
            mk_p.reshape(1, bp, N_MEM, XA_HEADS, XA_HD), mv_p.reshape(1, bp, N_MEM, XA_HEADS, XA_HD),
            ss[None], shs[None], vs.reshape(1, bs, seq_s, BRANCH_W))
```

```python
import itertools

import numpy as np
import jax
import jax.numpy as jnp
from jax import lax
from jax.experimental import pallas as pl
from jax.experimental.pallas import tpu as pltpu

F32 = jnp.float32
BF16 = jnp.bfloat16

D_MODEL = 1024
BRANCH_W = 512
CHUNK = 128
GM_GROUPS = 8
GM_HD = BRANCH_W // GM_GROUPS
RW_HEADS = 8
RW_HD = BRANCH_W // RW_HEADS
RW_W_LORA = 64
RW_A_LORA = 64
RW_G_LORA = 128
RW_COLS = 3 * BRANCH_W + RW_W_LORA + RW_A_LORA + RW_G_LORA
XA_HEADS = 4
XA_HD = BRANCH_W // XA_HEADS
N_MEM = 256
N_BRANCH = 3
GATE_COLS = N_BRANCH * D_MODEL
N_GROUPS = 4
EXP_PER_GROUP = 8
N_EXPERTS = N_GROUPS * EXP_PER_GROUP
EXP_FF = 512
RMS_EPS = 1e-6
LN_EPS = 1e-5
GN_EPS = 64e-5

LANES = 128
RW_ROWS = 64
RW_GROUPS = 2
MIB = 1024 * 1024
ROUTE_E1, ROUTE_E2, ROUTE_W1, ROUTE_W2 = range(4)


def _dot(a, b):
    return jnp.dot(a.astype(BF16), b.astype(BF16), preferred_element_type=F32)


def _dot_nt(a, b):
    return lax.dot_general(a.astype(BF16), b.astype(BF16), (((1,), (1,)), ((), ())),
                           preferred_element_type=F32)


def _dot_tn(a, b):
    return lax.dot_general(a.astype(BF16), b.astype(BF16), (((0,), (0,)), ((), ())),
                           preferred_element_type=F32)


def _sigmoid(x):
    return 1.0 / (1.0 + jnp.exp(-x))


def _gelu(x):
    c = np.float32(np.sqrt(2.0 / np.pi))
    return x * (0.5 * (1.0 + jnp.tanh(c * (x + 0.044715 * (x * x * x)))))


def _softplus(x):
    return jnp.maximum(x, 0.0) + jnp.log(1.0 + jnp.exp(-jnp.abs(x)))


def _rmsnorm(x, g):
    return x * lax.rsqrt(jnp.mean(x * x, axis=-1, keepdims=True) + RMS_EPS) * g


def _split3(x):
    hi = x.astype(BF16)
    r1 = x - hi.astype(F32)
    mid = r1.astype(BF16)
    lo = (r1 - mid.astype(F32)).astype(BF16)
    return hi, mid, lo


def _params(sem, vmem_mib):
    return pltpu.CompilerParams(dimension_semantics=sem, vmem_limit_bytes=vmem_mib * MIB)


def _full(shape):
    nd = len(shape)
    return pl.BlockSpec(shape, lambda *_: (0,) * nd)


def _gmlp_chunk(pu, pv, ln_g, ln_b, wmix_ref, bmix_ref):
    u = _gelu(pu)
    vf = _gelu(pv)
    mu = jnp.mean(vf, axis=-1, keepdims=True)
    vc = vf - mu
    var = jnp.mean(vc * vc, axis=-1, keepdims=True)
    vn = vc * lax.rsqrt(var + LN_EPS) * ln_g + ln_b
    lane = lax.broadcasted_iota(jnp.int32, (CHUNK, LANES), 1)
    lo_half = lane < GM_HD
    outs = []
    for p in range(GM_GROUPS // 2):
        vp = vn[:, p * LANES:(p + 1) * LANES]
        s = (_dot(wmix_ref[2 * p], jnp.where(lo_half, vp, 0.0))
             + _dot(wmix_ref[2 * p + 1], jnp.where(lo_half, 0.0, vp)))
        outs.append(u[:, p * LANES:(p + 1) * LANES] * (s + bmix_ref[:, p * LANES:(p + 1) * LANES]))
    return jnp.concatenate(outs, axis=-1), vn


def _in_proj_kernel(x_ref, g_ref, wgm_ref, wrw_ref, wq_ref, wgt_ref, lng_ref, lnb_ref, wmix_ref, bmix_ref,
                    ogm_ref, rw_ref, q_ref, gt_ref, *vn_refs):
    xb = _rmsnorm(x_ref[...], g_ref[...]).astype(BF16)
    gm = jnp.dot(xb, wgm_ref[...], preferred_element_type=F32)
    rw_ref[...] = jnp.dot(xb, wrw_ref[...], preferred_element_type=F32)
    q_ref[...] = jnp.dot(xb, wq_ref[...], preferred_element_type=F32)
    gt_ref[...] = jnp.dot(xb, wgt_ref[...], preferred_element_type=F32)
    for c in range(x_ref.shape[0] // CHUNK):
        rows = slice(c * CHUNK, (c + 1) * CHUNK)
        o, vn = _gmlp_chunk(gm[rows, 0:BRANCH_W], gm[rows, BRANCH_W:2 * BRANCH_W], lng_ref[...], lnb_ref[...],
                            wmix_ref, bmix_ref)
        ogm_ref[rows, :] = o.astype(ogm_ref.dtype)
        if vn_refs:
            vn_refs[0][rows, :] = vn


def _in_proj(x, g, w_segs, ln_g, ln_b, wmix, bmix, emit_vn, tm):
    n = x.shape[0]
    widths = [w.shape[1] for w in w_segs]
    out_w = [BRANCH_W] + widths[1:] + ([BRANCH_W] if emit_vn else [])
    out_dt = [BF16] + [F32] * (len(out_w) - 1)
    return pl.pallas_call(
        _in_proj_kernel,
        grid=(n // tm,),
        in_specs=[pl.BlockSpec((tm, D_MODEL), lambda i: (i, 0)), _full((1, D_MODEL))]
        + [pl.BlockSpec((D_MODEL, wd), lambda i: (0, 0), pipeline_mode=pl.Buffered(1)) for wd in widths]
        + [_full((1, BRANCH_W)), _full((1, BRANCH_W)), _full((GM_GROUPS, CHUNK, CHUNK)),
           _full((CHUNK, BRANCH_W))],
        out_specs=[pl.BlockSpec((tm, wd), lambda i: (i, 0)) for wd in out_w],
        out_shape=[jax.ShapeDtypeStruct((n, wd), dt) for wd, dt in zip(out_w, out_dt)],
        compiler_params=_params(("arbitrary",), 48),
        name="in_proj",
    )(x, g, *w_segs, ln_g, ln_b, wmix, bmix)


def _mem_kv_kernel(m_ref, wk_ref, wv_ref, k_ref, v_ref):
    mb = m_ref[...].astype(BF16)
    k_ref[...] = jnp.dot(mb, wk_ref[...], preferred_element_type=F32)
    v_ref[...] = jnp.dot(mb, wv_ref[...], preferred_element_type=F32)


def _mem_kv(mem, wk, wv, tm=512):
    n = mem.shape[0]
    return pl.pallas_call(
        _mem_kv_kernel,
        grid=(n // tm,),
        in_specs=[pl.BlockSpec((tm, D_MODEL), lambda i: (i, 0)),
                  _full((D_MODEL, BRANCH_W)), _full((D_MODEL, BRANCH_W))],
        out_specs=[pl.BlockSpec((tm, BRANCH_W), lambda i: (i, 0))] * 2,
        out_shape=[jax.ShapeDtypeStruct((n, BRANCH_W), F32)] * 2,
        compiler_params=_params(("arbitrary",), 32),
        name="mem_kv",
    )(mem, wk, wv)


def _rwkv_rows(p, first, prev, s_ref, o_ref, prm, nseq):
    (mu, w0, w2, a0, a2, g2, k_k, k_a, r_k, lnx_g, lnx_b) = prm
    rows = RW_ROWS
    ls = rows // nseq
    shifted = jnp.where(first, prev, pltpu.roll(p, 1, 0))
    xs = p + (shifted - p) * mu
    r = xs[:, 0:BRANCH_W]
    k = xs[:, BRANCH_W:2 * BRANCH_W]
    v = xs[:, 2 * BRANCH_W:3 * BRANCH_W]
    o = 3 * BRANCH_W
    wd = xs[:, o:o + RW_W_LORA]
    ad = xs[:, o + RW_W_LORA:o + RW_W_LORA + RW_A_LORA]
    gd = xs[:, o + RW_W_LORA + RW_A_LORA:RW_COLS]

    w_log = -_softplus(-(w0 + _dot(jnp.tanh(wd), w2))) - 0.5
    logw = -jnp.exp(w_log)
    a = _sigmoid(a0 + _dot(ad, a2))
    g = _dot(_sigmoid(gd), g2)
    kkr = k * k_k
    kf = k * (1.0 + (a - 1.0) * k_a)
    rkr = r * kf * r_k

    ri = lax.broadcasted_iota(jnp.int32, (rows, rows), 0)
    ci = lax.broadcasted_iota(jnp.int32, (rows, rows), 1)
    if nseq == 1:
        same = ci >= 0
    else:
        sh = int(np.log2(ls))
        same = lax.shift_right_logical(ri, sh) == lax.shift_right_logical(ci, sh)
    low_incl = same & (ci <= ri)
    low_strict = same & (ci < ri)
    m_incl = jnp.where(low_incl, 1.0, 0.0).astype(BF16)
    m_same = jnp.where(same, 1.0, 0.0).astype(BF16)
    hi, mid, lo = _split3(logw)
    cum = (jnp.dot(m_incl, hi, preferred_element_type=F32) + jnp.dot(m_incl, mid, preferred_element_type=F32)
           + jnp.dot(m_incl, lo, preferred_element_type=F32))
    tot = (jnp.dot(m_same, hi, preferred_element_type=F32) + jnp.dot(m_same, mid, preferred_element_type=F32)
           + jnp.dot(m_same, lo, preferred_element_type=F32))
    g_t = jnp.exp(cum)
    g_prev = jnp.exp(cum - logw)
    g_inv = jnp.exp(-cum)
    g_end = jnp.exp(tot - cum)
    g_tot = jnp.exp(tot)
    eye = jnp.where(ri == ci, 1.0, 0.0)
    yield

    hs = range(RW_HEADS)
    sls = [slice(h * RW_HD, (h + 1) * RW_HD) for h in hs]
    kk = [kkr[:, sl] for sl in sls]
    kk = [x * lax.rsqrt(jnp.maximum(jnp.sum(x * x, axis=-1, keepdims=True), 1e-24)) for x in kk]
    bv = [kk[h] * a[:, sls[h]] for h in hs]
    k_h = [kf[:, sl] for sl in sls]
    v_h = [v[:, sl] for sl in sls]
    at = [-kk[h] * g_prev[:, sls[h]] for h in hs]
    rt = [r[:, sl] * g_t[:, sl] for sl in sls]
    bt = [bv[h] * g_inv[:, sls[h]] for h in hs]
    kt = [k_h[h] * g_inv[:, sls[h]] for h in hs]
    bh = [bv[h] * g_end[:, sls[h]] for h in hs]
    kh = [k_h[h] * g_end[:, sls[h]] for h in hs]

    ar = [jnp.concatenate([at[h], rt[h]], axis=0) for h in hs]
    pb = [_dot_nt(ar[h], bt[h]) for h in hs]
    pk = [_dot_nt(ar[h], kt[h]) for h in hs]
    l_ab = [jnp.where(low_strict, x[:rows], 0.0) for x in pb]
    l_ak = [jnp.where(low_strict, x[:rows], 0.0) for x in pk]
    a_rb = [jnp.where(low_incl, x[rows:], 0.0) for x in pb]
    a_rk = [jnp.where(low_incl, x[rows:], 0.0) for x in pk]
    yield

    tm = [eye + x for x in l_ab]
    pw = [_dot(x, x) for x in l_ab]
    n_dbl = int(np.log2(ls)) - 1
    for it in range(n_dbl):
        yield
        if it < n_dbl - 1:
            z = [_dot(jnp.concatenate([tm[h], pw[h]], axis=0), pw[h]) for h in hs]
            tm = [tm[h] + z[h][:rows] for h in hs]
            pw = [z[h][rows:] for h in hs]
        else:
            tm = [tm[h] + _dot(tm[h], pw[h]) for h in hs]

    if nseq == 1:
        ars = [_dot_nt(ar[h], s_ref[0, h]) for h in hs]
        as0 = [x[:rows] for x in ars]
        rs0 = [x[rows:] for x in ars]
    else:
        as0, rs0 = [], []
        for h in hs:
            zs = [_dot_nt(jnp.concatenate([at[h][b * ls:(b + 1) * ls], rt[h][b * ls:(b + 1) * ls]], axis=0),
                          s_ref[b, h]) for b in range(nseq)]
            as0.append(jnp.concatenate([x[:ls] for x in zs], axis=0))
            rs0.append(jnp.concatenate([x[ls:] for x in zs], axis=0))

    lv = [_dot(l_ak[h], v_h[h]) for h in hs]
    yield
    u = [_dot(tm[h], as0[h] + lv[h]) for h in hs]
    yield
    y = [rs0[h] + _dot(a_rb[h], u[h]) + _dot(a_rk[h], v_h[h]) for h in hs]
    yield

    for h in hs:
        for b in range(nseq):
            rb = slice(b * ls, (b + 1) * ls)
            uv = jnp.concatenate([u[h][rb], v_h[h][rb]], axis=0)
            bk = jnp.concatenate([bh[h][rb], kh[h][rb]], axis=0)
            s_ref[b, h] = s_ref[b, h] * g_tot[b * ls:b * ls + 1, sls[h]] + _dot_tn(uv, bk)

    outs = []
    for h in hs:
        ym = jnp.mean(y[h], axis=-1, keepdims=True)
        yc = y[h] - ym
        yv = jnp.mean(yc * yc, axis=-1, keepdims=True)
        yn = yc * lax.rsqrt(yv + GN_EPS) * lnx_g[:, sls[h]] + lnx_b[:, sls[h]]
        bonus = jnp.sum(rkr[:, sls[h]], axis=-1, keepdims=True) * v_h[h]
        outs.append((yn + bonus) * g[:, sls[h]])
    o_ref[...] = jnp.concatenate(outs, axis=-1).astype(o_ref.dtype)


def _interleave(gens):
    for _ in itertools.zip_longest(*gens):
        pass


def _rwkv_prompt_kernel(p_ref, *refs):
    prm_refs, (o_ref, s_ref, sh_ref, carry_ref) = refs[:11], refs[11:]
    c = pl.program_id(1)

    @pl.when(c == 0)
    def _():
        s_ref[...] = jnp.zeros_like(s_ref)
        carry_ref[...] = jnp.zeros_like(carry_ref)

    first = lax.broadcasted_iota(jnp.int32, (RW_ROWS, 1), 0) == 0
    prm = tuple(x[...] for x in prm_refs)
    groups = range(p_ref.shape[0])
    _interleave([_rwkv_rows(p_ref[g], first, carry_ref[8 * g:8 * g + 1, :], s_ref.at[pl.ds(g, 1)],
                            o_ref.at[g], prm, nseq=1) for g in groups])
    for g in groups:
        last = p_ref[g, RW_ROWS - 1:RW_ROWS, :]
        carry_ref[8 * g:8 * g + 1, :] = last
        sh_ref[g] = last


def _rwkv_sample_kernel(p_ref, prev_ref, s0_ref, *refs):
    prm_refs, (o_ref, s_ref, sh_ref) = refs[:11], refs[11:]
    nseq = s0_ref.shape[0] // p_ref.shape[0]
    ls = RW_ROWS // nseq
    s_ref[...] = s0_ref[...]
    first = (lax.broadcasted_iota(jnp.int32, (RW_ROWS, 1), 0) & (ls - 1)) == 0
    prm = tuple(x[...] for x in prm_refs)
    groups = range(p_ref.shape[0])
    _interleave([_rwkv_rows(p_ref[g], first, prev_ref[g], s_ref.at[pl.ds(g * nseq, nseq)], o_ref.at[g],
                            prm, nseq=nseq) for g in groups])
    for g in groups:
        for b in range(nseq):
            sh_ref[g * nseq + b:g * nseq + b + 1, :] = p_ref[g, (b + 1) * ls - 1:(b + 1) * ls, :]


def _rw_param_specs():
    shapes = [(1, RW_COLS), (1, BRANCH_W), (RW_W_LORA, BRANCH_W), (1, BRANCH_W), (RW_A_LORA, BRANCH_W),
              (RW_G_LORA, BRANCH_W), (1, BRANCH_W), (1, BRANCH_W), (1, BRANCH_W), (1, BRANCH_W), (1, BRANCH_W)]
    return [_full(s) for s in shapes]


def _rwkv_prompt(p_rw, prm, batch, seq):
    nc = seq // RW_ROWS
    g = RW_GROUPS
    p3 = p_rw.reshape(batch, seq, RW_COLS)
    o, s_new, sh = pl.pallas_call(
        _rwkv_prompt_kernel,
        grid=(batch // g, nc),
        in_specs=[pl.BlockSpec((g, RW_ROWS, RW_COLS), lambda b, c: (b, c, 0))] + _rw_param_specs(),
        out_specs=[pl.BlockSpec((g, RW_ROWS, BRANCH_W), lambda b, c: (b, c, 0)),
                   pl.BlockSpec((g, RW_HEADS, RW_HD, RW_HD), lambda b, c: (b, 0, 0, 0)),
                   pl.BlockSpec((g, 1, RW_COLS), lambda b, c: (b, 0, 0))],
        out_shape=[jax.ShapeDtypeStruct((batch, seq, BRANCH_W), BF16),
                   jax.ShapeDtypeStruct((batch, RW_HEADS, RW_HD, RW_HD), F32),
                   jax.ShapeDtypeStruct((batch, 1, RW_COLS), F32)],
        scratch_shapes=[pltpu.VMEM((8 * g, RW_COLS), F32)],
        compiler_params=_params(("arbitrary", "arbitrary"), 56),
        name="rwkv_prompt",
    )(p3, *prm)
    return o.reshape(batch * seq, BRANCH_W), s_new, sh.reshape(batch, RW_COLS)


def _rwkv_sample(p_rw, prev_rows, s0, prm, seq):
    n = p_rw.shape[0]
    nseq = RW_ROWS // seq
    batch = n // seq
    g = RW_GROUPS
    ng = n // RW_ROWS
    o, s_new, sh = pl.pallas_call(
        _rwkv_sample_kernel,
        grid=(ng // g,),
        in_specs=[pl.BlockSpec((g, RW_ROWS, RW_COLS), lambda i: (i, 0, 0)),
                  pl.BlockSpec((g, RW_ROWS, RW_COLS), lambda i: (i, 0, 0)),
                  pl.BlockSpec((g * nseq, RW_HEADS, RW_HD, RW_HD), lambda i: (i, 0, 0, 0))] + _rw_param_specs(),
        out_specs=[pl.BlockSpec((g, RW_ROWS, BRANCH_W), lambda i: (i, 0, 0)),
                   pl.BlockSpec((g * nseq, RW_HEADS, RW_HD, RW_HD), lambda i: (i, 0, 0, 0)),
                   pl.BlockSpec((g * nseq, RW_COLS), lambda i: (i, 0))],
        out_shape=[jax.ShapeDtypeStruct((ng, RW_ROWS, BRANCH_W), BF16),
                   jax.ShapeDtypeStruct((batch, RW_HEADS, RW_HD, RW_HD), F32),
                   jax.ShapeDtypeStruct((batch, RW_COLS), F32)],
        compiler_params=_params(("arbitrary",), 56),
        name="rwkv_sample",
    )(p_rw.reshape(ng, RW_ROWS, RW_COLS), prev_rows.reshape(ng, RW_ROWS, RW_COLS), s0, *prm)
    return o.reshape(n, BRANCH_W), s_new, sh


def _attend_all(qkv):
    s = [_dot_nt(q, k) * (XA_HD ** -0.5) for q, k, _ in qkv]
    e = [jnp.exp(x - jnp.max(x, axis=-1, keepdims=True)) for x in s]
    pr = [x / jnp.sum(x, axis=-1, keepdims=True) for x in e]
    return [_dot(p, v) for p, (_, _, v) in zip(pr, qkv)]


def _xattn_prompt_kernel(q_ref, k_ref, v_ref, o_ref):
    sls = [slice(h * XA_HD, (h + 1) * XA_HD) for h in range(XA_HEADS)]
    outs = _attend_all([(q_ref[:, sl], k_ref[:, sl], v_ref[:, sl]) for sl in sls])
    for sl, o in zip(sls, outs):
        o_ref[:, sl] = o.astype(o_ref.dtype)


def _xattn_prompt(q, mk, mv, batch, seq, tq=512):
    nt = seq // tq
    return pl.pallas_call(
        _xattn_prompt_kernel,
        grid=(batch, nt),
        in_specs=[pl.BlockSpec((tq, BRANCH_W), lambda b, i: (b * nt + i, 0)),
                  pl.BlockSpec((N_MEM, BRANCH_W), lambda b, i: (b, 0)),
                  pl.BlockSpec((N_MEM, BRANCH_W), lambda b, i: (b, 0))],
        out_specs=pl.BlockSpec((tq, BRANCH_W), lambda b, i: (b * nt + i, 0)),
        out_shape=jax.ShapeDtypeStruct((batch * seq, BRANCH_W), BF16),
        compiler_params=_params(("arbitrary", "arbitrary"), 32),
        name="xattn_prompt",
    )(q, mk, mv)


def _xattn_sample_kernel(q_ref, k_ref, v_ref, o_ref):
    nb = k_ref.shape[0]
    ls = q_ref.shape[0] // nb
    where = [(b, slice(b * ls, (b + 1) * ls), slice(h * XA_HD, (h + 1) * XA_HD))
             for b in range(nb) for h in range(XA_HEADS)]
    outs = _attend_all([(q_ref[rows, sl], k_ref[b, :, sl], v_ref[b, :, sl]) for b, rows, sl in where])
    for (b, rows, sl), o in zip(where, outs):
        o_ref[rows, sl] = o.astype(o_ref.dtype)


def _xattn_sample(q, mk, mv, seq, nb=8):
    batch = mk.shape[0]
    mem_spec = pl.BlockSpec((nb, N_MEM, BRANCH_W), lambda i: (i, 0, 0))
    return pl.pallas_call(
        _xattn_sample_kernel,
        grid=(batch // nb,),
        in_specs=[pl.BlockSpec((nb * seq, BRANCH_W), lambda i: (i, 0)), mem_spec, mem_spec],
        out_specs=pl.BlockSpec((nb * seq, BRANCH_W), lambda i: (i, 0)),
        out_shape=jax.ShapeDtypeStruct((batch * seq, BRANCH_W), BF16),
        compiler_params=_params(("arbitrary",), 40),
        name="xattn_sample",
    )(q, mk, mv)


def _merge_kernel(x_ref, gm_ref, rw_ref, xa_ref, gt_ref, gb_ref, wb_ref, wo_ref, n2_ref,
                  wr_hi_ref, wr_lo_ref, br_ref, h_ref, hn_ref, route_ref, ridx_ref, cnt_ref):
    merged = None
    for n, br in enumerate((gm_ref, rw_ref, xa_ref)):
        cs = slice(n * D_MODEL, (n + 1) * D_MODEL)
        up = jnp.dot(br[...], wb_ref[n], preferred_element_type=F32)
        term = _sigmoid(gt_ref[:, cs] + gb_ref[:, cs]) * up
        merged = term if merged is None else merged + term
    h = x_ref[...] + _dot(merged, wo_ref[...])
    h_ref[...] = h
    hn = _rmsnorm(h, n2_ref[...])
    hn_ref[...] = hn

    hn_hi = hn.astype(BF16)
    hn_lo = (hn - hn_hi.astype(F32)).astype(BF16)
    logits = (jnp.dot(hn_hi, wr_hi_ref[...], preferred_element_type=F32)
              + jnp.dot(hn_lo, wr_hi_ref[...], preferred_element_type=F32)
              + jnp.dot(hn_hi, wr_lo_ref[...], preferred_element_type=F32)) + br_ref[...]
    lane = lax.broadcasted_iota(jnp.int32, logits.shape, 1)
    neg = -jnp.inf
    big = jnp.int32(1 << 20)
    gmask = (lane >= N_EXPERTS) & (lane < N_EXPERTS + N_GROUPS)
    gl = jnp.where(gmask, logits, neg)
    gmax = jnp.max(gl, axis=-1, keepdims=True)
    gsel = jnp.min(jnp.where(gl == gmax, lane, big), axis=-1, keepdims=True) - N_EXPERTS
    gsum = jnp.sum(jnp.where(gmask, jnp.exp(logits - gmax), 0.0), axis=-1, keepdims=True)
    pg_top = 1.0 / gsum
    emask = (lane >= gsel * EXP_PER_GROUP) & (lane < (gsel + 1) * EXP_PER_GROUP)
    el = jnp.where(emask, logits, neg)
    m1 = jnp.max(el, axis=-1, keepdims=True)
    i1 = jnp.min(jnp.where(el == m1, lane, big), axis=-1, keepdims=True)
    el2 = jnp.where(lane == i1, neg, el)
    m2 = jnp.max(el2, axis=-1, keepdims=True)
    i2 = jnp.min(jnp.where(el2 == m2, lane, big), axis=-1, keepdims=True)
    t2 = jnp.exp(m2 - m1)
    w1 = pg_top / (1.0 + t2)
    w2 = pg_top * t2 / (1.0 + t2)

    @pl.when(pl.program_id(0) == 0)
    def _():
        cnt_ref[...] = jnp.zeros_like(cnt_ref)

    onehot = jnp.where(lane == i1, 1.0, 0.0) + jnp.where(lane == i2, 1.0, 0.0)
    cnt_ref[0:1, :] = cnt_ref[0:1, :] + jnp.sum(onehot, axis=0, keepdims=True)

    route = jnp.zeros(logits.shape, F32)
    for col, val in ((ROUTE_E1, i1.astype(F32)), (ROUTE_E2, i2.astype(F32)), (ROUTE_W1, w1), (ROUTE_W2, w2)):
        route = jnp.where(lane == col, val, route)
    route_ref[...] = route
    ridx_ref[...] = jnp.transpose(route)[0:8, :].astype(jnp.int32)


def _merge(x, o_gm, o_rw, o_xa, p_gate, gate_b, w_branch, w_out, n2_g, wr_hi, wr_lo, b_r, tm=256):
    n = x.shape[0]
    row = lambda wd: pl.BlockSpec((tm, wd), lambda i: (i, 0))
    return pl.pallas_call(
        _merge_kernel,
        grid=(n // tm,),
        in_specs=[row(D_MODEL), row(BRANCH_W), row(BRANCH_W), row(BRANCH_W), row(GATE_COLS),
                  _full((1, GATE_COLS)), _full((N_BRANCH, BRANCH_W, D_MODEL)), _full((D_MODEL, D_MODEL)),
                  _full((1, D_MODEL)), _full((D_MODEL, LANES)), _full((D_MODEL, LANES)), _full((1, LANES))],
        out_specs=[row(D_MODEL), row(D_MODEL), row(LANES),
                   pl.BlockSpec((8, tm), lambda i: (0, i)), _full((8, LANES))],
        out_shape=[jax.ShapeDtypeStruct((n, D_MODEL), F32), jax.ShapeDtypeStruct((n, D_MODEL), F32),
                   jax.ShapeDtypeStruct((n, LANES), F32), jax.ShapeDtypeStruct((8, n), jnp.int32),
                   jax.ShapeDtypeStruct((8, LANES), F32)],
        compiler_params=_params(("arbitrary",), 40),
        name="merge",
    )(x, o_gm, o_rw, o_xa, p_gate, gate_b, w_branch, w_out, n2_g, wr_hi, wr_lo, b_r)


def _route_plan(ridx, counts, tm):
    n = ridx.shape[1]
    n2 = 2 * n
    n_tiles = n2 // tm + N_EXPERTS
    n_rows = n_tiles * tm
    key_bits = int(n_rows - 1).bit_length()
    assert n & (n - 1) == 0
    cnt = counts[0, :N_EXPERTS].astype(jnp.int32)
    tiles = (cnt + tm - 1) // tm
    tile_end = jnp.cumsum(tiles)
    n_used = tile_end[N_EXPERTS - 1:]
    tc = jnp.minimum(jnp.arange(n_tiles, dtype=jnp.int32), n_used - 1)
    tile_expert = jnp.sum((tc[:, None] >= tile_end[None, :]).astype(jnp.int32), axis=1)
    pad_end = jnp.cumsum(tiles * tm - cnt)
    d = jnp.arange(n_rows - n2, dtype=jnp.int32)
    pad_expert = jnp.sum((d[:, None] >= pad_end[None, :]).astype(jnp.int32), axis=1)
    eid = ridx[ROUTE_E1:ROUTE_E2 + 1].reshape(n2)
    keys = jnp.concatenate([(eid << key_bits) + jnp.arange(n2, dtype=jnp.int32),
                            (pad_expert << key_bits) + n2 + d])
    row_id = jnp.sort(keys) & ((1 << key_bits) - 1)
    src = jnp.where(row_id < n2, row_id & (n - 1), 0)
    return src.reshape(n_tiles, 1, tm), row_id.reshape(n_tiles, 1, tm), tile_expert, n_used


def _row_copy(src_ref, src_row, dst_ref, dst_row, sem):
    return pltpu.make_async_copy(src_ref.at[pl.ds(src_row, 1)], dst_ref.at[pl.ds(dst_row, 1)], sem)


def _experts_kernel(te_ref, nu_ref, src_ref, nxt_ref, dst_ref, hn_ref, wg_ref, wu_ref, wd_ref, o_ref,
                    xbuf, ybuf, wg_s, wu_s, wd_s, gsem, ssem):
    t = pl.program_id(0)
    tm = xbuf.shape[1]
    n2 = o_ref.shape[0] - N_EXPERTS * tm
    slot = t & 1
    other = 1 - slot

    def gather(idx_ref, s):
        for r in range(tm):
            _row_copy(hn_ref, idx_ref[0, 0, r], xbuf.at[s], r, gsem.at[s]).start(priority=0)

    def wait_rows(buf, sem, s):
        for _ in range(tm):
            _row_copy(hn_ref, 0, buf.at[s], 0, sem.at[s]).wait()

    @pl.when(t == 0)
    def _():
        ybuf[1] = jnp.zeros(ybuf.shape[1:], ybuf.dtype)
        fills = [pltpu.make_async_copy(ybuf.at[1], o_ref.at[pl.ds(n2 + e * tm, tm)], ssem.at[1])
                 for e in range(N_EXPERTS)]
        for f in fills:
            f.start()
        for f in fills:
            f.wait()
        gather(src_ref, 0)

    @pl.when(t < nu_ref[0])
    def _():
        wait_rows(xbuf, gsem, slot)

        @pl.when(t >= 2)
        def _():
            wait_rows(ybuf, ssem, slot)

        @pl.when((t == 0) | (te_ref[t] != te_ref[jnp.maximum(t - 1, 0)]))
        def _():
            wg_s[...] = wg_ref[...].astype(BF16)
            wu_s[...] = wu_ref[...].astype(BF16)
            wd_s[...] = wd_ref[...].astype(BF16)

        gather(nxt_ref, other)
        xb = xbuf[slot].astype(BF16)
        gate = jnp.dot(xb, wg_s[...], preferred_element_type=F32)
        up = jnp.dot(xb, wu_s[...], preferred_element_type=F32)
        ybuf[slot] = _dot(gate * _sigmoid(gate) * up, wd_s[...])
        for r in range(tm):
            _row_copy(ybuf.at[slot], r, o_ref, dst_ref[0, 0, r], ssem.at[slot]).start(priority=1)

        @pl.when(t == nu_ref[0] - 1)
        def _():
            wait_rows(xbuf, gsem, other)

            @pl.when(t >= 1)
            def _():
                wait_rows(ybuf, ssem, other)

            wait_rows(ybuf, ssem, slot)


def _experts(hn, src, dst, tile_expert, n_used, wg, wu, wd, tm):
    n = hn.shape[0]
    n_tiles = src.shape[0]
    idx = lambda f: pl.BlockSpec((1, 1, tm), lambda t, te, nu: (f(t), 0, 0), memory_space=pltpu.SMEM)
    weight = lambda shape: pl.BlockSpec((None,) + shape, lambda t, te, nu: (te[t], 0, 0))
    return pl.pallas_call(
        _experts_kernel,
        grid_spec=pltpu.PrefetchScalarGridSpec(
            num_scalar_prefetch=2,
            grid=(n_tiles,),
            in_specs=[idx(lambda t: t), idx(lambda t: jnp.minimum(t + 1, n_tiles - 1)), idx(lambda t: t),
                      pl.BlockSpec(memory_space=pl.ANY),
                      weight((D_MODEL, EXP_FF)), weight((D_MODEL, EXP_FF)), weight((EXP_FF, D_MODEL))],
            out_specs=pl.BlockSpec(memory_space=pl.ANY),
            scratch_shapes=[pltpu.VMEM((2, tm, D_MODEL), F32), pltpu.VMEM((2, tm, D_MODEL), F32),
                            pltpu.VMEM((D_MODEL, EXP_FF), BF16), pltpu.VMEM((D_MODEL, EXP_FF), BF16),
                            pltpu.VMEM((EXP_FF, D_MODEL), BF16),
                            pltpu.SemaphoreType.DMA((2,)), pltpu.SemaphoreType.DMA((2,))]),
        out_shape=jax.ShapeDtypeStruct((2 * n + N_EXPERTS * tm, D_MODEL), F32),
        compiler_params=_params(("arbitrary",), 40),
        name="moe_experts",
    )(tile_expert, n_used, src, src, dst, hn, wg, wu, wd)


def _combine_kernel(h_ref, route_ref, fg_ref, o1_ref, o2_ref, y_ref):
    rt = route_ref[...]
    lane = lax.broadcasted_iota(jnp.int32, rt.shape, 1)
    w1 = jnp.sum(jnp.where(lane == ROUTE_W1, rt, 0.0), axis=-1, keepdims=True)
    w2 = jnp.sum(jnp.where(lane == ROUTE_W2, rt, 0.0), axis=-1, keepdims=True)
    y_ref[...] = _rmsnorm(h_ref[...] + w1 * o1_ref[...] + w2 * o2_ref[...], fg_ref[...])


def _combine(h, route, o, final_g, tm=512):
    n = h.shape[0]
    nb = n // tm
    return pl.pallas_call(
        _combine_kernel,
        grid=(nb,),
        in_specs=[pl.BlockSpec((tm, D_MODEL), lambda i: (i, 0)),
                  pl.BlockSpec((tm, LANES), lambda i: (i, 0)),
                  _full((1, D_MODEL)),
                  pl.BlockSpec((tm, D_MODEL), lambda i: (i, 0)),
                  pl.BlockSpec((tm, D_MODEL), lambda i: (nb + i, 0))],
        out_specs=pl.BlockSpec((tm, D_MODEL), lambda i: (i, 0)),
        out_shape=jax.ShapeDtypeStruct((n, D_MODEL), F32),
        compiler_params=_params(("arbitrary",), 40),
        name="moe_combine",
    )(h, route, final_g, o, o)


def _moe(h, hn, route, ridx, counts, wg, wu, wd, final_g, tm=256):
    src, dst, tile_expert, n_used = _route_plan(ridx, counts, tm)
    o = _experts(hn, src, dst, tile_expert, n_used, wg, wu, wd, tm)
    return _combine(h, route, o, final_g)


def _layer(x2d, mem_k, mem_v, rw_state, w, *, prompt, batch, seq):
    n = x2d.shape[0]
    tm_in = 256
    grp = "prompt" if prompt else "sample"
    o_gm, p_rw, p_q, p_gate, *vn = _in_proj(x2d, w["norm1_g"], w["w_in_segs"], w["gm_ln_g"], w["gm_ln_b"],
                                            w["gm_mix_" + grp], w["gm_bias_" + grp], not prompt, tm_in)
    vn = vn[0] if vn else None
    if prompt:
        o_rw, s_new, sh_new = _rwkv_prompt(p_rw, w["rw_prm"], batch, seq)
        o_xa = _xattn_prompt(p_q, mem_k, mem_v, batch, seq)
    else:
        s0, shift = rw_state
        prev_rows = jnp.pad(shift[:, None, :], ((0, 0), (0, seq - 1), (0, 0))).reshape(n, RW_COLS)
        o_rw, s_new, sh_new = _rwkv_sample(p_rw, prev_rows, s0, w["rw_prm"], seq)
        o_xa = _xattn_sample(p_q, mem_k, mem_v, seq)
    h, hn, route, ridx, counts = _merge(x2d, o_gm, o_rw, o_xa, p_gate, w["gate_b"], w["w_branch"], w["w_out"],
                                        w["norm2_g"], w["wr_hi"], w["wr_lo"], w["b_r"])
    y = _moe(h, hn, route, ridx, counts, w["e_wg"], w["e_wu"], w["e_wd"], w["final_g"])
    return y, s_new, sh_new, vn


def kernel(x_prompt, x_sample, state_rwkv_S, state_rwkv_shift, cache_mem_k, cache_mem_v, mem_prompt, norm1_g, w_in, gate_b, gm_ln_g, gm_ln_b, gm_ws, gm_bs, rw_mu, rw_w0, rw_w2, rw_a0, rw_a2, rw_g2, rw_k_k, rw_k_a, rw_r_k, rw_lnx_g, rw_lnx_b, xa_wk, xa_wv, w_branch, w_out, norm2_g, rg_w, rg_b, re_w, re_b, e_wg, e_wu, e_wd, final_g):
    bp, seq_p, _ = x_prompt.shape
    bs, seq_s, _ = x_sample.shape
    depth = w_in.shape[0]
    assert depth == 1 and seq_p % CHUNK == 0 and RW_ROWS % seq_s == 0 and CHUNK % seq_s == 0

    l = 0
    row = lambda a: a.reshape(1, -1)
    seg = (0, 2 * BRANCH_W, 2 * BRANCH_W + RW_COLS, 3 * BRANCH_W + RW_COLS, 3 * BRANCH_W + RW_COLS + GATE_COLS)
    w_causal = jnp.tril(gm_ws[l])
    nrep = CHUNK // seq_s
    blk = w_causal[:, :seq_s, :seq_s]
    eye_rep = jnp.eye(nrep, dtype=F32)
    mix_sample = jnp.einsum("ab,gts->gatbs", eye_rep, blk).reshape(GM_GROUPS, CHUNK, CHUNK)
    bias_prompt = jnp.repeat(gm_bs[l].T, GM_HD, axis=1)
    bias_sample = jnp.tile(bias_prompt[:seq_s], (nrep, 1))

    wr = jnp.zeros((D_MODEL, LANES), F32)
    wr = wr.at[:, :N_EXPERTS].set(jnp.transpose(re_w[l], (1, 0, 2)).reshape(D_MODEL, N_EXPERTS))
    wr = wr.at[:, N_EXPERTS:N_EXPERTS + N_GROUPS].set(rg_w[l])
    wr_hi = wr.astype(BF16)
    wr_lo = (wr - wr_hi.astype(F32)).astype(BF16)
    b_r = jnp.zeros((1, LANES), F32)
    b_r = b_r.at[0, :N_EXPERTS].set(re_b[l].reshape(-1)).at[0, N_EXPERTS:N_EXPERTS + N_GROUPS].set(rg_b[l])

    w = dict(
        norm1_g=row(norm1_g[l]),
        w_in_segs=[w_in[l][:, a:b].astype(BF16) for a, b in zip(seg[:-1], seg[1:])],
        gm_ln_g=row(gm_ln_g[l]), gm_ln_b=row(gm_ln_b[l]),
        gm_mix_prompt=w_causal.astype(BF16), gm_bias_prompt=bias_prompt,
        gm_mix_sample=mix_sample.astype(BF16), gm_bias_sample=bias_sample,
        rw_prm=(row(rw_mu[l]), row(rw_w0[l]), rw_w2[l].astype(BF16), row(rw_a0[l]), rw_a2[l].astype(BF16),
                rw_g2[l].astype(BF16), row(rw_k_k[l]), row(rw_k_a[l]), row(rw_r_k[l]),
                row(rw_lnx_g[l]), row(rw_lnx_b[l])),
        gate_b=row(gate_b[l]), w_branch=w_branch[l].astype(BF16), w_out=w_out[l].astype(BF16),
        norm2_g=row(norm2_g[l]), wr_hi=wr_hi, wr_lo=wr_lo, b_r=b_r,
        e_wg=e_wg[l].reshape(N_EXPERTS, D_MODEL, EXP_FF),
        e_wu=e_wu[l].reshape(N_EXPERTS, D_MODEL, EXP_FF),
        e_wd=e_wd[l].reshape(N_EXPERTS, EXP_FF, D_MODEL),
        final_g=row(final_g),
    )

    mk_p, mv_p = _mem_kv(mem_prompt.reshape(bp * N_MEM, D_MODEL), xa_wk[l].astype(BF16), xa_wv[l].astype(BF16))
    yp, sp, shp, _ = _layer(x_prompt.reshape(bp * seq_p, D_MODEL), mk_p, mv_p, None, w,
                            prompt=True, batch=bp, seq=seq_p)
    ys, ss, shs, vs = _layer(x_sample.reshape(bs * seq_s, D_MODEL),
                             cache_mem_k[l].reshape(bs, N_MEM, BRANCH_W).astype(BF16),
                             cache_mem_v[l].reshape(bs, N_MEM, BRANCH_W).astype(BF16),
                             (state_rwkv_S[l], state_rwkv_shift[l]), w, prompt=False, batch=bs, seq=seq_s)

    return (yp.reshape(bp, seq_p, D_MODEL), ys.reshape(bs, seq_s, D_MODEL),
            sp[None], shp[None],
            mk_p.reshape(1, bp, N_MEM, XA_HEADS, XA_HD), mv_p.reshape(1, bp, N_MEM, XA_HEADS, XA_HD),
            ss[None], shs[None], vs.reshape(1, bs, seq_s, BRANCH_W))
```

```python
import functools
import itertools

import numpy as np
import jax
import jax.numpy as jnp
from jax import lax
from jax.experimental import pallas as pl
from jax.experimental.pallas import tpu as pltpu

F32 = jnp.float32
BF16 = jnp.bfloat16

D_MODEL = 1024
BRANCH_W = 512
CHUNK = 128
GM_GROUPS = 8
GM_HD = BRANCH_W // GM_GROUPS
RW_HEADS = 8
RW_HD = BRANCH_W // RW_HEADS
RW_W_LORA = 64
RW_A_LORA = 64
RW_G_LORA = 128
RW_COLS = 3 * BRANCH_W + RW_W_LORA + RW_A_LORA + RW_G_LORA
XA_HEADS = 4
XA_HD = BRANCH_W // XA_HEADS
N_MEM = 256
N_BRANCH = 3
GATE_COLS = N_BRANCH * D_MODEL
N_GROUPS = 4
EXP_PER_GROUP = 8
N_EXPERTS = N_GROUPS * EXP_PER_GROUP
EXP_FF = 512
RMS_EPS = 1e-6
LN_EPS = 1e-5
GN_EPS = 64e-5

LANES = 128
RW_ROWS = 64
RW_GROUPS = 2
MIB = 1024 * 1024
ROUTE_E1, ROUTE_E2, ROUTE_W1, ROUTE_W2 = range(4)


def _dot(a, b):
    return jnp.dot(a.astype(BF16), b.astype(BF16), preferred_element_type=F32)


def _dot_nt(a, b):
    return lax.dot_general(a.astype(BF16), b.astype(BF16), (((1,), (1,)), ((), ())),
                           preferred_element_type=F32)


def _dot_tn(a, b):
    return lax.dot_general(a.astype(BF16), b.astype(BF16), (((0,), (0,)), ((), ())),
                           preferred_element_type=F32)


def _sigmoid(x):
    return 1.0 / (1.0 + jnp.exp(-x))


def _gelu(x):
    c = np.float32(np.sqrt(2.0 / np.pi))
    return x * (0.5 * (1.0 + jnp.tanh(c * (x + 0.044715 * (x * x * x)))))


def _softplus(x):
    return jnp.maximum(x, 0.0) + jnp.log(1.0 + jnp.exp(-jnp.abs(x)))


def _rmsnorm(x, g):
    return x * lax.rsqrt(jnp.mean(x * x, axis=-1, keepdims=True) + RMS_EPS) * g


def _split3(x):
    hi = x.astype(BF16)
    r1 = x - hi.astype(F32)
    mid = r1.astype(BF16)
    lo = (r1 - mid.astype(F32)).astype(BF16)
    return hi, mid, lo


def _params(sem, vmem_mib):
    return pltpu.CompilerParams(dimension_semantics=sem, vmem_limit_bytes=vmem_mib * MIB)


def _full(shape):
    nd = len(shape)
    return pl.BlockSpec(shape, lambda *_: (0,) * nd)


def _gmlp_chunk(pu, pv, ln_g, ln_b, wmix_ref, bmix_ref):
    u = _gelu(pu)
    vf = _gelu(pv)
    mu = jnp.mean(vf, axis=-1, keepdims=True)
    vc = vf - mu
    var = jnp.mean(vc * vc, axis=-1, keepdims=True)
    vn = vc * lax.rsqrt(var + LN_EPS) * ln_g + ln_b
    lane = lax.broadcasted_iota(jnp.int32, (CHUNK, LANES), 1)
    lo_half = lane < GM_HD
    outs = []
    for p in range(GM_GROUPS // 2):
        vp = vn[:, p * LANES:(p + 1) * LANES]
        s = (_dot(wmix_ref[2 * p], jnp.where(lo_half, vp, 0.0))
             + _dot(wmix_ref[2 * p + 1], jnp.where(lo_half, 0.0, vp)))
        outs.append(u[:, p * LANES:(p + 1) * LANES] * (s + bmix_ref[:, p * LANES:(p + 1) * LANES]))
    return jnp.concatenate(outs, axis=-1), vn


def _in_proj_kernel(x_ref, g_ref, wgm_ref, wrw_ref, wq_ref, wgt_ref, lng_ref, lnb_ref, wmix_ref, bmix_ref,
                    ogm_ref, rw_ref, q_ref, gt_ref, *vn_refs):
    xb = _rmsnorm(x_ref[...], g_ref[...]).astype(BF16)
    gm = jnp.dot(xb, wgm_ref[...], preferred_element_type=F32)
    rw_ref[...] = jnp.dot(xb, wrw_ref[...], preferred_element_type=F32)
    q_ref[...] = jnp.dot(xb, wq_ref[...], preferred_element_type=F32)
    gt_ref[...] = jnp.dot(xb, wgt_ref[...], preferred_element_type=F32)
    for c in range(x_ref.shape[0] // CHUNK):
        rows = slice(c * CHUNK, (c + 1) * CHUNK)
        o, vn = _gmlp_chunk(gm[rows, 0:BRANCH_W], gm[rows, BRANCH_W:2 * BRANCH_W], lng_ref[...], lnb_ref[...],
                            wmix_ref, bmix_ref)
        ogm_ref[rows, :] = o.astype(ogm_ref.dtype)
        if vn_refs:
            vn_refs[0][rows, :] = vn


def _in_proj(x, g, w_segs, ln_g, ln_b, wmix, bmix, emit_vn, tm):
    n = x.shape[0]
    widths = [w.shape[1] for w in w_segs]
    out_w = [BRANCH_W] + widths[1:] + ([BRANCH_W] if emit_vn else [])
    out_dt = [BF16] + [F32] * (len(out_w) - 1)
    return pl.pallas_call(
        _in_proj_kernel,
        grid=(n // tm,),
        in_specs=[pl.BlockSpec((tm, D_MODEL), lambda i: (i, 0)), _full((1, D_MODEL))]
        + [pl.BlockSpec((D_MODEL, wd), lambda i: (0, 0), pipeline_mode=pl.Buffered(1)) for wd in widths]
        + [_full((1, BRANCH_W)), _full((1, BRANCH_W)), _full((GM_GROUPS, CHUNK, CHUNK)),
           _full((CHUNK, BRANCH_W))],
        out_specs=[pl.BlockSpec((tm, wd), lambda i: (i, 0)) for wd in out_w],
        out_shape=[jax.ShapeDtypeStruct((n, wd), dt) for wd, dt in zip(out_w, out_dt)],
        compiler_params=_params(("arbitrary",), 48),
        name="in_proj",
    )(x, g, *w_segs, ln_g, ln_b, wmix, bmix)


def _mem_kv_kernel(m_ref, wk_ref, wv_ref, k_ref, v_ref):
    mb = m_ref[...].astype(BF16)
    k_ref[...] = jnp.dot(mb, wk_ref[...], preferred_element_type=F32)
    v_ref[...] = jnp.dot(mb, wv_ref[...], preferred_element_type=F32)


def _mem_kv(mem, wk, wv, tm=512):
    n = mem.shape[0]
    return pl.pallas_call(
        _mem_kv_kernel,
        grid=(n // tm,),
        in_specs=[pl.BlockSpec((tm, D_MODEL), lambda i: (i, 0)),
                  _full((D_MODEL, BRANCH_W)), _full((D_MODEL, BRANCH_W))],
        out_specs=[pl.BlockSpec((tm, BRANCH_W), lambda i: (i, 0))] * 2,
        out_shape=[jax.ShapeDtypeStruct((n, BRANCH_W), F32)] * 2,
        compiler_params=_params(("arbitrary",), 32),
        name="mem_kv",
    )(mem, wk, wv)


def _rwkv_rows(p, first, prev, s_ref, o_ref, prm, nseq):
    (mu, w0, w2, a0, a2, g2, k_k, k_a, r_k, lnx_g, lnx_b) = prm
    rows = RW_ROWS
    ls = rows // nseq
    shifted = jnp.where(first, prev, pltpu.roll(p, 1, 0))
    xs = p + (shifted - p) * mu
    r = xs[:, 0:BRANCH_W]
    k = xs[:, BRANCH_W:2 * BRANCH_W]
    v = xs[:, 2 * BRANCH_W:3 * BRANCH_W]
    o = 3 * BRANCH_W
    wd = xs[:, o:o + RW_W_LORA]
    ad = xs[:, o + RW_W_LORA:o + RW_W_LORA + RW_A_LORA]
    gd = xs[:, o + RW_W_LORA + RW_A_LORA:RW_COLS]

    w_log = -_softplus(-(w0 + _dot(jnp.tanh(wd), w2))) - 0.5
    logw = -jnp.exp(w_log)
    a = _sigmoid(a0 + _dot(ad, a2))
    g = _dot(_sigmoid(gd), g2)
    kkr = k * k_k
    kf = k * (1.0 + (a - 1.0) * k_a)
    rkr = r * kf * r_k

    ri = lax.broadcasted_iota(jnp.int32, (rows, rows), 0)
    ci = lax.broadcasted_iota(jnp.int32, (rows, rows), 1)
    if nseq == 1:
        same = ci >= 0
    else:
        sh = int(np.log2(ls))
        same = lax.shift_right_logical(ri, sh) == lax.shift_right_logical(ci, sh)
    low_incl = same & (ci <= ri)
    low_strict = same & (ci < ri)
    m_incl = jnp.where(low_incl, 1.0, 0.0).astype(BF16)
    m_same = jnp.where(same, 1.0, 0.0).astype(BF16)
    hi, mid, lo = _split3(logw)
    cum = (jnp.dot(m_incl, hi, preferred_element_type=F32) + jnp.dot(m_incl, mid, preferred_element_type=F32)
           + jnp.dot(m_incl, lo, preferred_element_type=F32))
    tot = (jnp.dot(m_same, hi, preferred_element_type=F32) + jnp.dot(m_same, mid, preferred_element_type=F32)
           + jnp.dot(m_same, lo, preferred_element_type=F32))
    g_t = jnp.exp(cum)
    g_prev = jnp.exp(cum - logw)
    g_inv = jnp.exp(-cum)
    g_end = jnp.exp(tot - cum)
    g_tot = jnp.exp(tot)
    eye = jnp.where(ri == ci, 1.0, 0.0)
    yield

    hs = range(RW_HEADS)
    sls = [slice(h * RW_HD, (h + 1) * RW_HD) for h in hs]
    kk = [kkr[:, sl] for sl in sls]
    kk = [x * lax.rsqrt(jnp.maximum(jnp.sum(x * x, axis=-1, keepdims=True), 1e-24)) for x in kk]
    bv = [kk[h] * a[:, sls[h]] for h in hs]
    k_h = [kf[:, sl] for sl in sls]
    v_h = [v[:, sl] for sl in sls]
    at = [-kk[h] * g_prev[:, sls[h]] for h in hs]
    rt = [r[:, sl] * g_t[:, sl] for sl in sls]
    bt = [bv[h] * g_inv[:, sls[h]] for h in hs]
    kt = [k_h[h] * g_inv[:, sls[h]] for h in hs]
    bh = [bv[h] * g_end[:, sls[h]] for h in hs]
    kh = [k_h[h] * g_end[:, sls[h]] for h in hs]

    ar = [jnp.concatenate([at[h], rt[h]], axis=0) for h in hs]
    pb = [_dot_nt(ar[h], bt[h]) for h in hs]
    pk = [_dot_nt(ar[h], kt[h]) for h in hs]
    l_ab = [jnp.where(low_strict, x[:rows], 0.0) for x in pb]
    l_ak = [jnp.where(low_strict, x[:rows], 0.0) for x in pk]
    a_rb = [jnp.where(low_incl, x[rows:], 0.0) for x in pb]
    a_rk = [jnp.where(low_incl, x[rows:], 0.0) for x in pk]
    yield

    tm = [eye + x for x in l_ab]
    pw = [_dot(x, x) for x in l_ab]
    n_dbl = int(np.log2(ls)) - 1
    for it in range(n_dbl):
        yield
        if it < n_dbl - 1:
            z = [_dot(jnp.concatenate([tm[h], pw[h]], axis=0), pw[h]) for h in hs]
            tm = [tm[h] + z[h][:rows] for h in hs]
            pw = [z[h][rows:] for h in hs]
        else:
            tm = [tm[h] + _dot(tm[h], pw[h]) for h in hs]

    if nseq == 1:
        ars = [_dot_nt(ar[h], s_ref[0, h]) for h in hs]
        as0 = [x[:rows] for x in ars]
        rs0 = [x[rows:] for x in ars]
    else:
        as0, rs0 = [], []
        for h in hs:
            zs = [_dot_nt(jnp.concatenate([at[h][b * ls:(b + 1) * ls], rt[h][b * ls:(b + 1) * ls]], axis=0),
                          s_ref[b, h]) for b in range(nseq)]
            as0.append(jnp.concatenate([x[:ls] for x in zs], axis=0))
            rs0.append(jnp.concatenate([x[ls:] for x in zs], axis=0))

    lv = [_dot(l_ak[h], v_h[h]) for h in hs]
    yield
    u = [_dot(tm[h], as0[h] + lv[h]) for h in hs]
    yield
    y = [rs0[h] + _dot(a_rb[h], u[h]) + _dot(a_rk[h], v_h[h]) for h in hs]
    yield

    for h in hs:
        for b in range(nseq):
            rb = slice(b * ls, (b + 1) * ls)
            uv = jnp.concatenate([u[h][rb], v_h[h][rb]], axis=0)
            bk = jnp.concatenate([bh[h][rb], kh[h][rb]], axis=0)
            s_ref[b, h] = s_ref[b, h] * g_tot[b * ls:b * ls + 1, sls[h]] + _dot_tn(uv, bk)

    outs = []
    for h in hs:
        ym = jnp.mean(y[h], axis=-1, keepdims=True)
        yc = y[h] - ym
        yv = jnp.mean(yc * yc, axis=-1, keepdims=True)
        yn = yc * lax.rsqrt(yv + GN_EPS) * lnx_g[:, sls[h]] + lnx_b[:, sls[h]]
        bonus = jnp.sum(rkr[:, sls[h]], axis=-1, keepdims=True) * v_h[h]
        outs.append((yn + bonus) * g[:, sls[h]])
    o_ref[...] = jnp.concatenate(outs, axis=-1).astype(o_ref.dtype)


def _interleave(gens):
    for _ in itertools.zip_longest(*gens):
        pass


def _rwkv_prompt_kernel(p_ref, *refs):
    prm_refs, (o_ref, s_ref, sh_ref, carry_ref) = refs[:11], refs[11:]
    c = pl.program_id(1)

    @pl.when(c == 0)
    def _():
        s_ref[...] = jnp.zeros_like(s_ref)
        carry_ref[...] = jnp.zeros_like(carry_ref)

    first = lax.broadcasted_iota(jnp.int32, (RW_ROWS, 1), 0) == 0
    prm = tuple(x[...] for x in prm_refs)
    groups = range(p_ref.shape[0])
    _interleave([_rwkv_rows(p_ref[g], first, carry_ref[8 * g:8 * g + 1, :], s_ref.at[pl.ds(g, 1)],
                            o_ref.at[g], prm, nseq=1) for g in groups])
    for g in groups:
        last = p_ref[g, RW_ROWS - 1:RW_ROWS, :]
        carry_ref[8 * g:8 * g + 1, :] = last
        sh_ref[g] = last


def _rwkv_sample_kernel(p_ref, prev_ref, s0_ref, *refs):
    prm_refs, (o_ref, s_ref, sh_ref) = refs[:11], refs[11:]
    nseq = s0_ref.shape[0] // p_ref.shape[0]
    ls = RW_ROWS // nseq
    s_ref[...] = s0_ref[...]
    first = (lax.broadcasted_iota(jnp.int32, (RW_ROWS, 1), 0) & (ls - 1)) == 0
    prm = tuple(x[...] for x in prm_refs)
    groups = range(p_ref.shape[0])
    _interleave([_rwkv_rows(p_ref[g], first, prev_ref[g], s_ref.at[pl.ds(g * nseq, nseq)], o_ref.at[g],
                            prm, nseq=nseq) for g in groups])
    for g in groups:
        for b in range(nseq):
            sh_ref[g * nseq + b:g * nseq + b + 1, :] = p_ref[g, (b + 1) * ls - 1:(b + 1) * ls, :]


def _rw_param_specs():
    shapes = [(1, RW_COLS), (1, BRANCH_W), (RW_W_LORA, BRANCH_W), (1, BRANCH_W), (RW_A_LORA, BRANCH_W),
              (RW_G_LORA, BRANCH_W), (1, BRANCH_W), (1, BRANCH_W), (1, BRANCH_W), (1, BRANCH_W), (1, BRANCH_W)]
    return [_full(s) for s in shapes]


def _rwkv_prompt(p_rw, prm, batch, seq):
    nc = seq // RW_ROWS
    g = RW_GROUPS
    p3 = p_rw.reshape(batch, seq, RW_COLS)
    o, s_new, sh = pl.pallas_call(
        _rwkv_prompt_kernel,
        grid=(batch // g, nc),
        in_specs=[pl.BlockSpec((g, RW_ROWS, RW_COLS), lambda b, c: (b, c, 0))] + _rw_param_specs(),
        out_specs=[pl.BlockSpec((g, RW_ROWS, BRANCH_W), lambda b, c: (b, c, 0)),
                   pl.BlockSpec((g, RW_HEADS, RW_HD, RW_HD), lambda b, c: (b, 0, 0, 0)),
                   pl.BlockSpec((g, 1, RW_COLS), lambda b, c: (b, 0, 0))],
        out_shape=[jax.ShapeDtypeStruct((batch, seq, BRANCH_W), BF16),
                   jax.ShapeDtypeStruct((batch, RW_HEADS, RW_HD, RW_HD), F32),
                   jax.ShapeDtypeStruct((batch, 1, RW_COLS), F32)],
        scratch_shapes=[pltpu.VMEM((8 * g, RW_COLS), F32)],
        compiler_params=_params(("arbitrary", "arbitrary"), 56),
        name="rwkv_prompt",
    )(p3, *prm)
    return o.reshape(batch * seq, BRANCH_W), s_new, sh.reshape(batch, RW_COLS)


def _rwkv_sample(p_rw, prev_rows, s0, prm, seq):
    n = p_rw.shape[0]
    nseq = RW_ROWS // seq
    batch = n // seq
    g = RW_GROUPS
    ng = n // RW_ROWS
    o, s_new, sh = pl.pallas_call(
        _rwkv_sample_kernel,
        grid=(ng // g,),
        in_specs=[pl.BlockSpec((g, RW_ROWS, RW_COLS), lambda i: (i, 0, 0)),
                  pl.BlockSpec((g, RW_ROWS, RW_COLS), lambda i: (i, 0, 0)),
                  pl.BlockSpec((g * nseq, RW_HEADS, RW_HD, RW_HD), lambda i: (i, 0, 0, 0))] + _rw_param_specs(),
        out_specs=[pl.BlockSpec((g, RW_ROWS, BRANCH_W), lambda i: (i, 0, 0)),
                   pl.BlockSpec((g * nseq, RW_HEADS, RW_HD, RW_HD), lambda i: (i, 0, 0, 0)),
                   pl.BlockSpec((g * nseq, RW_COLS), lambda i: (i, 0))],
        out_shape=[jax.ShapeDtypeStruct((ng, RW_ROWS, BRANCH_W), BF16),
                   jax.ShapeDtypeStruct((batch, RW_HEADS, RW_HD, RW_HD), F32),
                   jax.ShapeDtypeStruct((batch, RW_COLS), F32)],
        compiler_params=_params(("arbitrary",), 56),
        name="rwkv_sample",
    )(p_rw.reshape(ng, RW_ROWS, RW_COLS), prev_rows.reshape(ng, RW_ROWS, RW_COLS), s0, *prm)
    return o.reshape(n, BRANCH_W), s_new, sh


def _attend_all(qkv):
    s = [_dot_nt(q, k) * (XA_HD ** -0.5) for q, k, _ in qkv]
    e = [jnp.exp(x - jnp.max(x, axis=-1, keepdims=True)) for x in s]
    pr = [x / jnp.sum(x, axis=-1, keepdims=True) for x in e]
    return [_dot(p, v) for p, (_, _, v) in zip(pr, qkv)]


def _xattn_prompt_kernel(q_ref, k_ref, v_ref, o_ref):
    sls = [slice(h * XA_HD, (h + 1) * XA_HD) for h in range(XA_HEADS)]
    outs = _attend_all([(q_ref[:, sl], k_ref[:, sl], v_ref[:, sl]) for sl in sls])
    for sl, o in zip(sls, outs):
        o_ref[:, sl] = o.astype(o_ref.dtype)


def _xattn_prompt(q, mk, mv, batch, seq, tq=512):
    nt = seq // tq
    return pl.pallas_call(
        _xattn_prompt_kernel,
        grid=(batch, nt),
        in_specs=[pl.BlockSpec((tq, BRANCH_W), lambda b, i: (b * nt + i, 0)),
                  pl.BlockSpec((N_MEM, BRANCH_W), lambda b, i: (b, 0)),
                  pl.BlockSpec((N_MEM, BRANCH_W), lambda b, i: (b, 0))],
        out_specs=pl.BlockSpec((tq, BRANCH_W), lambda b, i: (b * nt + i, 0)),
        out_shape=jax.ShapeDtypeStruct((batch * seq, BRANCH_W), BF16),
        compiler_params=_params(("arbitrary", "arbitrary"), 32),
        name="xattn_prompt",
    )(q, mk, mv)


def _xattn_sample_kernel(q_ref, k_ref, v_ref, o_ref):
    nb = k_ref.shape[0]
    ls = q_ref.shape[0] // nb
    where = [(b, slice(b * ls, (b + 1) * ls), slice(h * XA_HD, (h + 1) * XA_HD))
             for b in range(nb) for h in range(XA_HEADS)]
    outs = _attend_all([(q_ref[rows, sl], k_ref[b, :, sl], v_ref[b, :, sl]) for b, rows, sl in where])
    for (b, rows, sl), o in zip(where, outs):
        o_ref[rows, sl] = o.astype(o_ref.dtype)


def _xattn_sample(q, mk, mv, seq, nb=8):
    batch = mk.shape[0]
    mem_spec = pl.BlockSpec((nb, N_MEM, BRANCH_W), lambda i: (i, 0, 0))
    return pl.pallas_call(
        _xattn_sample_kernel,
        grid=(batch // nb,),
        in_specs=[pl.BlockSpec((nb * seq, BRANCH_W), lambda i: (i, 0)), mem_spec, mem_spec],
        out_specs=pl.BlockSpec((nb * seq, BRANCH_W), lambda i: (i, 0)),
        out_shape=jax.ShapeDtypeStruct((batch * seq, BRANCH_W), BF16),
        compiler_params=_params(("arbitrary",), 40),
        name="xattn_sample",
    )(q, mk, mv)


def _merge_kernel(n_active, *refs):
    i = pl.program_id(0)

    @pl.when(i < n_active)
    def _():
        _merge_rows(*refs)

    @pl.when(i >= n_active)
    def _():
        for ref in refs[-5:-1]:
            ref[...] = jnp.zeros_like(ref)


def _merge_rows(x_ref, gm_ref, rw_ref, xa_ref, gt_ref, gb_ref, wb_ref, wo_ref, n2_ref,
                wr_hi_ref, wr_lo_ref, br_ref, cnt_in_ref, *refs):
    h_ref, hn_ref, route_ref, ridx_ref, cnt_ref = refs[-5:]
    merged = None
    for n, br in enumerate((gm_ref, rw_ref, xa_ref)):
        cs = slice(n * D_MODEL, (n + 1) * D_MODEL)
        up = jnp.dot(br[...], wb_ref[n], preferred_element_type=F32)
        term = _sigmoid(gt_ref[:, cs] + gb_ref[:, cs]) * up
        merged = term if merged is None else merged + term
    h = x_ref[...] + _dot(merged, wo_ref[...])
    h_ref[...] = h
    hn = _rmsnorm(h, n2_ref[...])
    hn_ref[...] = hn

    hn_hi = hn.astype(BF16)
    hn_lo = (hn - hn_hi.astype(F32)).astype(BF16)
    logits = (jnp.dot(hn_hi, wr_hi_ref[...], preferred_element_type=F32)
              + jnp.dot(hn_lo, wr_hi_ref[...], preferred_element_type=F32)
              + jnp.dot(hn_hi, wr_lo_ref[...], preferred_element_type=F32)) + br_ref[...]
    lane = lax.broadcasted_iota(jnp.int32, logits.shape, 1)
    neg = -jnp.inf
    big = jnp.int32(1 << 20)
    gmask = (lane >= N_EXPERTS) & (lane < N_EXPERTS + N_GROUPS)
    gl = jnp.where(gmask, logits, neg)
    gmax = jnp.max(gl, axis=-1, keepdims=True)
    gsel = jnp.min(jnp.where(gl == gmax, lane, big), axis=-1, keepdims=True) - N_EXPERTS
    gsum = jnp.sum(jnp.where(gmask, jnp.exp(logits - gmax), 0.0), axis=-1, keepdims=True)
    pg_top = 1.0 / gsum
    emask = (lane >= gsel * EXP_PER_GROUP) & (lane < (gsel + 1) * EXP_PER_GROUP)
    el = jnp.where(emask, logits, neg)
    m1 = jnp.max(el, axis=-1, keepdims=True)
    i1 = jnp.min(jnp.where(el == m1, lane, big), axis=-1, keepdims=True)
    el2 = jnp.where(lane == i1, neg, el)
    m2 = jnp.max(el2, axis=-1, keepdims=True)
    i2 = jnp.min(jnp.where(el2 == m2, lane, big), axis=-1, keepdims=True)
    t2 = jnp.exp(m2 - m1)
    w1 = pg_top / (1.0 + t2)
    w2 = pg_top * t2 / (1.0 + t2)

    @pl.when(pl.program_id(0) == 0)
    def _():
        cnt_ref[...] = cnt_in_ref[...]

    onehot = jnp.where(lane == i1, 1.0, 0.0) + jnp.where(lane == i2, 1.0, 0.0)
    cnt_ref[0:1, :] = cnt_ref[0:1, :] + jnp.sum(onehot, axis=0, keepdims=True)

    route = jnp.zeros(logits.shape, F32)
    for col, val in ((ROUTE_E1, i1.astype(F32)), (ROUTE_E2, i2.astype(F32)), (ROUTE_W1, w1), (ROUTE_W2, w2)):
        route = jnp.where(lane == col, val, route)
    route_ref[...] = route
    ridx_ref[...] = jnp.transpose(route)[0:8, :].astype(jnp.int32)


def _merge(x, o_gm, o_rw, o_xa, p_gate, gate_b, w_branch, w_out, n2_g, wr_hi, wr_lo, b_r, n_total, row0, prev,
           tm=256):
    n = x.shape[0]
    t0 = row0 // tm
    n_act = n // tm
    n_steps = n_total // tm if prev is None else n_act
    row = lambda wd: pl.BlockSpec((tm, wd), lambda i: (jnp.minimum(i, n_act - 1), 0))
    out_row = lambda wd: pl.BlockSpec((tm, wd), lambda i: (t0 + i, 0))
    out_shape = [jax.ShapeDtypeStruct((n_total, D_MODEL), F32), jax.ShapeDtypeStruct((n_total, D_MODEL), F32),
                 jax.ShapeDtypeStruct((n_total, LANES), F32), jax.ShapeDtypeStruct((8, n_total), jnp.int32),
                 jax.ShapeDtypeStruct((8, LANES), F32)]
    if prev is None:
        carried, counts_in, aliases = [], jnp.zeros((8, LANES), F32), {}
    else:
        carried, counts_in = list(prev[:4]), prev[4]
        aliases = {13 + k: k for k in range(4)}
    return pl.pallas_call(
        functools.partial(_merge_kernel, n_act),
        grid=(n_steps,),
        in_specs=[row(D_MODEL), row(BRANCH_W), row(BRANCH_W), row(BRANCH_W), row(GATE_COLS),
                  _full((1, GATE_COLS)), _full((N_BRANCH, BRANCH_W, D_MODEL)), _full((D_MODEL, D_MODEL)),
                  _full((1, D_MODEL)), _full((D_MODEL, LANES)), _full((D_MODEL, LANES)), _full((1, LANES)),
                  _full((8, LANES))] + [pl.BlockSpec(memory_space=pl.ANY)] * len(carried),
        out_specs=[out_row(D_MODEL), out_row(D_MODEL), out_row(LANES),
                   pl.BlockSpec((8, tm), lambda i: (0, t0 + i)), _full((8, LANES))],
        out_shape=out_shape,
        input_output_aliases=aliases,
        compiler_params=_params(("arbitrary",), 40),
        name="merge",
    )(x, o_gm, o_rw, o_xa, p_gate, gate_b, w_branch, w_out, n2_g, wr_hi, wr_lo, b_r, counts_in, *carried)


def _route_plan(ridx, counts, tm):
    n = ridx.shape[1]
    n2 = 2 * n
    n_tiles = n2 // tm + N_EXPERTS
    n_rows = n_tiles * tm
    key_bits = int(n_rows - 1).bit_length()
    cnt = counts[0, :N_EXPERTS].astype(jnp.int32)
    tiles = (cnt + tm - 1) // tm
    tile_end = jnp.cumsum(tiles)
    n_used = tile_end[N_EXPERTS - 1:]
    tc = jnp.minimum(jnp.arange(n_tiles, dtype=jnp.int32), n_used - 1)
    tile_expert = jnp.sum((tc[:, None] >= tile_end[None, :]).astype(jnp.int32), axis=1)
    pad_end = jnp.cumsum(tiles * tm - cnt)
    d = jnp.arange(n_rows - n2, dtype=jnp.int32)
    pad_expert = jnp.sum((d[:, None] >= pad_end[None, :]).astype(jnp.int32), axis=1)
    eid = ridx[ROUTE_E1:ROUTE_E2 + 1].reshape(n2)
    keys = jnp.concatenate([(eid << key_bits) + jnp.arange(n2, dtype=jnp.int32),
                            (pad_expert << key_bits) + n2 + d])
    row_id = jnp.sort(keys) & ((1 << key_bits) - 1)
    src = jnp.where(row_id < n2, jnp.where(row_id >= n, row_id - n, row_id), 0)
    return src.reshape(n_tiles, 1, tm), row_id.reshape(n_tiles, 1, tm), tile_expert, n_used


def _row_copy(src_ref, src_row, dst_ref, dst_row, sem):
    return pltpu.make_async_copy(src_ref.at[pl.ds(src_row, 1)], dst_ref.at[pl.ds(dst_row, 1)], sem)


def _experts_kernel(te_ref, nu_ref, src_ref, nxt_ref, dst_ref, hn_ref, wg_ref, wu_ref, wd_ref, o_ref,
                    xbuf, ybuf, wg_s, wu_s, wd_s, gsem, ssem):
    t = pl.program_id(0)
    tm = xbuf.shape[1]
    n2 = o_ref.shape[0] - N_EXPERTS * tm
    slot = t & 1
    other = 1 - slot

    def gather(idx_ref, s):
        for r in range(tm):
            _row_copy(hn_ref, idx_ref[0, 0, r], xbuf.at[s], r, gsem.at[s]).start(priority=0)

    def wait_rows(buf, sem, s):
        for _ in range(tm):
            _row_copy(hn_ref, 0, buf.at[s], 0, sem.at[s]).wait()

    @pl.when(t == 0)
    def _():
        ybuf[1] = jnp.zeros(ybuf.shape[1:], ybuf.dtype)
        fills = [pltpu.make_async_copy(ybuf.at[1], o_ref.at[pl.ds(n2 + e * tm, tm)], ssem.at[1])
                 for e in range(N_EXPERTS)]
        for f in fills:
            f.start()
        for f in fills:
            f.wait()
        gather(src_ref, 0)

    @pl.when(t < nu_ref[0])
    def _():
        wait_rows(xbuf, gsem, slot)

        @pl.when(t >= 2)
        def _():
            wait_rows(ybuf, ssem, slot)

        @pl.when((t == 0) | (te_ref[t] != te_ref[jnp.maximum(t - 1, 0)]))
        def _():
            wg_s[...] = wg_ref[...].astype(BF16)
            wu_s[...] = wu_ref[...].astype(BF16)
            wd_s[...] = wd_ref[...].astype(BF16)

        gather(nxt_ref, other)
        xb = xbuf[slot].astype(BF16)
        gate = jnp.dot(xb, wg_s[...], preferred_element_type=F32)
        up = jnp.dot(xb, wu_s[...], preferred_element_type=F32)
        ybuf[slot] = _dot(gate * _sigmoid(gate) * up, wd_s[...])
        for r in range(tm):
            _row_copy(ybuf.at[slot], r, o_ref, dst_ref[0, 0, r], ssem.at[slot]).start(priority=1)

        @pl.when(t == nu_ref[0] - 1)
        def _():
            wait_rows(xbuf, gsem, other)

            @pl.when(t >= 1)
            def _():
                wait_rows(ybuf, ssem, other)

            wait_rows(ybuf, ssem, slot)


def _experts(hn, src, dst, tile_expert, n_used, wg, wu, wd, tm):
    n = hn.shape[0]
    n_tiles = src.shape[0]
    idx = lambda f: pl.BlockSpec((1, 1, tm), lambda t, te, nu: (f(t), 0, 0), memory_space=pltpu.SMEM)
    weight = lambda shape: pl.BlockSpec((None,) + shape, lambda t, te, nu: (te[t], 0, 0))
    return pl.pallas_call(
        _experts_kernel,
        grid_spec=pltpu.PrefetchScalarGridSpec(
            num_scalar_prefetch=2,
            grid=(n_tiles,),
            in_specs=[idx(lambda t: t), idx(lambda t: jnp.minimum(t + 1, n_tiles - 1)), idx(lambda t: t),
                      pl.BlockSpec(memory_space=pl.ANY),
                      weight((D_MODEL, EXP_FF)), weight((D_MODEL, EXP_FF)), weight((EXP_FF, D_MODEL))],
            out_specs=pl.BlockSpec(memory_space=pl.ANY),
            scratch_shapes=[pltpu.VMEM((2, tm, D_MODEL), F32), pltpu.VMEM((2, tm, D_MODEL), F32),
                            pltpu.VMEM((D_MODEL, EXP_FF), BF16), pltpu.VMEM((D_MODEL, EXP_FF), BF16),
                            pltpu.VMEM((EXP_FF, D_MODEL), BF16),
                            pltpu.SemaphoreType.DMA((2,)), pltpu.SemaphoreType.DMA((2,))]),
        out_shape=jax.ShapeDtypeStruct((2 * n + N_EXPERTS * tm, D_MODEL), F32),
        compiler_params=_params(("arbitrary",), 40),
        name="moe_experts",
    )(tile_expert, n_used, src, src, dst, hn, wg, wu, wd)


def _combine_kernel(h_ref, route_ref, fg_ref, o1_ref, o2_ref, y_ref):
    rt = route_ref[...]
    lane = lax.broadcasted_iota(jnp.int32, rt.shape, 1)
    w1 = jnp.sum(jnp.where(lane == ROUTE_W1, rt, 0.0), axis=-1, keepdims=True)
    w2 = jnp.sum(jnp.where(lane == ROUTE_W2, rt, 0.0), axis=-1, keepdims=True)
    y_ref[...] = _rmsnorm(h_ref[...] + w1 * o1_ref[...] + w2 * o2_ref[...], fg_ref[...])


def _combine(h, route, o, final_g, row0, n_out, tm=512):
    n = h.shape[0]
    t0 = row0 // tm
    return pl.pallas_call(
        _combine_kernel,
        grid=(n_out // tm,),
        in_specs=[pl.BlockSpec((tm, D_MODEL), lambda i: (t0 + i, 0)),
                  pl.BlockSpec((tm, LANES), lambda i: (t0 + i, 0)),
                  _full((1, D_MODEL)),
                  pl.BlockSpec((tm, D_MODEL), lambda i: (t0 + i, 0)),
                  pl.BlockSpec((tm, D_MODEL), lambda i: (n // tm + t0 + i, 0))],
        out_specs=pl.BlockSpec((tm, D_MODEL), lambda i: (i, 0)),
        out_shape=jax.ShapeDtypeStruct((n_out, D_MODEL), F32),
        compiler_params=_params(("arbitrary",), 40),
        name="moe_combine",
    )(h, route, final_g, o, o)


def _moe(merged, group_rows, wg, wu, wd, final_g, tm=256):
    h, hn, route, ridx, counts = merged
    src, dst, tile_expert, n_used = _route_plan(ridx, counts, tm)
    o = _experts(hn, src, dst, tile_expert, n_used, wg, wu, wd, tm)
    return [_combine(h, route, o, final_g, row0, rows) for row0, rows in group_rows]


def _branches(x2d, mem_k, mem_v, rw_state, w, merged, n_total, row0, *, prompt, batch, seq):
    n = x2d.shape[0]
    tm_in = 256
    grp = "prompt" if prompt else "sample"
    o_gm, p_rw, p_q, p_gate, *vn = _in_proj(x2d, w["norm1_g"], w["w_in_segs"], w["gm_ln_g"], w["gm_ln_b"],
                                            w["gm_mix_" + grp], w["gm_bias_" + grp], not prompt, tm_in)
    vn = vn[0] if vn else None
    if prompt:
        o_rw, s_new, sh_new = _rwkv_prompt(p_rw, w["rw_prm"], batch, seq)
        o_xa = _xattn_prompt(p_q, mem_k, mem_v, batch, seq)
    else:
        s0, shift = rw_state
        prev_rows = jnp.pad(shift[:, None, :], ((0, 0), (0, seq - 1), (0, 0))).reshape(n, RW_COLS)
        o_rw, s_new, sh_new = _rwkv_sample(p_rw, prev_rows, s0, w["rw_prm"], seq)
        o_xa = _xattn_sample(p_q, mem_k, mem_v, seq)
    merged = _merge(x2d, o_gm, o_rw, o_xa, p_gate, w["gate_b"], w["w_branch"], w["w_out"],
                    w["norm2_g"], w["wr_hi"], w["wr_lo"], w["b_r"], n_total, row0, merged)
    return merged, s_new, sh_new, vn


def kernel(x_prompt, x_sample, state_rwkv_S, state_rwkv_shift, cache_mem_k, cache_mem_v, mem_prompt, norm1_g, w_in, gate_b, gm_ln_g, gm_ln_b, gm_ws, gm_bs, rw_mu, rw_w0, rw_w2, rw_a0, rw_a2, rw_g2, rw_k_k, rw_k_a, rw_r_k, rw_lnx_g, rw_lnx_b, xa_wk, xa_wv, w_branch, w_out, norm2_g, rg_w, rg_b, re_w, re_b, e_wg, e_wu, e_wd, final_g):
    bp, seq_p, _ = x_prompt.shape
    bs, seq_s, _ = x_sample.shape
    depth = w_in.shape[0]
    assert depth == 1 and seq_p % CHUNK == 0 and RW_ROWS % seq_s == 0 and CHUNK % seq_s == 0

    l = 0
    row = lambda a: a.reshape(1, -1)
    seg = (0, 2 * BRANCH_W, 2 * BRANCH_W + RW_COLS, 3 * BRANCH_W + RW_COLS, 3 * BRANCH_W + RW_COLS + GATE_COLS)
    w_causal = jnp.tril(gm_ws[l])
    nrep = CHUNK // seq_s
    blk = w_causal[:, :seq_s, :seq_s]
    eye_rep = jnp.eye(nrep, dtype=F32)
    mix_sample = jnp.einsum("ab,gts->gatbs", eye_rep, blk).reshape(GM_GROUPS, CHUNK, CHUNK)
    bias_prompt = jnp.repeat(gm_bs[l].T, GM_HD, axis=1)
    bias_sample = jnp.tile(bias_prompt[:seq_s], (nrep, 1))

    wr = jnp.zeros((D_MODEL, LANES), F32)
    wr = wr.at[:, :N_EXPERTS].set(jnp.transpose(re_w[l], (1, 0, 2)).reshape(D_MODEL, N_EXPERTS))
    wr = wr.at[:, N_EXPERTS:N_EXPERTS + N_GROUPS].set(rg_w[l])
    wr_hi = wr.astype(BF16)
    wr_lo = (wr - wr_hi.astype(F32)).astype(BF16)
    b_r = jnp.zeros((1, LANES), F32)
    b_r = b_r.at[0, :N_EXPERTS].set(re_b[l].reshape(-1)).at[0, N_EXPERTS:N_EXPERTS + N_GROUPS].set(rg_b[l])

    w = dict(
        norm1_g=row(norm1_g[l]),
        w_in_segs=[w_in[l][:, a:b].astype(BF16) for a, b in zip(seg[:-1], seg[1:])],
        gm_ln_g=row(gm_ln_g[l]), gm_ln_b=row(gm_ln_b[l]),
        gm_mix_prompt=w_causal.astype(BF16), gm_bias_prompt=bias_prompt,
        gm_mix_sample=mix_sample.astype(BF16), gm_bias_sample=bias_sample,
        rw_prm=(row(rw_mu[l]), row(rw_w0[l]), rw_w2[l].astype(BF16), row(rw_a0[l]), rw_a2[l].astype(BF16),
                rw_g2[l].astype(BF16), row(rw_k_k[l]), row(rw_k_a[l]), row(rw_r_k[l]),
                row(rw_lnx_g[l]), row(rw_lnx_b[l])),
        gate_b=row(gate_b[l]), w_branch=w_branch[l].astype(BF16), w_out=w_out[l].astype(BF16),
        norm2_g=row(norm2_g[l]), wr_hi=wr_hi, wr_lo=wr_lo, b_r=b_r,
        e_wg=e_wg[l].reshape(N_EXPERTS, D_MODEL, EXP_FF),
        e_wu=e_wu[l].reshape(N_EXPERTS, D_MODEL, EXP_FF),
        e_wd=e_wd[l].reshape(N_EXPERTS, EXP_FF, D_MODEL),
        final_g=row(final_g),
    )

    mk_p, mv_p = _mem_kv(mem_prompt.reshape(bp * N_MEM, D_MODEL), xa_wk[l].astype(BF16), xa_wv[l].astype(BF16))
    n_p, n_s = bp * seq_p, bs * seq_s
    merged, sp, shp, _ = _branches(x_prompt.reshape(n_p, D_MODEL), mk_p, mv_p, None, w, None, n_p + n_s, 0,
                                   prompt=True, batch=bp, seq=seq_p)
    merged, ss, shs, vs = _branches(x_sample.reshape(n_s, D_MODEL),
                                    cache_mem_k[l].reshape(bs, N_MEM, BRANCH_W).astype(BF16),
                                    cache_mem_v[l].reshape(bs, N_MEM, BRANCH_W).astype(BF16),
                                    (state_rwkv_S[l], state_rwkv_shift[l]), w, merged, n_p + n_s, n_p,
                                    prompt=False, batch=bs, seq=seq_s)
    yp, ys = _moe(merged, [(0, n_p), (n_p, n_s)], w["e_wg"], w["e_wu"], w["e_wd"], w["final_g"])

    return (yp.reshape(bp, seq_p, D_MODEL), ys.reshape(bs, seq_s, D_MODEL),
            sp[None], shp[None],
            mk_p.reshape(1, bp, N_MEM, XA_HEADS, XA_HD), mv_p.reshape(1, bp, N_MEM, XA_HEADS, XA_HD),
            ss[None], shs[None], vs.reshape(1, bs, seq_s, BRANCH_W))
```

```python
import functools
import itertools

import numpy as np
import jax
import jax.numpy as jnp
from jax import lax
from jax.experimental import pallas as pl
from jax.experimental.pallas import tpu as pltpu

F32 = jnp.float32
BF16 = jnp.bfloat16

D_MODEL = 1024
BRANCH_W = 512
CHUNK = 128
GM_GROUPS = 8
GM_HD = BRANCH_W // GM_GROUPS
RW_HEADS = 8
RW_HD = BRANCH_W // RW_HEADS
RW_W_LORA = 64
RW_A_LORA = 64
RW_G_LORA = 128
RW_COLS = 3 * BRANCH_W + RW_W_LORA + RW_A_LORA + RW_G_LORA
XA_HEADS = 4
XA_HD = BRANCH_W // XA_HEADS
N_MEM = 256
N_BRANCH = 3
GATE_COLS = N_BRANCH * D_MODEL
N_GROUPS = 4
EXP_PER_GROUP = 8
N_EXPERTS = N_GROUPS * EXP_PER_GROUP
EXP_FF = 512
RMS_EPS = 1e-6
LN_EPS = 1e-5
GN_EPS = 64e-5

LANES = 128
RW_ROWS = 64
RW_GROUPS = 2
MIB = 1024 * 1024
ROUTE_E1, ROUTE_E2, ROUTE_W1, ROUTE_W2 = range(4)


def _dot(a, b):
    return jnp.dot(a.astype(BF16), b.astype(BF16), preferred_element_type=F32)


def _dot_nt(a, b):
    return lax.dot_general(a.astype(BF16), b.astype(BF16), (((1,), (1,)), ((), ())),
                           preferred_element_type=F32)


def _dot_tn(a, b):
    return lax.dot_general(a.astype(BF16), b.astype(BF16), (((0,), (0,)), ((), ())),
                           preferred_element_type=F32)


def _sigmoid(x):
    return 1.0 / (1.0 + jnp.exp(-x))


def _gelu(x):
    c = np.float32(np.sqrt(2.0 / np.pi))
    return x * (0.5 * (1.0 + jnp.tanh(c * (x + 0.044715 * (x * x * x)))))


def _softplus(x):
    return jnp.maximum(x, 0.0) + jnp.log(1.0 + jnp.exp(-jnp.abs(x)))


def _rmsnorm(x, g):
    return x * lax.rsqrt(jnp.mean(x * x, axis=-1, keepdims=True) + RMS_EPS) * g


def _split3(x):
    hi = x.astype(BF16)
    r1 = x - hi.astype(F32)
    mid = r1.astype(BF16)
    lo = (r1 - mid.astype(F32)).astype(BF16)
    return hi, mid, lo


def _params(sem, vmem_mib):
    return pltpu.CompilerParams(dimension_semantics=sem, vmem_limit_bytes=vmem_mib * MIB)


def _full(shape):
    nd = len(shape)
    return pl.BlockSpec(shape, lambda *_: (0,) * nd)


def _gmlp_chunk(pu, pv, ln_g, ln_b, wmix_ref, bmix_ref):
    u = _gelu(pu)
    vf = _gelu(pv)
    mu = jnp.mean(vf, axis=-1, keepdims=True)
    vc = vf - mu
    var = jnp.mean(vc * vc, axis=-1, keepdims=True)
    vn = vc * lax.rsqrt(var + LN_EPS) * ln_g + ln_b
    lane = lax.broadcasted_iota(jnp.int32, (CHUNK, LANES), 1)
    lo_half = lane < GM_HD
    outs = []
    for p in range(GM_GROUPS // 2):
        vp = vn[:, p * LANES:(p + 1) * LANES]
        s = (_dot(wmix_ref[2 * p], jnp.where(lo_half, vp, 0.0))
             + _dot(wmix_ref[2 * p + 1], jnp.where(lo_half, 0.0, vp)))
        outs.append(u[:, p * LANES:(p + 1) * LANES] * (s + bmix_ref[:, p * LANES:(p + 1) * LANES]))
    return jnp.concatenate(outs, axis=-1), vn


def _in_proj_kernel(x_ref, g_ref, wgm_ref, wrw_ref, wq_ref, wgt_ref, lng_ref, lnb_ref, wmix_ref, bmix_ref,
                    ogm_ref, rw_ref, q_ref, gt_ref, *vn_refs):
    xb = _rmsnorm(x_ref[...], g_ref[...]).astype(BF16)
    gm = jnp.dot(xb, wgm_ref[...], preferred_element_type=F32)
    rw_ref[...] = jnp.dot(xb, wrw_ref[...], preferred_element_type=F32)
    q_ref[...] = jnp.dot(xb, wq_ref[...], preferred_element_type=F32)
    gt_ref[...] = jnp.dot(xb, wgt_ref[...], preferred_element_type=F32)
    for c in range(x_ref.shape[0] // CHUNK):
        rows = slice(c * CHUNK, (c + 1) * CHUNK)
        o, vn = _gmlp_chunk(gm[rows, 0:BRANCH_W], gm[rows, BRANCH_W:2 * BRANCH_W], lng_ref[...], lnb_ref[...],
                            wmix_ref, bmix_ref)
        ogm_ref[rows, :] = o.astype(ogm_ref.dtype)
        if vn_refs:
            vn_refs[0][rows, :] = vn


def _in_proj(x, g, w_segs, ln_g, ln_b, wmix, bmix, emit_vn, tm):
    n = x.shape[0]
    widths = [w.shape[1] for w in w_segs]
    out_w = [BRANCH_W] + widths[1:] + ([BRANCH_W] if emit_vn else [])
    out_dt = [BF16] + [F32] * (len(out_w) - 1)
    return pl.pallas_call(
        _in_proj_kernel,
        grid=(n // tm,),
        in_specs=[pl.BlockSpec((tm, D_MODEL), lambda i: (i, 0)), _full((1, D_MODEL))]
        + [pl.BlockSpec((D_MODEL, wd), lambda i: (0, 0), pipeline_mode=pl.Buffered(1)) for wd in widths]
        + [_full((1, BRANCH_W)), _full((1, BRANCH_W)), _full((GM_GROUPS, CHUNK, CHUNK)),
           _full((CHUNK, BRANCH_W))],
        out_specs=[pl.BlockSpec((tm, wd), lambda i: (i, 0)) for wd in out_w],
        out_shape=[jax.ShapeDtypeStruct((n, wd), dt) for wd, dt in zip(out_w, out_dt)],
        compiler_params=_params(("arbitrary",), 48),
        name="in_proj",
    )(x, g, *w_segs, ln_g, ln_b, wmix, bmix)


def _mem_kv_kernel(m_ref, wk_ref, wv_ref, k_ref, v_ref):
    mb = m_ref[...].astype(BF16)
    k_ref[...] = jnp.dot(mb, wk_ref[...], preferred_element_type=F32)
    v_ref[...] = jnp.dot(mb, wv_ref[...], preferred_element_type=F32)


def _mem_kv(mem, wk, wv, tm=512):
    n = mem.shape[0]
    return pl.pallas_call(
        _mem_kv_kernel,
        grid=(n // tm,),
        in_specs=[pl.BlockSpec((tm, D_MODEL), lambda i: (i, 0)),
                  _full((D_MODEL, BRANCH_W)), _full((D_MODEL, BRANCH_W))],
        out_specs=[pl.BlockSpec((tm, BRANCH_W), lambda i: (i, 0))] * 2,
        out_shape=[jax.ShapeDtypeStruct((n, BRANCH_W), F32)] * 2,
        compiler_params=_params(("arbitrary",), 32),
        name="mem_kv",
    )(mem, wk, wv)


def _rwkv_rows(p, first, prev, s_ref, o_ref, prm, nseq):
    (mu, w0, w2, a0, a2, g2, k_k, k_a, r_k, lnx_g, lnx_b) = prm
    rows = RW_ROWS
    ls = rows // nseq
    shifted = jnp.where(first, prev, pltpu.roll(p, 1, 0))
    xs = p + (shifted - p) * mu
    r = xs[:, 0:BRANCH_W]
    k = xs[:, BRANCH_W:2 * BRANCH_W]
    v = xs[:, 2 * BRANCH_W:3 * BRANCH_W]
    o = 3 * BRANCH_W
    wd = xs[:, o:o + RW_W_LORA]
    ad = xs[:, o + RW_W_LORA:o + RW_W_LORA + RW_A_LORA]
    gd = xs[:, o + RW_W_LORA + RW_A_LORA:RW_COLS]

    w_log = -_softplus(-(w0 + _dot(jnp.tanh(wd), w2))) - 0.5
    logw = -jnp.exp(w_log)
    a = _sigmoid(a0 + _dot(ad, a2))
    g = _dot(_sigmoid(gd), g2)
    kkr = k * k_k
    kf = k * (1.0 + (a - 1.0) * k_a)
    rkr = r * kf * r_k

    ri = lax.broadcasted_iota(jnp.int32, (rows, rows), 0)
    ci = lax.broadcasted_iota(jnp.int32, (rows, rows), 1)
    if nseq == 1:
        same = ci >= 0
    else:
        sh = int(np.log2(ls))
        same = lax.shift_right_logical(ri, sh) == lax.shift_right_logical(ci, sh)
    low_incl = same & (ci <= ri)
    low_strict = same & (ci < ri)
    m_incl = jnp.where(low_incl, 1.0, 0.0).astype(BF16)
    m_same = jnp.where(same, 1.0, 0.0).astype(BF16)
    hi, mid, lo = _split3(logw)
    cum = (jnp.dot(m_incl, hi, preferred_element_type=F32) + jnp.dot(m_incl, mid, preferred_element_type=F32)
           + jnp.dot(m_incl, lo, preferred_element_type=F32))
    tot = (jnp.dot(m_same, hi, preferred_element_type=F32) + jnp.dot(m_same, mid, preferred_element_type=F32)
           + jnp.dot(m_same, lo, preferred_element_type=F32))
    g_t = jnp.exp(cum)
    g_prev = jnp.exp(cum - logw)
    g_inv = jnp.exp(-cum)
    g_end = jnp.exp(tot - cum)
    g_tot = jnp.exp(tot)
    eye = jnp.where(ri == ci, 1.0, 0.0)
    yield

    hs = range(RW_HEADS)
    sls = [slice(h * RW_HD, (h + 1) * RW_HD) for h in hs]
    kk = [kkr[:, sl] for sl in sls]
    kk = [x * lax.rsqrt(jnp.maximum(jnp.sum(x * x, axis=-1, keepdims=True), 1e-24)) for x in kk]
    bv = [kk[h] * a[:, sls[h]] for h in hs]
    k_h = [kf[:, sl] for sl in sls]
    v_h = [v[:, sl] for sl in sls]
    at = [-kk[h] * g_prev[:, sls[h]] for h in hs]
    rt = [r[:, sl] * g_t[:, sl] for sl in sls]
    bt = [bv[h] * g_inv[:, sls[h]] for h in hs]
    kt = [k_h[h] * g_inv[:, sls[h]] for h in hs]
    bh = [bv[h] * g_end[:, sls[h]] for h in hs]
    kh = [k_h[h] * g_end[:, sls[h]] for h in hs]

    ar = [jnp.concatenate([at[h], rt[h]], axis=0) for h in hs]
    bk_t = [jnp.concatenate([bt[h], kt[h]], axis=0) for h in hs]
    pp = [_dot_nt(ar[h], bk_t[h]) for h in hs]
    ri2 = lax.broadcasted_iota(jnp.int32, (rows, 2 * rows), 0)
    ci2 = lax.broadcasted_iota(jnp.int32, (rows, 2 * rows), 1)
    k_half = ci2 >= rows
    cpos = jnp.where(k_half, ci2 - rows, ci2)
    if nseq == 1:
        same2 = cpos >= 0
    else:
        same2 = lax.shift_right_logical(ri2, sh) == lax.shift_right_logical(cpos, sh)
    l_ab = [jnp.where(low_strict, x[:rows, :rows], 0.0) for x in pp]
    l_kv = [jnp.where(same2 & k_half & (cpos < ri2), x[:rows], 0.0) for x in pp]
    a_y = [jnp.where(same2 & (cpos <= ri2), x[rows:], 0.0) for x in pp]
    yield

    tm = [eye + x for x in l_ab]
    pw = [_dot(x, x) for x in l_ab]
    n_dbl = int(np.log2(ls)) - 1
    for it in range(n_dbl):
        yield
        if it < n_dbl - 1:
            z = [_dot(jnp.concatenate([tm[h], pw[h]], axis=0), pw[h]) for h in hs]
            tm = [tm[h] + z[h][:rows] for h in hs]
            pw = [z[h][rows:] for h in hs]
        else:
            tm = [tm[h] + _dot(tm[h], pw[h]) for h in hs]

    if nseq == 1:
        ars = [_dot_nt(ar[h], s_ref[0, h]) for h in hs]
        as0 = [x[:rows] for x in ars]
        rs0 = [x[rows:] for x in ars]
    else:
        as0, rs0 = [], []
        for h in hs:
            zs = [_dot_nt(jnp.concatenate([at[h][b * ls:(b + 1) * ls], rt[h][b * ls:(b + 1) * ls]], axis=0),
                          s_ref[b, h]) for b in range(nseq)]
            as0.append(jnp.concatenate([x[:ls] for x in zs], axis=0))
            rs0.append(jnp.concatenate([x[ls:] for x in zs], axis=0))

    lv = [_dot(l_kv[h], jnp.concatenate([v_h[h], v_h[h]], axis=0)) for h in hs]
    yield
    u = [_dot(tm[h], as0[h] + lv[h]) for h in hs]
    yield
    y = [rs0[h] + _dot(a_y[h], jnp.concatenate([u[h], v_h[h]], axis=0)) for h in hs]
    yield

    for h in hs:
        for b in range(nseq):
            rb = slice(b * ls, (b + 1) * ls)
            uv = jnp.concatenate([u[h][rb], v_h[h][rb]], axis=0)
            bk = jnp.concatenate([bh[h][rb], kh[h][rb]], axis=0)
            s_ref[b, h] = s_ref[b, h] * g_tot[b * ls:b * ls + 1, sls[h]] + _dot_tn(uv, bk)

    outs = []
    for h in hs:
        ym = jnp.mean(y[h], axis=-1, keepdims=True)
        yc = y[h] - ym
        yv = jnp.mean(yc * yc, axis=-1, keepdims=True)
        yn = yc * lax.rsqrt(yv + GN_EPS) * lnx_g[:, sls[h]] + lnx_b[:, sls[h]]
        bonus = jnp.sum(rkr[:, sls[h]], axis=-1, keepdims=True) * v_h[h]
        outs.append((yn + bonus) * g[:, sls[h]])
    o_ref[...] = jnp.concatenate(outs, axis=-1).astype(o_ref.dtype)


def _interleave(gens):
    for _ in itertools.zip_longest(*gens):
        pass


def _rwkv_prompt_kernel(p_ref, *refs):
    prm_refs, (o_ref, s_ref, sh_ref, carry_ref) = refs[:11], refs[11:]
    c = pl.program_id(1)

    @pl.when(c == 0)
    def _():
        s_ref[...] = jnp.zeros_like(s_ref)
        carry_ref[...] = jnp.zeros_like(carry_ref)

    first = lax.broadcasted_iota(jnp.int32, (RW_ROWS, 1), 0) == 0
    prm = tuple(x[...] for x in prm_refs)
    groups = range(p_ref.shape[0])
    _interleave([_rwkv_rows(p_ref[g], first, carry_ref[8 * g:8 * g + 1, :], s_ref.at[pl.ds(g, 1)],
                            o_ref.at[g], prm, nseq=1) for g in groups])
    for g in groups:
        last = p_ref[g, RW_ROWS - 1:RW_ROWS, :]
        carry_ref[8 * g:8 * g + 1, :] = last
        sh_ref[g] = last


def _rwkv_sample_kernel(p_ref, prev_ref, s0_ref, *refs):
    prm_refs, (o_ref, s_ref, sh_ref) = refs[:11], refs[11:]
    nseq = s0_ref.shape[0] // p_ref.shape[0]
    ls = RW_ROWS // nseq
    s_ref[...] = s0_ref[...]
    first = (lax.broadcasted_iota(jnp.int32, (RW_ROWS, 1), 0) & (ls - 1)) == 0
    prm = tuple(x[...] for x in prm_refs)
    groups = range(p_ref.shape[0])
    _interleave([_rwkv_rows(p_ref[g], first, prev_ref[g], s_ref.at[pl.ds(g * nseq, nseq)], o_ref.at[g],
                            prm, nseq=nseq) for g in groups])
    for g in groups:
        for b in range(nseq):
            sh_ref[g * nseq + b:g * nseq + b + 1, :] = p_ref[g, (b + 1) * ls - 1:(b + 1) * ls, :]


def _rw_param_specs():
    shapes = [(1, RW_COLS), (1, BRANCH_W), (RW_W_LORA, BRANCH_W), (1, BRANCH_W), (RW_A_LORA, BRANCH_W),
              (RW_G_LORA, BRANCH_W), (1, BRANCH_W), (1, BRANCH_W), (1, BRANCH_W), (1, BRANCH_W), (1, BRANCH_W)]
    return [_full(s) for s in shapes]


def _rwkv_prompt(p_rw, prm, batch, seq):
    nc = seq // RW_ROWS
    g = RW_GROUPS
    p3 = p_rw.reshape(batch, seq, RW_COLS)
    o, s_new, sh = pl.pallas_call(
        _rwkv_prompt_kernel,
        grid=(batch // g, nc),
        in_specs=[pl.BlockSpec((g, RW_ROWS, RW_COLS), lambda b, c: (b, c, 0))] + _rw_param_specs(),
        out_specs=[pl.BlockSpec((g, RW_ROWS, BRANCH_W), lambda b, c: (b, c, 0)),
                   pl.BlockSpec((g, RW_HEADS, RW_HD, RW_HD), lambda b, c: (b, 0, 0, 0)),
                   pl.BlockSpec((g, 1, RW_COLS), lambda b, c: (b, 0, 0))],
        out_shape=[jax.ShapeDtypeStruct((batch, seq, BRANCH_W), BF16),
                   jax.ShapeDtypeStruct((batch, RW_HEADS, RW_HD, RW_HD), F32),
                   jax.ShapeDtypeStruct((batch, 1, RW_COLS), F32)],
        scratch_shapes=[pltpu.VMEM((8 * g, RW_COLS), F32)],
        compiler_params=_params(("arbitrary", "arbitrary"), 56),
        name="rwkv_prompt",
    )(p3, *prm)
    return o.reshape(batch * seq, BRANCH_W), s_new, sh.reshape(batch, RW_COLS)


def _rwkv_sample(p_rw, prev_rows, s0, prm, seq):
    n = p_rw.shape[0]
    nseq = RW_ROWS // seq
    batch = n // seq
    g = RW_GROUPS
    ng = n // RW_ROWS
    o, s_new, sh = pl.pallas_call(
        _rwkv_sample_kernel,
        grid=(ng // g,),
        in_specs=[pl.BlockSpec((g, RW_ROWS, RW_COLS), lambda i: (i, 0, 0)),
                  pl.BlockSpec((g, RW_ROWS, RW_COLS), lambda i: (i, 0, 0)),
                  pl.BlockSpec((g * nseq, RW_HEADS, RW_HD, RW_HD), lambda i: (i, 0, 0, 0))] + _rw_param_specs(),
        out_specs=[pl.BlockSpec((g, RW_ROWS, BRANCH_W), lambda i: (i, 0, 0)),
                   pl.BlockSpec((g * nseq, RW_HEADS, RW_HD, RW_HD), lambda i: (i, 0, 0, 0)),
                   pl.BlockSpec((g * nseq, RW_COLS), lambda i: (i, 0))],
        out_shape=[jax.ShapeDtypeStruct((ng, RW_ROWS, BRANCH_W), BF16),
                   jax.ShapeDtypeStruct((batch, RW_HEADS, RW_HD, RW_HD), F32),
                   jax.ShapeDtypeStruct((batch, RW_COLS), F32)],
        compiler_params=_params(("arbitrary",), 56),
        name="rwkv_sample",
    )(p_rw.reshape(ng, RW_ROWS, RW_COLS), prev_rows.reshape(ng, RW_ROWS, RW_COLS), s0, *prm)
    return o.reshape(n, BRANCH_W), s_new, sh


def _attend_all(qkv):
    s = [_dot_nt(q, k) * (XA_HD ** -0.5) for q, k, _ in qkv]
    e = [jnp.exp(x - jnp.max(x, axis=-1, keepdims=True)) for x in s]
    pr = [x / jnp.sum(x, axis=-1, keepdims=True) for x in e]
    return [_dot(p, v) for p, (_, _, v) in zip(pr, qkv)]


def _xattn_prompt_kernel(q_ref, k_ref, v_ref, o_ref):
    sls = [slice(h * XA_HD, (h + 1) * XA_HD) for h in range(XA_HEADS)]
    outs = _attend_all([(q_ref[:, sl], k_ref[:, sl], v_ref[:, sl]) for sl in sls])
    for sl, o in zip(sls, outs):
        o_ref[:, sl] = o.astype(o_ref.dtype)


def _xattn_prompt(q, mk, mv, batch, seq, tq=512):
    nt = seq // tq
    return pl.pallas_call(
        _xattn_prompt_kernel,
        grid=(batch, nt),
        in_specs=[pl.BlockSpec((tq, BRANCH_W), lambda b, i: (b * nt + i, 0)),
                  pl.BlockSpec((N_MEM, BRANCH_W), lambda b, i: (b, 0)),
                  pl.BlockSpec((N_MEM, BRANCH_W), lambda b, i: (b, 0))],
        out_specs=pl.BlockSpec((tq, BRANCH_W), lambda b, i: (b * nt + i, 0)),
        out_shape=jax.ShapeDtypeStruct((batch * seq, BRANCH_W), BF16),
        compiler_params=_params(("arbitrary", "arbitrary"), 32),
        name="xattn_prompt",
    )(q, mk, mv)


def _xattn_sample_kernel(q_ref, k_ref, v_ref, o_ref):
    nb = k_ref.shape[0]
    ls = q_ref.shape[0] // nb
    nq = XA_HEADS * ls
    seqs = range(nb)
    rows = [slice(b * ls, (b + 1) * ls) for b in seqs]
    qs = [jnp.concatenate([q_ref[rows[b], h * XA_HD:(h + 1) * XA_HD] for h in range(XA_HEADS)], axis=0)
          for b in seqs]
    ri = lax.broadcasted_iota(jnp.int32, (N_MEM * XA_HEADS, nq), 0)
    ci = lax.broadcasted_iota(jnp.int32, (N_MEM * XA_HEADS, nq), 1)
    own = (ri & (XA_HEADS - 1)) == lax.shift_right_logical(ci, int(np.log2(ls)))
    s = [jnp.where(own, _dot_nt(k_ref[b], qs[b]) * (XA_HD ** -0.5), -jnp.inf) for b in seqs]
    e = [jnp.exp(x - jnp.max(x, axis=0, keepdims=True)) for x in s]
    pr = [x / jnp.sum(x, axis=0, keepdims=True) for x in e]
    outs = [_dot_tn(pr[b], v_ref[b]) for b in seqs]
    for b in seqs:
        o_ref[rows[b], :] = jnp.concatenate([outs[b][h * ls:(h + 1) * ls] for h in range(XA_HEADS)],
                                            axis=-1).astype(o_ref.dtype)


def _xattn_sample(q, mk, mv, seq, nb=8):
    batch = mk.shape[0]
    mem_spec = pl.BlockSpec((nb, N_MEM * XA_HEADS, XA_HD), lambda i: (i, 0, 0))
    return pl.pallas_call(
        _xattn_sample_kernel,
        grid=(batch // nb,),
        in_specs=[pl.BlockSpec((nb * seq, BRANCH_W), lambda i: (i, 0)), mem_spec, mem_spec],
        out_specs=pl.BlockSpec((nb * seq, BRANCH_W), lambda i: (i, 0)),
        out_shape=jax.ShapeDtypeStruct((batch * seq, BRANCH_W), BF16),
        compiler_params=_params(("arbitrary",), 40),
        name="xattn_sample",
    )(q, mk, mv)


def _merge_kernel(n_active, *refs):
    i = pl.program_id(0)

    @pl.when(i < n_active)
    def _():
        _merge_rows(*refs)

    @pl.when(i >= n_active)
    def _():
        for ref in refs[-5:-1]:
            ref[...] = jnp.zeros_like(ref)


def _merge_rows(x_ref, gm_ref, rw_ref, xa_ref, gt_ref, gb_ref, wb_ref, wo_ref, n2_ref,
                wr_hi_ref, wr_lo_ref, br_ref, cnt_in_ref, *refs):
    h_ref, hn_ref, route_ref, ridx_ref, cnt_ref = refs[-5:]
    merged = None
    for n, br in enumerate((gm_ref, rw_ref, xa_ref)):
        cs = slice(n * D_MODEL, (n + 1) * D_MODEL)
        up = jnp.dot(br[...], wb_ref[n], preferred_element_type=F32)
        term = _sigmoid(gt_ref[:, cs] + gb_ref[:, cs]) * up
        merged = term if merged is None else merged + term
    h = x_ref[...] + _dot(merged, wo_ref[...])
    h_ref[...] = h
    hn = _rmsnorm(h, n2_ref[...])
    hn_ref[...] = hn

    hn_hi = hn.astype(BF16)
    hn_lo = (hn - hn_hi.astype(F32)).astype(BF16)
    logits = (jnp.dot(hn_hi, wr_hi_ref[...], preferred_element_type=F32)
              + jnp.dot(hn_lo, wr_hi_ref[...], preferred_element_type=F32)
              + jnp.dot(hn_hi, wr_lo_ref[...], preferred_element_type=F32)) + br_ref[...]
    lane = lax.broadcasted_iota(jnp.int32, logits.shape, 1)
    neg = -jnp.inf
    big = jnp.int32(1 << 20)
    gmask = (lane >= N_EXPERTS) & (lane < N_EXPERTS + N_GROUPS)
    gl = jnp.where(gmask, logits, neg)
    gmax = jnp.max(gl, axis=-1, keepdims=True)
    gsel = jnp.min(jnp.where(gl == gmax, lane, big), axis=-1, keepdims=True) - N_EXPERTS
    gsum = jnp.sum(jnp.where(gmask, jnp.exp(logits - gmax), 0.0), axis=-1, keepdims=True)
    pg_top = 1.0 / gsum
    emask = (lane >= gsel * EXP_PER_GROUP) & (lane < (gsel + 1) * EXP_PER_GROUP)
    el = jnp.where(emask, logits, neg)
    m1 = jnp.max(el, axis=-1, keepdims=True)
    i1 = jnp.min(jnp.where(el == m1, lane, big), axis=-1, keepdims=True)
    el2 = jnp.where(lane == i1, neg, el)
    m2 = jnp.max(el2, axis=-1, keepdims=True)
    i2 = jnp.min(jnp.where(el2 == m2, lane, big), axis=-1, keepdims=True)
    t2 = jnp.exp(m2 - m1)
    w1 = pg_top / (1.0 + t2)
    w2 = pg_top * t2 / (1.0 + t2)

    @pl.when(pl.program_id(0) == 0)
    def _():
        cnt_ref[...] = cnt_in_ref[...]

    onehot = jnp.where(lane == i1, 1.0, 0.0) + jnp.where(lane == i2, 1.0, 0.0)
    cnt_ref[0:1, :] = cnt_ref[0:1, :] + jnp.sum(onehot, axis=0, keepdims=True)

    route = jnp.zeros(logits.shape, F32)
    for col, val in ((ROUTE_E1, i1.astype(F32)), (ROUTE_E2, i2.astype(F32)), (ROUTE_W1, w1), (ROUTE_W2, w2)):
        route = jnp.where(lane == col, val, route)
    route_ref[...] = route
    ridx_ref[...] = jnp.transpose(route)[0:8, :].astype(jnp.int32)


def _merge(x, o_gm, o_rw, o_xa, p_gate, gate_b, w_branch, w_out, n2_g, wr_hi, wr_lo, b_r, n_total, row0, prev,
           tm=256):
    n = x.shape[0]
    t0 = row0 // tm
    n_act = n // tm
    n_steps = n_total // tm if prev is None else n_act
    row = lambda wd: pl.BlockSpec((tm, wd), lambda i: (jnp.minimum(i, n_act - 1), 0))
    out_row = lambda wd: pl.BlockSpec((tm, wd), lambda i: (t0 + i, 0))
    out_shape = [jax.ShapeDtypeStruct((n_total, D_MODEL), F32), jax.ShapeDtypeStruct((n_total, D_MODEL), F32),
                 jax.ShapeDtypeStruct((n_total, LANES), F32), jax.ShapeDtypeStruct((8, n_total), jnp.int32),
                 jax.ShapeDtypeStruct((8, LANES), F32)]
    if prev is None:
        carried, counts_in, aliases = [], jnp.zeros((8, LANES), F32), {}
    else:
        carried, counts_in = list(prev[:4]), prev[4]
        aliases = {13 + k: k for k in range(4)}
    return pl.pallas_call(
        functools.partial(_merge_kernel, n_act),
        grid=(n_steps,),
        in_specs=[row(D_MODEL), row(BRANCH_W), row(BRANCH_W), row(BRANCH_W), row(GATE_COLS),
                  _full((1, GATE_COLS)), _full((N_BRANCH, BRANCH_W, D_MODEL)), _full((D_MODEL, D_MODEL)),
                  _full((1, D_MODEL)), _full((D_MODEL, LANES)), _full((D_MODEL, LANES)), _full((1, LANES)),
                  _full((8, LANES))] + [pl.BlockSpec(memory_space=pl.ANY)] * len(carried),
        out_specs=[out_row(D_MODEL), out_row(D_MODEL), out_row(LANES),
                   pl.BlockSpec((8, tm), lambda i: (0, t0 + i)), _full((8, LANES))],
        out_shape=out_shape,
        input_output_aliases=aliases,
        compiler_params=_params(("arbitrary",), 40),
        name="merge",
    )(x, o_gm, o_rw, o_xa, p_gate, gate_b, w_branch, w_out, n2_g, wr_hi, wr_lo, b_r, counts_in, *carried)


def _route_plan(ridx, counts, tm):
    n = ridx.shape[1]
    n2 = 2 * n
    n_tiles = n2 // tm + N_EXPERTS
    n_rows = n_tiles * tm
    key_bits = int(n_rows - 1).bit_length()
    cnt = counts[0, :N_EXPERTS].astype(jnp.int32)
    tiles = (cnt + tm - 1) // tm
    tile_end = jnp.cumsum(tiles)
    n_used = tile_end[N_EXPERTS - 1:]
    tc = jnp.minimum(jnp.arange(n_tiles, dtype=jnp.int32), n_used - 1)
    tile_expert = jnp.sum((tc[:, None] >= tile_end[None, :]).astype(jnp.int32), axis=1)
    pad_end = jnp.cumsum(tiles * tm - cnt)
    d = jnp.arange(n_rows - n2, dtype=jnp.int32)
    pad_expert = jnp.sum((d[:, None] >= pad_end[None, :]).astype(jnp.int32), axis=1)
    eid = ridx[ROUTE_E1:ROUTE_E2 + 1].reshape(n2)
    keys = jnp.concatenate([(eid << key_bits) + jnp.arange(n2, dtype=jnp.int32),
                            (pad_expert << key_bits) + n2 + d])
    row_id = jnp.sort(keys) & ((1 << key_bits) - 1)
    src = jnp.where(row_id < n2, jnp.where(row_id >= n, row_id - n, row_id), 0)
    return src.reshape(n_tiles, 1, tm), row_id.reshape(n_tiles, 1, tm), tile_expert, n_used


def _row_copy(src_ref, src_row, dst_ref, dst_row, sem):
    return pltpu.make_async_copy(src_ref.at[pl.ds(src_row, 1)], dst_ref.at[pl.ds(dst_row, 1)], sem)


def _experts_kernel(te_ref, nu_ref, src_ref, nxt_ref, dst_ref, hn_ref, wg_ref, wu_ref, wd_ref, o_ref,
                    xbuf, ybuf, wg_s, wu_s, wd_s, gsem, ssem):
    t = pl.program_id(0)
    tm = xbuf.shape[1]
    n2 = o_ref.shape[0] - N_EXPERTS * tm
    slot = t & 1
    other = 1 - slot

    def gather(idx_ref, s):
        for r in range(tm):
            _row_copy(hn_ref, idx_ref[0, 0, r], xbuf.at[s], r, gsem.at[s]).start(priority=0)

    def wait_rows(buf, sem, s):
        for _ in range(tm):
            _row_copy(hn_ref, 0, buf.at[s], 0, sem.at[s]).wait()

    @pl.when(t == 0)
    def _():
        ybuf[1] = jnp.zeros(ybuf.shape[1:], ybuf.dtype)
        fills = [pltpu.make_async_copy(ybuf.at[1], o_ref.at[pl.ds(n2 + e * tm, tm)], ssem.at[1])
                 for e in range(N_EXPERTS)]
        for f in fills:
            f.start()
        for f in fills:
            f.wait()
        gather(src_ref, 0)

    @pl.when(t < nu_ref[0])
    def _():
        wait_rows(xbuf, gsem, slot)

        @pl.when(t >= 2)
        def _():
            wait_rows(ybuf, ssem, slot)

        @pl.when((t == 0) | (te_ref[t] != te_ref[jnp.maximum(t - 1, 0)]))
        def _():
            wg_s[...] = wg_ref[...].astype(BF16)
            wu_s[...] = wu_ref[...].astype(BF16)
            wd_s[...] = wd_ref[...].astype(BF16)

        gather(nxt_ref, other)
        xb = xbuf[slot].astype(BF16)
        gate = jnp.dot(xb, wg_s[...], preferred_element_type=F32)
        up = jnp.dot(xb, wu_s[...], preferred_element_type=F32)
        ybuf[slot] = _dot(gate * _sigmoid(gate) * up, wd_s[...])
        for r in range(tm):
            _row_copy(ybuf.at[slot], r, o_ref, dst_ref[0, 0, r], ssem.at[slot]).start(priority=1)

        @pl.when(t == nu_ref[0] - 1)
        def _():
            wait_rows(xbuf, gsem, other)

            @pl.when(t >= 1)
            def _():
                wait_rows(ybuf, ssem, other)

            wait_rows(ybuf, ssem, slot)


def _experts(hn, src, dst, tile_expert, n_used, wg, wu, wd, tm):
    n = hn.shape[0]
    n_tiles = src.shape[0]
    idx = lambda f: pl.BlockSpec((1, 1, tm), lambda t, te, nu: (f(t), 0, 0), memory_space=pltpu.SMEM)
    weight = lambda shape: pl.BlockSpec((None,) + shape, lambda t, te, nu: (te[t], 0, 0))
    return pl.pallas_call(
        _experts_kernel,
        grid_spec=pltpu.PrefetchScalarGridSpec(
            num_scalar_prefetch=2,
            grid=(n_tiles,),
            in_specs=[idx(lambda t: t), idx(lambda t: jnp.minimum(t + 1, n_tiles - 1)), idx(lambda t: t),
                      pl.BlockSpec(memory_space=pl.ANY),
                      weight((D_MODEL, EXP_FF)), weight((D_MODEL, EXP_FF)), weight((EXP_FF, D_MODEL))],
            out_specs=pl.BlockSpec(memory_space=pl.ANY),
            scratch_shapes=[pltpu.VMEM((2, tm, D_MODEL), F32), pltpu.VMEM((2, tm, D_MODEL), F32),
                            pltpu.VMEM((D_MODEL, EXP_FF), BF16), pltpu.VMEM((D_MODEL, EXP_FF), BF16),
                            pltpu.VMEM((EXP_FF, D_MODEL), BF16),
                            pltpu.SemaphoreType.DMA((2,)), pltpu.SemaphoreType.DMA((2,))]),
        out_shape=jax.ShapeDtypeStruct((2 * n + N_EXPERTS * tm, D_MODEL), F32),
        compiler_params=_params(("arbitrary",), 40),
        name="moe_experts",
    )(tile_expert, n_used, src, src, dst, hn, wg, wu, wd)


def _combine_kernel(h_ref, route_ref, fg_ref, o1_ref, o2_ref, y_ref):
    rt = route_ref[...]
    lane = lax.broadcasted_iota(jnp.int32, rt.shape, 1)
    w1 = jnp.sum(jnp.where(lane == ROUTE_W1, rt, 0.0), axis=-1, keepdims=True)
    w2 = jnp.sum(jnp.where(lane == ROUTE_W2, rt, 0.0), axis=-1, keepdims=True)
    y_ref[...] = _rmsnorm(h_ref[...] + w1 * o1_ref[...] + w2 * o2_ref[...], fg_ref[...])


def _combine(h, route, o, final_g, row0, n_out, tm=512):
    n = h.shape[0]
    t0 = row0 // tm
    return pl.pallas_call(
        _combine_kernel,
        grid=(n_out // tm,),
        in_specs=[pl.BlockSpec((tm, D_MODEL), lambda i: (t0 + i, 0)),
                  pl.BlockSpec((tm, LANES), lambda i: (t0 + i, 0)),
                  _full((1, D_MODEL)),
                  pl.BlockSpec((tm, D_MODEL), lambda i: (t0 + i, 0)),
                  pl.BlockSpec((tm, D_MODEL), lambda i: (n // tm + t0 + i, 0))],
        out_specs=pl.BlockSpec((tm, D_MODEL), lambda i: (i, 0)),
        out_shape=jax.ShapeDtypeStruct((n_out, D_MODEL), F32),
        compiler_params=_params(("arbitrary",), 40),
        name="moe_combine",
    )(h, route, final_g, o, o)


def _moe(merged, group_rows, wg, wu, wd, final_g, tm=256):
    h, hn, route, ridx, counts = merged
    src, dst, tile_expert, n_used = _route_plan(ridx, counts, tm)
    o = _experts(hn, src, dst, tile_expert, n_used, wg, wu, wd, tm)
    return [_combine(h, route, o, final_g, row0, rows) for row0, rows in group_rows]


def _branches(x2d, mem_k, mem_v, rw_state, w, merged, n_total, row0, *, prompt, batch, seq):
    n = x2d.shape[0]
    tm_in = 256
    grp = "prompt" if prompt else "sample"
    o_gm, p_rw, p_q, p_gate, *vn = _in_proj(x2d, w["norm1_g"], w["w_in_segs"], w["gm_ln_g"], w["gm_ln_b"],
                                            w["gm_mix_" + grp], w["gm_bias_" + grp], not prompt, tm_in)
    vn = vn[0] if vn else None
    if prompt:
        o_rw, s_new, sh_new = _rwkv_prompt(p_rw, w["rw_prm"], batch, seq)
        o_xa = _xattn_prompt(p_q, mem_k, mem_v, batch, seq)
    else:
        s0, shift = rw_state
        prev_rows = jnp.pad(shift[:, None, :], ((0, 0), (0, seq - 1), (0, 0))).reshape(n, RW_COLS)
        o_rw, s_new, sh_new = _rwkv_sample(p_rw, prev_rows, s0, w["rw_prm"], seq)
        o_xa = _xattn_sample(p_q, mem_k, mem_v, seq)
    merged = _merge(x2d, o_gm, o_rw, o_xa, p_gate, w["gate_b"], w["w_branch"], w["w_out"],
                    w["norm2_g"], w["wr_hi"], w["wr_lo"], w["b_r"], n_total, row0, merged)
    return merged, s_new, sh_new, vn


def kernel(x_prompt, x_sample, state_rwkv_S, state_rwkv_shift, cache_mem_k, cache_mem_v, mem_prompt, norm1_g, w_in, gate_b, gm_ln_g, gm_ln_b, gm_ws, gm_bs, rw_mu, rw_w0, rw_w2, rw_a0, rw_a2, rw_g2, rw_k_k, rw_k_a, rw_r_k, rw_lnx_g, rw_lnx_b, xa_wk, xa_wv, w_branch, w_out, norm2_g, rg_w, rg_b, re_w, re_b, e_wg, e_wu, e_wd, final_g):
    bp, seq_p, _ = x_prompt.shape
    bs, seq_s, _ = x_sample.shape
    depth = w_in.shape[0]
    assert depth == 1 and seq_p % CHUNK == 0 and RW_ROWS % seq_s == 0 and CHUNK % seq_s == 0

    l = 0
    row = lambda a: a.reshape(1, -1)
    seg = (0, 2 * BRANCH_W, 2 * BRANCH_W + RW_COLS, 3 * BRANCH_W + RW_COLS, 3 * BRANCH_W + RW_COLS + GATE_COLS)
    w_causal = jnp.tril(gm_ws[l])
    nrep = CHUNK // seq_s
    blk = w_causal[:, :seq_s, :seq_s]
    eye_rep = jnp.eye(nrep, dtype=F32)
    mix_sample = jnp.einsum("ab,gts->gatbs", eye_rep, blk).reshape(GM_GROUPS, CHUNK, CHUNK)
    bias_prompt = jnp.repeat(gm_bs[l].T, GM_HD, axis=1)
    bias_sample = jnp.tile(bias_prompt[:seq_s], (nrep, 1))

    wr = jnp.zeros((D_MODEL, LANES), F32)
    wr = wr.at[:, :N_EXPERTS].set(jnp.transpose(re_w[l], (1, 0, 2)).reshape(D_MODEL, N_EXPERTS))
    wr = wr.at[:, N_EXPERTS:N_EXPERTS + N_GROUPS].set(rg_w[l])
    wr_hi = wr.astype(BF16)
    wr_lo = (wr - wr_hi.astype(F32)).astype(BF16)
    b_r = jnp.zeros((1, LANES), F32)
    b_r = b_r.at[0, :N_EXPERTS].set(re_b[l].reshape(-1)).at[0, N_EXPERTS:N_EXPERTS + N_GROUPS].set(rg_b[l])

    w = dict(
        norm1_g=row(norm1_g[l]),
        w_in_segs=[w_in[l][:, a:b].astype(BF16) for a, b in zip(seg[:-1], seg[1:])],
        gm_ln_g=row(gm_ln_g[l]), gm_ln_b=row(gm_ln_b[l]),
        gm_mix_prompt=w_causal.astype(BF16), gm_bias_prompt=bias_prompt,
        gm_mix_sample=mix_sample.astype(BF16), gm_bias_sample=bias_sample,
        rw_prm=(row(rw_mu[l]), row(rw_w0[l]), rw_w2[l].astype(BF16), row(rw_a0[l]), rw_a2[l].astype(BF16),
                rw_g2[l].astype(BF16), row(rw_k_k[l]), row(rw_k_a[l]), row(rw_r_k[l]),
                row(rw_lnx_g[l]), row(rw_lnx_b[l])),
        gate_b=row(gate_b[l]), w_branch=w_branch[l].astype(BF16), w_out=w_out[l].astype(BF16),
        norm2_g=row(norm2_g[l]), wr_hi=wr_hi, wr_lo=wr_lo, b_r=b_r,
        e_wg=e_wg[l].reshape(N_EXPERTS, D_MODEL, EXP_FF),
        e_wu=e_wu[l].reshape(N_EXPERTS, D_MODEL, EXP_FF),
        e_wd=e_wd[l].reshape(N_EXPERTS, EXP_FF, D_MODEL),
        final_g=row(final_g),
    )

    mk_p, mv_p = _mem_kv(mem_prompt.reshape(bp * N_MEM, D_MODEL), xa_wk[l].astype(BF16), xa_wv[l].astype(BF16))
    n_p, n_s = bp * seq_p, bs * seq_s
    merged, sp, shp, _ = _branches(x_prompt.reshape(n_p, D_MODEL), mk_p, mv_p, None, w, None, n_p + n_s, 0,
                                   prompt=True, batch=bp, seq=seq_p)
    merged, ss, shs, vs = _branches(x_sample.reshape(n_s, D_MODEL),
                                    cache_mem_k[l].reshape(bs, N_MEM * XA_HEADS, XA_HD),
                                    cache_mem_v[l].reshape(bs, N_MEM * XA_HEADS, XA_HD),
                                    (state_rwkv_S[l], state_rwkv_shift[l]), w, merged, n_p + n_s, n_p,
                                    prompt=False, batch=bs, seq=seq_s)
    yp, ys = _moe(merged, [(0, n_p), (n_p, n_s)], w["e_wg"], w["e_wu"], w["e_wd"], w["final_g"])

    return (yp.reshape(bp, seq_p, D_MODEL), ys.reshape(bs, seq_s, D_MODEL),
            sp[None], shp[None],
            mk_p.reshape(1, bp, N_MEM, XA_HEADS, XA_HD), mv_p.reshape(1, bp, N_MEM, XA_HEADS, XA_HD),
            ss[None], shs[None], vs.reshape(1, bs, seq_s, BRANCH_W))
```

```python
import functools
import itertools

import numpy as np
import jax
import jax.numpy as jnp
from jax import lax
from jax.experimental import pallas as pl
from jax.experimental.pallas import tpu as pltpu

F32 = jnp.float32
BF16 = jnp.bfloat16

D_MODEL = 1024
BRANCH_W = 512
CHUNK = 128
GM_GROUPS = 8
GM_HD = BRANCH_W // GM_GROUPS
RW_HEADS = 8
RW_HD = BRANCH_W // RW_HEADS
RW_W_LORA = 64
RW_A_LORA = 64
RW_G_LORA = 128
RW_COLS = 3 * BRANCH_W + RW_W_LORA + RW_A_LORA + RW_G_LORA
XA_HEADS = 4
XA_HD = BRANCH_W // XA_HEADS
N_MEM = 256
N_BRANCH = 3
GATE_COLS = N_BRANCH * D_MODEL
N_GROUPS = 4
EXP_PER_GROUP = 8
N_EXPERTS = N_GROUPS * EXP_PER_GROUP
EXP_FF = 512
RMS_EPS = 1e-6
LN_EPS = 1e-5
GN_EPS = 64e-5

LANES = 128
MIB = 1024 * 1024
IN_PROJ_ROWS = 256
MEM_KV_ROWS = 512
RW_ROWS = 64
RW_GROUPS = 2
XATTN_ROWS = 512
XATTN_SEQS = 8
MERGE_ROWS = 256
MOE_ROWS = 256
COMBINE_ROWS = 512
VMEM_SMALL, VMEM_MID, VMEM_IN_PROJ, VMEM_RWKV = 32, 40, 48, 56
ROUTE_E1, ROUTE_E2, ROUTE_W1, ROUTE_W2 = range(4)


def _dot(a, b):
    return jnp.dot(a.astype(BF16), b.astype(BF16), preferred_element_type=F32)


def _dot_nt(a, b):
    return lax.dot_general(a.astype(BF16), b.astype(BF16), (((1,), (1,)), ((), ())),
                           preferred_element_type=F32)


def _dot_tn(a, b):
    return lax.dot_general(a.astype(BF16), b.astype(BF16), (((0,), (0,)), ((), ())),
                           preferred_element_type=F32)


def _sigmoid(x):
    return 1.0 / (1.0 + jnp.exp(-x))


def _gelu(x):
    c = np.float32(np.sqrt(2.0 / np.pi))
    return x * (0.5 * (1.0 + jnp.tanh(c * (x + 0.044715 * (x * x * x)))))


def _softplus(x):
    return jnp.maximum(x, 0.0) + jnp.log(1.0 + jnp.exp(-jnp.abs(x)))


def _rmsnorm(x, g):
    return x * lax.rsqrt(jnp.mean(x * x, axis=-1, keepdims=True) + RMS_EPS) * g


def _split3(x):
    hi = x.astype(BF16)
    r1 = x - hi.astype(F32)
    mid = r1.astype(BF16)
    lo = (r1 - mid.astype(F32)).astype(BF16)
    return hi, mid, lo


def _params(sem, vmem_mib):
    return pltpu.CompilerParams(dimension_semantics=sem, vmem_limit_bytes=vmem_mib * MIB)


def _full(shape):
    nd = len(shape)
    return pl.BlockSpec(shape, lambda *_: (0,) * nd)


def _gmlp_chunk(pu, pv, ln_g, ln_b, wmix_ref, bmix_ref):
    u = _gelu(pu)
    vf = _gelu(pv)
    mu = jnp.mean(vf, axis=-1, keepdims=True)
    vc = vf - mu
    var = jnp.mean(vc * vc, axis=-1, keepdims=True)
    vn = vc * lax.rsqrt(var + LN_EPS) * ln_g + ln_b
    lane = lax.broadcasted_iota(jnp.int32, (CHUNK, LANES), 1)
    lo_half = lane < GM_HD
    outs = []
    for p in range(GM_GROUPS // 2):
        vp = vn[:, p * LANES:(p + 1) * LANES]
        s = (_dot(wmix_ref[2 * p], jnp.where(lo_half, vp, 0.0))
             + _dot(wmix_ref[2 * p + 1], jnp.where(lo_half, 0.0, vp)))
        outs.append(u[:, p * LANES:(p + 1) * LANES] * (s + bmix_ref[:, p * LANES:(p + 1) * LANES]))
    return jnp.concatenate(outs, axis=-1), vn


def _in_proj_kernel(x_ref, g_ref, wgm_ref, wrw_ref, wq_ref, wgt_ref, lng_ref, lnb_ref, wmix_ref, bmix_ref,
                    ogm_ref, rw_ref, q_ref, gt_ref, *vn_refs):
    xb = _rmsnorm(x_ref[...], g_ref[...]).astype(BF16)
    gm = jnp.dot(xb, wgm_ref[...], preferred_element_type=F32)
    rw_ref[...] = jnp.dot(xb, wrw_ref[...], preferred_element_type=F32)
    q_ref[...] = jnp.dot(xb, wq_ref[...], preferred_element_type=F32)
    gt_ref[...] = jnp.dot(xb, wgt_ref[...], preferred_element_type=F32)
    for c in range(x_ref.shape[0] // CHUNK):
        rows = slice(c * CHUNK, (c + 1) * CHUNK)
        o, vn = _gmlp_chunk(gm[rows, 0:BRANCH_W], gm[rows, BRANCH_W:2 * BRANCH_W], lng_ref[...], lnb_ref[...],
                            wmix_ref, bmix_ref)
        ogm_ref[rows, :] = o.astype(ogm_ref.dtype)
        if vn_refs:
            vn_refs[0][rows, :] = vn


def _in_proj(x, g, w_segs, ln_g, ln_b, wmix, bmix, emit_vn, tm):
    n = x.shape[0]
    widths = [w.shape[1] for w in w_segs]
    out_w = [BRANCH_W] + widths[1:] + ([BRANCH_W] if emit_vn else [])
    out_dt = [BF16] + [F32] * (len(out_w) - 1)
    return pl.pallas_call(
        _in_proj_kernel,
        grid=(n // tm,),
        in_specs=[pl.BlockSpec((tm, D_MODEL), lambda i: (i, 0)), _full((1, D_MODEL))]
        + [pl.BlockSpec((D_MODEL, wd), lambda i: (0, 0), pipeline_mode=pl.Buffered(1)) for wd in widths]
        + [_full((1, BRANCH_W)), _full((1, BRANCH_W)), _full((GM_GROUPS, CHUNK, CHUNK)),
           _full((CHUNK, BRANCH_W))],
        out_specs=[pl.BlockSpec((tm, wd), lambda i: (i, 0)) for wd in out_w],
        out_shape=[jax.ShapeDtypeStruct((n, wd), dt) for wd, dt in zip(out_w, out_dt)],
        compiler_params=_params(("arbitrary",), VMEM_IN_PROJ),
        name="in_proj",
    )(x, g, *w_segs, ln_g, ln_b, wmix, bmix)


def _mem_kv_kernel(m_ref, wk_ref, wv_ref, k_ref, v_ref):
    mb = m_ref[...].astype(BF16)
    k_ref[...] = jnp.dot(mb, wk_ref[...], preferred_element_type=F32)
    v_ref[...] = jnp.dot(mb, wv_ref[...], preferred_element_type=F32)


def _mem_kv(mem, wk, wv, tm=MEM_KV_ROWS):
    n = mem.shape[0]
    return pl.pallas_call(
        _mem_kv_kernel,
        grid=(n // tm,),
        in_specs=[pl.BlockSpec((tm, D_MODEL), lambda i: (i, 0)),
                  _full((D_MODEL, BRANCH_W)), _full((D_MODEL, BRANCH_W))],
        out_specs=[pl.BlockSpec((tm, BRANCH_W), lambda i: (i, 0))] * 2,
        out_shape=[jax.ShapeDtypeStruct((n, BRANCH_W), F32)] * 2,
        compiler_params=_params(("arbitrary",), VMEM_SMALL),
        name="mem_kv",
    )(mem, wk, wv)


def _rwkv_rows(p, first, prev, s_ref, o_ref, prm, nseq):
    (mu, w0, w2, a0, a2, g2, k_k, k_a, r_k, lnx_g, lnx_b) = prm
    rows = RW_ROWS
    ls = rows // nseq
    shifted = jnp.where(first, prev, pltpu.roll(p, 1, 0))
    xs = p + (shifted - p) * mu
    r = xs[:, 0:BRANCH_W]
    k = xs[:, BRANCH_W:2 * BRANCH_W]
    v = xs[:, 2 * BRANCH_W:3 * BRANCH_W]
    o = 3 * BRANCH_W
    wd = xs[:, o:o + RW_W_LORA]
    ad = xs[:, o + RW_W_LORA:o + RW_W_LORA + RW_A_LORA]
    gd = xs[:, o + RW_W_LORA + RW_A_LORA:RW_COLS]

    w_log = -_softplus(-(w0 + _dot(jnp.tanh(wd), w2))) - 0.5
    logw = -jnp.exp(w_log)
    a = _sigmoid(a0 + _dot(ad, a2))
    g = _dot(_sigmoid(gd), g2)
    kkr = k * k_k
    kf = k * (1.0 + (a - 1.0) * k_a)
    rkr = r * kf * r_k

    ri = lax.broadcasted_iota(jnp.int32, (rows, rows), 0)
    ci = lax.broadcasted_iota(jnp.int32, (rows, rows), 1)
    if nseq == 1:
        same = ci >= 0
    else:
        sh = int(np.log2(ls))
        same = lax.shift_right_logical(ri, sh) == lax.shift_right_logical(ci, sh)
    low_incl = same & (ci <= ri)
    low_strict = same & (ci < ri)
    m_incl = jnp.where(low_incl, 1.0, 0.0).astype(BF16)
    m_same = jnp.where(same, 1.0, 0.0).astype(BF16)
    hi, mid, lo = _split3(logw)
    cum = (jnp.dot(m_incl, hi, preferred_element_type=F32) + jnp.dot(m_incl, mid, preferred_element_type=F32)
           + jnp.dot(m_incl, lo, preferred_element_type=F32))
    tot = (jnp.dot(m_same, hi, preferred_element_type=F32) + jnp.dot(m_same, mid, preferred_element_type=F32)
           + jnp.dot(m_same, lo, preferred_element_type=F32))
    g_t = jnp.exp(cum)
    g_prev = jnp.exp(cum - logw)
    g_inv = jnp.exp(-cum)
    g_end = jnp.exp(tot - cum)
    g_tot = jnp.exp(tot)
    eye = jnp.where(ri == ci, 1.0, 0.0)
    yield

    hs = range(RW_HEADS)
    sls = [slice(h * RW_HD, (h + 1) * RW_HD) for h in hs]
    kk = [kkr[:, sl] for sl in sls]
    kk = [x * lax.rsqrt(jnp.maximum(jnp.sum(x * x, axis=-1, keepdims=True), 1e-24)) for x in kk]
    bv = [kk[h] * a[:, sls[h]] for h in hs]
    k_h = [kf[:, sl] for sl in sls]
    v_h = [v[:, sl] for sl in sls]
    at = [-kk[h] * g_prev[:, sls[h]] for h in hs]
    rt = [r[:, sl] * g_t[:, sl] for sl in sls]
    bt = [bv[h] * g_inv[:, sls[h]] for h in hs]
    kt = [k_h[h] * g_inv[:, sls[h]] for h in hs]
    bh = [bv[h] * g_end[:, sls[h]] for h in hs]
    kh = [k_h[h] * g_end[:, sls[h]] for h in hs]

    ar = [jnp.concatenate([at[h], rt[h]], axis=0) for h in hs]
    bk_t = [jnp.concatenate([bt[h], kt[h]], axis=0) for h in hs]
    pp = [_dot_nt(ar[h], bk_t[h]) for h in hs]
    ri2 = lax.broadcasted_iota(jnp.int32, (rows, 2 * rows), 0)
    ci2 = lax.broadcasted_iota(jnp.int32, (rows, 2 * rows), 1)
    k_half = ci2 >= rows
    cpos = jnp.where(k_half, ci2 - rows, ci2)
    if nseq == 1:
        same2 = cpos >= 0
    else:
        same2 = lax.shift_right_logical(ri2, sh) == lax.shift_right_logical(cpos, sh)
    l_ab = [jnp.where(low_strict, x[:rows, :rows], 0.0) for x in pp]
    l_kv = [jnp.where(same2 & k_half & (cpos < ri2), x[:rows], 0.0) for x in pp]
    a_y = [jnp.where(same2 & (cpos <= ri2), x[rows:], 0.0) for x in pp]
    yield

    tm = [eye + x for x in l_ab]
    pw = [_dot(x, x) for x in l_ab]
    n_dbl = int(np.log2(ls)) - 1
    for it in range(n_dbl):
        yield
        if it < n_dbl - 1:
            z = [_dot(jnp.concatenate([tm[h], pw[h]], axis=0), pw[h]) for h in hs]
            tm = [tm[h] + z[h][:rows] for h in hs]
            pw = [z[h][rows:] for h in hs]
        else:
            tm = [tm[h] + _dot(tm[h], pw[h]) for h in hs]

    if nseq == 1:
        ars = [_dot_nt(ar[h], s_ref[0, h]) for h in hs]
        as0 = [x[:rows] for x in ars]
        rs0 = [x[rows:] for x in ars]
    else:
        as0, rs0 = [], []
        for h in hs:
            zs = [_dot_nt(jnp.concatenate([at[h][b * ls:(b + 1) * ls], rt[h][b * ls:(b + 1) * ls]], axis=0),
                          s_ref[b, h]) for b in range(nseq)]
            as0.append(jnp.concatenate([x[:ls] for x in zs], axis=0))
            rs0.append(jnp.concatenate([x[ls:] for x in zs], axis=0))

    lv = [_dot(l_kv[h], jnp.concatenate([v_h[h], v_h[h]], axis=0)) for h in hs]
    yield
    u = [_dot(tm[h], as0[h] + lv[h]) for h in hs]
    yield
    y = [rs0[h] + _dot(a_y[h], jnp.concatenate([u[h], v_h[h]], axis=0)) for h in hs]
    yield

    for h in hs:
        for b in range(nseq):
            rb = slice(b * ls, (b + 1) * ls)
            uv = jnp.concatenate([u[h][rb], v_h[h][rb]], axis=0)
            bk = jnp.concatenate([bh[h][rb], kh[h][rb]], axis=0)
            s_ref[b, h] = s_ref[b, h] * g_tot[b * ls:b * ls + 1, sls[h]] + _dot_tn(uv, bk)

    outs = []
    for h in hs:
        ym = jnp.mean(y[h], axis=-1, keepdims=True)
        yc = y[h] - ym
        yv = jnp.mean(yc * yc, axis=-1, keepdims=True)
        yn = yc * lax.rsqrt(yv + GN_EPS) * lnx_g[:, sls[h]] + lnx_b[:, sls[h]]
        bonus = jnp.sum(rkr[:, sls[h]], axis=-1, keepdims=True) * v_h[h]
        outs.append((yn + bonus) * g[:, sls[h]])
    o_ref[...] = jnp.concatenate(outs, axis=-1).astype(o_ref.dtype)


def _interleave(gens):
    for _ in itertools.zip_longest(*gens):
        pass


def _rwkv_prompt_kernel(p_ref, *refs):
    prm_refs, (o_ref, s_ref, sh_ref, carry_ref) = refs[:11], refs[11:]
    c = pl.program_id(1)

    @pl.when(c == 0)
    def _():
        s_ref[...] = jnp.zeros_like(s_ref)
        carry_ref[...] = jnp.zeros_like(carry_ref)

    first = lax.broadcasted_iota(jnp.int32, (RW_ROWS, 1), 0) == 0
    prm = tuple(x[...] for x in prm_refs)
    groups = range(p_ref.shape[0])
    _interleave([_rwkv_rows(p_ref[g], first, carry_ref[8 * g:8 * g + 1, :], s_ref.at[pl.ds(g, 1)],
                            o_ref.at[g], prm, nseq=1) for g in groups])
    for g in groups:
        last = p_ref[g, RW_ROWS - 1:RW_ROWS, :]
        carry_ref[8 * g:8 * g + 1, :] = last
        sh_ref[g] = last


def _rwkv_sample_kernel(p_ref, prev_ref, s0_ref, *refs):
    prm_refs, (o_ref, s_ref, sh_ref) = refs[:11], refs[11:]
    nseq = s0_ref.shape[0] // p_ref.shape[0]
    ls = RW_ROWS // nseq
    s_ref[...] = s0_ref[...]
    first = (lax.broadcasted_iota(jnp.int32, (RW_ROWS, 1), 0) & (ls - 1)) == 0
    prm = tuple(x[...] for x in prm_refs)
    groups = range(p_ref.shape[0])
    _interleave([_rwkv_rows(p_ref[g], first, prev_ref[g], s_ref.at[pl.ds(g * nseq, nseq)], o_ref.at[g],
                            prm, nseq=nseq) for g in groups])
    for g in groups:
        for b in range(nseq):
            sh_ref[g * nseq + b:g * nseq + b + 1, :] = p_ref[g, (b + 1) * ls - 1:(b + 1) * ls, :]


def _rw_param_specs():
    shapes = [(1, RW_COLS), (1, BRANCH_W), (RW_W_LORA, BRANCH_W), (1, BRANCH_W), (RW_A_LORA, BRANCH_W),
              (RW_G_LORA, BRANCH_W), (1, BRANCH_W), (1, BRANCH_W), (1, BRANCH_W), (1, BRANCH_W), (1, BRANCH_W)]
    return [_full(s) for s in shapes]


def _rwkv_prompt(p_rw, prm, batch, seq):
    nc = seq // RW_ROWS
    g = RW_GROUPS
    p3 = p_rw.reshape(batch, seq, RW_COLS)
    o, s_new, sh = pl.pallas_call(
        _rwkv_prompt_kernel,
        grid=(batch // g, nc),
        in_specs=[pl.BlockSpec((g, RW_ROWS, RW_COLS), lambda b, c: (b, c, 0))] + _rw_param_specs(),
        out_specs=[pl.BlockSpec((g, RW_ROWS, BRANCH_W), lambda b, c: (b, c, 0)),
                   pl.BlockSpec((g, RW_HEADS, RW_HD, RW_HD), lambda b, c: (b, 0, 0, 0)),
                   pl.BlockSpec((g, 1, RW_COLS), lambda b, c: (b, 0, 0))],
        out_shape=[jax.ShapeDtypeStruct((batch, seq, BRANCH_W), BF16),
                   jax.ShapeDtypeStruct((batch, RW_HEADS, RW_HD, RW_HD), F32),
                   jax.ShapeDtypeStruct((batch, 1, RW_COLS), F32)],
        scratch_shapes=[pltpu.VMEM((8 * g, RW_COLS), F32)],
        compiler_params=_params(("arbitrary", "arbitrary"), VMEM_RWKV),
        name="rwkv_prompt",
    )(p3, *prm)
    return o.reshape(batch * seq, BRANCH_W), s_new, sh.reshape(batch, RW_COLS)


def _rwkv_sample(p_rw, prev_rows, s0, prm, seq):
    n = p_rw.shape[0]
    nseq = RW_ROWS // seq
    batch = n // seq
    g = RW_GROUPS
    ng = n // RW_ROWS
    o, s_new, sh = pl.pallas_call(
        _rwkv_sample_kernel,
        grid=(ng // g,),
        in_specs=[pl.BlockSpec((g, RW_ROWS, RW_COLS), lambda i: (i, 0, 0)),
                  pl.BlockSpec((g, RW_ROWS, RW_COLS), lambda i: (i, 0, 0)),
                  pl.BlockSpec((g * nseq, RW_HEADS, RW_HD, RW_HD), lambda i: (i, 0, 0, 0))] + _rw_param_specs(),
        out_specs=[pl.BlockSpec((g, RW_ROWS, BRANCH_W), lambda i: (i, 0, 0)),
                   pl.BlockSpec((g * nseq, RW_HEADS, RW_HD, RW_HD), lambda i: (i, 0, 0, 0)),
                   pl.BlockSpec((g * nseq, RW_COLS), lambda i: (i, 0))],
        out_shape=[jax.ShapeDtypeStruct((ng, RW_ROWS, BRANCH_W), BF16),
                   jax.ShapeDtypeStruct((batch, RW_HEADS, RW_HD, RW_HD), F32),
                   jax.ShapeDtypeStruct((batch, RW_COLS), F32)],
        compiler_params=_params(("arbitrary",), VMEM_RWKV),
        name="rwkv_sample",
    )(p_rw.reshape(ng, RW_ROWS, RW_COLS), prev_rows.reshape(ng, RW_ROWS, RW_COLS), s0, *prm)
    return o.reshape(n, BRANCH_W), s_new, sh


def _attend_all(qkv):
    s = [_dot_nt(q, k) * (XA_HD ** -0.5) for q, k, _ in qkv]
    e = [jnp.exp(x - jnp.max(x, axis=-1, keepdims=True)) for x in s]
    pr = [x / jnp.sum(x, axis=-1, keepdims=True) for x in e]
    return [_dot(p, v) for p, (_, _, v) in zip(pr, qkv)]


def _xattn_prompt_kernel(q_ref, k_ref, v_ref, o_ref):
    sls = [slice(h * XA_HD, (h + 1) * XA_HD) for h in range(XA_HEADS)]
    outs = _attend_all([(q_ref[:, sl], k_ref[:, sl], v_ref[:, sl]) for sl in sls])
    for sl, o in zip(sls, outs):
        o_ref[:, sl] = o.astype(o_ref.dtype)


def _xattn_prompt(q, mk, mv, batch, seq, tq=XATTN_ROWS):
    nt = seq // tq
    return pl.pallas_call(
        _xattn_prompt_kernel,
        grid=(batch, nt),
        in_specs=[pl.BlockSpec((tq, BRANCH_W), lambda b, i: (b * nt + i, 0)),
                  pl.BlockSpec((N_MEM, BRANCH_W), lambda b, i: (b, 0)),
                  pl.BlockSpec((N_MEM, BRANCH_W), lambda b, i: (b, 0))],
        out_specs=pl.BlockSpec((tq, BRANCH_W), lambda b, i: (b * nt + i, 0)),
        out_shape=jax.ShapeDtypeStruct((batch * seq, BRANCH_W), BF16),
        compiler_params=_params(("arbitrary", "arbitrary"), VMEM_SMALL),
        name="xattn_prompt",
    )(q, mk, mv)


def _xattn_sample_kernel(q_ref, k_ref, v_ref, o_ref):
    nb = k_ref.shape[0]
    ls = q_ref.shape[0] // nb
    nq = XA_HEADS * ls
    seqs = range(nb)
    rows = [slice(b * ls, (b + 1) * ls) for b in seqs]
    qs = [jnp.concatenate([q_ref[rows[b], h * XA_HD:(h + 1) * XA_HD] for h in range(XA_HEADS)], axis=0)
          for b in seqs]
    ri = lax.broadcasted_iota(jnp.int32, (N_MEM * XA_HEADS, nq), 0)
    ci = lax.broadcasted_iota(jnp.int32, (N_MEM * XA_HEADS, nq), 1)
    own = (ri & (XA_HEADS - 1)) == lax.shift_right_logical(ci, int(np.log2(ls)))
    s = [jnp.where(own, _dot_nt(k_ref[b], qs[b]) * (XA_HD ** -0.5), -jnp.inf) for b in seqs]
    e = [jnp.exp(x - jnp.max(x, axis=0, keepdims=True)) for x in s]
    pr = [x / jnp.sum(x, axis=0, keepdims=True) for x in e]
    outs = [_dot_tn(pr[b], v_ref[b]) for b in seqs]
    for b in seqs:
        o_ref[rows[b], :] = jnp.concatenate([outs[b][h * ls:(h + 1) * ls] for h in range(XA_HEADS)],
                                            axis=-1).astype(o_ref.dtype)


def _xattn_sample(q, mk, mv, seq, nb=XATTN_SEQS):
    batch = mk.shape[0]
    mem_spec = pl.BlockSpec((nb, N_MEM * XA_HEADS, XA_HD), lambda i: (i, 0, 0))
    return pl.pallas_call(
        _xattn_sample_kernel,
        grid=(batch // nb,),
        in_specs=[pl.BlockSpec((nb * seq, BRANCH_W), lambda i: (i, 0)), mem_spec, mem_spec],
        out_specs=pl.BlockSpec((nb * seq, BRANCH_W), lambda i: (i, 0)),
        out_shape=jax.ShapeDtypeStruct((batch * seq, BRANCH_W), BF16),
        compiler_params=_params(("arbitrary",), VMEM_MID),
        name="xattn_sample",
    )(q, mk, mv)


def _merge_kernel(n_active, *refs):
    i = pl.program_id(0)

    @pl.when(i < n_active)
    def _():
        _merge_rows(*refs)

    @pl.when(i >= n_active)
    def _():
        for ref in refs[-5:-1]:
            ref[...] = jnp.zeros_like(ref)


def _merge_rows(x_ref, gm_ref, rw_ref, xa_ref, gt_ref, gb_ref, wb_ref, wo_ref, n2_ref,
                wr_hi_ref, wr_lo_ref, br_ref, cnt_in_ref, *refs):
    h_ref, hn_ref, route_ref, ridx_ref, cnt_ref = refs[-5:]
    merged = None
    for n, br in enumerate((gm_ref, rw_ref, xa_ref)):
        cs = slice(n * D_MODEL, (n + 1) * D_MODEL)
        up = jnp.dot(br[...], wb_ref[n], preferred_element_type=F32)
        term = _sigmoid(gt_ref[:, cs] + gb_ref[:, cs]) * up
        merged = term if merged is None else merged + term
    h = x_ref[...] + _dot(merged, wo_ref[...])
    h_ref[...] = h
    hn = _rmsnorm(h, n2_ref[...])
    hn_ref[...] = hn

    hn_hi = hn.astype(BF16)
    hn_lo = (hn - hn_hi.astype(F32)).astype(BF16)
    logits = (jnp.dot(hn_hi, wr_hi_ref[...], preferred_element_type=F32)
              + jnp.dot(hn_lo, wr_hi_ref[...], preferred_element_type=F32)
              + jnp.dot(hn_hi, wr_lo_ref[...], preferred_element_type=F32)) + br_ref[...]
    lane = lax.broadcasted_iota(jnp.int32, logits.shape, 1)
    neg = -jnp.inf
    big = jnp.int32(1 << 20)
    gmask = (lane >= N_EXPERTS) & (lane < N_EXPERTS + N_GROUPS)
    gl = jnp.where(gmask, logits, neg)
    gmax = jnp.max(gl, axis=-1, keepdims=True)
    gsel = jnp.min(jnp.where(gl == gmax, lane, big), axis=-1, keepdims=True) - N_EXPERTS
    gsum = jnp.sum(jnp.where(gmask, jnp.exp(logits - gmax), 0.0), axis=-1, keepdims=True)
    pg_top = 1.0 / gsum
    emask = (lane >= gsel * EXP_PER_GROUP) & (lane < (gsel + 1) * EXP_PER_GROUP)
    el = jnp.where(emask, logits, neg)
    m1 = jnp.max(el, axis=-1, keepdims=True)
    i1 = jnp.min(jnp.where(el == m1, lane, big), axis=-1, keepdims=True)
    el2 = jnp.where(lane == i1, neg, el)
    m2 = jnp.max(el2, axis=-1, keepdims=True)
    i2 = jnp.min(jnp.where(el2 == m2, lane, big), axis=-1, keepdims=True)
    t2 = jnp.exp(m2 - m1)
    w1 = pg_top / (1.0 + t2)
    w2 = pg_top * t2 / (1.0 + t2)

    @pl.when(pl.program_id(0) == 0)
    def _():
        cnt_ref[...] = cnt_in_ref[...]

    onehot = jnp.where(lane == i1, 1.0, 0.0) + jnp.where(lane == i2, 1.0, 0.0)
    cnt_ref[0:1, :] = cnt_ref[0:1, :] + jnp.sum(onehot, axis=0, keepdims=True)

    route = jnp.zeros(logits.shape, F32)
    for col, val in ((ROUTE_E1, i1.astype(F32)), (ROUTE_E2, i2.astype(F32)), (ROUTE_W1, w1), (ROUTE_W2, w2)):
        route = jnp.where(lane == col, val, route)
    route_ref[...] = route
    ridx_ref[...] = jnp.transpose(route)[0:8, :].astype(jnp.int32)


def _merge(x, o_gm, o_rw, o_xa, p_gate, gate_b, w_branch, w_out, n2_g, wr_hi, wr_lo, b_r, n_total, row0, prev,
           tm=MERGE_ROWS):
    n = x.shape[0]
    t0 = row0 // tm
    n_act = n // tm
    n_steps = n_total // tm if prev is None else n_act
    row = lambda wd: pl.BlockSpec((tm, wd), lambda i: (jnp.minimum(i, n_act - 1), 0))
    out_row = lambda wd: pl.BlockSpec((tm, wd), lambda i: (t0 + i, 0))
    out_shape = [jax.ShapeDtypeStruct((n_total, D_MODEL), F32), jax.ShapeDtypeStruct((n_total, D_MODEL), F32),
                 jax.ShapeDtypeStruct((n_total, LANES), F32), jax.ShapeDtypeStruct((8, n_total), jnp.int32),
                 jax.ShapeDtypeStruct((8, LANES), F32)]
    if prev is None:
        carried, counts_in, aliases = [], jnp.zeros((8, LANES), F32), {}
    else:
        carried, counts_in = list(prev[:4]), prev[4]
        aliases = {13 + k: k for k in range(4)}
    return pl.pallas_call(
        functools.partial(_merge_kernel, n_act),
        grid=(n_steps,),
        in_specs=[row(D_MODEL), row(BRANCH_W), row(BRANCH_W), row(BRANCH_W), row(GATE_COLS),
                  _full((1, GATE_COLS)), _full((N_BRANCH, BRANCH_W, D_MODEL)), _full((D_MODEL, D_MODEL)),
                  _full((1, D_MODEL)), _full((D_MODEL, LANES)), _full((D_MODEL, LANES)), _full((1, LANES)),
                  _full((8, LANES))] + [pl.BlockSpec(memory_space=pl.ANY)] * len(carried),
        out_specs=[out_row(D_MODEL), out_row(D_MODEL), out_row(LANES),
                   pl.BlockSpec((8, tm), lambda i: (0, t0 + i)), _full((8, LANES))],
        out_shape=out_shape,
        input_output_aliases=aliases,
        compiler_params=_params(("arbitrary",), VMEM_MID),
        name="merge",
    )(x, o_gm, o_rw, o_xa, p_gate, gate_b, w_branch, w_out, n2_g, wr_hi, wr_lo, b_r, counts_in, *carried)


def _route_plan(ridx, counts, tm):
    n = ridx.shape[1]
    n2 = 2 * n
    n_tiles = n2 // tm + N_EXPERTS
    n_rows = n_tiles * tm
    key_bits = int(n_rows - 1).bit_length()
    cnt = counts[0, :N_EXPERTS].astype(jnp.int32)
    tiles = (cnt + tm - 1) // tm
    tile_end = jnp.cumsum(tiles)
    n_used = tile_end[N_EXPERTS - 1:]
    tc = jnp.minimum(jnp.arange(n_tiles, dtype=jnp.int32), n_used - 1)
    tile_expert = jnp.sum((tc[:, None] >= tile_end[None, :]).astype(jnp.int32), axis=1)
    pad_end = jnp.cumsum(tiles * tm - cnt)
    d = jnp.arange(n_rows - n2, dtype=jnp.int32)
    pad_expert = jnp.sum((d[:, None] >= pad_end[None, :]).astype(jnp.int32), axis=1)
    eid = ridx[ROUTE_E1:ROUTE_E2 + 1].reshape(n2)
    keys = jnp.concatenate([(eid << key_bits) + jnp.arange(n2, dtype=jnp.int32),
                            (pad_expert << key_bits) + n2 + d])
    row_id = jnp.sort(keys) & ((1 << key_bits) - 1)
    src = jnp.where(row_id < n2, jnp.where(row_id >= n, row_id - n, row_id), 0)
    return src.reshape(n_tiles, 1, tm), row_id.reshape(n_tiles, 1, tm), tile_expert, n_used


def _row_copy(src_ref, src_row, dst_ref, dst_row, sem):
    return pltpu.make_async_copy(src_ref.at[pl.ds(src_row, 1)], dst_ref.at[pl.ds(dst_row, 1)], sem)


def _experts_kernel(te_ref, nu_ref, src_ref, nxt_ref, dst_ref, hn_ref, wg_ref, wu_ref, wd_ref, o_ref,
                    xbuf, ybuf, wg_s, wu_s, wd_s, gsem, ssem):
    t = pl.program_id(0)
    tm = xbuf.shape[1]
    n2 = o_ref.shape[0] - N_EXPERTS * tm
    slot = t & 1
    other = 1 - slot

    def gather(idx_ref, s):
        for r in range(tm):
            _row_copy(hn_ref, idx_ref[0, 0, r], xbuf.at[s], r, gsem.at[s]).start(priority=0)

    def wait_rows(buf, sem, s):
        for _ in range(tm):
            _row_copy(hn_ref, 0, buf.at[s], 0, sem.at[s]).wait()

    @pl.when(t == 0)
    def _():
        ybuf[1] = jnp.zeros(ybuf.shape[1:], ybuf.dtype)
        fills = [pltpu.make_async_copy(ybuf.at[1], o_ref.at[pl.ds(n2 + e * tm, tm)], ssem.at[1])
                 for e in range(N_EXPERTS)]
        for f in fills:
            f.start()
        for f in fills:
            f.wait()
        gather(src_ref, 0)

    @pl.when(t < nu_ref[0])
    def _():
        wait_rows(xbuf, gsem, slot)

        @pl.when(t >= 2)
        def _():
            wait_rows(ybuf, ssem, slot)

        @pl.when((t == 0) | (te_ref[t] != te_ref[jnp.maximum(t - 1, 0)]))
        def _():
            wg_s[...] = wg_ref[...].astype(BF16)
            wu_s[...] = wu_ref[...].astype(BF16)
            wd_s[...] = wd_ref[...].astype(BF16)

        gather(nxt_ref, other)
        xb = xbuf[slot].astype(BF16)
        gate = jnp.dot(xb, wg_s[...], preferred_element_type=F32)
        up = jnp.dot(xb, wu_s[...], preferred_element_type=F32)
        ybuf[slot] = _dot(gate * _sigmoid(gate) * up, wd_s[...])
        for r in range(tm):
            _row_copy(ybuf.at[slot], r, o_ref, dst_ref[0, 0, r], ssem.at[slot]).start(priority=1)

        @pl.when(t == nu_ref[0] - 1)
        def _():
            wait_rows(xbuf, gsem, other)

            @pl.when(t >= 1)
            def _():
                wait_rows(ybuf, ssem, other)

            wait_rows(ybuf, ssem, slot)


def _experts(hn, src, dst, tile_expert, n_used, wg, wu, wd, tm):
    n = hn.shape[0]
    n_tiles = src.shape[0]
    idx = lambda f: pl.BlockSpec((1, 1, tm), lambda t, te, nu: (f(t), 0, 0), memory_space=pltpu.SMEM)
    weight = lambda shape: pl.BlockSpec((None,) + shape, lambda t, te, nu: (te[t], 0, 0))
    return pl.pallas_call(
        _experts_kernel,
        grid_spec=pltpu.PrefetchScalarGridSpec(
            num_scalar_prefetch=2,
            grid=(n_tiles,),
            in_specs=[idx(lambda t: t), idx(lambda t: jnp.minimum(t + 1, n_tiles - 1)), idx(lambda t: t),
                      pl.BlockSpec(memory_space=pl.ANY),
                      weight((D_MODEL, EXP_FF)), weight((D_MODEL, EXP_FF)), weight((EXP_FF, D_MODEL))],
            out_specs=pl.BlockSpec(memory_space=pl.ANY),
            scratch_shapes=[pltpu.VMEM((2, tm, D_MODEL), F32), pltpu.VMEM((2, tm, D_MODEL), F32),
                            pltpu.VMEM((D_MODEL, EXP_FF), BF16), pltpu.VMEM((D_MODEL, EXP_FF), BF16),
                            pltpu.VMEM((EXP_FF, D_MODEL), BF16),
                            pltpu.SemaphoreType.DMA((2,)), pltpu.SemaphoreType.DMA((2,))]),
        out_shape=jax.ShapeDtypeStruct((2 * n + N_EXPERTS * tm, D_MODEL), F32),
        compiler_params=_params(("arbitrary",), VMEM_MID),
        name="moe_experts",
    )(tile_expert, n_used, src, src, dst, hn, wg, wu, wd)


def _combine_kernel(h_ref, route_ref, fg_ref, o1_ref, o2_ref, y_ref):
    rt = route_ref[...]
    lane = lax.broadcasted_iota(jnp.int32, rt.shape, 1)
    w1 = jnp.sum(jnp.where(lane == ROUTE_W1, rt, 0.0), axis=-1, keepdims=True)
    w2 = jnp.sum(jnp.where(lane == ROUTE_W2, rt, 0.0), axis=-1, keepdims=True)
    y_ref[...] = _rmsnorm(h_ref[...] + w1 * o1_ref[...] + w2 * o2_ref[...], fg_ref[...])


def _combine(h, route, o, final_g, row0, n_out, tm=COMBINE_ROWS):
    n = h.shape[0]
    t0 = row0 // tm
    return pl.pallas_call(
        _combine_kernel,
        grid=(n_out // tm,),
        in_specs=[pl.BlockSpec((tm, D_MODEL), lambda i: (t0 + i, 0)),
                  pl.BlockSpec((tm, LANES), lambda i: (t0 + i, 0)),
                  _full((1, D_MODEL)),
                  pl.BlockSpec((tm, D_MODEL), lambda i: (t0 + i, 0)),
                  pl.BlockSpec((tm, D_MODEL), lambda i: (n // tm + t0 + i, 0))],
        out_specs=pl.BlockSpec((tm, D_MODEL), lambda i: (i, 0)),
        out_shape=jax.ShapeDtypeStruct((n_out, D_MODEL), F32),
        compiler_params=_params(("arbitrary",), VMEM_MID),
        name="moe_combine",
    )(h, route, final_g, o, o)


def _moe(merged, group_rows, wg, wu, wd, final_g, tm=MOE_ROWS):
    h, hn, route, ridx, counts = merged
    src, dst, tile_expert, n_used = _route_plan(ridx, counts, tm)
    o = _experts(hn, src, dst, tile_expert, n_used, wg, wu, wd, tm)
    return [_combine(h, route, o, final_g, row0, rows) for row0, rows in group_rows]


def _branches(x2d, mem_k, mem_v, rw_state, w, merged, n_total, row0, *, prompt, batch, seq):
    n = x2d.shape[0]
    grp = "prompt" if prompt else "sample"
    o_gm, p_rw, p_q, p_gate, *vn = _in_proj(x2d, w["norm1_g"], w["w_in_segs"], w["gm_ln_g"], w["gm_ln_b"],
                                            w["gm_mix_" + grp], w["gm_bias_" + grp], not prompt, IN_PROJ_ROWS)
    vn = vn[0] if vn else None
    if prompt:
        o_rw, s_new, sh_new = _rwkv_prompt(p_rw, w["rw_prm"], batch, seq)
        o_xa = _xattn_prompt(p_q, mem_k, mem_v, batch, seq)
    else:
        s0, shift = rw_state
        prev_rows = jnp.pad(shift[:, None, :], ((0, 0), (0, seq - 1), (0, 0))).reshape(n, RW_COLS)
        o_rw, s_new, sh_new = _rwkv_sample(p_rw, prev_rows, s0, w["rw_prm"], seq)
        o_xa = _xattn_sample(p_q, mem_k, mem_v, seq)
    merged = _merge(x2d, o_gm, o_rw, o_xa, p_gate, w["gate_b"], w["w_branch"], w["w_out"],
                    w["norm2_g"], w["wr_hi"], w["wr_lo"], w["b_r"], n_total, row0, merged)
    return merged, s_new, sh_new, vn


def kernel(x_prompt, x_sample, state_rwkv_S, state_rwkv_shift, cache_mem_k, cache_mem_v, mem_prompt, norm1_g, w_in, gate_b, gm_ln_g, gm_ln_b, gm_ws, gm_bs, rw_mu, rw_w0, rw_w2, rw_a0, rw_a2, rw_g2, rw_k_k, rw_k_a, rw_r_k, rw_lnx_g, rw_lnx_b, xa_wk, xa_wv, w_branch, w_out, norm2_g, rg_w, rg_b, re_w, re_b, e_wg, e_wu, e_wd, final_g):
    bp, seq_p, _ = x_prompt.shape
    bs, seq_s, _ = x_sample.shape
    depth = w_in.shape[0]
    assert depth == 1 and seq_p % CHUNK == 0 and RW_ROWS % seq_s == 0 and CHUNK % seq_s == 0

    l = 0
    row = lambda a: a.reshape(1, -1)
    seg = (0, 2 * BRANCH_W, 2 * BRANCH_W + RW_COLS, 3 * BRANCH_W + RW_COLS, 3 * BRANCH_W + RW_COLS + GATE_COLS)
    w_causal = jnp.tril(gm_ws[l])
    nrep = CHUNK // seq_s
    blk = w_causal[:, :seq_s, :seq_s]
    eye_rep = jnp.eye(nrep, dtype=F32)
    mix_sample = jnp.einsum("ab,gts->gatbs", eye_rep, blk).reshape(GM_GROUPS, CHUNK, CHUNK)
    bias_prompt = jnp.repeat(gm_bs[l].T, GM_HD, axis=1)
    bias_sample = jnp.tile(bias_prompt[:seq_s], (nrep, 1))

    wr = jnp.zeros((D_MODEL, LANES), F32)
    wr = wr.at[:, :N_EXPERTS].set(jnp.transpose(re_w[l], (1, 0, 2)).reshape(D_MODEL, N_EXPERTS))
    wr = wr.at[:, N_EXPERTS:N_EXPERTS + N_GROUPS].set(rg_w[l])
    wr_hi = wr.astype(BF16)
    wr_lo = (wr - wr_hi.astype(F32)).astype(BF16)
    b_r = jnp.zeros((1, LANES), F32)
    b_r = b_r.at[0, :N_EXPERTS].set(re_b[l].reshape(-1)).at[0, N_EXPERTS:N_EXPERTS + N_GROUPS].set(rg_b[l])

    w = dict(
        norm1_g=row(norm1_g[l]),
        w_in_segs=[w_in[l][:, a:b].astype(BF16) for a, b in zip(seg[:-1], seg[1:])],
        gm_ln_g=row(gm_ln_g[l]), gm_ln_b=row(gm_ln_b[l]),
        gm_mix_prompt=w_causal.astype(BF16), gm_bias_prompt=bias_prompt,
        gm_mix_sample=mix_sample.astype(BF16), gm_bias_sample=bias_sample,
        rw_prm=(row(rw_mu[l]), row(rw_w0[l]), rw_w2[l].astype(BF16), row(rw_a0[l]), rw_a2[l].astype(BF16),
                rw_g2[l].astype(BF16), row(rw_k_k[l]), row(rw_k_a[l]), row(rw_r_k[l]),
                row(rw_lnx_g[l]), row(rw_lnx_b[l])),
        gate_b=row(gate_b[l]), w_branch=w_branch[l].astype(BF16), w_out=w_out[l].astype(BF16),
        norm2_g=row(norm2_g[l]), wr_hi=wr_hi, wr_lo=wr_lo, b_r=b_r,
        e_wg=e_wg[l].reshape(N_EXPERTS, D_MODEL, EXP_FF),
        e_wu=e_wu[l].reshape(N_EXPERTS, D_MODEL, EXP_FF),
        e_wd=e_wd[l].reshape(N_EXPERTS, EXP_FF, D_MODEL),
        final_g=row(final_g),
    )

    mk_p, mv_p = _mem_kv(mem_prompt.reshape(bp * N_MEM, D_MODEL), xa_wk[l].astype(BF16), xa_wv[l].astype(BF16))
    n_p, n_s = bp * seq_p, bs * seq_s
    merged, sp, shp, _ = _branches(x_prompt.reshape(n_p, D_MODEL), mk_p, mv_p, None, w, None, n_p + n_s, 0,
                                   prompt=True, batch=bp, seq=seq_p)
    merged, ss, shs, vs = _branches(x_sample.reshape(n_s, D_MODEL),
                                    cache_mem_k[l].reshape(bs, N_MEM * XA_HEADS, XA_HD),
                                    cache_mem_v[l].reshape(bs, N_MEM * XA_HEADS, XA_HD),
                                    (state_rwkv_S[l], state_rwkv_shift[l]), w, merged, n_p + n_s, n_p,
                                    prompt=False, batch=bs, seq=seq_s)
    yp, ys = _moe(merged, [(0, n_p), (n_p, n_s)], w["e_wg"], w["e_wu"], w["e_wd"], w["final_g"])

    return (yp.reshape(bp, seq_p, D_MODEL), ys.reshape(bs, seq_s, D_MODEL),
            sp[None], shp[None],
            mk_p.reshape(1, bp, N_MEM, XA_HEADS, XA_HD), mv_p.reshape(1, bp, N_MEM, XA_HEADS, XA_HD),
            ss[None], shs[None], vs.reshape(1, bs, seq_s, BRANCH_W))
```

```python
import functools
import itertools

import numpy as np
import jax
import jax.numpy as jnp
from jax import lax
from jax.experimental import pallas as pl
from jax.experimental.pallas import tpu as pltpu

F32 = jnp.float32
BF16 = jnp.bfloat16

D_MODEL = 1024
BRANCH_W = 512
CHUNK = 128
GM_GROUPS = 8
GM_HD = BRANCH_W // GM_GROUPS
RW_HEADS = 8
RW_HD = BRANCH_W // RW_HEADS
RW_W_LORA = 64
RW_A_LORA = 64
RW_G_LORA = 128
RW_COLS = 3 * BRANCH_W + RW_W_LORA + RW_A_LORA + RW_G_LORA
XA_HEADS = 4
XA_HD = BRANCH_W // XA_HEADS
N_MEM = 256
N_BRANCH = 3
GATE_COLS = N_BRANCH * D_MODEL
N_GROUPS = 4
EXP_PER_GROUP = 8
N_EXPERTS = N_GROUPS * EXP_PER_GROUP
EXP_FF = 512
RMS_EPS = 1e-6
LN_EPS = 1e-5
GN_EPS = 64e-5

LANES = 128
MIB = 1024 * 1024
IN_PROJ_ROWS = 256
MEM_KV_ROWS = 512
RW_ROWS = 64
RW_GROUPS = 2
XATTN_ROWS = 512
XATTN_SEQS = 8
MERGE_ROWS = 256
MOE_ROWS = 256
COMBINE_ROWS = 256
VMEM_SMALL, VMEM_MID, VMEM_IN_PROJ, VMEM_RWKV = 32, 40, 48, 56
ROUTE_E1, ROUTE_E2, ROUTE_W1, ROUTE_W2 = range(4)


def _dot(a, b):
    return jnp.dot(a.astype(BF16), b.astype(BF16), preferred_element_type=F32)


def _dot_nt(a, b):
    return lax.dot_general(a.astype(BF16), b.astype(BF16), (((1,), (1,)), ((), ())),
                           preferred_element_type=F32)


def _dot_tn(a, b):
    return lax.dot_general(a.astype(BF16), b.astype(BF16), (((0,), (0,)), ((), ())),
                           preferred_element_type=F32)


def _sigmoid(x):
    return 1.0 / (1.0 + jnp.exp(-x))


def _gelu(x):
    c = np.float32(np.sqrt(2.0 / np.pi))
    return x * (0.5 * (1.0 + jnp.tanh(c * (x + 0.044715 * (x * x * x)))))


def _softplus(x):
    return jnp.maximum(x, 0.0) + jnp.log(1.0 + jnp.exp(-jnp.abs(x)))


def _rmsnorm(x, g):
    return x * lax.rsqrt(jnp.mean(x * x, axis=-1, keepdims=True) + RMS_EPS) * g


def _split3(x):
    hi = x.astype(BF16)
    r1 = x - hi.astype(F32)
    mid = r1.astype(BF16)
    lo = (r1 - mid.astype(F32)).astype(BF16)
    return hi, mid, lo


def _params(sem, vmem_mib):
    return pltpu.CompilerParams(dimension_semantics=sem, vmem_limit_bytes=vmem_mib * MIB)


def _full(shape):
    nd = len(shape)
    return pl.BlockSpec(shape, lambda *_: (0,) * nd)


def _gmlp_chunk(pu, pv, ln_g, ln_b, wmix_ref, bmix_ref):
    u = _gelu(pu)
    vf = _gelu(pv)
    mu = jnp.mean(vf, axis=-1, keepdims=True)
    vc = vf - mu
    var = jnp.mean(vc * vc, axis=-1, keepdims=True)
    vn = vc * lax.rsqrt(var + LN_EPS) * ln_g + ln_b
    lane = lax.broadcasted_iota(jnp.int32, (CHUNK, LANES), 1)
    lo_half = lane < GM_HD
    outs = []
    for p in range(GM_GROUPS // 2):
        vp = vn[:, p * LANES:(p + 1) * LANES]
        s = (_dot(wmix_ref[2 * p], jnp.where(lo_half, vp, 0.0))
             + _dot(wmix_ref[2 * p + 1], jnp.where(lo_half, 0.0, vp)))
        outs.append(u[:, p * LANES:(p + 1) * LANES] * (s + bmix_ref[:, p * LANES:(p + 1) * LANES]))
    return jnp.concatenate(outs, axis=-1), vn


def _in_proj_kernel(x_ref, g_ref, wgm_ref, wrw_ref, wq_ref, wgt_ref, lng_ref, lnb_ref, wmix_ref, bmix_ref,
                    ogm_ref, rw_ref, q_ref, gt_ref, *vn_refs):
    xb = _rmsnorm(x_ref[...], g_ref[...]).astype(BF16)
    gm = jnp.dot(xb, wgm_ref[...], preferred_element_type=F32)
    rw_ref[...] = jnp.dot(xb, wrw_ref[...], preferred_element_type=F32)
    q_ref[...] = jnp.dot(xb, wq_ref[...], preferred_element_type=F32)
    gt_ref[...] = jnp.dot(xb, wgt_ref[...], preferred_element_type=F32)
    for c in range(x_ref.shape[0] // CHUNK):
        rows = slice(c * CHUNK, (c + 1) * CHUNK)
        o, vn = _gmlp_chunk(gm[rows, 0:BRANCH_W], gm[rows, BRANCH_W:2 * BRANCH_W], lng_ref[...], lnb_ref[...],
                            wmix_ref, bmix_ref)
        ogm_ref[rows, :] = o.astype(ogm_ref.dtype)
        if vn_refs:
            vn_refs[0][rows, :] = vn


def _in_proj(x, g, w_segs, ln_g, ln_b, wmix, bmix, emit_vn, tm):
    n = x.shape[0]
    widths = [w.shape[1] for w in w_segs]
    out_w = [BRANCH_W] + widths[1:] + ([BRANCH_W] if emit_vn else [])
    out_dt = [BF16] + [F32] * (len(out_w) - 1)
    return pl.pallas_call(
        _in_proj_kernel,
        grid=(n // tm,),
        in_specs=[pl.BlockSpec((tm, D_MODEL), lambda i: (i, 0)), _full((1, D_MODEL))]
        + [pl.BlockSpec((D_MODEL, wd), lambda i: (0, 0), pipeline_mode=pl.Buffered(1)) for wd in widths]
        + [_full((1, BRANCH_W)), _full((1, BRANCH_W)), _full((GM_GROUPS, CHUNK, CHUNK)),
           _full((CHUNK, BRANCH_W))],
        out_specs=[pl.BlockSpec((tm, wd), lambda i: (i, 0)) for wd in out_w],
        out_shape=[jax.ShapeDtypeStruct((n, wd), dt) for wd, dt in zip(out_w, out_dt)],
        compiler_params=_params(("arbitrary",), VMEM_IN_PROJ),
        name="in_proj",
    )(x, g, *w_segs, ln_g, ln_b, wmix, bmix)


def _mem_kv_kernel(m_ref, wk_ref, wv_ref, k_ref, v_ref):
    mb = m_ref[...].astype(BF16)
    k_ref[...] = jnp.dot(mb, wk_ref[...], preferred_element_type=F32)
    v_ref[...] = jnp.dot(mb, wv_ref[...], preferred_element_type=F32)


def _mem_kv(mem, wk, wv, tm=MEM_KV_ROWS):
    n = mem.shape[0]
    return pl.pallas_call(
        _mem_kv_kernel,
        grid=(n // tm,),
        in_specs=[pl.BlockSpec((tm, D_MODEL), lambda i: (i, 0)),
                  _full((D_MODEL, BRANCH_W)), _full((D_MODEL, BRANCH_W))],
        out_specs=[pl.BlockSpec((tm, BRANCH_W), lambda i: (i, 0))] * 2,
        out_shape=[jax.ShapeDtypeStruct((n, BRANCH_W), F32)] * 2,
        compiler_params=_params(("arbitrary",), VMEM_SMALL),
        name="mem_kv",
    )(mem, wk, wv)


def _rwkv_rows(p, first, prev, s_ref, o_ref, prm, nseq):
    (mu, w0, w2, a0, a2, g2, k_k, k_a, r_k, lnx_g, lnx_b) = prm
    rows = RW_ROWS
    ls = rows // nseq
    shifted = jnp.where(first, prev, pltpu.roll(p, 1, 0))
    xs = p + (shifted - p) * mu
    r = xs[:, 0:BRANCH_W]
    k = xs[:, BRANCH_W:2 * BRANCH_W]
    v = xs[:, 2 * BRANCH_W:3 * BRANCH_W]
    o = 3 * BRANCH_W
    wd = xs[:, o:o + RW_W_LORA]
    ad = xs[:, o + RW_W_LORA:o + RW_W_LORA + RW_A_LORA]
    gd = xs[:, o + RW_W_LORA + RW_A_LORA:RW_COLS]

    w_log = -_softplus(-(w0 + _dot(jnp.tanh(wd), w2))) - 0.5
    logw = -jnp.exp(w_log)
    a = _sigmoid(a0 + _dot(ad, a2))
    g = _dot(_sigmoid(gd), g2)
    kkr = k * k_k
    kf = k * (1.0 + (a - 1.0) * k_a)
    rkr = r * kf * r_k

    ri = lax.broadcasted_iota(jnp.int32, (rows, rows), 0)
    ci = lax.broadcasted_iota(jnp.int32, (rows, rows), 1)
    if nseq == 1:
        same = ci >= 0
    else:
        sh = int(np.log2(ls))
        same = lax.shift_right_logical(ri, sh) == lax.shift_right_logical(ci, sh)
    low_incl = same & (ci <= ri)
    low_strict = same & (ci < ri)
    m_incl = jnp.where(low_incl, 1.0, 0.0).astype(BF16)
    m_same = jnp.where(same, 1.0, 0.0).astype(BF16)
    hi, mid, lo = _split3(logw)
    cum = (jnp.dot(m_incl, hi, preferred_element_type=F32) + jnp.dot(m_incl, mid, preferred_element_type=F32)
           + jnp.dot(m_incl, lo, preferred_element_type=F32))
    tot = (jnp.dot(m_same, hi, preferred_element_type=F32) + jnp.dot(m_same, mid, preferred_element_type=F32)
           + jnp.dot(m_same, lo, preferred_element_type=F32))
    g_t = jnp.exp(cum)
    g_prev = jnp.exp(cum - logw)
    g_inv = jnp.exp(-cum)
    g_end = jnp.exp(tot - cum)
    g_tot = jnp.exp(tot)
    eye = jnp.where(ri == ci, 1.0, 0.0)
    yield

    hs = range(RW_HEADS)
    sls = [slice(h * RW_HD, (h + 1) * RW_HD) for h in hs]
    kk = [kkr[:, sl] for sl in sls]
    kk = [x * lax.rsqrt(jnp.maximum(jnp.sum(x * x, axis=-1, keepdims=True), 1e-24)) for x in kk]
    bv = [kk[h] * a[:, sls[h]] for h in hs]
    k_h = [kf[:, sl] for sl in sls]
    v_h = [v[:, sl] for sl in sls]
    at = [-kk[h] * g_prev[:, sls[h]] for h in hs]
    rt = [r[:, sl] * g_t[:, sl] for sl in sls]
    bt = [bv[h] * g_inv[:, sls[h]] for h in hs]
    kt = [k_h[h] * g_inv[:, sls[h]] for h in hs]
    bh = [bv[h] * g_end[:, sls[h]] for h in hs]
    kh = [k_h[h] * g_end[:, sls[h]] for h in hs]

    ar = [jnp.concatenate([at[h], rt[h]], axis=0) for h in hs]
    bk_t = [jnp.concatenate([bt[h], kt[h]], axis=0) for h in hs]
    pp = [_dot_nt(ar[h], bk_t[h]) for h in hs]
    ri2 = lax.broadcasted_iota(jnp.int32, (rows, 2 * rows), 0)
    ci2 = lax.broadcasted_iota(jnp.int32, (rows, 2 * rows), 1)
    k_half = ci2 >= rows
    cpos = jnp.where(k_half, ci2 - rows, ci2)
    if nseq == 1:
        same2 = cpos >= 0
    else:
        same2 = lax.shift_right_logical(ri2, sh) == lax.shift_right_logical(cpos, sh)
    l_ab = [jnp.where(low_strict, x[:rows, :rows], 0.0) for x in pp]
    l_kv = [jnp.where(same2 & k_half & (cpos < ri2), x[:rows], 0.0) for x in pp]
    a_y = [jnp.where(same2 & (cpos <= ri2), x[rows:], 0.0) for x in pp]
    yield

    tm = [eye + x for x in l_ab]
    pw = [_dot(x, x) for x in l_ab]
    n_dbl = int(np.log2(ls)) - 1
    for it in range(n_dbl):
        yield
        if it < n_dbl - 1:
            z = [_dot(jnp.concatenate([tm[h], pw[h]], axis=0), pw[h]) for h in hs]
            tm = [tm[h] + z[h][:rows] for h in hs]
            pw = [z[h][rows:] for h in hs]
        else:
            tm = [tm[h] + _dot(tm[h], pw[h]) for h in hs]

    if nseq == 1:
        ars = [_dot_nt(ar[h], s_ref[0, h]) for h in hs]
        as0 = [x[:rows] for x in ars]
        rs0 = [x[rows:] for x in ars]
    else:
        as0, rs0 = [], []
        for h in hs:
            zs = [_dot_nt(jnp.concatenate([at[h][b * ls:(b + 1) * ls], rt[h][b * ls:(b + 1) * ls]], axis=0),
                          s_ref[b, h]) for b in range(nseq)]
            as0.append(jnp.concatenate([x[:ls] for x in zs], axis=0))
            rs0.append(jnp.concatenate([x[ls:] for x in zs], axis=0))

    lv = [_dot(l_kv[h], jnp.concatenate([v_h[h], v_h[h]], axis=0)) for h in hs]
    yield
    u = [_dot(tm[h], as0[h] + lv[h]) for h in hs]
    yield
    y = [rs0[h] + _dot(a_y[h], jnp.concatenate([u[h], v_h[h]], axis=0)) for h in hs]
    yield

    for h in hs:
        for b in range(nseq):
            rb = slice(b * ls, (b + 1) * ls)
            uv = jnp.concatenate([u[h][rb], v_h[h][rb]], axis=0)
            bk = jnp.concatenate([bh[h][rb], kh[h][rb]], axis=0)
            s_ref[b, h] = s_ref[b, h] * g_tot[b * ls:b * ls + 1, sls[h]] + _dot_tn(uv, bk)

    outs = []
    for h in hs:
        ym = jnp.mean(y[h], axis=-1, keepdims=True)
        yc = y[h] - ym
        yv = jnp.mean(yc * yc, axis=-1, keepdims=True)
        yn = yc * lax.rsqrt(yv + GN_EPS) * lnx_g[:, sls[h]] + lnx_b[:, sls[h]]
        bonus = jnp.sum(rkr[:, sls[h]], axis=-1, keepdims=True) * v_h[h]
        outs.append((yn + bonus) * g[:, sls[h]])
    o_ref[...] = jnp.concatenate(outs, axis=-1).astype(o_ref.dtype)


def _interleave(gens):
    for _ in itertools.zip_longest(*gens):
        pass


def _rwkv_prompt_kernel(p_ref, *refs):
    prm_refs, (o_ref, s_ref, sh_ref, carry_ref) = refs[:11], refs[11:]
    c = pl.program_id(1)

    @pl.when(c == 0)
    def _():
        s_ref[...] = jnp.zeros_like(s_ref)
        carry_ref[...] = jnp.zeros_like(carry_ref)

    first = lax.broadcasted_iota(jnp.int32, (RW_ROWS, 1), 0) == 0
    prm = tuple(x[...] for x in prm_refs)
    groups = range(p_ref.shape[0])
    _interleave([_rwkv_rows(p_ref[g], first, carry_ref[8 * g:8 * g + 1, :], s_ref.at[pl.ds(g, 1)],
                            o_ref.at[g], prm, nseq=1) for g in groups])
    for g in groups:
        last = p_ref[g, RW_ROWS - 1:RW_ROWS, :]
        carry_ref[8 * g:8 * g + 1, :] = last
        sh_ref[g] = last


def _rwkv_sample_kernel(p_ref, prev_ref, s0_ref, *refs):
    prm_refs, (o_ref, s_ref, sh_ref) = refs[:11], refs[11:]
    nseq = s0_ref.shape[0] // p_ref.shape[0]
    ls = RW_ROWS // nseq
    s_ref[...] = s0_ref[...]
    first = (lax.broadcasted_iota(jnp.int32, (RW_ROWS, 1), 0) & (ls - 1)) == 0
    prm = tuple(x[...] for x in prm_refs)
    groups = range(p_ref.shape[0])
    _interleave([_rwkv_rows(p_ref[g], first, prev_ref[g], s_ref.at[pl.ds(g * nseq, nseq)], o_ref.at[g],
                            prm, nseq=nseq) for g in groups])
    for g in groups:
        for b in range(nseq):
            sh_ref[g * nseq + b:g * nseq + b + 1, :] = p_ref[g, (b + 1) * ls - 1:(b + 1) * ls, :]


def _rw_param_specs():
    shapes = [(1, RW_COLS), (1, BRANCH_W), (RW_W_LORA, BRANCH_W), (1, BRANCH_W), (RW_A_LORA, BRANCH_W),
              (RW_G_LORA, BRANCH_W), (1, BRANCH_W), (1, BRANCH_W), (1, BRANCH_W), (1, BRANCH_W), (1, BRANCH_W)]
    return [_full(s) for s in shapes]


def _rwkv_prompt(p_rw, prm, batch, seq):
    nc = seq // RW_ROWS
    g = RW_GROUPS
    p3 = p_rw.reshape(batch, seq, RW_COLS)
    o, s_new, sh = pl.pallas_call(
        _rwkv_prompt_kernel,
        grid=(batch // g, nc),
        in_specs=[pl.BlockSpec((g, RW_ROWS, RW_COLS), lambda b, c: (b, c, 0))] + _rw_param_specs(),
        out_specs=[pl.BlockSpec((g, RW_ROWS, BRANCH_W), lambda b, c: (b, c, 0)),
                   pl.BlockSpec((g, RW_HEADS, RW_HD, RW_HD), lambda b, c: (b, 0, 0, 0)),
                   pl.BlockSpec((g, 1, RW_COLS), lambda b, c: (b, 0, 0))],
        out_shape=[jax.ShapeDtypeStruct((batch, seq, BRANCH_W), BF16),
                   jax.ShapeDtypeStruct((batch, RW_HEADS, RW_HD, RW_HD), F32),
                   jax.ShapeDtypeStruct((batch, 1, RW_COLS), F32)],
        scratch_shapes=[pltpu.VMEM((8 * g, RW_COLS), F32)],
        compiler_params=_params(("arbitrary", "arbitrary"), VMEM_RWKV),
        name="rwkv_prompt",
    )(p3, *prm)
    return o.reshape(batch * seq, BRANCH_W), s_new, sh.reshape(batch, RW_COLS)


def _rwkv_sample(p_rw, prev_rows, s0, prm, seq):
    n = p_rw.shape[0]
    nseq = RW_ROWS // seq
    batch = n // seq
    g = RW_GROUPS
    ng = n // RW_ROWS
    o, s_new, sh = pl.pallas_call(
        _rwkv_sample_kernel,
        grid=(ng // g,),
        in_specs=[pl.BlockSpec((g, RW_ROWS, RW_COLS), lambda i: (i, 0, 0)),
                  pl.BlockSpec((g, RW_ROWS, RW_COLS), lambda i: (i, 0, 0)),
                  pl.BlockSpec((g * nseq, RW_HEADS, RW_HD, RW_HD), lambda i: (i, 0, 0, 0))] + _rw_param_specs(),
        out_specs=[pl.BlockSpec((g, RW_ROWS, BRANCH_W), lambda i: (i, 0, 0)),
                   pl.BlockSpec((g * nseq, RW_HEADS, RW_HD, RW_HD), lambda i: (i, 0, 0, 0)),
                   pl.BlockSpec((g * nseq, RW_COLS), lambda i: (i, 0))],
        out_shape=[jax.ShapeDtypeStruct((ng, RW_ROWS, BRANCH_W), BF16),
                   jax.ShapeDtypeStruct((batch, RW_HEADS, RW_HD, RW_HD), F32),
                   jax.ShapeDtypeStruct((batch, RW_COLS), F32)],
        compiler_params=_params(("arbitrary",), VMEM_RWKV),
        name="rwkv_sample",
    )(p_rw.reshape(ng, RW_ROWS, RW_COLS), prev_rows.reshape(ng, RW_ROWS, RW_COLS), s0, *prm)
    return o.reshape(n, BRANCH_W), s_new, sh


def _attend_all(qkv):
    s = [_dot_nt(q, k) * (XA_HD ** -0.5) for q, k, _ in qkv]
    e = [jnp.exp(x - jnp.max(x, axis=-1, keepdims=True)) for x in s]
    pr = [x / jnp.sum(x, axis=-1, keepdims=True) for x in e]
    return [_dot(p, v) for p, (_, _, v) in zip(pr, qkv)]


def _xattn_prompt_kernel(q_ref, k_ref, v_ref, o_ref):
    sls = [slice(h * XA_HD, (h + 1) * XA_HD) for h in range(XA_HEADS)]
    outs = _attend_all([(q_ref[:, sl], k_ref[:, sl], v_ref[:, sl]) for sl in sls])
    for sl, o in zip(sls, outs):
        o_ref[:, sl] = o.astype(o_ref.dtype)


def _xattn_prompt(q, mk, mv, batch, seq, tq=XATTN_ROWS):
    nt = seq // tq
    return pl.pallas_call(
        _xattn_prompt_kernel,
        grid=(batch, nt),
        in_specs=[pl.BlockSpec((tq, BRANCH_W), lambda b, i: (b * nt + i, 0)),
                  pl.BlockSpec((N_MEM, BRANCH_W), lambda b, i: (b, 0)),
                  pl.BlockSpec((N_MEM, BRANCH_W), lambda b, i: (b, 0))],
        out_specs=pl.BlockSpec((tq, BRANCH_W), lambda b, i: (b * nt + i, 0)),
        out_shape=jax.ShapeDtypeStruct((batch * seq, BRANCH_W), BF16),
        compiler_params=_params(("arbitrary", "arbitrary"), VMEM_SMALL),
        name="xattn_prompt",
    )(q, mk, mv)


def _xattn_sample_kernel(q_ref, k_ref, v_ref, o_ref):
    nb = k_ref.shape[0]
    ls = q_ref.shape[0] // nb
    nq = XA_HEADS * ls
    seqs = range(nb)
    rows = [slice(b * ls, (b + 1) * ls) for b in seqs]
    qs = [jnp.concatenate([q_ref[rows[b], h * XA_HD:(h + 1) * XA_HD] for h in range(XA_HEADS)], axis=0)
          for b in seqs]
    ri = lax.broadcasted_iota(jnp.int32, (N_MEM * XA_HEADS, nq), 0)
    ci = lax.broadcasted_iota(jnp.int32, (N_MEM * XA_HEADS, nq), 1)
    own = (ri & (XA_HEADS - 1)) == lax.shift_right_logical(ci, int(np.log2(ls)))
    s = [jnp.where(own, _dot_nt(k_ref[b], qs[b]) * (XA_HD ** -0.5), -jnp.inf) for b in seqs]
    e = [jnp.exp(x - jnp.max(x, axis=0, keepdims=True)) for x in s]
    pr = [x / jnp.sum(x, axis=0, keepdims=True) for x in e]
    outs = [_dot_tn(pr[b], v_ref[b]) for b in seqs]
    for b in seqs:
        o_ref[rows[b], :] = jnp.concatenate([outs[b][h * ls:(h + 1) * ls] for h in range(XA_HEADS)],
                                            axis=-1).astype(o_ref.dtype)


def _xattn_sample(q, mk, mv, seq, nb=XATTN_SEQS):
    batch = mk.shape[0]
    mem_spec = pl.BlockSpec((nb, N_MEM * XA_HEADS, XA_HD), lambda i: (i, 0, 0))
    return pl.pallas_call(
        _xattn_sample_kernel,
        grid=(batch // nb,),
        in_specs=[pl.BlockSpec((nb * seq, BRANCH_W), lambda i: (i, 0)), mem_spec, mem_spec],
        out_specs=pl.BlockSpec((nb * seq, BRANCH_W), lambda i: (i, 0)),
        out_shape=jax.ShapeDtypeStruct((batch * seq, BRANCH_W), BF16),
        compiler_params=_params(("arbitrary",), VMEM_MID),
        name="xattn_sample",
    )(q, mk, mv)


def _merge_kernel(n_active, *refs):
    i = pl.program_id(0)

    @pl.when(i < n_active)
    def _():
        _merge_rows(*refs)

    @pl.when(i >= n_active)
    def _():
        for ref in refs[-5:-1]:
            ref[...] = jnp.zeros_like(ref)


def _merge_rows(x_ref, gm_ref, rw_ref, xa_ref, gt_ref, gb_ref, wb_ref, wo_ref, n2_ref,
                wr_hi_ref, wr_lo_ref, br_ref, cnt_in_ref, *refs):
    h_ref, hn_ref, route_ref, ridx_ref, cnt_ref = refs[-5:]
    merged = None
    for n, br in enumerate((gm_ref, rw_ref, xa_ref)):
        cs = slice(n * D_MODEL, (n + 1) * D_MODEL)
        up = jnp.dot(br[...], wb_ref[n], preferred_element_type=F32)
        term = _sigmoid(gt_ref[:, cs] + gb_ref[:, cs]) * up
        merged = term if merged is None else merged + term
    h = x_ref[...] + _dot(merged, wo_ref[...])
    h_ref[...] = h
    hn = _rmsnorm(h, n2_ref[...])
    hn_ref[...] = hn

    hn_hi = hn.astype(BF16)
    hn_lo = (hn - hn_hi.astype(F32)).astype(BF16)
    logits = (jnp.dot(hn_hi, wr_hi_ref[...], preferred_element_type=F32)
              + jnp.dot(hn_lo, wr_hi_ref[...], preferred_element_type=F32)
              + jnp.dot(hn_hi, wr_lo_ref[...], preferred_element_type=F32)) + br_ref[...]
    lane = lax.broadcasted_iota(jnp.int32, logits.shape, 1)
    neg = -jnp.inf
    big = jnp.int32(1 << 20)
    gmask = (lane >= N_EXPERTS) & (lane < N_EXPERTS + N_GROUPS)
    gl = jnp.where(gmask, logits, neg)
    gmax = jnp.max(gl, axis=-1, keepdims=True)
    gsel = jnp.min(jnp.where(gl == gmax, lane, big), axis=-1, keepdims=True) - N_EXPERTS
    gsum = jnp.sum(jnp.where(gmask, jnp.exp(logits - gmax), 0.0), axis=-1, keepdims=True)
    pg_top = 1.0 / gsum
    emask = (lane >= gsel * EXP_PER_GROUP) & (lane < (gsel + 1) * EXP_PER_GROUP)
    el = jnp.where(emask, logits, neg)
    m1 = jnp.max(el, axis=-1, keepdims=True)
    i1 = jnp.min(jnp.where(el == m1, lane, big), axis=-1, keepdims=True)
    el2 = jnp.where(lane == i1, neg, el)
    m2 = jnp.max(el2, axis=-1, keepdims=True)
    i2 = jnp.min(jnp.where(el2 == m2, lane, big), axis=-1, keepdims=True)
    t2 = jnp.exp(m2 - m1)
    w1 = pg_top / (1.0 + t2)
    w2 = pg_top * t2 / (1.0 + t2)

    @pl.when(pl.program_id(0) == 0)
    def _():
        cnt_ref[...] = cnt_in_ref[...]

    onehot = jnp.where(lane == i1, 1.0, 0.0) + jnp.where(lane == i2, 1.0, 0.0)
    cnt_ref[0:1, :] = cnt_ref[0:1, :] + jnp.sum(onehot, axis=0, keepdims=True)

    route = jnp.zeros(logits.shape, F32)
    for col, val in ((ROUTE_E1, i1.astype(F32)), (ROUTE_E2, i2.astype(F32)), (ROUTE_W1, w1), (ROUTE_W2, w2)):
        route = jnp.where(lane == col, val, route)
    route_ref[...] = route
    ridx_ref[...] = jnp.transpose(route)[0:8, :].astype(jnp.int32)


def _merge(x, o_gm, o_rw, o_xa, p_gate, gate_b, w_branch, w_out, n2_g, wr_hi, wr_lo, b_r, n_total, row0, prev,
           tm=MERGE_ROWS):
    n = x.shape[0]
    t0 = row0 // tm
    n_act = n // tm
    n_steps = n_total // tm if prev is None else n_act
    row = lambda wd: pl.BlockSpec((tm, wd), lambda i: (jnp.minimum(i, n_act - 1), 0))
    out_row = lambda wd: pl.BlockSpec((tm, wd), lambda i: (t0 + i, 0))
    out_shape = [jax.ShapeDtypeStruct((n_total, D_MODEL), F32), jax.ShapeDtypeStruct((n_total, D_MODEL), F32),
                 jax.ShapeDtypeStruct((n_total, LANES), F32), jax.ShapeDtypeStruct((8, n_total), jnp.int32),
                 jax.ShapeDtypeStruct((8, LANES), F32)]
    if prev is None:
        carried, counts_in, aliases = [], jnp.zeros((8, LANES), F32), {}
    else:
        carried, counts_in = list(prev[:4]), prev[4]
        aliases = {13 + k: k for k in range(4)}
    return pl.pallas_call(
        functools.partial(_merge_kernel, n_act),
        grid=(n_steps,),
        in_specs=[row(D_MODEL), row(BRANCH_W), row(BRANCH_W), row(BRANCH_W), row(GATE_COLS),
                  _full((1, GATE_COLS)), _full((N_BRANCH, BRANCH_W, D_MODEL)), _full((D_MODEL, D_MODEL)),
                  _full((1, D_MODEL)), _full((D_MODEL, LANES)), _full((D_MODEL, LANES)), _full((1, LANES)),
                  _full((8, LANES))] + [pl.BlockSpec(memory_space=pl.ANY)] * len(carried),
        out_specs=[out_row(D_MODEL), out_row(D_MODEL), out_row(LANES),
                   pl.BlockSpec((8, tm), lambda i: (0, t0 + i)), _full((8, LANES))],
        out_shape=out_shape,
        input_output_aliases=aliases,
        compiler_params=_params(("arbitrary",), VMEM_MID),
        name="merge",
    )(x, o_gm, o_rw, o_xa, p_gate, gate_b, w_branch, w_out, n2_g, wr_hi, wr_lo, b_r, counts_in, *carried)


def _route_plan(ridx, counts, tm, tm_out):
    n = ridx.shape[1]
    n2 = 2 * n
    n_tiles = n2 // tm + N_EXPERTS
    n_rows = n_tiles * tm
    key_bits = int(n_rows - 1).bit_length()
    cnt = counts[0, :N_EXPERTS].astype(jnp.int32)
    tiles = (cnt + tm - 1) // tm
    tile_end = jnp.cumsum(tiles)
    n_used = tile_end[N_EXPERTS - 1:]
    tc = jnp.minimum(jnp.arange(n_tiles, dtype=jnp.int32), n_used - 1)
    tile_expert = jnp.sum((tc[:, None] >= tile_end[None, :]).astype(jnp.int32), axis=1)
    pad_end = jnp.cumsum(tiles * tm - cnt)
    d = jnp.arange(n_rows - n2, dtype=jnp.int32)
    pad_expert = jnp.sum((d[:, None] >= pad_end[None, :]).astype(jnp.int32), axis=1)
    eid = ridx[ROUTE_E1:ROUTE_E2 + 1].reshape(n2)
    keys = jnp.concatenate([(eid << key_bits) + jnp.arange(n2, dtype=jnp.int32),
                            (pad_expert << key_bits) + n2 + d])
    row_id = jnp.sort(keys) & ((1 << key_bits) - 1)
    src = jnp.where(row_id < n2, jnp.where(row_id >= n, row_id - n, row_id), 0)
    _, row_of = lax.sort((row_id, jnp.arange(n_rows, dtype=jnp.int32)), num_keys=1)
    nb = n // tm_out
    pos = jnp.concatenate([row_of[:n].reshape(nb, 1, tm_out), row_of[n:n2].reshape(nb, 1, tm_out)], axis=2)
    return src.reshape(n_tiles, 1, tm), pos, tile_expert, n_used


def _row_copy(src_ref, src_row, dst_ref, dst_row, sem):
    return pltpu.make_async_copy(src_ref.at[pl.ds(src_row, 1)], dst_ref.at[pl.ds(dst_row, 1)], sem)


def _experts_kernel(te_ref, nu_ref, src_ref, nxt_ref, hn_ref, wg_ref, wu_ref, wd_ref, y_ref,
                    xbuf, wg_s, wu_s, wd_s, gsem):
    t = pl.program_id(0)
    tm = xbuf.shape[1]
    slot = t & 1
    other = 1 - slot

    def gather(idx_ref, s):
        for r in range(tm):
            _row_copy(hn_ref, idx_ref[0, 0, r], xbuf.at[s], r, gsem.at[s]).start()

    def wait_gather(s):
        for _ in range(tm):
            _row_copy(hn_ref, 0, xbuf.at[s], 0, gsem.at[s]).wait()

    @pl.when(t == 0)
    def _():
        gather(src_ref, 0)

    @pl.when(t < nu_ref[0])
    def _():
        wait_gather(slot)

        @pl.when((t == 0) | (te_ref[t] != te_ref[jnp.maximum(t - 1, 0)]))
        def _():
            wg_s[...] = wg_ref[...].astype(BF16)
            wu_s[...] = wu_ref[...].astype(BF16)
            wd_s[...] = wd_ref[...].astype(BF16)

        gather(nxt_ref, other)
        xb = xbuf[slot].astype(BF16)
        gate = jnp.dot(xb, wg_s[...], preferred_element_type=F32)
        up = jnp.dot(xb, wu_s[...], preferred_element_type=F32)
        y_ref[...] = _dot(gate * _sigmoid(gate) * up, wd_s[...])

        @pl.when(t == nu_ref[0] - 1)
        def _():
            wait_gather(other)

    @pl.when(t >= nu_ref[0])
    def _():
        y_ref[...] = jnp.zeros_like(y_ref)


def _experts(hn, src, tile_expert, n_used, wg, wu, wd, tm):
    n_tiles = src.shape[0]
    idx = lambda f: pl.BlockSpec((1, 1, tm), lambda t, te, nu: (f(t), 0, 0), memory_space=pltpu.SMEM)
    weight = lambda shape: pl.BlockSpec((None,) + shape, lambda t, te, nu: (te[t], 0, 0))
    return pl.pallas_call(
        _experts_kernel,
        grid_spec=pltpu.PrefetchScalarGridSpec(
            num_scalar_prefetch=2,
            grid=(n_tiles,),
            in_specs=[idx(lambda t: t), idx(lambda t: jnp.minimum(t + 1, n_tiles - 1)),
                      pl.BlockSpec(memory_space=pl.ANY),
                      weight((D_MODEL, EXP_FF)), weight((D_MODEL, EXP_FF)), weight((EXP_FF, D_MODEL))],
            out_specs=pl.BlockSpec((tm, D_MODEL), lambda t, te, nu: (t, 0)),
            scratch_shapes=[pltpu.VMEM((2, tm, D_MODEL), F32),
                            pltpu.VMEM((D_MODEL, EXP_FF), BF16), pltpu.VMEM((D_MODEL, EXP_FF), BF16),
                            pltpu.VMEM((EXP_FF, D_MODEL), BF16), pltpu.SemaphoreType.DMA((2,))]),
        out_shape=jax.ShapeDtypeStruct((n_tiles * tm, D_MODEL), F32),
        compiler_params=_params(("arbitrary",), VMEM_MID),
        name="moe_experts",
    )(tile_expert, n_used, src, src, hn, wg, wu, wd)


def _combine_kernel(pos_ref, h_ref, route_ref, fg_ref, ys_ref, y_ref, o1_ref, o2_ref, sem):
    tm = h_ref.shape[0]

    def issue(i, c):
        _row_copy(ys_ref, pos_ref[0, 0, i], o1_ref, i, sem).start()
        _row_copy(ys_ref, pos_ref[0, 0, tm + i], o2_ref, i, sem).start()
        return c

    def drain(i, c):
        _row_copy(ys_ref, 0, o1_ref, 0, sem).wait()
        return c

    lax.fori_loop(0, tm, issue, 0, unroll=8)
    lax.fori_loop(0, 2 * tm, drain, 0, unroll=8)
    rt = route_ref[...]
    lane = lax.broadcasted_iota(jnp.int32, rt.shape, 1)
    w1 = jnp.sum(jnp.where(lane == ROUTE_W1, rt, 0.0), axis=-1, keepdims=True)
    w2 = jnp.sum(jnp.where(lane == ROUTE_W2, rt, 0.0), axis=-1, keepdims=True)
    y_ref[...] = _rmsnorm(h_ref[...] + w1 * o1_ref[...] + w2 * o2_ref[...], fg_ref[...])


def _combine(h, route, pos, ys, final_g, row0, n_out, tm):
    t0 = row0 // tm
    return pl.pallas_call(
        _combine_kernel,
        grid=(n_out // tm,),
        in_specs=[pl.BlockSpec((1, 1, 2 * tm), lambda i: (t0 + i, 0, 0), memory_space=pltpu.SMEM),
                  pl.BlockSpec((tm, D_MODEL), lambda i: (t0 + i, 0)),
                  pl.BlockSpec((tm, LANES), lambda i: (t0 + i, 0)),
                  _full((1, D_MODEL)),
                  pl.BlockSpec(memory_space=pl.ANY)],
        out_specs=pl.BlockSpec((tm, D_MODEL), lambda i: (i, 0)),
        out_shape=jax.ShapeDtypeStruct((n_out, D_MODEL), F32),
        scratch_shapes=[pltpu.VMEM((tm, D_MODEL), F32), pltpu.VMEM((tm, D_MODEL), F32),
                        pltpu.SemaphoreType.DMA],
        compiler_params=_params(("arbitrary",), VMEM_SMALL),
        name="moe_combine",
    )(pos, h, route, final_g, ys)


def _moe(merged, group_rows, wg, wu, wd, final_g, tm=MOE_ROWS, tm_out=COMBINE_ROWS):
    h, hn, route, ridx, counts = merged
    src, pos, tile_expert, n_used = _route_plan(ridx, counts, tm, tm_out)
    ys = _experts(hn, src, tile_expert, n_used, wg, wu, wd, tm)
    return [_combine(h, route, pos, ys, final_g, row0, rows, tm_out) for row0, rows in group_rows]


def _branches(x2d, mem_k, mem_v, rw_state, w, merged, n_total, row0, *, prompt, batch, seq):
    n = x2d.shape[0]
    grp = "prompt" if prompt else "sample"
    o_gm, p_rw, p_q, p_gate, *vn = _in_proj(x2d, w["norm1_g"], w["w_in_segs"], w["gm_ln_g"], w["gm_ln_b"],
                                            w["gm_mix_" + grp], w["gm_bias_" + grp], not prompt, IN_PROJ_ROWS)
    vn = vn[0] if vn else None
    if prompt:
        o_rw, s_new, sh_new = _rwkv_prompt(p_rw, w["rw_prm"], batch, seq)
        o_xa = _xattn_prompt(p_q, mem_k, mem_v, batch, seq)
    else:
        s0, shift = rw_state
        prev_rows = jnp.pad(shift[:, None, :], ((0, 0), (0, seq - 1), (0, 0))).reshape(n, RW_COLS)
        o_rw, s_new, sh_new = _rwkv_sample(p_rw, prev_rows, s0, w["rw_prm"], seq)
        o_xa = _xattn_sample(p_q, mem_k, mem_v, seq)
    merged = _merge(x2d, o_gm, o_rw, o_xa, p_gate, w["gate_b"], w["w_branch"], w["w_out"],
                    w["norm2_g"], w["wr_hi"], w["wr_lo"], w["b_r"], n_total, row0, merged)
    return merged, s_new, sh_new, vn


def kernel(x_prompt, x_sample, state_rwkv_S, state_rwkv_shift, cache_mem_k, cache_mem_v, mem_prompt, norm1_g, w_in, gate_b, gm_ln_g, gm_ln_b, gm_ws, gm_bs, rw_mu, rw_w0, rw_w2, rw_a0, rw_a2, rw_g2, rw_k_k, rw_k_a, rw_r_k, rw_lnx_g, rw_lnx_b, xa_wk, xa_wv, w_branch, w_out, norm2_g, rg_w, rg_b, re_w, re_b, e_wg, e_wu, e_wd, final_g):
    bp, seq_p, _ = x_prompt.shape
    bs, seq_s, _ = x_sample.shape
    depth = w_in.shape[0]
    assert depth == 1 and seq_p % CHUNK == 0 and RW_ROWS % seq_s == 0 and CHUNK % seq_s == 0

    l = 0
    row = lambda a: a.reshape(1, -1)
    seg = (0, 2 * BRANCH_W, 2 * BRANCH_W + RW_COLS, 3 * BRANCH_W + RW_COLS, 3 * BRANCH_W + RW_COLS + GATE_COLS)
    w_causal = jnp.tril(gm_ws[l])
    nrep = CHUNK // seq_s
    blk = w_causal[:, :seq_s, :seq_s]
    eye_rep = jnp.eye(nrep, dtype=F32)
    mix_sample = jnp.einsum("ab,gts->gatbs", eye_rep, blk).reshape(GM_GROUPS, CHUNK, CHUNK)
    bias_prompt = jnp.repeat(gm_bs[l].T, GM_HD, axis=1)
    bias_sample = jnp.tile(bias_prompt[:seq_s], (nrep, 1))

    wr = jnp.zeros((D_MODEL, LANES), F32)
    wr = wr.at[:, :N_EXPERTS].set(jnp.transpose(re_w[l], (1, 0, 2)).reshape(D_MODEL, N_EXPERTS))
    wr = wr.at[:, N_EXPERTS:N_EXPERTS + N_GROUPS].set(rg_w[l])
    wr_hi = wr.astype(BF16)
    wr_lo = (wr - wr_hi.astype(F32)).astype(BF16)
    b_r = jnp.zeros((1, LANES), F32)
    b_r = b_r.at[0, :N_EXPERTS].set(re_b[l].reshape(-1)).at[0, N_EXPERTS:N_EXPERTS + N_GROUPS].set(rg_b[l])

    w = dict(
        norm1_g=row(norm1_g[l]),
        w_in_segs=[w_in[l][:, a:b].astype(BF16) for a, b in zip(seg[:-1], seg[1:])],
        gm_ln_g=row(gm_ln_g[l]), gm_ln_b=row(gm_ln_b[l]),
        gm_mix_prompt=w_causal.astype(BF16), gm_bias_prompt=bias_prompt,
        gm_mix_sample=mix_sample.astype(BF16), gm_bias_sample=bias_sample,
        rw_prm=(row(rw_mu[l]), row(rw_w0[l]), rw_w2[l].astype(BF16), row(rw_a0[l]), rw_a2[l].astype(BF16),
                rw_g2[l].astype(BF16), row(rw_k_k[l]), row(rw_k_a[l]), row(rw_r_k[l]),
                row(rw_lnx_g[l]), row(rw_lnx_b[l])),
        gate_b=row(gate_b[l]), w_branch=w_branch[l].astype(BF16), w_out=w_out[l].astype(BF16),
        norm2_g=row(norm2_g[l]), wr_hi=wr_hi, wr_lo=wr_lo, b_r=b_r,
        e_wg=e_wg[l].reshape(N_EXPERTS, D_MODEL, EXP_FF),
        e_wu=e_wu[l].reshape(N_EXPERTS, D_MODEL, EXP_FF),
        e_wd=e_wd[l].reshape(N_EXPERTS, EXP_FF, D_MODEL),
        final_g=row(final_g),
    )

    mk_p, mv_p = _mem_kv(mem_prompt.reshape(bp * N_MEM, D_MODEL), xa_wk[l].astype(BF16), xa_wv[l].astype(BF16))
    n_p, n_s = bp * seq_p, bs * seq_s
    merged, sp, shp, _ = _branches(x_prompt.reshape(n_p, D_MODEL), mk_p, mv_p, None, w, None, n_p + n_s, 0,
                                   prompt=True, batch=bp, seq=seq_p)
    merged, ss, shs, vs = _branches(x_sample.reshape(n_s, D_MODEL),
                                    cache_mem_k[l].reshape(bs, N_MEM * XA_HEADS, XA_HD),
                                    cache_mem_v[l].reshape(bs, N_MEM * XA_HEADS, XA_HD),
                                    (state_rwkv_S[l], state_rwkv_shift[l]), w, merged, n_p + n_s, n_p,
                                    prompt=False, batch=bs, seq=seq_s)
    yp, ys = _moe(merged, [(0, n_p), (n_p, n_s)], w["e_wg"], w["e_wu"], w["e_wd"], w["final_g"])

    return (yp.reshape(bp, seq_p, D_MODEL), ys.reshape(bs, seq_s, D_MODEL),
            sp[None], shp[None],
            mk_p.reshape(1, bp, N_MEM, XA_HEADS, XA_HD), mv_p.reshape(1, bp, N_MEM, XA_HEADS, XA_HD),
            ss[None], shs[None], vs.reshape(1, bs, seq_s, BRANCH_W))
```

```python
import functools
import itertools

import numpy as np
import jax
import jax.numpy as jnp
from jax import lax
from jax.experimental import pallas as pl
from jax.experimental.pallas import tpu as pltpu

F32 = jnp.float32
BF16 = jnp.bfloat16

D_MODEL = 1024
BRANCH_W = 512
CHUNK = 128
GM_GROUPS = 8
GM_HD = BRANCH_W // GM_GROUPS
RW_HEADS = 8
RW_HD = BRANCH_W // RW_HEADS
RW_W_LORA = 64
RW_A_LORA = 64
RW_G_LORA = 128
RW_COLS = 3 * BRANCH_W + RW_W_LORA + RW_A_LORA + RW_G_LORA
XA_HEADS = 4
XA_HD = BRANCH_W // XA_HEADS
N_MEM = 256
N_BRANCH = 3
GATE_COLS = N_BRANCH * D_MODEL
N_GROUPS = 4
EXP_PER_GROUP = 8
N_EXPERTS = N_GROUPS * EXP_PER_GROUP
EXP_FF = 512
RMS_EPS = 1e-6
LN_EPS = 1e-5
GN_EPS = 64e-5

LANES = 128
MIB = 1024 * 1024
IN_PROJ_ROWS = 256
MEM_KV_ROWS = 512
RW_ROWS = 64
RW_GROUPS = 2
XATTN_ROWS = 512
XATTN_SEQS = 8
MERGE_ROWS = 256
MOE_ROWS = 256
COMBINE_ROWS = 512
VMEM_SMALL, VMEM_MID, VMEM_IN_PROJ, VMEM_RWKV = 32, 40, 48, 56
ROUTE_E1, ROUTE_E2, ROUTE_W1, ROUTE_W2 = range(4)


def _dot(a, b):
    return jnp.dot(a.astype(BF16), b.astype(BF16), preferred_element_type=F32)


def _dot_nt(a, b):
    return lax.dot_general(a.astype(BF16), b.astype(BF16), (((1,), (1,)), ((), ())),
                           preferred_element_type=F32)


def _dot_tn(a, b):
    return lax.dot_general(a.astype(BF16), b.astype(BF16), (((0,), (0,)), ((), ())),
                           preferred_element_type=F32)


def _sigmoid(x):
    return 1.0 / (1.0 + jnp.exp(-x))


def _gelu(x):
    c = np.float32(np.sqrt(2.0 / np.pi))
    return x * (0.5 * (1.0 + jnp.tanh(c * (x + 0.044715 * (x * x * x)))))


def _softplus(x):
    return jnp.maximum(x, 0.0) + jnp.log(1.0 + jnp.exp(-jnp.abs(x)))


def _rmsnorm(x, g):
    return x * lax.rsqrt(jnp.mean(x * x, axis=-1, keepdims=True) + RMS_EPS) * g


def _split3(x):
    hi = x.astype(BF16)
    r1 = x - hi.astype(F32)
    mid = r1.astype(BF16)
    lo = (r1 - mid.astype(F32)).astype(BF16)
    return hi, mid, lo


def _params(sem, vmem_mib):
    return pltpu.CompilerParams(dimension_semantics=sem, vmem_limit_bytes=vmem_mib * MIB)


def _full(shape):
    nd = len(shape)
    return pl.BlockSpec(shape, lambda *_: (0,) * nd)


def _gmlp_chunk(pu, pv, ln_g, ln_b, wmix_ref, bmix_ref):
    u = _gelu(pu)
    vf = _gelu(pv)
    mu = jnp.mean(vf, axis=-1, keepdims=True)
    vc = vf - mu
    var = jnp.mean(vc * vc, axis=-1, keepdims=True)
    vn = vc * lax.rsqrt(var + LN_EPS) * ln_g + ln_b
    lane = lax.broadcasted_iota(jnp.int32, (CHUNK, LANES), 1)
    lo_half = lane < GM_HD
    outs = []
    for p in range(GM_GROUPS // 2):
        vp = vn[:, p * LANES:(p + 1) * LANES]
        s = (_dot(wmix_ref[2 * p], jnp.where(lo_half, vp, 0.0))
             + _dot(wmix_ref[2 * p + 1], jnp.where(lo_half, 0.0, vp)))
        outs.append(u[:, p * LANES:(p + 1) * LANES] * (s + bmix_ref[:, p * LANES:(p + 1) * LANES]))
    return jnp.concatenate(outs, axis=-1), vn


def _in_proj_kernel(x_ref, g_ref, wgm_ref, wrw_ref, wq_ref, wgt_ref, lng_ref, lnb_ref, wmix_ref, bmix_ref,
                    ogm_ref, rw_ref, q_ref, gt_ref, *vn_refs):
    xb = _rmsnorm(x_ref[...], g_ref[...]).astype(BF16)
    gm = jnp.dot(xb, wgm_ref[...], preferred_element_type=F32)
    rw_ref[...] = jnp.dot(xb, wrw_ref[...], preferred_element_type=F32)
    q_ref[...] = jnp.dot(xb, wq_ref[...], preferred_element_type=F32)
    gt_ref[...] = jnp.dot(xb, wgt_ref[...], preferred_element_type=F32)
    for c in range(x_ref.shape[0] // CHUNK):
        rows = slice(c * CHUNK, (c + 1) * CHUNK)
        o, vn = _gmlp_chunk(gm[rows, 0:BRANCH_W], gm[rows, BRANCH_W:2 * BRANCH_W], lng_ref[...], lnb_ref[...],
                            wmix_ref, bmix_ref)
        ogm_ref[rows, :] = o.astype(ogm_ref.dtype)
        if vn_refs:
            vn_refs[0][rows, :] = vn


def _in_proj(x, g, w_segs, ln_g, ln_b, wmix, bmix, emit_vn, tm):
    n = x.shape[0]
    widths = [w.shape[1] for w in w_segs]
    out_w = [BRANCH_W] + widths[1:] + ([BRANCH_W] if emit_vn else [])
    out_dt = [BF16] + [F32] * (len(out_w) - 1)
    return pl.pallas_call(
        _in_proj_kernel,
        grid=(n // tm,),
        in_specs=[pl.BlockSpec((tm, D_MODEL), lambda i: (i, 0)), _full((1, D_MODEL))]
        + [pl.BlockSpec((D_MODEL, wd), lambda i: (0, 0), pipeline_mode=pl.Buffered(1)) for wd in widths]
        + [_full((1, BRANCH_W)), _full((1, BRANCH_W)), _full((GM_GROUPS, CHUNK, CHUNK)),
           _full((CHUNK, BRANCH_W))],
        out_specs=[pl.BlockSpec((tm, wd), lambda i: (i, 0)) for wd in out_w],
        out_shape=[jax.ShapeDtypeStruct((n, wd), dt) for wd, dt in zip(out_w, out_dt)],
        compiler_params=_params(("arbitrary",), VMEM_IN_PROJ),
        name="in_proj",
    )(x, g, *w_segs, ln_g, ln_b, wmix, bmix)


def _mem_kv_kernel(m_ref, wk_ref, wv_ref, k_ref, v_ref):
    mb = m_ref[...].astype(BF16)
    k_ref[...] = jnp.dot(mb, wk_ref[...], preferred_element_type=F32)
    v_ref[...] = jnp.dot(mb, wv_ref[...], preferred_element_type=F32)


def _mem_kv(mem, wk, wv, tm=MEM_KV_ROWS):
    n = mem.shape[0]
    return pl.pallas_call(
        _mem_kv_kernel,
        grid=(n // tm,),
        in_specs=[pl.BlockSpec((tm, D_MODEL), lambda i: (i, 0)),
                  _full((D_MODEL, BRANCH_W)), _full((D_MODEL, BRANCH_W))],
        out_specs=[pl.BlockSpec((tm, BRANCH_W), lambda i: (i, 0))] * 2,
        out_shape=[jax.ShapeDtypeStruct((n, BRANCH_W), F32)] * 2,
        compiler_params=_params(("arbitrary",), VMEM_SMALL),
        name="mem_kv",
    )(mem, wk, wv)


def _rwkv_rows(p, first, prev, s_ref, o_ref, prm, nseq):
    (mu, w0, w2, a0, a2, g2, k_k, k_a, r_k, lnx_g, lnx_b) = prm
    rows = RW_ROWS
    ls = rows // nseq
    shifted = jnp.where(first, prev, pltpu.roll(p, 1, 0))
    xs = p + (shifted - p) * mu
    r = xs[:, 0:BRANCH_W]
    k = xs[:, BRANCH_W:2 * BRANCH_W]
    v = xs[:, 2 * BRANCH_W:3 * BRANCH_W]
    o = 3 * BRANCH_W
    wd = xs[:, o:o + RW_W_LORA]
    ad = xs[:, o + RW_W_LORA:o + RW_W_LORA + RW_A_LORA]
    gd = xs[:, o + RW_W_LORA + RW_A_LORA:RW_COLS]

    w_log = -_softplus(-(w0 + _dot(jnp.tanh(wd), w2))) - 0.5
    logw = -jnp.exp(w_log)
    a = _sigmoid(a0 + _dot(ad, a2))
    g = _dot(_sigmoid(gd), g2)
    kkr = k * k_k
    kf = k * (1.0 + (a - 1.0) * k_a)
    rkr = r * kf * r_k

    ri = lax.broadcasted_iota(jnp.int32, (rows, rows), 0)
    ci = lax.broadcasted_iota(jnp.int32, (rows, rows), 1)
    if nseq == 1:
        same = ci >= 0
    else:
        sh = int(np.log2(ls))
        same = lax.shift_right_logical(ri, sh) == lax.shift_right_logical(ci, sh)
    low_incl = same & (ci <= ri)
    low_strict = same & (ci < ri)
    m_incl = jnp.where(low_incl, 1.0, 0.0).astype(BF16)
    m_same = jnp.where(same, 1.0, 0.0).astype(BF16)
    hi, mid, lo = _split3(logw)
    cum = (jnp.dot(m_incl, hi, preferred_element_type=F32) + jnp.dot(m_incl, mid, preferred_element_type=F32)
           + jnp.dot(m_incl, lo, preferred_element_type=F32))
    tot = (jnp.dot(m_same, hi, preferred_element_type=F32) + jnp.dot(m_same, mid, preferred_element_type=F32)
           + jnp.dot(m_same, lo, preferred_element_type=F32))
    g_t = jnp.exp(cum)
    g_prev = jnp.exp(cum - logw)
    g_inv = jnp.exp(-cum)
    g_end = jnp.exp(tot - cum)
    g_tot = jnp.exp(tot)
    eye = jnp.where(ri == ci, 1.0, 0.0)
    yield

    hs = range(RW_HEADS)
    sls = [slice(h * RW_HD, (h + 1) * RW_HD) for h in hs]
    kk = [kkr[:, sl] for sl in sls]
    kk = [x * lax.rsqrt(jnp.maximum(jnp.sum(x * x, axis=-1, keepdims=True), 1e-24)) for x in kk]
    bv = [kk[h] * a[:, sls[h]] for h in hs]
    k_h = [kf[:, sl] for sl in sls]
    v_h = [v[:, sl] for sl in sls]
    at = [-kk[h] * g_prev[:, sls[h]] for h in hs]
    rt = [r[:, sl] * g_t[:, sl] for sl in sls]
    bt = [bv[h] * g_inv[:, sls[h]] for h in hs]
    kt = [k_h[h] * g_inv[:, sls[h]] for h in hs]
    bh = [bv[h] * g_end[:, sls[h]] for h in hs]
    kh = [k_h[h] * g_end[:, sls[h]] for h in hs]

    ar = [jnp.concatenate([at[h], rt[h]], axis=0) for h in hs]
    bk_t = [jnp.concatenate([bt[h], kt[h]], axis=0) for h in hs]
    pp = [_dot_nt(ar[h], bk_t[h]) for h in hs]
    ri2 = lax.broadcasted_iota(jnp.int32, (rows, 2 * rows), 0)
    ci2 = lax.broadcasted_iota(jnp.int32, (rows, 2 * rows), 1)
    k_half = ci2 >= rows
    cpos = jnp.where(k_half, ci2 - rows, ci2)
    if nseq == 1:
        same2 = cpos >= 0
    else:
        same2 = lax.shift_right_logical(ri2, sh) == lax.shift_right_logical(cpos, sh)
    l_ab = [jnp.where(low_strict, x[:rows, :rows], 0.0) for x in pp]
    l_kv = [jnp.where(same2 & k_half & (cpos < ri2), x[:rows], 0.0) for x in pp]
    a_y = [jnp.where(same2 & (cpos <= ri2), x[rows:], 0.0) for x in pp]
    yield

    tm = [eye + x for x in l_ab]
    pw = [_dot(x, x) for x in l_ab]
    n_dbl = int(np.log2(ls)) - 1
    for it in range(n_dbl):
        yield
        if it < n_dbl - 1:
            z = [_dot(jnp.concatenate([tm[h], pw[h]], axis=0), pw[h]) for h in hs]
            tm = [tm[h] + z[h][:rows] for h in hs]
            pw = [z[h][rows:] for h in hs]
        else:
            tm = [tm[h] + _dot(tm[h], pw[h]) for h in hs]

    if nseq == 1:
        ars = [_dot_nt(ar[h], s_ref[0, h]) for h in hs]
        as0 = [x[:rows] for x in ars]
        rs0 = [x[rows:] for x in ars]
    else:
        as0, rs0 = [], []
        for h in hs:
            zs = [_dot_nt(jnp.concatenate([at[h][b * ls:(b + 1) * ls], rt[h][b * ls:(b + 1) * ls]], axis=0),
                          s_ref[b, h]) for b in range(nseq)]
            as0.append(jnp.concatenate([x[:ls] for x in zs], axis=0))
            rs0.append(jnp.concatenate([x[ls:] for x in zs], axis=0))

    lv = [_dot(l_kv[h], jnp.concatenate([v_h[h], v_h[h]], axis=0)) for h in hs]
    yield
    u = [_dot(tm[h], as0[h] + lv[h]) for h in hs]
    yield
    y = [rs0[h] + _dot(a_y[h], jnp.concatenate([u[h], v_h[h]], axis=0)) for h in hs]
    yield

    for h in hs:
        for b in range(nseq):
            rb = slice(b * ls, (b + 1) * ls)
            uv = jnp.concatenate([u[h][rb], v_h[h][rb]], axis=0)
            bk = jnp.concatenate([bh[h][rb], kh[h][rb]], axis=0)
            s_ref[b, h] = s_ref[b, h] * g_tot[b * ls:b * ls + 1, sls[h]] + _dot_tn(uv, bk)

    outs = []
    for h in hs:
        ym = jnp.mean(y[h], axis=-1, keepdims=True)
        yc = y[h] - ym
        yv = jnp.mean(yc * yc, axis=-1, keepdims=True)
        yn = yc * lax.rsqrt(yv + GN_EPS) * lnx_g[:, sls[h]] + lnx_b[:, sls[h]]
        bonus = jnp.sum(rkr[:, sls[h]], axis=-1, keepdims=True) * v_h[h]
        outs.append((yn + bonus) * g[:, sls[h]])
    o_ref[...] = jnp.concatenate(outs, axis=-1).astype(o_ref.dtype)


def _interleave(gens):
    for _ in itertools.zip_longest(*gens):
        pass


def _rwkv_prompt_kernel(p_ref, *refs):
    prm_refs, (o_ref, s_ref, sh_ref, carry_ref) = refs[:11], refs[11:]
    c = pl.program_id(1)

    @pl.when(c == 0)
    def _():
        s_ref[...] = jnp.zeros_like(s_ref)
        carry_ref[...] = jnp.zeros_like(carry_ref)

    first = lax.broadcasted_iota(jnp.int32, (RW_ROWS, 1), 0) == 0
    prm = tuple(x[...] for x in prm_refs)
    groups = range(p_ref.shape[0])
    _interleave([_rwkv_rows(p_ref[g], first, carry_ref[8 * g:8 * g + 1, :], s_ref.at[pl.ds(g, 1)],
                            o_ref.at[g], prm, nseq=1) for g in groups])
    for g in groups:
        last = p_ref[g, RW_ROWS - 1:RW_ROWS, :]
        carry_ref[8 * g:8 * g + 1, :] = last
        sh_ref[g] = last


def _rwkv_sample_kernel(p_ref, prev_ref, s0_ref, *refs):
    prm_refs, (o_ref, s_ref, sh_ref) = refs[:11], refs[11:]
    nseq = s0_ref.shape[0] // p_ref.shape[0]
    ls = RW_ROWS // nseq
    s_ref[...] = s0_ref[...]
    first = (lax.broadcasted_iota(jnp.int32, (RW_ROWS, 1), 0) & (ls - 1)) == 0
    prm = tuple(x[...] for x in prm_refs)
    groups = range(p_ref.shape[0])
    _interleave([_rwkv_rows(p_ref[g], first, prev_ref[g], s_ref.at[pl.ds(g * nseq, nseq)], o_ref.at[g],
                            prm, nseq=nseq) for g in groups])
    for g in groups:
        for b in range(nseq):
            sh_ref[g * nseq + b:g * nseq + b + 1, :] = p_ref[g, (b + 1) * ls - 1:(b + 1) * ls, :]


def _rw_param_specs():
    shapes = [(1, RW_COLS), (1, BRANCH_W), (RW_W_LORA, BRANCH_W), (1, BRANCH_W), (RW_A_LORA, BRANCH_W),
              (RW_G_LORA, BRANCH_W), (1, BRANCH_W), (1, BRANCH_W), (1, BRANCH_W), (1, BRANCH_W), (1, BRANCH_W)]
    return [_full(s) for s in shapes]


def _rwkv_prompt(p_rw, prm, batch, seq):
    nc = seq // RW_ROWS
    g = RW_GROUPS
    p3 = p_rw.reshape(batch, seq, RW_COLS)
    o, s_new, sh = pl.pallas_call(
        _rwkv_prompt_kernel,
        grid=(batch // g, nc),
        in_specs=[pl.BlockSpec((g, RW_ROWS, RW_COLS), lambda b, c: (b, c, 0))] + _rw_param_specs(),
        out_specs=[pl.BlockSpec((g, RW_ROWS, BRANCH_W), lambda b, c: (b, c, 0)),
                   pl.BlockSpec((g, RW_HEADS, RW_HD, RW_HD), lambda b, c: (b, 0, 0, 0)),
                   pl.BlockSpec((g, 1, RW_COLS), lambda b, c: (b, 0, 0))],
        out_shape=[jax.ShapeDtypeStruct((batch, seq, BRANCH_W), BF16),
                   jax.ShapeDtypeStruct((batch, RW_HEADS, RW_HD, RW_HD), F32),
                   jax.ShapeDtypeStruct((batch, 1, RW_COLS), F32)],
        scratch_shapes=[pltpu.VMEM((8 * g, RW_COLS), F32)],
        compiler_params=_params(("arbitrary", "arbitrary"), VMEM_RWKV),
        name="rwkv_prompt",
    )(p3, *prm)
    return o.reshape(batch * seq, BRANCH_W), s_new, sh.reshape(batch, RW_COLS)


def _rwkv_sample(p_rw, prev_rows, s0, prm, seq):
    n = p_rw.shape[0]
    nseq = RW_ROWS // seq
    batch = n // seq
    g = RW_GROUPS
    ng = n // RW_ROWS
    o, s_new, sh = pl.pallas_call(
        _rwkv_sample_kernel,
        grid=(ng // g,),
        in_specs=[pl.BlockSpec((g, RW_ROWS, RW_COLS), lambda i: (i, 0, 0)),
                  pl.BlockSpec((g, RW_ROWS, RW_COLS), lambda i: (i, 0, 0)),
                  pl.BlockSpec((g * nseq, RW_HEADS, RW_HD, RW_HD), lambda i: (i, 0, 0, 0))] + _rw_param_specs(),
        out_specs=[pl.BlockSpec((g, RW_ROWS, BRANCH_W), lambda i: (i, 0, 0)),
                   pl.BlockSpec((g * nseq, RW_HEADS, RW_HD, RW_HD), lambda i: (i, 0, 0, 0)),
                   pl.BlockSpec((g * nseq, RW_COLS), lambda i: (i, 0))],
        out_shape=[jax.ShapeDtypeStruct((ng, RW_ROWS, BRANCH_W), BF16),
                   jax.ShapeDtypeStruct((batch, RW_HEADS, RW_HD, RW_HD), F32),
                   jax.ShapeDtypeStruct((batch, RW_COLS), F32)],
        compiler_params=_params(("arbitrary",), VMEM_RWKV),
        name="rwkv_sample",
    )(p_rw.reshape(ng, RW_ROWS, RW_COLS), prev_rows.reshape(ng, RW_ROWS, RW_COLS), s0, *prm)
    return o.reshape(n, BRANCH_W), s_new, sh


def _attend_all(qkv):
    s = [_dot_nt(q, k) * (XA_HD ** -0.5) for q, k, _ in qkv]
    e = [jnp.exp(x - jnp.max(x, axis=-1, keepdims=True)) for x in s]
    pr = [x / jnp.sum(x, axis=-1, keepdims=True) for x in e]
    return [_dot(p, v) for p, (_, _, v) in zip(pr, qkv)]


def _xattn_prompt_kernel(q_ref, k_ref, v_ref, o_ref):
    sls = [slice(h * XA_HD, (h + 1) * XA_HD) for h in range(XA_HEADS)]
    outs = _attend_all([(q_ref[:, sl], k_ref[:, sl], v_ref[:, sl]) for sl in sls])
    for sl, o in zip(sls, outs):
        o_ref[:, sl] = o.astype(o_ref.dtype)


def _xattn_prompt(q, mk, mv, batch, seq, tq=XATTN_ROWS):
    nt = seq // tq
    return pl.pallas_call(
        _xattn_prompt_kernel,
        grid=(batch, nt),
        in_specs=[pl.BlockSpec((tq, BRANCH_W), lambda b, i: (b * nt + i, 0)),
                  pl.BlockSpec((N_MEM, BRANCH_W), lambda b, i: (b, 0)),
                  pl.BlockSpec((N_MEM, BRANCH_W), lambda b, i: (b, 0))],
        out_specs=pl.BlockSpec((tq, BRANCH_W), lambda b, i: (b * nt + i, 0)),
        out_shape=jax.ShapeDtypeStruct((batch * seq, BRANCH_W), BF16),
        compiler_params=_params(("arbitrary", "arbitrary"), VMEM_SMALL),
        name="xattn_prompt",
    )(q, mk, mv)


def _xattn_sample_kernel(q_ref, k_ref, v_ref, o_ref):
    nb = k_ref.shape[0]
    ls = q_ref.shape[0] // nb
    nq = XA_HEADS * ls
    seqs = range(nb)
    rows = [slice(b * ls, (b + 1) * ls) for b in seqs]
    qs = [jnp.concatenate([q_ref[rows[b], h * XA_HD:(h + 1) * XA_HD] for h in range(XA_HEADS)], axis=0)
          for b in seqs]
    ri = lax.broadcasted_iota(jnp.int32, (N_MEM * XA_HEADS, nq), 0)
    ci = lax.broadcasted_iota(jnp.int32, (N_MEM * XA_HEADS, nq), 1)
    own = (ri & (XA_HEADS - 1)) == lax.shift_right_logical(ci, int(np.log2(ls)))
    s = [jnp.where(own, _dot_nt(k_ref[b], qs[b]) * (XA_HD ** -0.5), -jnp.inf) for b in seqs]
    e = [jnp.exp(x - jnp.max(x, axis=0, keepdims=True)) for x in s]
    pr = [x / jnp.sum(x, axis=0, keepdims=True) for x in e]
    outs = [_dot_tn(pr[b], v_ref[b]) for b in seqs]
    for b in seqs:
        o_ref[rows[b], :] = jnp.concatenate([outs[b][h * ls:(h + 1) * ls] for h in range(XA_HEADS)],
                                            axis=-1).astype(o_ref.dtype)


def _xattn_sample(q, mk, mv, seq, nb=XATTN_SEQS):
    batch = mk.shape[0]
    mem_spec = pl.BlockSpec((nb, N_MEM * XA_HEADS, XA_HD), lambda i: (i, 0, 0))
    return pl.pallas_call(
        _xattn_sample_kernel,
        grid=(batch // nb,),
        in_specs=[pl.BlockSpec((nb * seq, BRANCH_W), lambda i: (i, 0)), mem_spec, mem_spec],
        out_specs=pl.BlockSpec((nb * seq, BRANCH_W), lambda i: (i, 0)),
        out_shape=jax.ShapeDtypeStruct((batch * seq, BRANCH_W), BF16),
        compiler_params=_params(("arbitrary",), VMEM_MID),
        name="xattn_sample",
    )(q, mk, mv)


def _merge_kernel(n_active, *refs):
    i = pl.program_id(0)

    @pl.when(i < n_active)
    def _():
        _merge_rows(*refs)

    @pl.when(i >= n_active)
    def _():
        for ref in refs[-5:-1]:
            ref[...] = jnp.zeros_like(ref)


def _merge_rows(x_ref, gm_ref, rw_ref, xa_ref, gt_ref, gb_ref, wb_ref, wo_ref, n2_ref,
                wr_hi_ref, wr_lo_ref, br_ref, cnt_in_ref, *refs):
    h_ref, hn_ref, route_ref, ridx_ref, cnt_ref = refs[-5:]
    merged = None
    for n, br in enumerate((gm_ref, rw_ref, xa_ref)):
        cs = slice(n * D_MODEL, (n + 1) * D_MODEL)
        up = jnp.dot(br[...], wb_ref[n], preferred_element_type=F32)
        term = _sigmoid(gt_ref[:, cs] + gb_ref[:, cs]) * up
        merged = term if merged is None else merged + term
    h = x_ref[...] + _dot(merged, wo_ref[...])
    h_ref[...] = h
    hn = _rmsnorm(h, n2_ref[...])
    hn_ref[...] = hn

    hn_hi = hn.astype(BF16)
    hn_lo = (hn - hn_hi.astype(F32)).astype(BF16)
    logits = (jnp.dot(hn_hi, wr_hi_ref[...], preferred_element_type=F32)
              + jnp.dot(hn_lo, wr_hi_ref[...], preferred_element_type=F32)
              + jnp.dot(hn_hi, wr_lo_ref[...], preferred_element_type=F32)) + br_ref[...]
    lane = lax.broadcasted_iota(jnp.int32, logits.shape, 1)
    neg = -jnp.inf
    big = jnp.int32(1 << 20)
    gmask = (lane >= N_EXPERTS) & (lane < N_EXPERTS + N_GROUPS)
    gl = jnp.where(gmask, logits, neg)
    gmax = jnp.max(gl, axis=-1, keepdims=True)
    gsel = jnp.min(jnp.where(gl == gmax, lane, big), axis=-1, keepdims=True) - N_EXPERTS
    gsum = jnp.sum(jnp.where(gmask, jnp.exp(logits - gmax), 0.0), axis=-1, keepdims=True)
    pg_top = 1.0 / gsum
    emask = (lane >= gsel * EXP_PER_GROUP) & (lane < (gsel + 1) * EXP_PER_GROUP)
    el = jnp.where(emask, logits, neg)
    m1 = jnp.max(el, axis=-1, keepdims=True)
    i1 = jnp.min(jnp.where(el == m1, lane, big), axis=-1, keepdims=True)
    el2 = jnp.where(lane == i1, neg, el)
    m2 = jnp.max(el2, axis=-1, keepdims=True)
    i2 = jnp.min(jnp.where(el2 == m2, lane, big), axis=-1, keepdims=True)
    t2 = jnp.exp(m2 - m1)
    w1 = pg_top / (1.0 + t2)
    w2 = pg_top * t2 / (1.0 + t2)

    @pl.when(pl.program_id(0) == 0)
    def _():
        cnt_ref[...] = cnt_in_ref[...]

    onehot = jnp.where(lane == i1, 1.0, 0.0) + jnp.where(lane == i2, 1.0, 0.0)
    cnt_ref[0:1, :] = cnt_ref[0:1, :] + jnp.sum(onehot, axis=0, keepdims=True)

    route = jnp.zeros(logits.shape, F32)
    for col, val in ((ROUTE_E1, i1.astype(F32)), (ROUTE_E2, i2.astype(F32)), (ROUTE_W1, w1), (ROUTE_W2, w2)):
        route = jnp.where(lane == col, val, route)
    route_ref[...] = route
    ridx_ref[...] = jnp.transpose(route)[0:8, :].astype(jnp.int32)


def _merge(x, o_gm, o_rw, o_xa, p_gate, gate_b, w_branch, w_out, n2_g, wr_hi, wr_lo, b_r, n_total, row0, prev,
           tm=MERGE_ROWS):
    n = x.shape[0]
    t0 = row0 // tm
    n_act = n // tm
    n_steps = n_total // tm if prev is None else n_act
    row = lambda wd: pl.BlockSpec((tm, wd), lambda i: (jnp.minimum(i, n_act - 1), 0))
    out_row = lambda wd: pl.BlockSpec((tm, wd), lambda i: (t0 + i, 0))
    out_shape = [jax.ShapeDtypeStruct((n_total, D_MODEL), F32), jax.ShapeDtypeStruct((n_total, D_MODEL), F32),
                 jax.ShapeDtypeStruct((n_total, LANES), F32), jax.ShapeDtypeStruct((8, n_total), jnp.int32),
                 jax.ShapeDtypeStruct((8, LANES), F32)]
    if prev is None:
        carried, counts_in, aliases = [], jnp.zeros((8, LANES), F32), {}
    else:
        carried, counts_in = list(prev[:4]), prev[4]
        aliases = {13 + k: k for k in range(4)}
    return pl.pallas_call(
        functools.partial(_merge_kernel, n_act),
        grid=(n_steps,),
        in_specs=[row(D_MODEL), row(BRANCH_W), row(BRANCH_W), row(BRANCH_W), row(GATE_COLS),
                  _full((1, GATE_COLS)), _full((N_BRANCH, BRANCH_W, D_MODEL)), _full((D_MODEL, D_MODEL)),
                  _full((1, D_MODEL)), _full((D_MODEL, LANES)), _full((D_MODEL, LANES)), _full((1, LANES)),
                  _full((8, LANES))] + [pl.BlockSpec(memory_space=pl.ANY)] * len(carried),
        out_specs=[out_row(D_MODEL), out_row(D_MODEL), out_row(LANES),
                   pl.BlockSpec((8, tm), lambda i: (0, t0 + i)), _full((8, LANES))],
        out_shape=out_shape,
        input_output_aliases=aliases,
        compiler_params=_params(("arbitrary",), VMEM_MID),
        name="merge",
    )(x, o_gm, o_rw, o_xa, p_gate, gate_b, w_branch, w_out, n2_g, wr_hi, wr_lo, b_r, counts_in, *carried)


def _route_plan(ridx, counts, tm):
    n = ridx.shape[1]
    n2 = 2 * n
    n_tiles = n2 // tm + N_EXPERTS
    n_rows = n_tiles * tm
    key_bits = int(n_rows - 1).bit_length()
    cnt = counts[0, :N_EXPERTS].astype(jnp.int32)
    tiles = (cnt + tm - 1) // tm
    tile_end = jnp.cumsum(tiles)
    n_used = tile_end[N_EXPERTS - 1:]
    tc = jnp.minimum(jnp.arange(n_tiles, dtype=jnp.int32), n_used - 1)
    tile_expert = jnp.sum((tc[:, None] >= tile_end[None, :]).astype(jnp.int32), axis=1)
    pad_end = jnp.cumsum(tiles * tm - cnt)
    d = jnp.arange(n_rows - n2, dtype=jnp.int32)
    pad_expert = jnp.sum((d[:, None] >= pad_end[None, :]).astype(jnp.int32), axis=1)
    eid = ridx[ROUTE_E1:ROUTE_E2 + 1].reshape(n2)
    keys = jnp.concatenate([(eid << key_bits) + jnp.arange(n2, dtype=jnp.int32),
                            (pad_expert << key_bits) + n2 + d])
    row_id = jnp.sort(keys) & ((1 << key_bits) - 1)
    src = jnp.where(row_id < n2, jnp.where(row_id >= n, row_id - n, row_id), 0)
    return src.reshape(n_tiles, 1, tm), row_id.reshape(n_tiles, 1, tm), tile_expert, n_used


def _row_copy(src_ref, src_row, dst_ref, dst_row, sem):
    return pltpu.make_async_copy(src_ref.at[pl.ds(src_row, 1)], dst_ref.at[pl.ds(dst_row, 1)], sem)


def _experts_kernel(te_ref, nu_ref, src_ref, nxt_ref, nx2_ref, dst_ref, hn_ref, wg_ref, wu_ref, wd_ref, o_ref,
                    xbuf, ybuf, wg_s, wu_s, wd_s, gsem, ssem):
    t = pl.program_id(0)
    tm = xbuf.shape[1]
    n2 = o_ref.shape[0] - N_EXPERTS * tm
    slot = t & 1
    other = 1 - slot
    g0 = lax.rem(t, 3)
    g1 = lax.rem(t + 1, 3)
    g2 = lax.rem(t + 2, 3)

    def gather(idx_ref, s):
        for r in range(tm):
            _row_copy(hn_ref, idx_ref[0, 0, r], xbuf.at[s], r, gsem.at[s]).start(priority=0)

    def wait_rows(buf, sem, s):
        for _ in range(tm):
            _row_copy(hn_ref, 0, buf.at[s], 0, sem.at[s]).wait()

    @pl.when(t == 0)
    def _():
        ybuf[1] = jnp.zeros(ybuf.shape[1:], ybuf.dtype)
        fills = [pltpu.make_async_copy(ybuf.at[1], o_ref.at[pl.ds(n2 + e * tm, tm)], ssem.at[1])
                 for e in range(N_EXPERTS)]
        for f in fills:
            f.start()
        for f in fills:
            f.wait()
        gather(src_ref, 0)
        gather(nxt_ref, 1)

    @pl.when(t < nu_ref[0])
    def _():
        gather(nx2_ref, g2)
        wait_rows(xbuf, gsem, g0)

        @pl.when(t >= 2)
        def _():
            wait_rows(ybuf, ssem, slot)

        @pl.when((t == 0) | (te_ref[t] != te_ref[jnp.maximum(t - 1, 0)]))
        def _():
            wg_s[...] = wg_ref[...].astype(BF16)
            wu_s[...] = wu_ref[...].astype(BF16)
            wd_s[...] = wd_ref[...].astype(BF16)

        xb = xbuf[g0].astype(BF16)
        gate = jnp.dot(xb, wg_s[...], preferred_element_type=F32)
        up = jnp.dot(xb, wu_s[...], preferred_element_type=F32)
        ybuf[slot] = _dot(gate * _sigmoid(gate) * up, wd_s[...])
        for r in range(tm):
            _row_copy(ybuf.at[slot], r, o_ref, dst_ref[0, 0, r], ssem.at[slot]).start(priority=1)

        @pl.when(t == nu_ref[0] - 1)
        def _():
            wait_rows(xbuf, gsem, g1)
            wait_rows(xbuf, gsem, g2)

            @pl.when(t >= 1)
            def _():
                wait_rows(ybuf, ssem, other)

            wait_rows(ybuf, ssem, slot)


def _experts(hn, src, dst, tile_expert, n_used, wg, wu, wd, tm):
    n = hn.shape[0]
    n_tiles = src.shape[0]
    idx = lambda f: pl.BlockSpec((1, 1, tm), lambda t, te, nu: (f(t), 0, 0), memory_space=pltpu.SMEM)
    weight = lambda shape: pl.BlockSpec((None,) + shape, lambda t, te, nu: (te[t], 0, 0))
    return pl.pallas_call(
        _experts_kernel,
        grid_spec=pltpu.PrefetchScalarGridSpec(
            num_scalar_prefetch=2,
            grid=(n_tiles,),
            in_specs=[idx(lambda t: t), idx(lambda t: jnp.minimum(t + 1, n_tiles - 1)),
                      idx(lambda t: jnp.minimum(t + 2, n_tiles - 1)), idx(lambda t: t),
                      pl.BlockSpec(memory_space=pl.ANY),
                      weight((D_MODEL, EXP_FF)), weight((D_MODEL, EXP_FF)), weight((EXP_FF, D_MODEL))],
            out_specs=pl.BlockSpec(memory_space=pl.ANY),
            scratch_shapes=[pltpu.VMEM((3, tm, D_MODEL), F32), pltpu.VMEM((2, tm, D_MODEL), F32),
                            pltpu.VMEM((D_MODEL, EXP_FF), BF16), pltpu.VMEM((D_MODEL, EXP_FF), BF16),
                            pltpu.VMEM((EXP_FF, D_MODEL), BF16),
                            pltpu.SemaphoreType.DMA((3,)), pltpu.SemaphoreType.DMA((2,))]),
        out_shape=jax.ShapeDtypeStruct((2 * n + N_EXPERTS * tm, D_MODEL), F32),
        compiler_params=_params(("arbitrary",), VMEM_MID),
        name="moe_experts",
    )(tile_expert, n_used, src, src, src, dst, hn, wg, wu, wd)


def _combine_kernel(h_ref, route_ref, fg_ref, o1_ref, o2_ref, y_ref):
    rt = route_ref[...]
    lane = lax.broadcasted_iota(jnp.int32, rt.shape, 1)
    w1 = jnp.sum(jnp.where(lane == ROUTE_W1, rt, 0.0), axis=-1, keepdims=True)
    w2 = jnp.sum(jnp.where(lane == ROUTE_W2, rt, 0.0), axis=-1, keepdims=True)
    y_ref[...] = _rmsnorm(h_ref[...] + w1 * o1_ref[...] + w2 * o2_ref[...], fg_ref[...])


def _combine(h, route, o, final_g, row0, n_out, tm=COMBINE_ROWS):
    n = h.shape[0]
    t0 = row0 // tm
    return pl.pallas_call(
        _combine_kernel,
        grid=(n_out // tm,),
        in_specs=[pl.BlockSpec((tm, D_MODEL), lambda i: (t0 + i, 0)),
                  pl.BlockSpec((tm, LANES), lambda i: (t0 + i, 0)),
                  _full((1, D_MODEL)),
                  pl.BlockSpec((tm, D_MODEL), lambda i: (t0 + i, 0)),
                  pl.BlockSpec((tm, D_MODEL), lambda i: (n // tm + t0 + i, 0))],
        out_specs=pl.BlockSpec((tm, D_MODEL), lambda i: (i, 0)),
        out_shape=jax.ShapeDtypeStruct((n_out, D_MODEL), F32),
        compiler_params=_params(("arbitrary",), VMEM_MID),
        name="moe_combine",
    )(h, route, final_g, o, o)


def _moe(merged, group_rows, wg, wu, wd, final_g, tm=MOE_ROWS):
    h, hn, route, ridx, counts = merged
    src, dst, tile_expert, n_used = _route_plan(ridx, counts, tm)
    o = _experts(hn, src, dst, tile_expert, n_used, wg, wu, wd, tm)
    return [_combine(h, route, o, final_g, row0, rows) for row0, rows in group_rows]


def _branches(x2d, mem_k, mem_v, rw_state, w, merged, n_total, row0, *, prompt, batch, seq):
    n = x2d.shape[0]
    grp = "prompt" if prompt else "sample"
    o_gm, p_rw, p_q, p_gate, *vn = _in_proj(x2d, w["norm1_g"], w["w_in_segs"], w["gm_ln_g"], w["gm_ln_b"],
                                            w["gm_mix_" + grp], w["gm_bias_" + grp], not prompt, IN_PROJ_ROWS)
    vn = vn[0] if vn else None
    if prompt:
        o_rw, s_new, sh_new = _rwkv_prompt(p_rw, w["rw_prm"], batch, seq)
        o_xa = _xattn_prompt(p_q, mem_k, mem_v, batch, seq)
    else:
        s0, shift = rw_state
        prev_rows = jnp.pad(shift[:, None, :], ((0, 0), (0, seq - 1), (0, 0))).reshape(n, RW_COLS)
        o_rw, s_new, sh_new = _rwkv_sample(p_rw, prev_rows, s0, w["rw_prm"], seq)
        o_xa = _xattn_sample(p_q, mem_k, mem_v, seq)
    merged = _merge(x2d, o_gm, o_rw, o_xa, p_gate, w["gate_b"], w["w_branch"], w["w_out"],
                    w["norm2_g"], w["wr_hi"], w["wr_lo"], w["b_r"], n_total, row0, merged)
    return merged, s_new, sh_new, vn


def kernel(x_prompt, x_sample, state_rwkv_S, state_rwkv_shift, cache_mem_k, cache_mem_v, mem_prompt, norm1_g, w_in, gate_b, gm_ln_g, gm_ln_b, gm_ws, gm_bs, rw_mu, rw_w0, rw_w2, rw_a0, rw_a2, rw_g2, rw_k_k, rw_k_a, rw_r_k, rw_lnx_g, rw_lnx_b, xa_wk, xa_wv, w_branch, w_out, norm2_g, rg_w, rg_b, re_w, re_b, e_wg, e_wu, e_wd, final_g):
    bp, seq_p, _ = x_prompt.shape
    bs, seq_s, _ = x_sample.shape
    depth = w_in.shape[0]
    assert depth == 1 and seq_p % CHUNK == 0 and RW_ROWS % seq_s == 0 and CHUNK % seq_s == 0

    l = 0
    row = lambda a: a.reshape(1, -1)
    seg = (0, 2 * BRANCH_W, 2 * BRANCH_W + RW_COLS, 3 * BRANCH_W + RW_COLS, 3 * BRANCH_W + RW_COLS + GATE_COLS)
    w_causal = jnp.tril(gm_ws[l])
    nrep = CHUNK // seq_s
    blk = w_causal[:, :seq_s, :seq_s]
    eye_rep = jnp.eye(nrep, dtype=F32)
    mix_sample = jnp.einsum("ab,gts->gatbs", eye_rep, blk).reshape(GM_GROUPS, CHUNK, CHUNK)
    bias_prompt = jnp.repeat(gm_bs[l].T, GM_HD, axis=1)
    bias_sample = jnp.tile(bias_prompt[:seq_s], (nrep, 1))

    wr = jnp.zeros((D_MODEL, LANES), F32)
    wr = wr.at[:, :N_EXPERTS].set(jnp.transpose(re_w[l], (1, 0, 2)).reshape(D_MODEL, N_EXPERTS))
    wr = wr.at[:, N_EXPERTS:N_EXPERTS + N_GROUPS].set(rg_w[l])
    wr_hi = wr.astype(BF16)
    wr_lo = (wr - wr_hi.astype(F32)).astype(BF16)
    b_r = jnp.zeros((1, LANES), F32)
    b_r = b_r.at[0, :N_EXPERTS].set(re_b[l].reshape(-1)).at[0, N_EXPERTS:N_EXPERTS + N_GROUPS].set(rg_b[l])

    w = dict(
        norm1_g=row(norm1_g[l]),
        w_in_segs=[w_in[l][:, a:b].astype(BF16) for a, b in zip(seg[:-1], seg[1:])],
        gm_ln_g=row(gm_ln_g[l]), gm_ln_b=row(gm_ln_b[l]),
        gm_mix_prompt=w_causal.astype(BF16), gm_bias_prompt=bias_prompt,
        gm_mix_sample=mix_sample.astype(BF16), gm_bias_sample=bias_sample,
        rw_prm=(row(rw_mu[l]), row(rw_w0[l]), rw_w2[l].astype(BF16), row(rw_a0[l]), rw_a2[l].astype(BF16),
                rw_g2[l].astype(BF16), row(rw_k_k[l]), row(rw_k_a[l]), row(rw_r_k[l]),
                row(rw_lnx_g[l]), row(rw_lnx_b[l])),
        gate_b=row(gate_b[l]), w_branch=w_branch[l].astype(BF16), w_out=w_out[l].astype(BF16),
        norm2_g=row(norm2_g[l]), wr_hi=wr_hi, wr_lo=wr_lo, b_r=b_r,
        e_wg=e_wg[l].reshape(N_EXPERTS, D_MODEL, EXP_FF),
        e_wu=e_wu[l].reshape(N_EXPERTS, D_MODEL, EXP_FF),
        e_wd=e_wd[l].reshape(N_EXPERTS, EXP_FF, D_MODEL),
        final_g=row(final_g),
    )

    mk_p, mv_p = _mem_kv(mem_prompt.reshape(bp * N_MEM, D_MODEL), xa_wk[l].astype(BF16), xa_wv[l].astype(BF16))
    n_p, n_s = bp * seq_p, bs * seq_s
    merged, sp, shp, _ = _branches(x_prompt.reshape(n_p, D_MODEL), mk_p, mv_p, None, w, None, n_p + n_s, 0,
                                   prompt=True, batch=bp, seq=seq_p)
    merged, ss, shs, vs = _branches(x_sample.reshape(n_s, D_MODEL),
                                    cache_mem_k[l].reshape(bs, N_MEM * XA_HEADS, XA_HD),
                                    cache_mem_v[l].reshape(bs, N_MEM * XA_HEADS, XA_HD),
                                    (state_rwkv_S[l], state_rwkv_shift[l]), w, merged, n_p + n_s, n_p,
                                    prompt=False, batch=bs, seq=seq_s)
    yp, ys = _moe(merged, [(0, n_p), (n_p, n_s)], w["e_wg"], w["e_wu"], w["e_wd"], w["final_g"])

    return (yp.reshape(bp, seq_p, D_MODEL), ys.reshape(bs, seq_s, D_MODEL),
            sp[None], shp[None],
            mk_p.reshape(1, bp, N_MEM, XA_HEADS, XA_HD), mv_p.reshape(1, bp, N_MEM, XA_HEADS, XA_HD),
            ss[None], shs[None], vs.reshape(1, bs, seq_s, BRANCH_W))
```

```python
import functools
import itertools

import numpy as np
import jax
import jax.numpy as jnp
from jax import lax
from jax.experimental import pallas as pl
from jax.experimental.pallas import tpu as pltpu

F32 = jnp.float32
BF16 = jnp.bfloat16

D_MODEL = 1024
BRANCH_W = 512
CHUNK = 128
GM_GROUPS = 8
GM_HD = BRANCH_W // GM_GROUPS
RW_HEADS = 8
RW_HD = BRANCH_W // RW_HEADS
RW_W_LORA = 64
RW_A_LORA = 64
RW_G_LORA = 128
RW_COLS = 3 * BRANCH_W + RW_W_LORA + RW_A_LORA + RW_G_LORA
XA_HEADS = 4
XA_HD = BRANCH_W // XA_HEADS
N_MEM = 256
N_BRANCH = 3
GATE_COLS = N_BRANCH * D_MODEL
N_GROUPS = 4
EXP_PER_GROUP = 8
N_EXPERTS = N_GROUPS * EXP_PER_GROUP
EXP_FF = 512
RMS_EPS = 1e-6
LN_EPS = 1e-5
GN_EPS = 64e-5

LANES = 128
MIB = 1024 * 1024
IN_PROJ_ROWS = 256
MEM_KV_ROWS = 512
RW_ROWS = 64
RW_GROUPS = 2
XATTN_ROWS = 512
XATTN_SEQS = 8
MERGE_ROWS = 256
MOE_ROWS = 256
COMBINE_ROWS = 512
VMEM_SMALL, VMEM_MID, VMEM_IN_PROJ, VMEM_RWKV = 32, 40, 48, 56
ROUTE_E1, ROUTE_E2, ROUTE_W1, ROUTE_W2 = range(4)


def _dot(a, b):
    return jnp.dot(a.astype(BF16), b.astype(BF16), preferred_element_type=F32)


def _dot_nt(a, b):
    return lax.dot_general(a.astype(BF16), b.astype(BF16), (((1,), (1,)), ((), ())),
                           preferred_element_type=F32)


def _dot_tn(a, b):
    return lax.dot_general(a.astype(BF16), b.astype(BF16), (((0,), (0,)), ((), ())),
                           preferred_element_type=F32)


def _sigmoid(x):
    return 1.0 / (1.0 + jnp.exp(-x))


def _gelu(x):
    c = np.float32(np.sqrt(2.0 / np.pi))
    return x * (0.5 * (1.0 + jnp.tanh(c * (x + 0.044715 * (x * x * x)))))


def _softplus(x):
    return jnp.maximum(x, 0.0) + jnp.log(1.0 + jnp.exp(-jnp.abs(x)))


def _rmsnorm(x, g):
    return x * lax.rsqrt(jnp.mean(x * x, axis=-1, keepdims=True) + RMS_EPS) * g


def _split3(x):
    hi = x.astype(BF16)
    r1 = x - hi.astype(F32)
    mid = r1.astype(BF16)
    lo = (r1 - mid.astype(F32)).astype(BF16)
    return hi, mid, lo


def _params(sem, vmem_mib):
    return pltpu.CompilerParams(dimension_semantics=sem, vmem_limit_bytes=vmem_mib * MIB)


def _full(shape):
    nd = len(shape)
    return pl.BlockSpec(shape, lambda *_: (0,) * nd)


def _gmlp_chunk(pu, pv, ln_g, ln_b, wmix_ref, bmix_ref):
    u = _gelu(pu)
    vf = _gelu(pv)
    mu = jnp.mean(vf, axis=-1, keepdims=True)
    vc = vf - mu
    var = jnp.mean(vc * vc, axis=-1, keepdims=True)
    vn = vc * lax.rsqrt(var + LN_EPS) * ln_g + ln_b
    lane = lax.broadcasted_iota(jnp.int32, (CHUNK, LANES), 1)
    lo_half = lane < GM_HD
    outs = []
    for p in range(GM_GROUPS // 2):
        vp = vn[:, p * LANES:(p + 1) * LANES]
        s = (_dot(wmix_ref[2 * p], jnp.where(lo_half, vp, 0.0))
             + _dot(wmix_ref[2 * p + 1], jnp.where(lo_half, 0.0, vp)))
        outs.append(u[:, p * LANES:(p + 1) * LANES] * (s + bmix_ref[:, p * LANES:(p + 1) * LANES]))
    return jnp.concatenate(outs, axis=-1), vn


def _in_proj_kernel(x_ref, g_ref, wgm_ref, wrw_ref, wq_ref, wgt_ref, lng_ref, lnb_ref, wmix_ref, bmix_ref,
                    ogm_ref, rw_ref, q_ref, gt_ref, *vn_refs):
    xb = _rmsnorm(x_ref[...], g_ref[...]).astype(BF16)
    gm = jnp.dot(xb, wgm_ref[...], preferred_element_type=F32)
    rw_ref[...] = jnp.dot(xb, wrw_ref[...], preferred_element_type=F32)
    q_ref[...] = jnp.dot(xb, wq_ref[...], preferred_element_type=F32)
    gt_ref[...] = jnp.dot(xb, wgt_ref[...], preferred_element_type=F32)
    for c in range(x_ref.shape[0] // CHUNK):
        rows = slice(c * CHUNK, (c + 1) * CHUNK)
        o, vn = _gmlp_chunk(gm[rows, 0:BRANCH_W], gm[rows, BRANCH_W:2 * BRANCH_W], lng_ref[...], lnb_ref[...],
                            wmix_ref, bmix_ref)
        ogm_ref[rows, :] = o.astype(ogm_ref.dtype)
        if vn_refs:
            vn_refs[0][rows, :] = vn


def _in_proj(x, g, w_segs, ln_g, ln_b, wmix, bmix, emit_vn, tm):
    n = x.shape[0]
    widths = [w.shape[1] for w in w_segs]
    out_w = [BRANCH_W] + widths[1:] + ([BRANCH_W] if emit_vn else [])
    out_dt = [BF16] + [F32] * (len(out_w) - 1)
    return pl.pallas_call(
        _in_proj_kernel,
        grid=(n // tm,),
        in_specs=[pl.BlockSpec((tm, D_MODEL), lambda i: (i, 0)), _full((1, D_MODEL))]
        + [pl.BlockSpec((D_MODEL, wd), lambda i: (0, 0), pipeline_mode=pl.Buffered(1)) for wd in widths]
        + [_full((1, BRANCH_W)), _full((1, BRANCH_W)), _full((GM_GROUPS, CHUNK, CHUNK)),
           _full((CHUNK, BRANCH_W))],
        out_specs=[pl.BlockSpec((tm, wd), lambda i: (i, 0)) for wd in out_w],
        out_shape=[jax.ShapeDtypeStruct((n, wd), dt) for wd, dt in zip(out_w, out_dt)],
        compiler_params=_params(("arbitrary",), VMEM_IN_PROJ),
        name="in_proj",
    )(x, g, *w_segs, ln_g, ln_b, wmix, bmix)


def _mem_kv_kernel(m_ref, wk_ref, wv_ref, k_ref, v_ref):
    mb = m_ref[...].astype(BF16)
    k_ref[...] = jnp.dot(mb, wk_ref[...], preferred_element_type=F32)
    v_ref[...] = jnp.dot(mb, wv_ref[...], preferred_element_type=F32)


def _mem_kv(mem, wk, wv, tm=MEM_KV_ROWS):
    n = mem.shape[0]
    return pl.pallas_call(
        _mem_kv_kernel,
        grid=(n // tm,),
        in_specs=[pl.BlockSpec((tm, D_MODEL), lambda i: (i, 0)),
                  _full((D_MODEL, BRANCH_W)), _full((D_MODEL, BRANCH_W))],
        out_specs=[pl.BlockSpec((tm, BRANCH_W), lambda i: (i, 0))] * 2,
        out_shape=[jax.ShapeDtypeStruct((n, BRANCH_W), F32)] * 2,
        compiler_params=_params(("arbitrary",), VMEM_SMALL),
        name="mem_kv",
    )(mem, wk, wv)


def _rwkv_rows(p, first, prev, s_ref, o_ref, prm, nseq):
    (mu, w0, w2, a0, a2, g2, k_k, k_a, r_k, lnx_g, lnx_b) = prm
    rows = RW_ROWS
    ls = rows // nseq
    shifted = jnp.where(first, prev, pltpu.roll(p, 1, 0))
    xs = p + (shifted - p) * mu
    r = xs[:, 0:BRANCH_W]
    k = xs[:, BRANCH_W:2 * BRANCH_W]
    v = xs[:, 2 * BRANCH_W:3 * BRANCH_W]
    o = 3 * BRANCH_W
    wd = xs[:, o:o + RW_W_LORA]
    ad = xs[:, o + RW_W_LORA:o + RW_W_LORA + RW_A_LORA]
    gd = xs[:, o + RW_W_LORA + RW_A_LORA:RW_COLS]

    w_log = -_softplus(-(w0 + _dot(jnp.tanh(wd), w2))) - 0.5
    logw = -jnp.exp(w_log)
    a = _sigmoid(a0 + _dot(ad, a2))
    g = _dot(_sigmoid(gd), g2)
    kkr = k * k_k
    kf = k * (1.0 + (a - 1.0) * k_a)
    rkr = r * kf * r_k

    ri = lax.broadcasted_iota(jnp.int32, (rows, rows), 0)
    ci = lax.broadcasted_iota(jnp.int32, (rows, rows), 1)
    if nseq == 1:
        same = ci >= 0
    else:
        sh = int(np.log2(ls))
        same = lax.shift_right_logical(ri, sh) == lax.shift_right_logical(ci, sh)
    low_incl = same & (ci <= ri)
    low_strict = same & (ci < ri)
    m_incl = jnp.where(low_incl, 1.0, 0.0).astype(BF16)
    m_same = jnp.where(same, 1.0, 0.0).astype(BF16)
    hi, mid, lo = _split3(logw)
    cum = (jnp.dot(m_incl, hi, preferred_element_type=F32) + jnp.dot(m_incl, mid, preferred_element_type=F32)
           + jnp.dot(m_incl, lo, preferred_element_type=F32))
    tot = (jnp.dot(m_same, hi, preferred_element_type=F32) + jnp.dot(m_same, mid, preferred_element_type=F32)
           + jnp.dot(m_same, lo, preferred_element_type=F32))
    g_t = jnp.exp(cum)
    g_prev = jnp.exp(cum - logw)
    g_inv = jnp.exp(-cum)
    g_end = jnp.exp(tot - cum)
    g_tot = jnp.exp(tot)
    eye = jnp.where(ri == ci, 1.0, 0.0)
    yield

    hs = range(RW_HEADS)
    sls = [slice(h * RW_HD, (h + 1) * RW_HD) for h in hs]
    kk = [kkr[:, sl] for sl in sls]
    kk = [x * lax.rsqrt(jnp.maximum(jnp.sum(x * x, axis=-1, keepdims=True), 1e-24)) for x in kk]
    bv = [kk[h] * a[:, sls[h]] for h in hs]
    k_h = [kf[:, sl] for sl in sls]
    v_h = [v[:, sl] for sl in sls]
    at = [-kk[h] * g_prev[:, sls[h]] for h in hs]
    rt = [r[:, sl] * g_t[:, sl] for sl in sls]
    bt = [bv[h] * g_inv[:, sls[h]] for h in hs]
    kt = [k_h[h] * g_inv[:, sls[h]] for h in hs]
    bh = [bv[h] * g_end[:, sls[h]] for h in hs]
    kh = [k_h[h] * g_end[:, sls[h]] for h in hs]

    ar = [jnp.concatenate([at[h], rt[h]], axis=0) for h in hs]
    bk_t = [jnp.concatenate([bt[h], kt[h]], axis=0) for h in hs]
    pp = [_dot_nt(ar[h], bk_t[h]) for h in hs]
    ri2 = lax.broadcasted_iota(jnp.int32, (rows, 2 * rows), 0)
    ci2 = lax.broadcasted_iota(jnp.int32, (rows, 2 * rows), 1)
    k_half = ci2 >= rows
    cpos = jnp.where(k_half, ci2 - rows, ci2)
    if nseq == 1:
        same2 = cpos >= 0
    else:
        same2 = lax.shift_right_logical(ri2, sh) == lax.shift_right_logical(cpos, sh)
    l_ab = [jnp.where(low_strict, x[:rows, :rows], 0.0) for x in pp]
    l_kv = [jnp.where(same2 & k_half & (cpos < ri2), x[:rows], 0.0) for x in pp]
    a_y = [jnp.where(same2 & (cpos <= ri2), x[rows:], 0.0) for x in pp]
    yield

    tm = [eye + x for x in l_ab]
    pw = [_dot(x, x) for x in l_ab]
    n_dbl = int(np.log2(ls)) - 1
    for it in range(n_dbl):
        yield
        if it < n_dbl - 1:
            z = [_dot(jnp.concatenate([tm[h], pw[h]], axis=0), pw[h]) for h in hs]
            tm = [tm[h] + z[h][:rows] for h in hs]
            pw = [z[h][rows:] for h in hs]
        else:
            tm = [tm[h] + _dot(tm[h], pw[h]) for h in hs]

    if nseq == 1:
        ars = [_dot_nt(ar[h], s_ref[0, h]) for h in hs]
        as0 = [x[:rows] for x in ars]
        rs0 = [x[rows:] for x in ars]
    else:
        as0, rs0 = [], []
        for h in hs:
            zs = [_dot_nt(jnp.concatenate([at[h][b * ls:(b + 1) * ls], rt[h][b * ls:(b + 1) * ls]], axis=0),
                          s_ref[b, h]) for b in range(nseq)]
            as0.append(jnp.concatenate([x[:ls] for x in zs], axis=0))
            rs0.append(jnp.concatenate([x[ls:] for x in zs], axis=0))

    lv = [_dot(l_kv[h], jnp.concatenate([v_h[h], v_h[h]], axis=0)) for h in hs]
    yield
    u = [_dot(tm[h], as0[h] + lv[h]) for h in hs]
    yield
    y = [rs0[h] + _dot(a_y[h], jnp.concatenate([u[h], v_h[h]], axis=0)) for h in hs]
    yield

    for h in hs:
        for b in range(nseq):
            rb = slice(b * ls, (b + 1) * ls)
            uv = jnp.concatenate([u[h][rb], v_h[h][rb]], axis=0)
            bk = jnp.concatenate([bh[h][rb], kh[h][rb]], axis=0)
            s_ref[b, h] = s_ref[b, h] * g_tot[b * ls:b * ls + 1, sls[h]] + _dot_tn(uv, bk)

    outs = []
    for h in hs:
        ym = jnp.mean(y[h], axis=-1, keepdims=True)
        yc = y[h] - ym
        yv = jnp.mean(yc * yc, axis=-1, keepdims=True)
        yn = yc * lax.rsqrt(yv + GN_EPS) * lnx_g[:, sls[h]] + lnx_b[:, sls[h]]
        bonus = jnp.sum(rkr[:, sls[h]], axis=-1, keepdims=True) * v_h[h]
        outs.append((yn + bonus) * g[:, sls[h]])
    o_ref[...] = jnp.concatenate(outs, axis=-1).astype(o_ref.dtype)


def _interleave(gens):
    for _ in itertools.zip_longest(*gens):
        pass


def _rwkv_prompt_kernel(p_ref, *refs):
    prm_refs, (o_ref, s_ref, sh_ref, carry_ref) = refs[:11], refs[11:]
    c = pl.program_id(1)

    @pl.when(c == 0)
    def _():
        s_ref[...] = jnp.zeros_like(s_ref)
        carry_ref[...] = jnp.zeros_like(carry_ref)

    first = lax.broadcasted_iota(jnp.int32, (RW_ROWS, 1), 0) == 0
    prm = tuple(x[...] for x in prm_refs)
    groups = range(p_ref.shape[0])
    _interleave([_rwkv_rows(p_ref[g], first, carry_ref[8 * g:8 * g + 1, :], s_ref.at[pl.ds(g, 1)],
                            o_ref.at[g], prm, nseq=1) for g in groups])
    for g in groups:
        last = p_ref[g, RW_ROWS - 1:RW_ROWS, :]
        carry_ref[8 * g:8 * g + 1, :] = last
        sh_ref[g] = last


def _rwkv_sample_kernel(p_ref, prev_ref, s0_ref, *refs):
    prm_refs, (o_ref, s_ref, sh_ref) = refs[:11], refs[11:]
    nseq = s0_ref.shape[0] // p_ref.shape[0]
    ls = RW_ROWS // nseq
    s_ref[...] = s0_ref[...]
    first = (lax.broadcasted_iota(jnp.int32, (RW_ROWS, 1), 0) & (ls - 1)) == 0
    prm = tuple(x[...] for x in prm_refs)
    groups = range(p_ref.shape[0])
    _interleave([_rwkv_rows(p_ref[g], first, prev_ref[g], s_ref.at[pl.ds(g * nseq, nseq)], o_ref.at[g],
                            prm, nseq=nseq) for g in groups])
    for g in groups:
        for b in range(nseq):
            sh_ref[g * nseq + b:g * nseq + b + 1, :] = p_ref[g, (b + 1) * ls - 1:(b + 1) * ls, :]


def _rw_param_specs():
    shapes = [(1, RW_COLS), (1, BRANCH_W), (RW_W_LORA, BRANCH_W), (1, BRANCH_W), (RW_A_LORA, BRANCH_W),
              (RW_G_LORA, BRANCH_W), (1, BRANCH_W), (1, BRANCH_W), (1, BRANCH_W), (1, BRANCH_W), (1, BRANCH_W)]
    return [_full(s) for s in shapes]


def _rwkv_prompt(p_rw, prm, batch, seq):
    nc = seq // RW_ROWS
    g = RW_GROUPS
    p3 = p_rw.reshape(batch, seq, RW_COLS)
    o, s_new, sh = pl.pallas_call(
        _rwkv_prompt_kernel,
        grid=(batch // g, nc),
        in_specs=[pl.BlockSpec((g, RW_ROWS, RW_COLS), lambda b, c: (b, c, 0))] + _rw_param_specs(),
        out_specs=[pl.BlockSpec((g, RW_ROWS, BRANCH_W), lambda b, c: (b, c, 0)),
                   pl.BlockSpec((g, RW_HEADS, RW_HD, RW_HD), lambda b, c: (b, 0, 0, 0)),
                   pl.BlockSpec((g, 1, RW_COLS), lambda b, c: (b, 0, 0))],
        out_shape=[jax.ShapeDtypeStruct((batch, seq, BRANCH_W), BF16),
                   jax.ShapeDtypeStruct((batch, RW_HEADS, RW_HD, RW_HD), F32),
                   jax.ShapeDtypeStruct((batch, 1, RW_COLS), F32)],
        scratch_shapes=[pltpu.VMEM((8 * g, RW_COLS), F32)],
        compiler_params=_params(("arbitrary", "arbitrary"), VMEM_RWKV),
        name="rwkv_prompt",
    )(p3, *prm)
    return o.reshape(batch * seq, BRANCH_W), s_new, sh.reshape(batch, RW_COLS)


def _rwkv_sample(p_rw, prev_rows, s0, prm, seq):
    n = p_rw.shape[0]
    nseq = RW_ROWS // seq
    batch = n // seq
    g = RW_GROUPS
    ng = n // RW_ROWS
    o, s_new, sh = pl.pallas_call(
        _rwkv_sample_kernel,
        grid=(ng // g,),
        in_specs=[pl.BlockSpec((g, RW_ROWS, RW_COLS), lambda i: (i, 0, 0)),
                  pl.BlockSpec((g, RW_ROWS, RW_COLS), lambda i: (i, 0, 0)),
                  pl.BlockSpec((g * nseq, RW_HEADS, RW_HD, RW_HD), lambda i: (i, 0, 0, 0))] + _rw_param_specs(),
        out_specs=[pl.BlockSpec((g, RW_ROWS, BRANCH_W), lambda i: (i, 0, 0)),
                   pl.BlockSpec((g * nseq, RW_HEADS, RW_HD, RW_HD), lambda i: (i, 0, 0, 0)),
                   pl.BlockSpec((g * nseq, RW_COLS), lambda i: (i, 0))],
        out_shape=[jax.ShapeDtypeStruct((ng, RW_ROWS, BRANCH_W), BF16),
                   jax.ShapeDtypeStruct((batch, RW_HEADS, RW_HD, RW_HD), F32),
                   jax.ShapeDtypeStruct((batch, RW_COLS), F32)],
        compiler_params=_params(("arbitrary",), VMEM_RWKV),
        name="rwkv_sample",
    )(p_rw.reshape(ng, RW_ROWS, RW_COLS), prev_rows.reshape(ng, RW_ROWS, RW_COLS), s0, *prm)
    return o.reshape(n, BRANCH_W), s_new, sh


def _attend_all(qkv):
    s = [_dot_nt(q, k) * (XA_HD ** -0.5) for q, k, _ in qkv]
    e = [jnp.exp(x - jnp.max(x, axis=-1, keepdims=True)) for x in s]
    pr = [x / jnp.sum(x, axis=-1, keepdims=True) for x in e]
    return [_dot(p, v) for p, (_, _, v) in zip(pr, qkv)]


def _xattn_prompt_kernel(q_ref, k_ref, v_ref, o_ref):
    sls = [slice(h * XA_HD, (h + 1) * XA_HD) for h in range(XA_HEADS)]
    outs = _attend_all([(q_ref[:, sl], k_ref[:, sl], v_ref[:, sl]) for sl in sls])
    for sl, o in zip(sls, outs):
        o_ref[:, sl] = o.astype(o_ref.dtype)


def _xattn_prompt(q, mk, mv, batch, seq, tq=XATTN_ROWS):
    nt = seq // tq
    return pl.pallas_call(
        _xattn_prompt_kernel,
        grid=(batch, nt),
        in_specs=[pl.BlockSpec((tq, BRANCH_W), lambda b, i: (b * nt + i, 0)),
                  pl.BlockSpec((N_MEM, BRANCH_W), lambda b, i: (b, 0)),
                  pl.BlockSpec((N_MEM, BRANCH_W), lambda b, i: (b, 0))],
        out_specs=pl.BlockSpec((tq, BRANCH_W), lambda b, i: (b * nt + i, 0)),
        out_shape=jax.ShapeDtypeStruct((batch * seq, BRANCH_W), BF16),
        compiler_params=_params(("arbitrary", "arbitrary"), VMEM_SMALL),
        name="xattn_prompt",
    )(q, mk, mv)


def _xattn_sample_kernel(q_ref, k_ref, v_ref, o_ref):
    nb = k_ref.shape[0]
    ls = q_ref.shape[0] // nb
    nq = XA_HEADS * ls
    seqs = range(nb)
    rows = [slice(b * ls, (b + 1) * ls) for b in seqs]
    qs = [jnp.concatenate([q_ref[rows[b], h * XA_HD:(h + 1) * XA_HD] for h in range(XA_HEADS)], axis=0)
          for b in seqs]
    ri = lax.broadcasted_iota(jnp.int32, (N_MEM * XA_HEADS, nq), 0)
    ci = lax.broadcasted_iota(jnp.int32, (N_MEM * XA_HEADS, nq), 1)
    own = (ri & (XA_HEADS - 1)) == lax.shift_right_logical(ci, int(np.log2(ls)))
    s = [jnp.where(own, _dot_nt(k_ref[b], qs[b]) * (XA_HD ** -0.5), -jnp.inf) for b in seqs]
    e = [jnp.exp(x - jnp.max(x, axis=0, keepdims=True)) for x in s]
    pr = [x / jnp.sum(x, axis=0, keepdims=True) for x in e]
    outs = [_dot_tn(pr[b], v_ref[b]) for b in seqs]
    for b in seqs:
        o_ref[rows[b], :] = jnp.concatenate([outs[b][h * ls:(h + 1) * ls] for h in range(XA_HEADS)],
                                            axis=-1).astype(o_ref.dtype)


def _xattn_sample(q, mk, mv, seq, nb=XATTN_SEQS):
    batch = mk.shape[0]
    mem_spec = pl.BlockSpec((nb, N_MEM * XA_HEADS, XA_HD), lambda i: (i, 0, 0))
    return pl.pallas_call(
        _xattn_sample_kernel,
        grid=(batch // nb,),
        in_specs=[pl.BlockSpec((nb * seq, BRANCH_W), lambda i: (i, 0)), mem_spec, mem_spec],
        out_specs=pl.BlockSpec((nb * seq, BRANCH_W), lambda i: (i, 0)),
        out_shape=jax.ShapeDtypeStruct((batch * seq, BRANCH_W), BF16),
        compiler_params=_params(("arbitrary",), VMEM_MID),
        name="xattn_sample",
    )(q, mk, mv)


def _merge_kernel(n_active, *refs):
    i = pl.program_id(0)

    @pl.when(i < n_active)
    def _():
        _merge_rows(*refs)

    @pl.when(i >= n_active)
    def _():
        for ref in refs[-4:-1]:
            ref[...] = jnp.zeros_like(ref)


def _merge_rows(x_ref, gm_ref, rw_ref, xa_ref, gt_ref, gb_ref, wb_ref, wo_ref, n2_ref,
                wr_hi_ref, wr_lo_ref, br_ref, cnt_in_ref, *refs):
    h_ref, route_ref, ridx_ref, cnt_ref = refs[-4:]
    merged = None
    for n, br in enumerate((gm_ref, rw_ref, xa_ref)):
        cs = slice(n * D_MODEL, (n + 1) * D_MODEL)
        up = jnp.dot(br[...], wb_ref[n], preferred_element_type=F32)
        term = _sigmoid(gt_ref[:, cs] + gb_ref[:, cs]) * up
        merged = term if merged is None else merged + term
    h = x_ref[...] + _dot(merged, wo_ref[...])
    h_ref[...] = h
    hn = _rmsnorm(h, n2_ref[...])

    hn_hi = hn.astype(BF16)
    hn_lo = (hn - hn_hi.astype(F32)).astype(BF16)
    logits = (jnp.dot(hn_hi, wr_hi_ref[...], preferred_element_type=F32)
              + jnp.dot(hn_lo, wr_hi_ref[...], preferred_element_type=F32)
              + jnp.dot(hn_hi, wr_lo_ref[...], preferred_element_type=F32)) + br_ref[...]
    lane = lax.broadcasted_iota(jnp.int32, logits.shape, 1)
    neg = -jnp.inf
    big = jnp.int32(1 << 20)
    gmask = (lane >= N_EXPERTS) & (lane < N_EXPERTS + N_GROUPS)
    gl = jnp.where(gmask, logits, neg)
    gmax = jnp.max(gl, axis=-1, keepdims=True)
    gsel = jnp.min(jnp.where(gl == gmax, lane, big), axis=-1, keepdims=True) - N_EXPERTS
    gsum = jnp.sum(jnp.where(gmask, jnp.exp(logits - gmax), 0.0), axis=-1, keepdims=True)
    pg_top = 1.0 / gsum
    emask = (lane >= gsel * EXP_PER_GROUP) & (lane < (gsel + 1) * EXP_PER_GROUP)
    el = jnp.where(emask, logits, neg)
    m1 = jnp.max(el, axis=-1, keepdims=True)
    i1 = jnp.min(jnp.where(el == m1, lane, big), axis=-1, keepdims=True)
    el2 = jnp.where(lane == i1, neg, el)
    m2 = jnp.max(el2, axis=-1, keepdims=True)
    i2 = jnp.min(jnp.where(el2 == m2, lane, big), axis=-1, keepdims=True)
    t2 = jnp.exp(m2 - m1)
    w1 = pg_top / (1.0 + t2)
    w2 = pg_top * t2 / (1.0 + t2)

    @pl.when(pl.program_id(0) == 0)
    def _():
        cnt_ref[...] = cnt_in_ref[...]

    onehot = jnp.where(lane == i1, 1.0, 0.0) + jnp.where(lane == i2, 1.0, 0.0)
    cnt_ref[0:1, :] = cnt_ref[0:1, :] + jnp.sum(onehot, axis=0, keepdims=True)

    route = jnp.zeros(logits.shape, F32)
    for col, val in ((ROUTE_E1, i1.astype(F32)), (ROUTE_E2, i2.astype(F32)), (ROUTE_W1, w1), (ROUTE_W2, w2)):
        route = jnp.where(lane == col, val, route)
    route_ref[...] = route
    ridx_ref[...] = jnp.transpose(route)[0:8, :].astype(jnp.int32)


def _merge(x, o_gm, o_rw, o_xa, p_gate, gate_b, w_branch, w_out, n2_g, wr_hi, wr_lo, b_r, n_total, row0, prev,
           tm=MERGE_ROWS):
    n = x.shape[0]
    t0 = row0 // tm
    n_act = n // tm
    n_steps = n_total // tm if prev is None else n_act
    row = lambda wd: pl.BlockSpec((tm, wd), lambda i: (jnp.minimum(i, n_act - 1), 0))
    out_row = lambda wd: pl.BlockSpec((tm, wd), lambda i: (t0 + i, 0))
    out_shape = [jax.ShapeDtypeStruct((n_total, D_MODEL), F32),
                 jax.ShapeDtypeStruct((n_total, LANES), F32), jax.ShapeDtypeStruct((8, n_total), jnp.int32),
                 jax.ShapeDtypeStruct((8, LANES), F32)]
    if prev is None:
        carried, counts_in, aliases = [], jnp.zeros((8, LANES), F32), {}
    else:
        carried, counts_in = list(prev[:3]), prev[3]
        aliases = {13 + k: k for k in range(3)}
    return pl.pallas_call(
        functools.partial(_merge_kernel, n_act),
        grid=(n_steps,),
        in_specs=[row(D_MODEL), row(BRANCH_W), row(BRANCH_W), row(BRANCH_W), row(GATE_COLS),
                  _full((1, GATE_COLS)), _full((N_BRANCH, BRANCH_W, D_MODEL)), _full((D_MODEL, D_MODEL)),
                  _full((1, D_MODEL)), _full((D_MODEL, LANES)), _full((D_MODEL, LANES)), _full((1, LANES)),
                  _full((8, LANES))] + [pl.BlockSpec(memory_space=pl.ANY)] * len(carried),
        out_specs=[out_row(D_MODEL), out_row(LANES),
                   pl.BlockSpec((8, tm), lambda i: (0, t0 + i)), _full((8, LANES))],
        out_shape=out_shape,
        input_output_aliases=aliases,
        compiler_params=_params(("arbitrary",), VMEM_MID),
        name="merge",
    )(x, o_gm, o_rw, o_xa, p_gate, gate_b, w_branch, w_out, n2_g, wr_hi, wr_lo, b_r, counts_in, *carried)


def _route_plan(ridx, counts, tm):
    n = ridx.shape[1]
    n2 = 2 * n
    n_tiles = n2 // tm + N_EXPERTS
    n_rows = n_tiles * tm
    key_bits = int(n_rows - 1).bit_length()
    cnt = counts[0, :N_EXPERTS].astype(jnp.int32)
    tiles = (cnt + tm - 1) // tm
    tile_end = jnp.cumsum(tiles)
    n_used = tile_end[N_EXPERTS - 1:]
    tc = jnp.minimum(jnp.arange(n_tiles, dtype=jnp.int32), n_used - 1)
    tile_expert = jnp.sum((tc[:, None] >= tile_end[None, :]).astype(jnp.int32), axis=1)
    pad_end = jnp.cumsum(tiles * tm - cnt)
    d = jnp.arange(n_rows - n2, dtype=jnp.int32)
    pad_expert = jnp.sum((d[:, None] >= pad_end[None, :]).astype(jnp.int32), axis=1)
    eid = ridx[ROUTE_E1:ROUTE_E2 + 1].reshape(n2)
    keys = jnp.concatenate([(eid << key_bits) + jnp.arange(n2, dtype=jnp.int32),
                            (pad_expert << key_bits) + n2 + d])
    row_id = jnp.sort(keys) & ((1 << key_bits) - 1)
    src = jnp.where(row_id < n2, jnp.where(row_id >= n, row_id - n, row_id), 0)
    return src.reshape(n_tiles, 1, tm), row_id.reshape(n_tiles, 1, tm), tile_expert, n_used


def _row_copy(src_ref, src_row, dst_ref, dst_row, sem):
    return pltpu.make_async_copy(src_ref.at[pl.ds(src_row, 1)], dst_ref.at[pl.ds(dst_row, 1)], sem)


def _experts_kernel(te_ref, nu_ref, src_ref, nxt_ref, nx2_ref, dst_ref, n2_ref, hn_ref, wg_ref, wu_ref, wd_ref, o_ref,
                    xbuf, ybuf, wg_s, wu_s, wd_s, gsem, ssem):
    t = pl.program_id(0)
    tm = xbuf.shape[1]
    n2 = o_ref.shape[0] - N_EXPERTS * tm
    slot = t & 1
    other = 1 - slot
    g0 = lax.rem(t, 3)
    g1 = lax.rem(t + 1, 3)
    g2 = lax.rem(t + 2, 3)

    def gather(idx_ref, s):
        for r in range(tm):
            _row_copy(hn_ref, idx_ref[0, 0, r], xbuf.at[s], r, gsem.at[s]).start(priority=0)

    def wait_rows(buf, sem, s):
        for _ in range(tm):
            _row_copy(hn_ref, 0, buf.at[s], 0, sem.at[s]).wait()

    @pl.when(t == 0)
    def _():
        ybuf[1] = jnp.zeros(ybuf.shape[1:], ybuf.dtype)
        fills = [pltpu.make_async_copy(ybuf.at[1], o_ref.at[pl.ds(n2 + e * tm, tm)], ssem.at[1])
                 for e in range(N_EXPERTS)]
        for f in fills:
            f.start()
        for f in fills:
            f.wait()
        gather(src_ref, 0)
        gather(nxt_ref, 1)

    @pl.when(t < nu_ref[0])
    def _():
        gather(nx2_ref, g2)
        wait_rows(xbuf, gsem, g0)

        @pl.when(t >= 2)
        def _():
            wait_rows(ybuf, ssem, slot)

        @pl.when((t == 0) | (te_ref[t] != te_ref[jnp.maximum(t - 1, 0)]))
        def _():
            wg_s[...] = wg_ref[...].astype(BF16)
            wu_s[...] = wu_ref[...].astype(BF16)
            wd_s[...] = wd_ref[...].astype(BF16)

        xb = _rmsnorm(xbuf[g0], n2_ref[...]).astype(BF16)
        gate = jnp.dot(xb, wg_s[...], preferred_element_type=F32)
        up = jnp.dot(xb, wu_s[...], preferred_element_type=F32)
        ybuf[slot] = _dot(gate * _sigmoid(gate) * up, wd_s[...])
        for r in range(tm):
            _row_copy(ybuf.at[slot], r, o_ref, dst_ref[0, 0, r], ssem.at[slot]).start(priority=1)

        @pl.when(t == nu_ref[0] - 1)
        def _():
            wait_rows(xbuf, gsem, g1)
            wait_rows(xbuf, gsem, g2)

            @pl.when(t >= 1)
            def _():
                wait_rows(ybuf, ssem, other)

            wait_rows(ybuf, ssem, slot)


def _experts(hn, n2_g, src, dst, tile_expert, n_used, wg, wu, wd, tm):
    n = hn.shape[0]
    n_tiles = src.shape[0]
    idx = lambda f: pl.BlockSpec((1, 1, tm), lambda t, te, nu: (f(t), 0, 0), memory_space=pltpu.SMEM)
    weight = lambda shape: pl.BlockSpec((None,) + shape, lambda t, te, nu: (te[t], 0, 0))
    return pl.pallas_call(
        _experts_kernel,
        grid_spec=pltpu.PrefetchScalarGridSpec(
            num_scalar_prefetch=2,
            grid=(n_tiles,),
            in_specs=[idx(lambda t: t), idx(lambda t: jnp.minimum(t + 1, n_tiles - 1)),
                      idx(lambda t: jnp.minimum(t + 2, n_tiles - 1)), idx(lambda t: t),
                      pl.BlockSpec((1, D_MODEL), lambda t, te, nu: (0, 0)), pl.BlockSpec(memory_space=pl.ANY),
                      weight((D_MODEL, EXP_FF)), weight((D_MODEL, EXP_FF)), weight((EXP_FF, D_MODEL))],
            out_specs=pl.BlockSpec(memory_space=pl.ANY),
            scratch_shapes=[pltpu.VMEM((3, tm, D_MODEL), F32), pltpu.VMEM((2, tm, D_MODEL), F32),
                            pltpu.VMEM((D_MODEL, EXP_FF), BF16), pltpu.VMEM((D_MODEL, EXP_FF), BF16),
                            pltpu.VMEM((EXP_FF, D_MODEL), BF16),
                            pltpu.SemaphoreType.DMA((3,)), pltpu.SemaphoreType.DMA((2,))]),
        out_shape=jax.ShapeDtypeStruct((2 * n + N_EXPERTS * tm, D_MODEL), F32),
        compiler_params=_params(("arbitrary",), VMEM_MID),
        name="moe_experts",
    )(tile_expert, n_used, src, src, src, dst, n2_g, hn, wg, wu, wd)


def _combine_kernel(h_ref, route_ref, fg_ref, o1_ref, o2_ref, y_ref):
    rt = route_ref[...]
    lane = lax.broadcasted_iota(jnp.int32, rt.shape, 1)
    w1 = jnp.sum(jnp.where(lane == ROUTE_W1, rt, 0.0), axis=-1, keepdims=True)
    w2 = jnp.sum(jnp.where(lane == ROUTE_W2, rt, 0.0), axis=-1, keepdims=True)
    y_ref[...] = _rmsnorm(h_ref[...] + w1 * o1_ref[...] + w2 * o2_ref[...], fg_ref[...])


def _combine(h, route, o, final_g, row0, n_out, tm=COMBINE_ROWS):
    n = h.shape[0]
    t0 = row0 // tm
    return pl.pallas_call(
        _combine_kernel,
        grid=(n_out // tm,),
        in_specs=[pl.BlockSpec((tm, D_MODEL), lambda i: (t0 + i, 0)),
                  pl.BlockSpec((tm, LANES), lambda i: (t0 + i, 0)),
                  _full((1, D_MODEL)),
                  pl.BlockSpec((tm, D_MODEL), lambda i: (t0 + i, 0)),
                  pl.BlockSpec((tm, D_MODEL), lambda i: (n // tm + t0 + i, 0))],
        out_specs=pl.BlockSpec((tm, D_MODEL), lambda i: (i, 0)),
        out_shape=jax.ShapeDtypeStruct((n_out, D_MODEL), F32),
        compiler_params=_params(("arbitrary",), VMEM_MID),
        name="moe_combine",
    )(h, route, final_g, o, o)


def _moe(merged, group_rows, n2_g, wg, wu, wd, final_g, tm=MOE_ROWS):
    h, route, ridx, counts = merged
    src, dst, tile_expert, n_used = _route_plan(ridx, counts, tm)
    o = _experts(h, n2_g, src, dst, tile_expert, n_used, wg, wu, wd, tm)
    return [_combine(h, route, o, final_g, row0, rows) for row0, rows in group_rows]


def _branches(x2d, mem_k, mem_v, rw_state, w, merged, n_total, row0, *, prompt, batch, seq):
    n = x2d.shape[0]
    grp = "prompt" if prompt else "sample"
    o_gm, p_rw, p_q, p_gate, *vn = _in_proj(x2d, w["norm1_g"], w["w_in_segs"], w["gm_ln_g"], w["gm_ln_b"],
                                            w["gm_mix_" + grp], w["gm_bias_" + grp], not prompt, IN_PROJ_ROWS)
    vn = vn[0] if vn else None
    if prompt:
        o_rw, s_new, sh_new = _rwkv_prompt(p_rw, w["rw_prm"], batch, seq)
        o_xa = _xattn_prompt(p_q, mem_k, mem_v, batch, seq)
    else:
        s0, shift = rw_state
        prev_rows = jnp.pad(shift[:, None, :], ((0, 0), (0, seq - 1), (0, 0))).reshape(n, RW_COLS)
        o_rw, s_new, sh_new = _rwkv_sample(p_rw, prev_rows, s0, w["rw_prm"], seq)
        o_xa = _xattn_sample(p_q, mem_k, mem_v, seq)
    merged = _merge(x2d, o_gm, o_rw, o_xa, p_gate, w["gate_b"], w["w_branch"], w["w_out"],
                    w["norm2_g"], w["wr_hi"], w["wr_lo"], w["b_r"], n_total, row0, merged)
    return merged, s_new, sh_new, vn


def kernel(x_prompt, x_sample, state_rwkv_S, state_rwkv_shift, cache_mem_k, cache_mem_v, mem_prompt, norm1_g, w_in, gate_b, gm_ln_g, gm_ln_b, gm_ws, gm_bs, rw_mu, rw_w0, rw_w2, rw_a0, rw_a2, rw_g2, rw_k_k, rw_k_a, rw_r_k, rw_lnx_g, rw_lnx_b, xa_wk, xa_wv, w_branch, w_out, norm2_g, rg_w, rg_b, re_w, re_b, e_wg, e_wu, e_wd, final_g):
    bp, seq_p, _ = x_prompt.shape
    bs, seq_s, _ = x_sample.shape
    depth = w_in.shape[0]
    assert depth == 1 and seq_p % CHUNK == 0 and RW_ROWS % seq_s == 0 and CHUNK % seq_s == 0

    l = 0
    row = lambda a: a.reshape(1, -1)
    seg = (0, 2 * BRANCH_W, 2 * BRANCH_W + RW_COLS, 3 * BRANCH_W + RW_COLS, 3 * BRANCH_W + RW_COLS + GATE_COLS)
    w_causal = jnp.tril(gm_ws[l])
    nrep = CHUNK // seq_s
    blk = w_causal[:, :seq_s, :seq_s]
    eye_rep = jnp.eye(nrep, dtype=F32)
    mix_sample = jnp.einsum("ab,gts->gatbs", eye_rep, blk).reshape(GM_GROUPS, CHUNK, CHUNK)
    bias_prompt = jnp.repeat(gm_bs[l].T, GM_HD, axis=1)
    bias_sample = jnp.tile(bias_prompt[:seq_s], (nrep, 1))

    wr = jnp.zeros((D_MODEL, LANES), F32)
    wr = wr.at[:, :N_EXPERTS].set(jnp.transpose(re_w[l], (1, 0, 2)).reshape(D_MODEL, N_EXPERTS))
    wr = wr.at[:, N_EXPERTS:N_EXPERTS + N_GROUPS].set(rg_w[l])
    wr_hi = wr.astype(BF16)
    wr_lo = (wr - wr_hi.astype(F32)).astype(BF16)
    b_r = jnp.zeros((1, LANES), F32)
    b_r = b_r.at[0, :N_EXPERTS].set(re_b[l].reshape(-1)).at[0, N_EXPERTS:N_EXPERTS + N_GROUPS].set(rg_b[l])

    w = dict(
        norm1_g=row(norm1_g[l]),
        w_in_segs=[w_in[l][:, a:b].astype(BF16) for a, b in zip(seg[:-1], seg[1:])],
        gm_ln_g=row(gm_ln_g[l]), gm_ln_b=row(gm_ln_b[l]),
        gm_mix_prompt=w_causal.astype(BF16), gm_bias_prompt=bias_prompt,
        gm_mix_sample=mix_sample.astype(BF16), gm_bias_sample=bias_sample,
        rw_prm=(row(rw_mu[l]), row(rw_w0[l]), rw_w2[l].astype(BF16), row(rw_a0[l]), rw_a2[l].astype(BF16),
                rw_g2[l].astype(BF16), row(rw_k_k[l]), row(rw_k_a[l]), row(rw_r_k[l]),
                row(rw_lnx_g[l]), row(rw_lnx_b[l])),
        gate_b=row(gate_b[l]), w_branch=w_branch[l].astype(BF16), w_out=w_out[l].astype(BF16),
        norm2_g=row(norm2_g[l]), wr_hi=wr_hi, wr_lo=wr_lo, b_r=b_r,
        e_wg=e_wg[l].reshape(N_EXPERTS, D_MODEL, EXP_FF),
        e_wu=e_wu[l].reshape(N_EXPERTS, D_MODEL, EXP_FF),
        e_wd=e_wd[l].reshape(N_EXPERTS, EXP_FF, D_MODEL),
        final_g=row(final_g),
    )

    mk_p, mv_p = _mem_kv(mem_prompt.reshape(bp * N_MEM, D_MODEL), xa_wk[l].astype(BF16), xa_wv[l].astype(BF16))
    n_p, n_s = bp * seq_p, bs * seq_s
    merged, sp, shp, _ = _branches(x_prompt.reshape(n_p, D_MODEL), mk_p, mv_p, None, w, None, n_p + n_s, 0,
                                   prompt=True, batch=bp, seq=seq_p)
    merged, ss, shs, vs = _branches(x_sample.reshape(n_s, D_MODEL),
                                    cache_mem_k[l].reshape(bs, N_MEM * XA_HEADS, XA_HD),
                                    cache_mem_v[l].reshape(bs, N_MEM * XA_HEADS, XA_HD),
                                    (state_rwkv_S[l], state_rwkv_shift[l]), w, merged, n_p + n_s, n_p,
                                    prompt=False, batch=bs, seq=seq_s)
    yp, ys = _moe(merged, [(0, n_p), (n_p, n_s)], w["norm2_g"], w["e_wg"], w["e_wu"], w["e_wd"], w["final_g"])

    return (yp.reshape(bp, seq_p, D_MODEL), ys.reshape(bs, seq_s, D_MODEL),
            sp[None], shp[None],
            mk_p.reshape(1, bp, N_MEM, XA_HEADS, XA_HD), mv_p.reshape(1, bp, N_MEM, XA_HEADS, XA_HD),
            ss[None], shs[None], vs.reshape(1, bs, seq_s, BRANCH_W))
```

```python
import functools
import itertools

import numpy as np
import jax
import jax.numpy as jnp
from jax import lax
from jax.experimental import pallas as pl
from jax.experimental.pallas import tpu as pltpu

F32 = jnp.float32
BF16 = jnp.bfloat16

D_MODEL = 1024
BRANCH_W = 512
CHUNK = 128
GM_GROUPS = 8
GM_HD = BRANCH_W // GM_GROUPS
RW_HEADS = 8
RW_HD = BRANCH_W // RW_HEADS
RW_W_LORA = 64
RW_A_LORA = 64
RW_G_LORA = 128
RW_COLS = 3 * BRANCH_W + RW_W_LORA + RW_A_LORA + RW_G_LORA
XA_HEADS = 4
XA_HD = BRANCH_W // XA_HEADS
N_MEM = 256
N_BRANCH = 3
GATE_COLS = N_BRANCH * D_MODEL
N_GROUPS = 4
EXP_PER_GROUP = 8
N_EXPERTS = N_GROUPS * EXP_PER_GROUP
EXP_FF = 512
RMS_EPS = 1e-6
LN_EPS = 1e-5
GN_EPS = 64e-5

LANES = 128
MIB = 1024 * 1024
IN_PROJ_ROWS = 512
MEM_KV_ROWS = 512
RW_ROWS = 64
RW_GROUPS = 2
XATTN_ROWS = 512
XATTN_SEQS = 8
MERGE_ROWS = 256
MOE_ROWS = 256
COMBINE_ROWS = 512
VMEM_SMALL, VMEM_MID, VMEM_IN_PROJ, VMEM_RWKV = 32, 40, 48, 56
ROUTE_E1, ROUTE_E2, ROUTE_W1, ROUTE_W2 = range(4)


def _dot(a, b):
    return jnp.dot(a.astype(BF16), b.astype(BF16), preferred_element_type=F32)


def _dot_nt(a, b):
    return lax.dot_general(a.astype(BF16), b.astype(BF16), (((1,), (1,)), ((), ())),
                           preferred_element_type=F32)


def _dot_tn(a, b):
    return lax.dot_general(a.astype(BF16), b.astype(BF16), (((0,), (0,)), ((), ())),
                           preferred_element_type=F32)


def _sigmoid(x):
    return 1.0 / (1.0 + jnp.exp(-x))


def _gelu(x):
    c = np.float32(np.sqrt(2.0 / np.pi))
    return x * (0.5 * (1.0 + jnp.tanh(c * (x + 0.044715 * (x * x * x)))))


def _softplus(x):
    return jnp.maximum(x, 0.0) + jnp.log(1.0 + jnp.exp(-jnp.abs(x)))


def _rmsnorm(x, g):
    return x * lax.rsqrt(jnp.mean(x * x, axis=-1, keepdims=True) + RMS_EPS) * g


def _split3(x):
    hi = x.astype(BF16)
    r1 = x - hi.astype(F32)
    mid = r1.astype(BF16)
    lo = (r1 - mid.astype(F32)).astype(BF16)
    return hi, mid, lo


def _params(sem, vmem_mib):
    return pltpu.CompilerParams(dimension_semantics=sem, vmem_limit_bytes=vmem_mib * MIB)


def _full(shape):
    nd = len(shape)
    return pl.BlockSpec(shape, lambda *_: (0,) * nd)


def _gmlp_chunk(pu, pv, ln_g, ln_b, wmix_ref, bmix_ref):
    u = _gelu(pu)
    vf = _gelu(pv)
    mu = jnp.mean(vf, axis=-1, keepdims=True)
    vc = vf - mu
    var = jnp.mean(vc * vc, axis=-1, keepdims=True)
    vn = vc * lax.rsqrt(var + LN_EPS) * ln_g + ln_b
    lane = lax.broadcasted_iota(jnp.int32, (CHUNK, LANES), 1)
    lo_half = lane < GM_HD
    outs = []
    for p in range(GM_GROUPS // 2):
        vp = vn[:, p * LANES:(p + 1) * LANES]
        s = (_dot(wmix_ref[2 * p], jnp.where(lo_half, vp, 0.0))
             + _dot(wmix_ref[2 * p + 1], jnp.where(lo_half, 0.0, vp)))
        outs.append(u[:, p * LANES:(p + 1) * LANES] * (s + bmix_ref[:, p * LANES:(p + 1) * LANES]))
    return jnp.concatenate(outs, axis=-1), vn


def _in_proj_kernel(x_ref, g_ref, wgm_ref, wrw_ref, wq_ref, wgt_ref, lng_ref, lnb_ref, wmix_ref, bmix_ref,
                    ogm_ref, rw_ref, q_ref, gt_ref, *vn_refs):
    xb = _rmsnorm(x_ref[...], g_ref[...]).astype(BF16)
    gm = jnp.dot(xb, wgm_ref[...], preferred_element_type=F32)
    rw_ref[...] = jnp.dot(xb, wrw_ref[...], preferred_element_type=F32)
    q_ref[...] = jnp.dot(xb, wq_ref[...], preferred_element_type=F32)
    gt_ref[...] = jnp.dot(xb, wgt_ref[...], preferred_element_type=F32)
    for c in range(x_ref.shape[0] // CHUNK):
        rows = slice(c * CHUNK, (c + 1) * CHUNK)
        o, vn = _gmlp_chunk(gm[rows, 0:BRANCH_W], gm[rows, BRANCH_W:2 * BRANCH_W], lng_ref[...], lnb_ref[...],
                            wmix_ref, bmix_ref)
        ogm_ref[rows, :] = o.astype(ogm_ref.dtype)
        if vn_refs:
            vn_refs[0][rows, :] = vn


def _in_proj(x, g, w_segs, ln_g, ln_b, wmix, bmix, emit_vn, tm):
    n = x.shape[0]
    widths = [w.shape[1] for w in w_segs]
    out_w = [BRANCH_W] + widths[1:] + ([BRANCH_W] if emit_vn else [])
    out_dt = [BF16] + [F32] * (len(out_w) - 1)
    return pl.pallas_call(
        _in_proj_kernel,
        grid=(n // tm,),
        in_specs=[pl.BlockSpec((tm, D_MODEL), lambda i: (i, 0)), _full((1, D_MODEL))]
        + [pl.BlockSpec((D_MODEL, wd), lambda i: (0, 0), pipeline_mode=pl.Buffered(1)) for wd in widths]
        + [_full((1, BRANCH_W)), _full((1, BRANCH_W)), _full((GM_GROUPS, CHUNK, CHUNK)),
           _full((CHUNK, BRANCH_W))],
        out_specs=[pl.BlockSpec((tm, wd), lambda i: (i, 0)) for wd in out_w],
        out_shape=[jax.ShapeDtypeStruct((n, wd), dt) for wd, dt in zip(out_w, out_dt)],
        compiler_params=_params(("arbitrary",), VMEM_IN_PROJ),
        name="in_proj",
    )(x, g, *w_segs, ln_g, ln_b, wmix, bmix)


def _mem_kv_kernel(m_ref, wk_ref, wv_ref, k_ref, v_ref):
    mb = m_ref[...].astype(BF16)
    k_ref[...] = jnp.dot(mb, wk_ref[...], preferred_element_type=F32)
    v_ref[...] = jnp.dot(mb, wv_ref[...], preferred_element_type=F32)


def _mem_kv(mem, wk, wv, tm=MEM_KV_ROWS):
    n = mem.shape[0]
    return pl.pallas_call(
        _mem_kv_kernel,
        grid=(n // tm,),
        in_specs=[pl.BlockSpec((tm, D_MODEL), lambda i: (i, 0)),
                  _full((D_MODEL, BRANCH_W)), _full((D_MODEL, BRANCH_W))],
        out_specs=[pl.BlockSpec((tm, BRANCH_W), lambda i: (i, 0))] * 2,
        out_shape=[jax.ShapeDtypeStruct((n, BRANCH_W), F32)] * 2,
        compiler_params=_params(("arbitrary",), VMEM_SMALL),
        name="mem_kv",
    )(mem, wk, wv)


def _rwkv_rows(p, first, prev, s_ref, o_ref, prm, nseq):
    (mu, w0, w2, a0, a2, g2, k_k, k_a, r_k, lnx_g, lnx_b) = prm
    rows = RW_ROWS
    ls = rows // nseq
    shifted = jnp.where(first, prev, pltpu.roll(p, 1, 0))
    xs = p + (shifted - p) * mu
    r = xs[:, 0:BRANCH_W]
    k = xs[:, BRANCH_W:2 * BRANCH_W]
    v = xs[:, 2 * BRANCH_W:3 * BRANCH_W]
    o = 3 * BRANCH_W
    wd = xs[:, o:o + RW_W_LORA]
    ad = xs[:, o + RW_W_LORA:o + RW_W_LORA + RW_A_LORA]
    gd = xs[:, o + RW_W_LORA + RW_A_LORA:RW_COLS]

    w_log = -_softplus(-(w0 + _dot(jnp.tanh(wd), w2))) - 0.5
    logw = -jnp.exp(w_log)
    a = _sigmoid(a0 + _dot(ad, a2))
    g = _dot(_sigmoid(gd), g2)
    kkr = k * k_k
    kf = k * (1.0 + (a - 1.0) * k_a)
    rkr = r * kf * r_k

    ri = lax.broadcasted_iota(jnp.int32, (rows, rows), 0)
    ci = lax.broadcasted_iota(jnp.int32, (rows, rows), 1)
    if nseq == 1:
        same = ci >= 0
    else:
        sh = int(np.log2(ls))
        same = lax.shift_right_logical(ri, sh) == lax.shift_right_logical(ci, sh)
    low_incl = same & (ci <= ri)
    low_strict = same & (ci < ri)
    m_incl = jnp.where(low_incl, 1.0, 0.0).astype(BF16)
    m_same = jnp.where(same, 1.0, 0.0).astype(BF16)
    hi, mid, lo = _split3(logw)
    cum = (jnp.dot(m_incl, hi, preferred_element_type=F32) + jnp.dot(m_incl, mid, preferred_element_type=F32)
           + jnp.dot(m_incl, lo, preferred_element_type=F32))
    tot = (jnp.dot(m_same, hi, preferred_element_type=F32) + jnp.dot(m_same, mid, preferred_element_type=F32)
           + jnp.dot(m_same, lo, preferred_element_type=F32))
    g_t = jnp.exp(cum)
    g_prev = jnp.exp(cum - logw)
    g_inv = jnp.exp(-cum)
    g_end = jnp.exp(tot - cum)
    g_tot = jnp.exp(tot)
    eye = jnp.where(ri == ci, 1.0, 0.0)
    yield

    hs = range(RW_HEADS)
    sls = [slice(h * RW_HD, (h + 1) * RW_HD) for h in hs]
    kk = [kkr[:, sl] for sl in sls]
    kk = [x * lax.rsqrt(jnp.maximum(jnp.sum(x * x, axis=-1, keepdims=True), 1e-24)) for x in kk]
    bv = [kk[h] * a[:, sls[h]] for h in hs]
    k_h = [kf[:, sl] for sl in sls]
    v_h = [v[:, sl] for sl in sls]
    at = [-kk[h] * g_prev[:, sls[h]] for h in hs]
    rt = [r[:, sl] * g_t[:, sl] for sl in sls]
    bt = [bv[h] * g_inv[:, sls[h]] for h in hs]
    kt = [k_h[h] * g_inv[:, sls[h]] for h in hs]
    bh = [bv[h] * g_end[:, sls[h]] for h in hs]
    kh = [k_h[h] * g_end[:, sls[h]] for h in hs]

    ar = [jnp.concatenate([at[h], rt[h]], axis=0) for h in hs]
    bk_t = [jnp.concatenate([bt[h], kt[h]], axis=0) for h in hs]
    pp = [_dot_nt(ar[h], bk_t[h]) for h in hs]
    ri2 = lax.broadcasted_iota(jnp.int32, (rows, 2 * rows), 0)
    ci2 = lax.broadcasted_iota(jnp.int32, (rows, 2 * rows), 1)
    k_half = ci2 >= rows
    cpos = jnp.where(k_half, ci2 - rows, ci2)
    if nseq == 1:
        same2 = cpos >= 0
    else:
        same2 = lax.shift_right_logical(ri2, sh) == lax.shift_right_logical(cpos, sh)
    l_ab = [jnp.where(low_strict, x[:rows, :rows], 0.0) for x in pp]
    l_kv = [jnp.where(same2 & k_half & (cpos < ri2), x[:rows], 0.0) for x in pp]
    a_y = [jnp.where(same2 & (cpos <= ri2), x[rows:], 0.0) for x in pp]
    yield

    tm = [eye + x for x in l_ab]
    pw = [_dot(x, x) for x in l_ab]
    n_dbl = int(np.log2(ls)) - 1
    for it in range(n_dbl):
        yield
        if it < n_dbl - 1:
            z = [_dot(jnp.concatenate([tm[h], pw[h]], axis=0), pw[h]) for h in hs]
            tm = [tm[h] + z[h][:rows] for h in hs]
            pw = [z[h][rows:] for h in hs]
        else:
            tm = [tm[h] + _dot(tm[h], pw[h]) for h in hs]

    if nseq == 1:
        ars = [_dot_nt(ar[h], s_ref[0, h]) for h in hs]
        as0 = [x[:rows] for x in ars]
        rs0 = [x[rows:] for x in ars]
    else:
        as0, rs0 = [], []
        for h in hs:
            zs = [_dot_nt(jnp.concatenate([at[h][b * ls:(b + 1) * ls], rt[h][b * ls:(b + 1) * ls]], axis=0),
                          s_ref[b, h]) for b in range(nseq)]
            as0.append(jnp.concatenate([x[:ls] for x in zs], axis=0))
            rs0.append(jnp.concatenate([x[ls:] for x in zs], axis=0))

    lv = [_dot(l_kv[h], jnp.concatenate([v_h[h], v_h[h]], axis=0)) for h in hs]
    yield
    u = [_dot(tm[h], as0[h] + lv[h]) for h in hs]
    yield
    y = [rs0[h] + _dot(a_y[h], jnp.concatenate([u[h], v_h[h]], axis=0)) for h in hs]
    yield

    for h in hs:
        for b in range(nseq):
            rb = slice(b * ls, (b + 1) * ls)
            uv = jnp.concatenate([u[h][rb], v_h[h][rb]], axis=0)
            bk = jnp.concatenate([bh[h][rb], kh[h][rb]], axis=0)
            s_ref[b, h] = s_ref[b, h] * g_tot[b * ls:b * ls + 1, sls[h]] + _dot_tn(uv, bk)

    outs = []
    for h in hs:
        ym = jnp.mean(y[h], axis=-1, keepdims=True)
        yc = y[h] - ym
        yv = jnp.mean(yc * yc, axis=-1, keepdims=True)
        yn = yc * lax.rsqrt(yv + GN_EPS) * lnx_g[:, sls[h]] + lnx_b[:, sls[h]]
        bonus = jnp.sum(rkr[:, sls[h]], axis=-1, keepdims=True) * v_h[h]
        outs.append((yn + bonus) * g[:, sls[h]])
    o_ref[...] = jnp.concatenate(outs, axis=-1).astype(o_ref.dtype)


def _interleave(gens):
    for _ in itertools.zip_longest(*gens):
        pass


def _rwkv_prompt_kernel(p_ref, *refs):
    prm_refs, (o_ref, s_ref, sh_ref, carry_ref) = refs[:11], refs[11:]
    c = pl.program_id(1)

    @pl.when(c == 0)
    def _():
        s_ref[...] = jnp.zeros_like(s_ref)
        carry_ref[...] = jnp.zeros_like(carry_ref)

    first = lax.broadcasted_iota(jnp.int32, (RW_ROWS, 1), 0) == 0
    prm = tuple(x[...] for x in prm_refs)
    groups = range(p_ref.shape[0])
    _interleave([_rwkv_rows(p_ref[g], first, carry_ref[8 * g:8 * g + 1, :], s_ref.at[pl.ds(g, 1)],
                            o_ref.at[g], prm, nseq=1) for g in groups])
    for g in groups:
        last = p_ref[g, RW_ROWS - 1:RW_ROWS, :]
        carry_ref[8 * g:8 * g + 1, :] = last
        sh_ref[g] = last


def _rwkv_sample_kernel(p_ref, prev_ref, s0_ref, *refs):
    prm_refs, (o_ref, s_ref, sh_ref) = refs[:11], refs[11:]
    nseq = s0_ref.shape[0] // p_ref.shape[0]
    ls = RW_ROWS // nseq
    s_ref[...] = s0_ref[...]
    first = (lax.broadcasted_iota(jnp.int32, (RW_ROWS, 1), 0) & (ls - 1)) == 0
    prm = tuple(x[...] for x in prm_refs)
    groups = range(p_ref.shape[0])
    _interleave([_rwkv_rows(p_ref[g], first, prev_ref[g], s_ref.at[pl.ds(g * nseq, nseq)], o_ref.at[g],
                            prm, nseq=nseq) for g in groups])
    for g in groups:
        for b in range(nseq):
            sh_ref[g * nseq + b:g * nseq + b + 1, :] = p_ref[g, (b + 1) * ls - 1:(b + 1) * ls, :]


def _rw_param_specs():
    shapes = [(1, RW_COLS), (1, BRANCH_W), (RW_W_LORA, BRANCH_W), (1, BRANCH_W), (RW_A_LORA, BRANCH_W),
              (RW_G_LORA, BRANCH_W), (1, BRANCH_W), (1, BRANCH_W), (1, BRANCH_W), (1, BRANCH_W), (1, BRANCH_W)]
    return [_full(s) for s in shapes]


def _rwkv_prompt(p_rw, prm, batch, seq):
    nc = seq // RW_ROWS
    g = RW_GROUPS
    p3 = p_rw.reshape(batch, seq, RW_COLS)
    o, s_new, sh = pl.pallas_call(
        _rwkv_prompt_kernel,
        grid=(batch // g, nc),
        in_specs=[pl.BlockSpec((g, RW_ROWS, RW_COLS), lambda b, c: (b, c, 0))] + _rw_param_specs(),
        out_specs=[pl.BlockSpec((g, RW_ROWS, BRANCH_W), lambda b, c: (b, c, 0)),
                   pl.BlockSpec((g, RW_HEADS, RW_HD, RW_HD), lambda b, c: (b, 0, 0, 0)),
                   pl.BlockSpec((g, 1, RW_COLS), lambda b, c: (b, 0, 0))],
        out_shape=[jax.ShapeDtypeStruct((batch, seq, BRANCH_W), BF16),
                   jax.ShapeDtypeStruct((batch, RW_HEADS, RW_HD, RW_HD), F32),
                   jax.ShapeDtypeStruct((batch, 1, RW_COLS), F32)],
        scratch_shapes=[pltpu.VMEM((8 * g, RW_COLS), F32)],
        compiler_params=_params(("arbitrary", "arbitrary"), VMEM_RWKV),
        name="rwkv_prompt",
    )(p3, *prm)
    return o.reshape(batch * seq, BRANCH_W), s_new, sh.reshape(batch, RW_COLS)


def _rwkv_sample(p_rw, prev_rows, s0, prm, seq):
    n = p_rw.shape[0]
    nseq = RW_ROWS // seq
    batch = n // seq
    g = RW_GROUPS
    ng = n // RW_ROWS
    o, s_new, sh = pl.pallas_call(
        _rwkv_sample_kernel,
        grid=(ng // g,),
        in_specs=[pl.BlockSpec((g, RW_ROWS, RW_COLS), lambda i: (i, 0, 0)),
                  pl.BlockSpec((g, RW_ROWS, RW_COLS), lambda i: (i, 0, 0)),
                  pl.BlockSpec((g * nseq, RW_HEADS, RW_HD, RW_HD), lambda i: (i, 0, 0, 0))] + _rw_param_specs(),
        out_specs=[pl.BlockSpec((g, RW_ROWS, BRANCH_W), lambda i: (i, 0, 0)),
                   pl.BlockSpec((g * nseq, RW_HEADS, RW_HD, RW_HD), lambda i: (i, 0, 0, 0)),
                   pl.BlockSpec((g * nseq, RW_COLS), lambda i: (i, 0))],
        out_shape=[jax.ShapeDtypeStruct((ng, RW_ROWS, BRANCH_W), BF16),
                   jax.ShapeDtypeStruct((batch, RW_HEADS, RW_HD, RW_HD), F32),
                   jax.ShapeDtypeStruct((batch, RW_COLS), F32)],
        compiler_params=_params(("arbitrary",), VMEM_RWKV),
        name="rwkv_sample",
    )(p_rw.reshape(ng, RW_ROWS, RW_COLS), prev_rows.reshape(ng, RW_ROWS, RW_COLS), s0, *prm)
    return o.reshape(n, BRANCH_W), s_new, sh


def _attend_all(qkv):
    s = [_dot_nt(q, k) * (XA_HD ** -0.5) for q, k, _ in qkv]
    e = [jnp.exp(x - jnp.max(x, axis=-1, keepdims=True)) for x in s]
    pr = [x / jnp.sum(x, axis=-1, keepdims=True) for x in e]
    return [_dot(p, v) for p, (_, _, v) in zip(pr, qkv)]


def _xattn_prompt_kernel(q_ref, k_ref, v_ref, o_ref):
    sls = [slice(h * XA_HD, (h + 1) * XA_HD) for h in range(XA_HEADS)]
    outs = _attend_all([(q_ref[:, sl], k_ref[:, sl], v_ref[:, sl]) for sl in sls])
    for sl, o in zip(sls, outs):
        o_ref[:, sl] = o.astype(o_ref.dtype)


def _xattn_prompt(q, mk, mv, batch, seq, tq=XATTN_ROWS):
    nt = seq // tq
    return pl.pallas_call(
        _xattn_prompt_kernel,
        grid=(batch, nt),
        in_specs=[pl.BlockSpec((tq, BRANCH_W), lambda b, i: (b * nt + i, 0)),
                  pl.BlockSpec((N_MEM, BRANCH_W), lambda b, i: (b, 0)),
                  pl.BlockSpec((N_MEM, BRANCH_W), lambda b, i: (b, 0))],
        out_specs=pl.BlockSpec((tq, BRANCH_W), lambda b, i: (b * nt + i, 0)),
        out_shape=jax.ShapeDtypeStruct((batch * seq, BRANCH_W), BF16),
        compiler_params=_params(("arbitrary", "arbitrary"), VMEM_SMALL),
        name="xattn_prompt",
    )(q, mk, mv)


def _xattn_sample_kernel(q_ref, k_ref, v_ref, o_ref):
    nb = k_ref.shape[0]
    ls = q_ref.shape[0] // nb
    nq = XA_HEADS * ls
    seqs = range(nb)
    rows = [slice(b * ls, (b + 1) * ls) for b in seqs]
    qs = [jnp.concatenate([q_ref[rows[b], h * XA_HD:(h + 1) * XA_HD] for h in range(XA_HEADS)], axis=0)
          for b in seqs]
    ri = lax.broadcasted_iota(jnp.int32, (N_MEM * XA_HEADS, nq), 0)
    ci = lax.broadcasted_iota(jnp.int32, (N_MEM * XA_HEADS, nq), 1)
    own = (ri & (XA_HEADS - 1)) == lax.shift_right_logical(ci, int(np.log2(ls)))
    s = [jnp.where(own, _dot_nt(k_ref[b], qs[b]) * (XA_HD ** -0.5), -jnp.inf) for b in seqs]
    e = [jnp.exp(x - jnp.max(x, axis=0, keepdims=True)) for x in s]
    pr = [x / jnp.sum(x, axis=0, keepdims=True) for x in e]
    outs = [_dot_tn(pr[b], v_ref[b]) for b in seqs]
    for b in seqs:
        o_ref[rows[b], :] = jnp.concatenate([outs[b][h * ls:(h + 1) * ls] for h in range(XA_HEADS)],
                                            axis=-1).astype(o_ref.dtype)


def _xattn_sample(q, mk, mv, seq, nb=XATTN_SEQS):
    batch = mk.shape[0]
    mem_spec = pl.BlockSpec((nb, N_MEM * XA_HEADS, XA_HD), lambda i: (i, 0, 0))
    return pl.pallas_call(
        _xattn_sample_kernel,
        grid=(batch // nb,),
        in_specs=[pl.BlockSpec((nb * seq, BRANCH_W), lambda i: (i, 0)), mem_spec, mem_spec],
        out_specs=pl.BlockSpec((nb * seq, BRANCH_W), lambda i: (i, 0)),
        out_shape=jax.ShapeDtypeStruct((batch * seq, BRANCH_W), BF16),
        compiler_params=_params(("arbitrary",), VMEM_MID),
        name="xattn_sample",
    )(q, mk, mv)


def _merge_kernel(n_active, *refs):
    i = pl.program_id(0)

    @pl.when(i < n_active)
    def _():
        _merge_rows(*refs)

    @pl.when(i >= n_active)
    def _():
        for ref in refs[-5:-1]:
            ref[...] = jnp.zeros_like(ref)


def _merge_rows(x_ref, gm_ref, rw_ref, xa_ref, gt_ref, gb_ref, wb_ref, wo_ref, n2_ref,
                wr_hi_ref, wr_lo_ref, br_ref, cnt_in_ref, *refs):
    h_ref, hn_ref, route_ref, ridx_ref, cnt_ref = refs[-5:]
    merged = None
    for n, br in enumerate((gm_ref, rw_ref, xa_ref)):
        cs = slice(n * D_MODEL, (n + 1) * D_MODEL)
        up = jnp.dot(br[...], wb_ref[n], preferred_element_type=F32)
        term = _sigmoid(gt_ref[:, cs] + gb_ref[:, cs]) * up
        merged = term if merged is None else merged + term
    h = x_ref[...] + _dot(merged, wo_ref[...])
    h_ref[...] = h
    hn = _rmsnorm(h, n2_ref[...])
    hn_ref[...] = hn

    hn_hi = hn.astype(BF16)
    hn_lo = (hn - hn_hi.astype(F32)).astype(BF16)
    logits = (jnp.dot(hn_hi, wr_hi_ref[...], preferred_element_type=F32)
              + jnp.dot(hn_lo, wr_hi_ref[...], preferred_element_type=F32)
              + jnp.dot(hn_hi, wr_lo_ref[...], preferred_element_type=F32)) + br_ref[...]
    lane = lax.broadcasted_iota(jnp.int32, logits.shape, 1)
    neg = -jnp.inf
    big = jnp.int32(1 << 20)
    gmask = (lane >= N_EXPERTS) & (lane < N_EXPERTS + N_GROUPS)
    gl = jnp.where(gmask, logits, neg)
    gmax = jnp.max(gl, axis=-1, keepdims=True)
    gsel = jnp.min(jnp.where(gl == gmax, lane, big), axis=-1, keepdims=True) - N_EXPERTS
    gsum = jnp.sum(jnp.where(gmask, jnp.exp(logits - gmax), 0.0), axis=-1, keepdims=True)
    pg_top = 1.0 / gsum
    emask = (lane >= gsel * EXP_PER_GROUP) & (lane < (gsel + 1) * EXP_PER_GROUP)
    el = jnp.where(emask, logits, neg)
    m1 = jnp.max(el, axis=-1, keepdims=True)
    i1 = jnp.min(jnp.where(el == m1, lane, big), axis=-1, keepdims=True)
    el2 = jnp.where(lane == i1, neg, el)
    m2 = jnp.max(el2, axis=-1, keepdims=True)
    i2 = jnp.min(jnp.where(el2 == m2, lane, big), axis=-1, keepdims=True)
    t2 = jnp.exp(m2 - m1)
    w1 = pg_top / (1.0 + t2)
    w2 = pg_top * t2 / (1.0 + t2)

    @pl.when(pl.program_id(0) == 0)
    def _():
        cnt_ref[...] = cnt_in_ref[...]

    onehot = jnp.where(lane == i1, 1.0, 0.0) + jnp.where(lane == i2, 1.0, 0.0)
    cnt_ref[0:1, :] = cnt_ref[0:1, :] + jnp.sum(onehot, axis=0, keepdims=True)

    route = jnp.zeros(logits.shape, F32)
    for col, val in ((ROUTE_E1, i1.astype(F32)), (ROUTE_E2, i2.astype(F32)), (ROUTE_W1, w1), (ROUTE_W2, w2)):
        route = jnp.where(lane == col, val, route)
    route_ref[...] = route
    ridx_ref[...] = jnp.transpose(route)[0:8, :].astype(jnp.int32)


def _merge(x, o_gm, o_rw, o_xa, p_gate, gate_b, w_branch, w_out, n2_g, wr_hi, wr_lo, b_r, n_total, row0, prev,
           tm=MERGE_ROWS):
    n = x.shape[0]
    t0 = row0 // tm
    n_act = n // tm
    n_steps = n_total // tm if prev is None else n_act
    row = lambda wd: pl.BlockSpec((tm, wd), lambda i: (jnp.minimum(i, n_act - 1), 0))
    out_row = lambda wd: pl.BlockSpec((tm, wd), lambda i: (t0 + i, 0))
    out_shape = [jax.ShapeDtypeStruct((n_total, D_MODEL), F32), jax.ShapeDtypeStruct((n_total, D_MODEL), F32),
                 jax.ShapeDtypeStruct((n_total, LANES), F32), jax.ShapeDtypeStruct((8, n_total), jnp.int32),
                 jax.ShapeDtypeStruct((8, LANES), F32)]
    if prev is None:
        carried, counts_in, aliases = [], jnp.zeros((8, LANES), F32), {}
    else:
        carried, counts_in = list(prev[:4]), prev[4]
        aliases = {13 + k: k for k in range(4)}
    return pl.pallas_call(
        functools.partial(_merge_kernel, n_act),
        grid=(n_steps,),
        in_specs=[row(D_MODEL), row(BRANCH_W), row(BRANCH_W), row(BRANCH_W), row(GATE_COLS),
                  _full((1, GATE_COLS)), _full((N_BRANCH, BRANCH_W, D_MODEL)), _full((D_MODEL, D_MODEL)),
                  _full((1, D_MODEL)), _full((D_MODEL, LANES)), _full((D_MODEL, LANES)), _full((1, LANES)),
                  _full((8, LANES))] + [pl.BlockSpec(memory_space=pl.ANY)] * len(carried),
        out_specs=[out_row(D_MODEL), out_row(D_MODEL), out_row(LANES),
                   pl.BlockSpec((8, tm), lambda i: (0, t0 + i)), _full((8, LANES))],
        out_shape=out_shape,
        input_output_aliases=aliases,
        compiler_params=_params(("arbitrary",), VMEM_MID),
        name="merge",
    )(x, o_gm, o_rw, o_xa, p_gate, gate_b, w_branch, w_out, n2_g, wr_hi, wr_lo, b_r, counts_in, *carried)


def _route_plan(ridx, counts, tm):
    n = ridx.shape[1]
    n2 = 2 * n
    n_tiles = n2 // tm + N_EXPERTS
    n_rows = n_tiles * tm
    key_bits = int(n_rows - 1).bit_length()
    cnt = counts[0, :N_EXPERTS].astype(jnp.int32)
    tiles = (cnt + tm - 1) // tm
    tile_end = jnp.cumsum(tiles)
    n_used = tile_end[N_EXPERTS - 1:]
    tc = jnp.minimum(jnp.arange(n_tiles, dtype=jnp.int32), n_used - 1)
    tile_expert = jnp.sum((tc[:, None] >= tile_end[None, :]).astype(jnp.int32), axis=1)
    pad_end = jnp.cumsum(tiles * tm - cnt)
    d = jnp.arange(n_rows - n2, dtype=jnp.int32)
    pad_expert = jnp.sum((d[:, None] >= pad_end[None, :]).astype(jnp.int32), axis=1)
    eid = ridx[ROUTE_E1:ROUTE_E2 + 1].reshape(n2)
    keys = jnp.concatenate([(eid << key_bits) + jnp.arange(n2, dtype=jnp.int32),
                            (pad_expert << key_bits) + n2 + d])
    row_id = jnp.sort(keys) & ((1 << key_bits) - 1)
    src = jnp.where(row_id < n2, jnp.where(row_id >= n, row_id - n, row_id), 0)
    return src.reshape(n_tiles, 1, tm), row_id.reshape(n_tiles, 1, tm), tile_expert, n_used


def _row_copy(src_ref, src_row, dst_ref, dst_row, sem):
    return pltpu.make_async_copy(src_ref.at[pl.ds(src_row, 1)], dst_ref.at[pl.ds(dst_row, 1)], sem)


def _experts_kernel(te_ref, nu_ref, src_ref, nxt_ref, nx2_ref, dst_ref, hn_ref, wg_ref, wu_ref, wd_ref, o_ref,
                    xbuf, ybuf, wg_s, wu_s, wd_s, gsem, ssem):
    t = pl.program_id(0)
    tm = xbuf.shape[1]
    n2 = o_ref.shape[0] - N_EXPERTS * tm
    slot = t & 1
    other = 1 - slot
    g0 = lax.rem(t, 3)
    g1 = lax.rem(t + 1, 3)
    g2 = lax.rem(t + 2, 3)

    def gather(idx_ref, s):
        for r in range(tm):
            _row_copy(hn_ref, idx_ref[0, 0, r], xbuf.at[s], r, gsem.at[s]).start(priority=0)

    def wait_rows(buf, sem, s):
        for _ in range(tm):
            _row_copy(hn_ref, 0, buf.at[s], 0, sem.at[s]).wait()

    @pl.when(t == 0)
    def _():
        ybuf[1] = jnp.zeros(ybuf.shape[1:], ybuf.dtype)
        fills = [pltpu.make_async_copy(ybuf.at[1], o_ref.at[pl.ds(n2 + e * tm, tm)], ssem.at[1])
                 for e in range(N_EXPERTS)]
        for f in fills:
            f.start()
        for f in fills:
            f.wait()
        gather(src_ref, 0)
        gather(nxt_ref, 1)

    @pl.when(t < nu_ref[0])
    def _():
        gather(nx2_ref, g2)
        wait_rows(xbuf, gsem, g0)

        @pl.when(t >= 2)
        def _():
            wait_rows(ybuf, ssem, slot)

        @pl.when((t == 0) | (te_ref[t] != te_ref[jnp.maximum(t - 1, 0)]))
        def _():
            wg_s[...] = wg_ref[...].astype(BF16)
            wu_s[...] = wu_ref[...].astype(BF16)
            wd_s[...] = wd_ref[...].astype(BF16)

        xb = xbuf[g0].astype(BF16)
        gate = jnp.dot(xb, wg_s[...], preferred_element_type=F32)
        up = jnp.dot(xb, wu_s[...], preferred_element_type=F32)
        ybuf[slot] = _dot(gate * _sigmoid(gate) * up, wd_s[...])
        for r in range(tm):
            _row_copy(ybuf.at[slot], r, o_ref, dst_ref[0, 0, r], ssem.at[slot]).start(priority=1)

        @pl.when(t == nu_ref[0] - 1)
        def _():
            wait_rows(xbuf, gsem, g1)
            wait_rows(xbuf, gsem, g2)

            @pl.when(t >= 1)
            def _():
                wait_rows(ybuf, ssem, other)

            wait_rows(ybuf, ssem, slot)


def _experts(hn, src, dst, tile_expert, n_used, wg, wu, wd, tm):
    n = hn.shape[0]
    n_tiles = src.shape[0]
    idx = lambda f: pl.BlockSpec((1, 1, tm), lambda t, te, nu: (f(t), 0, 0), memory_space=pltpu.SMEM)
    weight = lambda shape: pl.BlockSpec((None,) + shape, lambda t, te, nu: (te[t], 0, 0))
    return pl.pallas_call(
        _experts_kernel,
        grid_spec=pltpu.PrefetchScalarGridSpec(
            num_scalar_prefetch=2,
            grid=(n_tiles,),
            in_specs=[idx(lambda t: t), idx(lambda t: jnp.minimum(t + 1, n_tiles - 1)),
                      idx(lambda t: jnp.minimum(t + 2, n_tiles - 1)), idx(lambda t: t),
                      pl.BlockSpec(memory_space=pl.ANY),
                      weight((D_MODEL, EXP_FF)), weight((D_MODEL, EXP_FF)), weight((EXP_FF, D_MODEL))],
            out_specs=pl.BlockSpec(memory_space=pl.ANY),
            scratch_shapes=[pltpu.VMEM((3, tm, D_MODEL), F32), pltpu.VMEM((2, tm, D_MODEL), F32),
                            pltpu.VMEM((D_MODEL, EXP_FF), BF16), pltpu.VMEM((D_MODEL, EXP_FF), BF16),
                            pltpu.VMEM((EXP_FF, D_MODEL), BF16),
                            pltpu.SemaphoreType.DMA((3,)), pltpu.SemaphoreType.DMA((2,))]),
        out_shape=jax.ShapeDtypeStruct((2 * n + N_EXPERTS * tm, D_MODEL), F32),
        compiler_params=_params(("arbitrary",), VMEM_MID),
        name="moe_experts",
    )(tile_expert, n_used, src, src, src, dst, hn, wg, wu, wd)


def _combine_kernel(h_ref, route_ref, fg_ref, o1_ref, o2_ref, y_ref):
    rt = route_ref[...]
    lane = lax.broadcasted_iota(jnp.int32, rt.shape, 1)
    w1 = jnp.sum(jnp.where(lane == ROUTE_W1, rt, 0.0), axis=-1, keepdims=True)
    w2 = jnp.sum(jnp.where(lane == ROUTE_W2, rt, 0.0), axis=-1, keepdims=True)
    y_ref[...] = _rmsnorm(h_ref[...] + w1 * o1_ref[...] + w2 * o2_ref[...], fg_ref[...])


def _combine(h, route, o, final_g, row0, n_out, tm=COMBINE_ROWS):
    n = h.shape[0]
    t0 = row0 // tm
    return pl.pallas_call(
        _combine_kernel,
        grid=(n_out // tm,),
        in_specs=[pl.BlockSpec((tm, D_MODEL), lambda i: (t0 + i, 0)),
                  pl.BlockSpec((tm, LANES), lambda i: (t0 + i, 0)),
                  _full((1, D_MODEL)),
                  pl.BlockSpec((tm, D_MODEL), lambda i: (t0 + i, 0)),
                  pl.BlockSpec((tm, D_MODEL), lambda i: (n // tm + t0 + i, 0))],
        out_specs=pl.BlockSpec((tm, D_MODEL), lambda i: (i, 0)),
        out_shape=jax.ShapeDtypeStruct((n_out, D_MODEL), F32),
        compiler_params=_params(("arbitrary",), VMEM_MID),
        name="moe_combine",
    )(h, route, final_g, o, o)


def _moe(merged, group_rows, wg, wu, wd, final_g, tm=MOE_ROWS):
    h, hn, route, ridx, counts = merged
    src, dst, tile_expert, n_used = _route_plan(ridx, counts, tm)
    o = _experts(hn, src, dst, tile_expert, n_used, wg, wu, wd, tm)
    return [_combine(h, route, o, final_g, row0, rows) for row0, rows in group_rows]


def _branches(x2d, mem_k, mem_v, rw_state, w, merged, n_total, row0, *, prompt, batch, seq):
    n = x2d.shape[0]
    grp = "prompt" if prompt else "sample"
    o_gm, p_rw, p_q, p_gate, *vn = _in_proj(x2d, w["norm1_g"], w["w_in_segs"], w["gm_ln_g"], w["gm_ln_b"],
                                            w["gm_mix_" + grp], w["gm_bias_" + grp], not prompt, IN_PROJ_ROWS)
    vn = vn[0] if vn else None
    if prompt:
        o_rw, s_new, sh_new = _rwkv_prompt(p_rw, w["rw_prm"], batch, seq)
        o_xa = _xattn_prompt(p_q, mem_k, mem_v, batch, seq)
    else:
        s0, shift = rw_state
        prev_rows = jnp.pad(shift[:, None, :], ((0, 0), (0, seq - 1), (0, 0))).reshape(n, RW_COLS)
        o_rw, s_new, sh_new = _rwkv_sample(p_rw, prev_rows, s0, w["rw_prm"], seq)
        o_xa = _xattn_sample(p_q, mem_k, mem_v, seq)
    merged = _merge(x2d, o_gm, o_rw, o_xa, p_gate, w["gate_b"], w["w_branch"], w["w_out"],
                    w["norm2_g"], w["wr_hi"], w["wr_lo"], w["b_r"], n_total, row0, merged)
    return merged, s_new, sh_new, vn


def kernel(x_prompt, x_sample, state_rwkv_S, state_rwkv_shift, cache_mem_k, cache_mem_v, mem_prompt, norm1_g, w_in, gate_b, gm_ln_g, gm_ln_b, gm_ws, gm_bs, rw_mu, rw_w0, rw_w2, rw_a0, rw_a2, rw_g2, rw_k_k, rw_k_a, rw_r_k, rw_lnx_g, rw_lnx_b, xa_wk, xa_wv, w_branch, w_out, norm2_g, rg_w, rg_b, re_w, re_b, e_wg, e_wu, e_wd, final_g):
    bp, seq_p, _ = x_prompt.shape
    bs, seq_s, _ = x_sample.shape
    depth = w_in.shape[0]
    assert depth == 1 and seq_p % CHUNK == 0 and RW_ROWS % seq_s == 0 and CHUNK % seq_s == 0

    l = 0
    row = lambda a: a.reshape(1, -1)
    seg = (0, 2 * BRANCH_W, 2 * BRANCH_W + RW_COLS, 3 * BRANCH_W + RW_COLS, 3 * BRANCH_W + RW_COLS + GATE_COLS)
    w_causal = jnp.tril(gm_ws[l])
    nrep = CHUNK // seq_s
    blk = w_causal[:, :seq_s, :seq_s]
    eye_rep = jnp.eye(nrep, dtype=F32)
    mix_sample = jnp.einsum("ab,gts->gatbs", eye_rep, blk).reshape(GM_GROUPS, CHUNK, CHUNK)
    bias_prompt = jnp.repeat(gm_bs[l].T, GM_HD, axis=1)
    bias_sample = jnp.tile(bias_prompt[:seq_s], (nrep, 1))

    wr = jnp.zeros((D_MODEL, LANES), F32)
    wr = wr.at[:, :N_EXPERTS].set(jnp.transpose(re_w[l], (1, 0, 2)).reshape(D_MODEL, N_EXPERTS))
    wr = wr.at[:, N_EXPERTS:N_EXPERTS + N_GROUPS].set(rg_w[l])
    wr_hi = wr.astype(BF16)
    wr_lo = (wr - wr_hi.astype(F32)).astype(BF16)
    b_r = jnp.zeros((1, LANES), F32)
    b_r = b_r.at[0, :N_EXPERTS].set(re_b[l].reshape(-1)).at[0, N_EXPERTS:N_EXPERTS + N_GROUPS].set(rg_b[l])

    w = dict(
        norm1_g=row(norm1_g[l]),
        w_in_segs=[w_in[l][:, a:b].astype(BF16) for a, b in zip(seg[:-1], seg[1:])],
        gm_ln_g=row(gm_ln_g[l]), gm_ln_b=row(gm_ln_b[l]),
        gm_mix_prompt=w_causal.astype(BF16), gm_bias_prompt=bias_prompt,
        gm_mix_sample=mix_sample.astype(BF16), gm_bias_sample=bias_sample,
        rw_prm=(row(rw_mu[l]), row(rw_w0[l]), rw_w2[l].astype(BF16), row(rw_a0[l]), rw_a2[l].astype(BF16),
                rw_g2[l].astype(BF16), row(rw_k_k[l]), row(rw_k_a[l]), row(rw_r_k[l]),
                row(rw_lnx_g[l]), row(rw_lnx_b[l])),
        gate_b=row(gate_b[l]), w_branch=w_branch[l].astype(BF16), w_out=w_out[l].astype(BF16),
        norm2_g=row(norm2_g[l]), wr_hi=wr_hi, wr_lo=wr_lo, b_r=b_r,
        e_wg=e_wg[l].reshape(N_EXPERTS, D_MODEL, EXP_FF),
        e_wu=e_wu[l].reshape(N_EXPERTS, D_MODEL, EXP_FF),
        e_wd=e_wd[l].reshape(N_EXPERTS, EXP_FF, D_MODEL),
        final_g=row(final_g),
    )

    mk_p, mv_p = _mem_kv(mem_prompt.reshape(bp * N_MEM, D_MODEL), xa_wk[l].astype(BF16), xa_wv[l].astype(BF16))
    n_p, n_s = bp * seq_p, bs * seq_s
    merged, sp, shp, _ = _branches(x_prompt.reshape(n_p, D_MODEL), mk_p, mv_p, None, w, None, n_p + n_s, 0,
                                   prompt=True, batch=bp, seq=seq_p)
    merged, ss, shs, vs = _branches(x_sample.reshape(n_s, D_MODEL),
                                    cache_mem_k[l].reshape(bs, N_MEM * XA_HEADS, XA_HD),
                                    cache_mem_v[l].reshape(bs, N_MEM * XA_HEADS, XA_HD),
                                    (state_rwkv_S[l], state_rwkv_shift[l]), w, merged, n_p + n_s, n_p,
                                    prompt=False, batch=bs, seq=seq_s)
    yp, ys = _moe(merged, [(0, n_p), (n_p, n_s)], w["e_wg"], w["e_wu"], w["e_wd"], w["final_g"])

    return (yp.reshape(bp, seq_p, D_MODEL), ys.reshape(bs, seq_s, D_MODEL),
            sp[None], shp[None],
            mk_p.reshape(1, bp, N_MEM, XA_HEADS, XA_HD), mv_p.reshape(1, bp, N_MEM, XA_HEADS, XA_HD),
            ss[None], shs[None], vs.reshape(1, bs, seq_s, BRANCH_W))
```
